```python
import math
import jax, jax.numpy as jnp
from jax import lax
import numpy as np

D_MODEL = 1024
BATCH = 8
SEQ = 2048
DEPTH = 2

MEM_LEN = 256
N_HEADS_GROUP = 4
HEAD_DIM = 64
GROUP_WIDTH = N_HEADS_GROUP * HEAD_DIM
N_GROUPS = 4
D_MIX = N_GROUPS * GROUP_WIDTH
ROPE_THETA = 10000.0
EPS = 1e-6
Q_BLOCK = 128
NEG_INF = -1e30
BIG = 1e30

NSA_CMP_BLOCK = 32
NSA_CMP_STRIDE = 16
NSA_SLC_BLOCK = 64
NSA_N_SELECT = 16
NSA_N_LOCAL = 2
NSA_WINDOW = 512
NSA_N_BRANCH = 3

DIFF_D = HEAD_DIM // 2

MLA_Q_RANK = 256
MLA_KV_RANK = 128
MLA_NOPE = 64
MLA_ROPE = 32
MLA_V = HEAD_DIM
MLA_QK = MLA_NOPE + MLA_ROPE

IN_SIZES = (
    GROUP_WIDTH, 6 * HEAD_DIM, NSA_N_BRANCH * N_HEADS_GROUP, GROUP_WIDTH,
    3 * GROUP_WIDTH, GROUP_WIDTH,
    MLA_Q_RANK, MLA_KV_RANK, MLA_ROPE, GROUP_WIDTH,
    GROUP_WIDTH, GROUP_WIDTH,
)
D_IN = sum(IN_SIZES)

kernel_name = 'hybrid_nsa_diff_mla_memory_block'


def rms_norm(x, g):
    xf = x.astype(jnp.float32)
    y = xf * lax.rsqrt(jnp.mean(xf * xf, axis=-1, keepdims=True) + EPS)
    return (y * g.astype(jnp.float32)).astype(x.dtype)


def rope(x, pos):
    half = x.shape[-1] // 2
    inv_freq = ROPE_THETA ** (-jnp.arange(half, dtype=jnp.float32) / half)
    ang = pos.astype(jnp.float32)[:, None] * inv_freq[None, :]
    cos = jnp.cos(ang)[:, None, :]
    sin = jnp.sin(ang)[:, None, :]
    xf = x.astype(jnp.float32)
    x1, x2 = xf[..., :half], xf[..., half:]
    return jnp.concatenate([x1 * cos - x2 * sin, x2 * cos + x1 * sin], axis=-1).astype(x.dtype)


def masked_softmax(s, mask):
    s = jnp.where(mask, s.astype(jnp.float32), NEG_INF)
    p = jax.nn.softmax(s, axis=-1)
    return jnp.where(mask, p, 0.0)


def causal_attention(q, k, v):
    B, S, H, Dk = q.shape
    nb = S // Q_BLOCK
    scale = Dk ** -0.5
    q_blocks = q.reshape(B, nb, Q_BLOCK, H, Dk).transpose(1, 0, 2, 3, 4)
    pos_blocks = jnp.arange(S).reshape(nb, Q_BLOCK)
    kpos = jnp.arange(S)

    def one_block(args):
        qi, qpos = args
        s = jnp.einsum('bqhd,bkhd->bhqk', qi, k).astype(jnp.float32) * scale
        p = masked_softmax(s, (kpos[None, :] <= qpos[:, None])[None, None])
        return jnp.einsum('bhqk,bkhd->bqhd', p.astype(v.dtype), v)

    o = lax.map(one_block, (q_blocks, pos_blocks))
    return o.transpose(1, 0, 2, 3, 4).reshape(B, S, H, v.shape[-1])


def nsa_mixer(q, kv, gate_logits, qk_gain, cmp_pe, w_cmp, pos):
    B, S, H, D = q.shape
    scale = D ** -0.5
    kc, vc, ks, vs, kw, vw = jnp.split(kv, 6, axis=-1)
    q = rope(rms_norm(q, qk_gain[0]), pos)

    nc = (S - NSA_CMP_BLOCK) // NSA_CMP_STRIDE + 1
    starts = jnp.arange(nc) * NSA_CMP_STRIDE
    idx = starts[:, None] + jnp.arange(NSA_CMP_BLOCK)[None, :]
    k_cmp = (kc[:, idx] + cmp_pe[0]).reshape(B, nc, NSA_CMP_BLOCK * D) @ w_cmp[0]
    v_cmp = (vc[:, idx] + cmp_pe[1]).reshape(B, nc, NSA_CMP_BLOCK * D) @ w_cmp[1]
    cmp_end = starts + NSA_CMP_BLOCK - 1
    k_cmp = rope(rms_norm(k_cmp, qk_gain[1])[:, :, None, :], cmp_end)[:, :, 0]
    s_cmp = jnp.einsum('bshd,bnd->bhsn', q, k_cmp).astype(jnp.float32) * scale
    p_cmp = masked_softmax(s_cmp, (cmp_end[None, :] <= pos[:, None])[None, None])
    o_cmp = jnp.einsum('bhsn,bnd->bshd', p_cmp.astype(v_cmp.dtype), v_cmp)

    ns = S // NSA_SLC_BLOCK
    ratio = NSA_SLC_BLOCK // NSA_CMP_STRIDE
    coef = np.convolve(np.ones(ratio), np.ones(NSA_CMP_BLOCK // NSA_CMP_STRIDE))
    need = ratio * (ns - 1) + len(coef)
    p_g = jnp.pad(p_cmp.sum(axis=1), ((0, 0), (0, 0), (0, need - nc)))
    p_slc = sum(float(c) * p_g[..., i: i + ratio * (ns - 1) + 1: ratio] for i, c in enumerate(coef))
    blk = jnp.arange(ns)[None, :]
    cur = (pos // NSA_SLC_BLOCK)[:, None]
    forced = (blk == 0) | ((blk <= cur) & (blk > cur - NSA_N_LOCAL))
    score = jnp.where(blk > cur, NEG_INF, jnp.where(forced, BIG, p_slc))
    n_sel = min(NSA_N_SELECT, ns)
    _, sel = lax.top_k(score, n_sel)

    ks = rope(rms_norm(ks, qk_gain[2])[:, :, None, :], pos)[:, :, 0]
    nb = S // Q_BLOCK
    q_blocks = q.reshape(B, nb, Q_BLOCK, H, D).transpose(1, 0, 2, 3, 4)
    sel_blocks = sel.reshape(B, nb, Q_BLOCK, n_sel).transpose(1, 0, 2, 3)
    pos_blocks = pos.reshape(nb, Q_BLOCK)
    in_block = jnp.arange(NSA_SLC_BLOCK)
    gather = jax.vmap(lambda t, i: t[i])

    def select_block(args):
        qi, si, qpos = args
        tok = (si[..., None] * NSA_SLC_BLOCK + in_block).reshape(B, Q_BLOCK, n_sel * NSA_SLC_BLOCK)
        k_sel = gather(ks, tok)
        v_sel = gather(vs, tok)
        s = jnp.einsum('bqhd,bqkd->bhqk', qi, k_sel).astype(jnp.float32) * scale
        p = masked_softmax(s, (tok <= qpos[None, :, None])[:, None])
        return jnp.einsum('bhqk,bqkd->bqhd', p.astype(v_sel.dtype), v_sel)

    o_slc = lax.map(select_block, (q_blocks, sel_blocks, pos_blocks))
    o_slc = o_slc.transpose(1, 0, 2, 3, 4).reshape(B, S, H, D)

    kw = rope(rms_norm(kw, qk_gain[3])[:, :, None, :], pos)[:, :, 0]
    nw = NSA_WINDOW // Q_BLOCK

    def band(t):
        tp = jnp.pad(t, ((0, 0), (NSA_WINDOW, 0), (0, 0))).reshape(B, nb + nw, Q_BLOCK, D)
        return jnp.concatenate([tp[:, i: i + nb] for i in range(nw + 1)], axis=2)

    kb, vb = band(kw), band(vw)
    qb = q.reshape(B, nb, Q_BLOCK, H, D)
    s_win = jnp.einsum('bnqhd,bnkd->bnhqk', qb, kb).astype(jnp.float32) * scale
    kpos = (jnp.arange(nb)[:, None] - nw) * Q_BLOCK + jnp.arange((nw + 1) * Q_BLOCK)[None, :]
    dist = pos_blocks[:, :, None] - kpos[:, None, :]
    win_mask = (kpos[:, None, :] >= 0) & (dist >= 0) & (dist < NSA_WINDOW)
    p_win = masked_softmax(s_win, win_mask[None, :, None])
    o_win = jnp.einsum('bnhqk,bnkd->bnqhd', p_win.astype(vb.dtype), vb).reshape(B, S, H, D)

    g = jax.nn.sigmoid(gate_logits.astype(jnp.float32)).reshape(B, S, NSA_N_BRANCH, H)[..., None]
    o = g[:, :, 0] * o_cmp + g[:, :, 1] * o_slc + g[:, :, 2] * o_win
    return o.astype(q.dtype).reshape(B, S, H * D)


def diff_mixer(q, k, v, qk_gain, lam, subln_gain, lambda_init, pos):
    B, S, _ = q.shape
    q = rope(rms_norm(q.reshape(B, S, N_HEADS_GROUP * 2, DIFF_D), qk_gain[0]), pos)
    k = rope(rms_norm(k.reshape(B, S, N_HEADS_GROUP * 2, DIFF_D), qk_gain[1]), pos)
    q = q.reshape(B, S, N_HEADS_GROUP, 2, DIFF_D)
    k = k.reshape(B, S, N_HEADS_GROUP, 2, DIFF_D)
    v = v.reshape(B, S, N_HEADS_GROUP, HEAD_DIM)
    o1 = causal_attention(q[:, :, :, 0], k[:, :, :, 0], v)
    o2 = causal_attention(q[:, :, :, 1], k[:, :, :, 1], v)
    lf = lam.astype(jnp.float32)
    lmbda = jnp.exp(jnp.sum(lf[0] * lf[1])) - jnp.exp(jnp.sum(lf[2] * lf[3])) + lambda_init
    o = o1.astype(jnp.float32) - lmbda * o2.astype(jnp.float32)
    o = rms_norm(o, subln_gain) * (1.0 - lambda_init)
    return o.astype(v.dtype).reshape(B, S, GROUP_WIDTH)


def mla_mixer(c_q, c_kv, k_rope, cq_gain, ckv_gain, w_uq, w_ukv, qk_gain, pos):
    B, S, _ = c_q.shape
    q = (rms_norm(c_q, cq_gain) @ w_uq).reshape(B, S, N_HEADS_GROUP, MLA_QK)
    kv = (rms_norm(c_kv, ckv_gain) @ w_ukv).reshape(B, S, N_HEADS_GROUP, MLA_NOPE + MLA_V)
    q = jnp.concatenate([q[..., :MLA_NOPE], rope(q[..., MLA_NOPE:], pos)], axis=-1)
    kr = rope(k_rope[:, :, None, :], pos)
    k = jnp.concatenate([kv[..., :MLA_NOPE], jnp.broadcast_to(kr, (B, S, N_HEADS_GROUP, MLA_ROPE))], axis=-1)
    v = kv[..., MLA_NOPE:]
    q = rms_norm(q, qk_gain[0])
    k = rms_norm(k, qk_gain[1])
    return causal_attention(q, k, v).reshape(B, S, GROUP_WIDTH)


def memory_mixer(q, mem, mem_gain, w_kv, qk_gain):
    B, S, _ = q.shape
    M = mem.shape[1]
    q = rms_norm(q.reshape(B, S, N_HEADS_GROUP, HEAD_DIM), qk_gain[0])
    k, v = jnp.split(rms_norm(mem, mem_gain) @ w_kv, 2, axis=-1)
    k = rms_norm(k.reshape(B, M, N_HEADS_GROUP, HEAD_DIM), qk_gain[1])
    v = v.reshape(B, M, N_HEADS_GROUP, HEAD_DIM)
    s = jnp.einsum('bshd,bmhd->bhsm', q, k).astype(jnp.float32) * HEAD_DIM ** -0.5
    p = jax.nn.softmax(s, axis=-1)
    return jnp.einsum('bhsm,bmhd->bshd', p.astype(v.dtype), v).reshape(B, S, GROUP_WIDTH)


def hybrid_layer(x, mem, layer_idx, norm_gain, w_in, w_out, nsa_qk_gain, nsa_cmp_pe, nsa_w_cmp,
                 diff_qk_gain, diff_lambda, diff_subln_gain, mla_cq_gain, mla_ckv_gain, mla_w_uq,
                 mla_w_ukv, mla_qk_gain, mem_norm_gain, mem_w_kv, mem_qk_gain):
    B, S, _ = x.shape
    pos = jnp.arange(S, dtype=jnp.int32)
    h = rms_norm(x, norm_gain)
    u = h @ w_in
    offsets = [int(o) for o in np.cumsum(IN_SIZES)[:-1]]
    (nsa_q, nsa_kv, nsa_gl, nsa_z, diff_qkv, diff_z,
     mla_cq, mla_ckv, mla_kr, mla_z, mem_q, mem_z) = jnp.split(u, offsets, axis=-1)

    y_nsa = nsa_mixer(nsa_q.reshape(B, S, N_HEADS_GROUP, HEAD_DIM), nsa_kv, nsa_gl,
                      nsa_qk_gain, nsa_cmp_pe, nsa_w_cmp, pos)
    dq, dk, dv = jnp.split(diff_qkv, 3, axis=-1)
    lambda_init = 0.8 - 0.6 * math.exp(-0.3 * layer_idx)
    y_diff = diff_mixer(dq, dk, dv, diff_qk_gain, diff_lambda, diff_subln_gain, lambda_init, pos)
    y_mla = mla_mixer(mla_cq, mla_ckv, mla_kr, mla_cq_gain, mla_ckv_gain, mla_w_uq, mla_w_ukv,
                      mla_qk_gain, pos)
    y_mem = memory_mixer(mem_q, mem, mem_norm_gain, mem_w_kv, mem_qk_gain)

    y = jnp.concatenate([y_nsa * jax.nn.silu(nsa_z), y_diff * jax.nn.silu(diff_z),
                         y_mla * jax.nn.silu(mla_z), y_mem * jax.nn.silu(mem_z)], axis=-1)
    return x + y @ w_out


def setup_inputs(seed: int = 0) -> dict:
    key = jax.random.key(seed)
    k = jax.random.split(key, 19)
    f32 = jnp.float32

    def dense(kk, shape, fan_in):
        return jax.random.normal(kk, shape, f32) * fan_in ** -0.5

    def gain(kk, shape):
        return 1.0 + 0.02 * jax.random.normal(kk, shape, f32)

    L = DEPTH
    return {
        'x': jax.random.normal(k[0], (BATCH, SEQ, D_MODEL), f32),
        'mem': jax.random.normal(k[1], (BATCH, MEM_LEN, D_MODEL), f32),
        'norm_gain': gain(k[2], (L, D_MODEL)),
        'w_in': dense(k[3], (L, D_MODEL, D_IN), D_MODEL),
        'w_out': dense(k[4], (L, D_MIX, D_MODEL), D_MIX),
        'nsa_qk_gain': gain(k[5], (L, 4, HEAD_DIM)),
        'nsa_cmp_pe': 0.1 * jax.random.normal(k[6], (L, 2, NSA_CMP_BLOCK, HEAD_DIM), f32),
        'nsa_w_cmp': dense(k[7], (L, 2, NSA_CMP_BLOCK * HEAD_DIM, HEAD_DIM), NSA_CMP_BLOCK * HEAD_DIM),
        'diff_qk_gain': gain(k[8], (L, 2, DIFF_D)),
        'diff_lambda': 0.1 * jax.random.normal(k[9], (L, 4, DIFF_D), f32),
        'diff_subln_gain': gain(k[10], (L, HEAD_DIM)),
        'mla_cq_gain': gain(k[11], (L, MLA_Q_RANK)),
        'mla_ckv_gain': gain(k[12], (L, MLA_KV_RANK)),
        'mla_w_uq': dense(k[13], (L, MLA_Q_RANK, N_HEADS_GROUP * MLA_QK), MLA_Q_RANK),
        'mla_w_ukv': dense(k[14], (L, MLA_KV_RANK, N_HEADS_GROUP * (MLA_NOPE + MLA_V)), MLA_KV_RANK),
        'mla_qk_gain': gain(k[15], (L, 2, MLA_QK)),
        'mem_norm_gain': gain(k[16], (L, D_MODEL)),
        'mem_w_kv': dense(k[17], (L, D_MODEL, 2 * GROUP_WIDTH), D_MODEL),
        'mem_qk_gain': gain(k[18], (L, 2, HEAD_DIM)),
    }


def reference(x, mem, norm_gain, w_in, w_out, nsa_qk_gain, nsa_cmp_pe, nsa_w_cmp, diff_qk_gain,
              diff_lambda, diff_subln_gain, mla_cq_gain, mla_ckv_gain, mla_w_uq, mla_w_ukv,
              mla_qk_gain, mem_norm_gain, mem_w_kv, mem_qk_gain):
    for l in range(DEPTH):
        x = hybrid_layer(x, mem, l, norm_gain[l], w_in[l], w_out[l], nsa_qk_gain[l], nsa_cmp_pe[l],
                         nsa_w_cmp[l], diff_qk_gain[l], diff_lambda[l], diff_subln_gain[l],
                         mla_cq_gain[l], mla_ckv_gain[l], mla_w_uq[l], mla_w_ukv[l], mla_qk_gain[l],
                         mem_norm_gain[l], mem_w_kv[l], mem_qk_gain[l])
    return x
```

```python
import functools
import math

import numpy as np
import jax
import jax.numpy as jnp
from jax import lax
from jax.experimental import pallas as pl
from jax.experimental.pallas import tpu as pltpu

F32 = jnp.float32
BF16 = jnp.bfloat16
HIGHEST = lax.Precision.HIGHEST

D_MODEL = 1024
DEPTH = 2
N_HEADS = 4
HEAD_DIM = 64
GROUP_WIDTH = N_HEADS * HEAD_DIM
ROPE_THETA = 10000.0
EPS = 1e-6
NEG_INF = -1e30
BIG = 1e30

NSA_CMP_BLOCK = 32
NSA_CMP_STRIDE = 16
NSA_SLC_BLOCK = 64
NSA_N_SELECT = 16
NSA_N_LOCAL = 2
NSA_WINDOW = 512
DIFF_D = HEAD_DIM // 2
MLA_Q_RANK = 256
MLA_KV_RANK = 128
MLA_NOPE = 64
MLA_ROPE = 32
MLA_QK = MLA_NOPE + MLA_ROPE

VMEM_LIMIT_BYTES = 48 * 1024 * 1024
LANES = 128

_SEGS = (
    ("nsa_q", 256), ("nsa_kcvc", 128), ("nsa_k2", 128), ("nsa_v2", 128), ("nsa_gl", 128), ("nsa_z", 256),
    ("diff_q", 256), ("diff_k", 256), ("diff_v", 256), ("diff_z", 256),
    ("mla_cq", 256), ("mla_ckv", 128), ("mla_kr", 128), ("mla_z", 256),
    ("mem_q", 256), ("mem_z", 256),
)
_SEG_W = tuple(w for _, w in _SEGS)
_D_IN_PAD = sum(_SEG_W)


def _in_col_map():
    r = lambda a, b: list(range(a, b))
    pad = lambda n: [-1] * n
    cols = []
    cols += r(0, 256)
    cols += r(256, 384)
    cols += r(384, 448) + r(512, 576)
    cols += r(448, 512) + r(576, 640)
    cols += r(640, 652) + pad(116)
    cols += r(652, 908)
    cols += r(908, 1164) + r(1164, 1420) + r(1420, 1676) + r(1676, 1932)
    cols += r(1932, 2188) + r(2188, 2316) + r(2316, 2348) + pad(96) + r(2348, 2604)
    cols += r(2604, 2860) + r(2860, 3116)
    cols = np.asarray(cols, np.int32)
    assert cols.shape[0] == _D_IN_PAD
    return np.maximum(cols, 0), (cols >= 0)


_IN_COLS, _IN_VALID = _in_col_map()


def _cparams(sem):
    return pltpu.CompilerParams(dimension_semantics=sem, vmem_limit_bytes=VMEM_LIMIT_BYTES)


def _group_rsqrt(x, group, denom):
    rows, width = x.shape
    lane = lax.broadcasted_iota(jnp.int32, (1, width), 1)
    sq = x * x
    rs = jnp.zeros_like(x)
    for g in range(width // group):
        m = (lane >= g * group) & (lane < (g + 1) * group)
        ms = jnp.sum(jnp.where(m, sq, 0.0), axis=-1, keepdims=True) / denom
        rs = jnp.where(m, lax.rsqrt(ms + EPS), rs)
    return rs


def _rope_rows(x, cos, sin_a, sin_b, half):
    width = x.shape[-1]
    return x * cos + pltpu.roll(x, width - half, 1) * sin_a + pltpu.roll(x, half, 1) * sin_b


def _rope_cols(x, cos, sin, half):
    x1, x2 = x[:, :half], x[:, half:]
    return jnp.concatenate([x1 * cos - x2 * sin, x2 * cos + x1 * sin], axis=1)


def _silu(z):
    return z * jax.nn.sigmoid(z)


def _online_update(carry, s, v_tile):
    m, l, acc = carry
    m_new = jnp.maximum(m, jnp.max(s, axis=-1, keepdims=True))
    alpha = jnp.exp(m - m_new)
    p = jnp.exp(s - m_new)
    l = alpha * l + jnp.sum(p, axis=-1, keepdims=True)
    lead = s.shape[:-1]
    pv = jnp.dot(p.reshape(-1, s.shape[-1]).astype(BF16), v_tile, preferred_element_type=F32)
    acc = alpha * acc + pv.reshape(lead + (v_tile.shape[-1],))
    return m_new, l, acc


def _causal_flash(qb, get_k, get_v, qi, t, dv):
    init = (jnp.full((t, 1), NEG_INF, F32), jnp.zeros((t, 1), F32), jnp.zeros((t, dv), F32))

    def body(j, carry):
        off = pl.multiple_of(j * t, t)
        s = jnp.dot(qb, get_k(off), preferred_element_type=F32)
        return _online_update(carry, s, get_v(off))

    carry = lax.fori_loop(0, qi, body, init)
    off = pl.multiple_of(qi * t, t)
    s = jnp.dot(qb, get_k(off), preferred_element_type=F32)
    row = lax.broadcasted_iota(jnp.int32, (t, t), 0)
    col = lax.broadcasted_iota(jnp.int32, (t, t), 1)
    s = jnp.where(col <= row, s, NEG_INF)
    m, l, acc = _online_update(carry, s, get_v(off))
    return acc / l


def _in_proj_kernel(x_ref, g_ref, w_ref, *out_refs):
    x = x_ref[...]
    ms = jnp.mean(x * x, axis=-1, keepdims=True)
    h = (x * lax.rsqrt(ms + EPS) * g_ref[...]).astype(BF16)
    off = 0
    for o_ref, w in zip(out_refs, _SEG_W):
        o_ref[...] = jnp.dot(h, w_ref[:, off:off + w], preferred_element_type=F32)
        off += w


def _in_proj(x2, gain, w_pad, tm=256):
    n = x2.shape[0]
    return pl.pallas_call(
        _in_proj_kernel,
        grid=(n // tm,),
        in_specs=[
            pl.BlockSpec((tm, D_MODEL), lambda i: (i, 0)),
            pl.BlockSpec((1, D_MODEL), lambda i: (0, 0)),
            pl.BlockSpec((D_MODEL, _D_IN_PAD), lambda i: (0, 0)),
        ],
        out_specs=[pl.BlockSpec((tm, w), lambda i: (i, 0)) for w in _SEG_W],
        out_shape=[jax.ShapeDtypeStruct((n, w), F32) for w in _SEG_W],
        compiler_params=_cparams(("parallel",)),
        name="in_proj",
    )(x2, gain, w_pad)


def _out_proj_kernel(x_ref, y0_ref, y1_ref, y2_ref, y3_ref, w_ref, o_ref):
    acc = x_ref[...]
    for g, y_ref in enumerate((y0_ref, y1_ref, y2_ref, y3_ref)):
        acc = acc + jnp.dot(y_ref[...], w_ref[g * GROUP_WIDTH:(g + 1) * GROUP_WIDTH, :],
                            preferred_element_type=F32)
    o_ref[...] = acc


def _out_proj(x2, ys, w_out_b, tm=512):
    n = x2.shape[0]
    yspec = pl.BlockSpec((tm, GROUP_WIDTH), lambda i: (i, 0))
    return pl.pallas_call(
        _out_proj_kernel,
        grid=(n // tm,),
        in_specs=[pl.BlockSpec((tm, D_MODEL), lambda i: (i, 0)), yspec, yspec, yspec, yspec,
                  pl.BlockSpec((D_MODEL, D_MODEL), lambda i: (0, 0))],
        out_specs=pl.BlockSpec((tm, D_MODEL), lambda i: (i, 0)),
        out_shape=jax.ShapeDtypeStruct((n, D_MODEL), F32),
        compiler_params=_cparams(("parallel",)),
        name="out_proj",
    )(x2, *ys, w_out_b)


def _diff_kernel(lambda_init, seq, t,
                 q_ref, k_ref, v_ref, z_ref, qg_ref, kg_ref, lam_ref, sg_ref,
                 cosq_ref, sina_ref, sinb_ref, cosk_ref, sink_ref,
                 y_ref, kT_s, vh_s):
    qi = pl.program_id(1)
    n_maps = 2 * N_HEADS
    half = DIFF_D // 2

    @pl.when(qi == 0)
    def _prep():
        ck = 512
        for c0 in range(0, seq, ck):
            kt = k_ref[0, c0:c0 + ck, :].T
            k3 = kt.reshape(n_maps, DIFF_D, ck)
            ms = jnp.mean(k3 * k3, axis=1, keepdims=True)
            kn = k3 * lax.rsqrt(ms + EPS) * kg_ref[...].reshape(1, DIFF_D, 1)
            kr = _rope_cols(kn, cosk_ref[:, c0:c0 + ck][None], sink_ref[:, c0:c0 + ck][None], half)
            kT_s[:, c0:c0 + ck] = kr.reshape(n_maps * DIFF_D, ck).astype(BF16)
        v = v_ref[0]
        for h in range(N_HEADS):
            vh_s[h] = v[:, h * HEAD_DIM:(h + 1) * HEAD_DIM].astype(BF16)

    q = q_ref[0]
    qn = q * _group_rsqrt(q, DIFF_D, float(DIFF_D)) * qg_ref[...]
    qr = _rope_rows(qn, cosq_ref[...], sina_ref[...], sinb_ref[...], half)
    qf = qr * (DIFF_D ** -0.5)

    lam = lam_ref[...]
    lmbda = (jnp.exp(jnp.sum(lam[0:1] * lam[1:2], axis=-1, keepdims=True))
             - jnp.exp(jnp.sum(lam[2:3] * lam[3:4], axis=-1, keepdims=True)) + lambda_init)

    outs = []
    for h in range(N_HEADS):
        get_v = lambda off, h=h: vh_s[h, pl.ds(off, t), :]
        o = []
        for mp in range(2):
            r0 = (2 * h + mp) * DIFF_D
            get_k = lambda off, r0=r0: kT_s[r0:r0 + DIFF_D, pl.ds(off, t)]
            o.append(_causal_flash(qf[:, r0:r0 + DIFF_D].astype(BF16), get_k, get_v, qi, t, HEAD_DIM))
        d = o[0] - lmbda * o[1]
        ms = jnp.mean(d * d, axis=-1, keepdims=True)
        outs.append(d * lax.rsqrt(ms + EPS) * sg_ref[...] * (1.0 - lambda_init))
    y = jnp.concatenate(outs, axis=-1) * _silu(z_ref[0])
    y_ref[0] = y.astype(BF16)


def _diff_call(q, k, v, z, qk_gain, lam, subln_gain, lambda_init, tabs32, t=256):
    b, seq, _ = q.shape
    cos_f, sin_a, sin_b, cos_t, sin_t = tabs32
    tile = pl.BlockSpec((1, t, GROUP_WIDTH), lambda bi, qi: (bi, qi, 0))
    full = pl.BlockSpec((1, seq, GROUP_WIDTH), lambda bi, qi: (bi, 0, 0))
    const = lambda shape: pl.BlockSpec(shape, lambda bi, qi: tuple(0 for _ in shape))
    tab = pl.BlockSpec((t, GROUP_WIDTH), lambda bi, qi: (qi, 0))
    return pl.pallas_call(
        functools.partial(_diff_kernel, lambda_init, seq, t),
        grid=(b, seq // t),
        in_specs=[tile, full, full, tile,
                  const((1, GROUP_WIDTH)), const((DIFF_D, 1)), const((4, DIFF_D)), const((1, HEAD_DIM)),
                  tab, tab, tab, const((DIFF_D // 2, seq)), const((DIFF_D // 2, seq))],
        out_specs=tile,
        out_shape=jax.ShapeDtypeStruct((b, seq, GROUP_WIDTH), BF16),
        scratch_shapes=[pltpu.VMEM((2 * N_HEADS * DIFF_D, seq), BF16),
                        pltpu.VMEM((N_HEADS, seq, HEAD_DIM), BF16)],
        compiler_params=_cparams(("parallel", "arbitrary")),
        name="diff_attn",
    )(q, k, v, z,
      jnp.tile(qk_gain[0], 2 * N_HEADS)[None, :], qk_gain[1][:, None], lam, subln_gain[None, :],
      cos_f, sin_a, sin_b, cos_t, sin_t)


def _mla_kernel(seq, t,
                cq_ref, ckv_ref, kr_ref, z_ref, cqg_ref, ckvg_ref, wuq_ref, wkT_ref, wv_ref,
                qgn_ref, qgr_ref, kgn_ref, kgr_ref,
                cosq_ref, sina_ref, sinb_ref, cosk_ref, sink_ref,
                y_ref, kT_s, vh_s):
    qi = pl.program_id(1)
    half = MLA_ROPE // 2
    nt = (((1,), (1,)), ((), ()))

    @pl.when(qi == 0)
    def _prep():
        ckv = ckv_ref[0]
        ms = jnp.mean(ckv * ckv, axis=-1, keepdims=True)
        cb = (ckv * lax.rsqrt(ms + EPS) * ckvg_ref[...]).astype(BF16)
        v = jnp.dot(cb, wv_ref[...], preferred_element_type=F32)
        for h in range(N_HEADS):
            vh_s[h] = v[:, h * HEAD_DIM:(h + 1) * HEAD_DIM].astype(BF16)
        ck = 512
        for c0 in range(0, seq, ck):
            knT = lax.dot_general(wkT_ref[...], cb[c0:c0 + ck], nt, preferred_element_type=F32)
            krT = kr_ref[0, c0:c0 + ck, :].T[:MLA_ROPE]
            krT = _rope_cols(krT[None], cosk_ref[:, c0:c0 + ck][None], sink_ref[:, c0:c0 + ck][None], half)[0]
            kr_ss = jnp.sum(krT * krT, axis=0, keepdims=True)
            for h in range(N_HEADS):
                kn = knT[h * MLA_NOPE:(h + 1) * MLA_NOPE]
                ms = (jnp.sum(kn * kn, axis=0, keepdims=True) + kr_ss) / float(MLA_QK)
                rs = lax.rsqrt(ms + EPS)
                kT_s[h, :, c0:c0 + ck] = jnp.concatenate(
                    [kn * rs * kgn_ref[...], krT * rs * kgr_ref[...]], axis=0).astype(BF16)

    cq = cq_ref[0]
    ms = jnp.mean(cq * cq, axis=-1, keepdims=True)
    cqb = (cq * lax.rsqrt(ms + EPS) * cqg_ref[...]).astype(BF16)
    qa = jnp.dot(cqb, wuq_ref[...], preferred_element_type=F32)
    qn = qa[:, :N_HEADS * MLA_NOPE]
    qr = _rope_rows(qa[:, N_HEADS * MLA_NOPE:], cosq_ref[...], sina_ref[...], sinb_ref[...], half)
    scale = MLA_QK ** -0.5

    outs = []
    for h in range(N_HEADS):
        qn_h = qn[:, h * MLA_NOPE:(h + 1) * MLA_NOPE]
        qr_h = qr[:, h * MLA_ROPE:(h + 1) * MLA_ROPE]
        ms = (jnp.sum(qn_h * qn_h, axis=-1, keepdims=True)
              + jnp.sum(qr_h * qr_h, axis=-1, keepdims=True)) / float(MLA_QK)
        rs = lax.rsqrt(ms + EPS)
        q_h = jnp.concatenate([qn_h * rs * qgn_ref[...], qr_h * rs * qgr_ref[...]], axis=-1)
        qb = (q_h * scale).astype(BF16)
        get_k = lambda off, h=h: kT_s[h, :, pl.ds(off, t)]
        get_v = lambda off, h=h: vh_s[h, pl.ds(off, t), :]
        outs.append(_causal_flash(qb, get_k, get_v, qi, t, HEAD_DIM))
    y = jnp.concatenate(outs, axis=-1) * _silu(z_ref[0])
    y_ref[0] = y.astype(BF16)


def _mla_call(cq, ckv, kr, z, cq_gain, ckv_gain, w_uq, w_ukv, qk_gain, tabs32, t=256):
    b, seq, _ = cq.shape
    cos_f, sin_a, sin_b, cos_t, sin_t = tabs32
    uq = w_uq.reshape(MLA_Q_RANK, N_HEADS, MLA_QK)
    w_uq_r = jnp.concatenate([uq[:, :, :MLA_NOPE].reshape(MLA_Q_RANK, -1),
                              uq[:, :, MLA_NOPE:].reshape(MLA_Q_RANK, -1)], axis=1).astype(BF16)
    ukv = w_ukv.reshape(MLA_KV_RANK, N_HEADS, MLA_NOPE + HEAD_DIM)
    w_kT = ukv[:, :, :MLA_NOPE].reshape(MLA_KV_RANK, -1).T.astype(BF16)
    w_v = ukv[:, :, MLA_NOPE:].reshape(MLA_KV_RANK, -1).astype(BF16)
    tile = pl.BlockSpec((1, t, GROUP_WIDTH), lambda bi, qi: (bi, qi, 0))
    const = lambda shape: pl.BlockSpec(shape, lambda bi, qi: tuple(0 for _ in shape))
    tab = pl.BlockSpec((t, N_HEADS * MLA_ROPE), lambda bi, qi: (qi, 0))
    return pl.pallas_call(
        functools.partial(_mla_kernel, seq, t),
        grid=(b, seq // t),
        in_specs=[tile,
                  pl.BlockSpec((1, seq, MLA_KV_RANK), lambda bi, qi: (bi, 0, 0)),
                  pl.BlockSpec((1, seq, LANES), lambda bi, qi: (bi, 0, 0)),
                  tile,
                  const((1, MLA_Q_RANK)), const((1, MLA_KV_RANK)),
                  const((MLA_Q_RANK, N_HEADS * MLA_QK)), const((N_HEADS * MLA_NOPE, MLA_KV_RANK)),
                  const((MLA_KV_RANK, GROUP_WIDTH)),
                  const((1, MLA_NOPE)), const((1, MLA_ROPE)), const((MLA_NOPE, 1)), const((MLA_ROPE, 1)),
                  tab, tab, tab, const((MLA_ROPE // 2, seq)), const((MLA_ROPE // 2, seq))],
        out_specs=tile,
        out_shape=jax.ShapeDtypeStruct((b, seq, GROUP_WIDTH), BF16),
        scratch_shapes=[pltpu.VMEM((N_HEADS, MLA_QK, seq), BF16),
                        pltpu.VMEM((N_HEADS, seq, HEAD_DIM), BF16)],
        compiler_params=_cparams(("parallel", "arbitrary")),
        name="mla_attn",
    )(cq, ckv, kr, z, cq_gain[None, :], ckv_gain[None, :], w_uq_r, w_kT, w_v,
      qk_gain[0, :MLA_NOPE][None, :], qk_gain[0, MLA_NOPE:][None, :],
      qk_gain[1, :MLA_NOPE][:, None], qk_gain[1, MLA_NOPE:][:, None],
      cos_f, sin_a, sin_b, cos_t, sin_t)


def _mem_kernel(q_ref, mem_ref, z_ref, mg_ref, wkT_ref, wv_ref, qg_ref, kg_ref, y_ref, kT_s, vh_s):
    qi = pl.program_id(1)
    nt = (((1,), (1,)), ((), ()))
    m_len = mem_ref.shape[1]

    @pl.when(qi == 0)
    def _prep():
        mem = mem_ref[0]
        ms = jnp.mean(mem * mem, axis=-1, keepdims=True)
        mb = (mem * lax.rsqrt(ms + EPS) * mg_ref[...]).astype(BF16)
        kT = lax.dot_general(wkT_ref[...], mb, nt, preferred_element_type=F32)
        k3 = kT.reshape(N_HEADS, HEAD_DIM, m_len)
        ms = jnp.mean(k3 * k3, axis=1, keepdims=True)
        kn = k3 * lax.rsqrt(ms + EPS) * kg_ref[...].reshape(1, HEAD_DIM, 1)
        kT_s[...] = kn.reshape(GROUP_WIDTH, m_len).astype(BF16)
        v = jnp.dot(mb, wv_ref[...], preferred_element_type=F32)
        for h in range(N_HEADS):
            vh_s[h] = v[:, h * HEAD_DIM:(h + 1) * HEAD_DIM].astype(BF16)

    q = q_ref[0]
    qn = q * _group_rsqrt(q, HEAD_DIM, float(HEAD_DIM)) * qg_ref[...]
    qf = qn * (HEAD_DIM ** -0.5)
    outs = []
    for h in range(N_HEADS):
        s = jnp.dot(qf[:, h * HEAD_DIM:(h + 1) * HEAD_DIM].astype(BF16), kT_s[h * HEAD_DIM:(h + 1) * HEAD_DIM, :],
                    preferred_element_type=F32)
        m = jnp.max(s, axis=-1, keepdims=True)
        e = jnp.exp(s - m)
        l = jnp.sum(e, axis=-1, keepdims=True)
        outs.append(jnp.dot(e.astype(BF16), vh_s[h], preferred_element_type=F32) / l)
    y = jnp.concatenate(outs, axis=-1) * _silu(z_ref[0])
    y_ref[0] = y.astype(BF16)


def _mem_call(q, mem, z, mem_gain, w_kv, qk_gain, t=256):
    b, seq, _ = q.shape
    m_len = mem.shape[1]
    w_kT = w_kv[:, :GROUP_WIDTH].T.astype(BF16)
    w_v = w_kv[:, GROUP_WIDTH:].astype(BF16)
    tile = pl.BlockSpec((1, t, GROUP_WIDTH), lambda bi, qi: (bi, qi, 0))
    const = lambda shape: pl.BlockSpec(shape, lambda bi, qi: tuple(0 for _ in shape))
    return pl.pallas_call(
        _mem_kernel,
        grid=(b, seq // t),
        in_specs=[tile, pl.BlockSpec((1, m_len, D_MODEL), lambda bi, qi: (bi, 0, 0)), tile,
                  const((1, D_MODEL)), const((GROUP_WIDTH, D_MODEL)), const((D_MODEL, GROUP_WIDTH)),
                  const((1, GROUP_WIDTH)), const((HEAD_DIM, 1))],
        out_specs=tile,
        out_shape=jax.ShapeDtypeStruct((b, seq, GROUP_WIDTH), BF16),
        scratch_shapes=[pltpu.VMEM((GROUP_WIDTH, m_len), BF16),
                        pltpu.VMEM((N_HEADS, m_len, HEAD_DIM), BF16)],
        compiler_params=_cparams(("parallel", "arbitrary")),
        name="mem_attn",
    )(q, mem, z, mem_gain[None, :], w_kT, w_v, jnp.tile(qk_gain[0], N_HEADS)[None, :], qk_gain[1][:, None])


def _nsa_kernel(seq, t,
                q_ref, kcvc_ref, k2_ref, v2_ref, gl_ref, z_ref,
                qg_ref, cg_ref, k2g_ref, pelo_ref, pehi_ref, wlo_ref, whi_ref, wloT_ref, whiT_ref,
                cosq_ref, sina_ref, sinb_ref, cosk_ref, sink_ref, cosc_ref, sinc_ref,
                cmat_ref, emat_ref,
                y_ref, k2T_s, v2_s, kcT_s, vc_s):
    qi = pl.program_id(1)
    half = HEAD_DIM // 2
    nt = (((1,), (1,)), ((), ()))
    n_chunk = seq // NSA_CMP_STRIDE
    n_cmp = n_chunk - 1
    n_blk = seq // NSA_SLC_BLOCK
    win_tiles = NSA_WINDOW // t

    @pl.when(qi == 0)
    def _prep():
        ch = kcvc_ref[0]
        chl = ch + pelo_ref[...]
        chh = ch + pehi_ref[...]
        a = jnp.dot(chl, wlo_ref[...], precision=HIGHEST, preferred_element_type=F32)
        bm = jnp.dot(chh, whi_ref[...], precision=HIGHEST, preferred_element_type=F32)
        vc_s[...] = (a + pltpu.roll(bm, n_chunk - 1, 0)).astype(BF16)
        at = lax.dot_general(wloT_ref[...], chl, nt, precision=HIGHEST, preferred_element_type=F32)
        bt = lax.dot_general(whiT_ref[...], chh, nt, precision=HIGHEST, preferred_element_type=F32)
        kc = (at + pltpu.roll(bt, n_chunk - 1, 1))[:HEAD_DIM]
        ms = jnp.mean(kc * kc, axis=0, keepdims=True)
        kc = kc * lax.rsqrt(ms + EPS) * cg_ref[...]
        kcT_s[...] = _rope_cols(kc[None], cosc_ref[...][None], sinc_ref[...][None], half)[0]
        ck = 512
        for c0 in range(0, seq, ck):
            kt = k2_ref[0, c0:c0 + ck, :].T
            k3 = kt.reshape(2, HEAD_DIM, ck)
            ms = jnp.mean(k3 * k3, axis=1, keepdims=True)
            kn = k3 * lax.rsqrt(ms + EPS) * k2g_ref[...].reshape(2, HEAD_DIM, 1)
            kr = _rope_cols(kn, cosk_ref[:, c0:c0 + ck][None], sink_ref[:, c0:c0 + ck][None], half)
            k2T_s[:, c0:c0 + ck] = kr.reshape(2 * HEAD_DIM, ck).astype(BF16)
        v2_s[...] = v2_ref[0].astype(BF16)

    qs = qi * t
    q = q_ref[0]
    qn = q * _group_rsqrt(q, HEAD_DIM, float(HEAD_DIM)) * qg_ref[...]
    qr = _rope_rows(qn, cosq_ref[...], sina_ref[...], sinb_ref[...], half) * (HEAD_DIM ** -0.5)
    q4f = jnp.concatenate([qr[:, h * HEAD_DIM:(h + 1) * HEAD_DIM] for h in range(N_HEADS)], axis=0)
    q4 = q4f.astype(BF16)
    pos_r = qs + lax.broadcasted_iota(jnp.int32, (t, 1), 0)

    sc = jnp.dot(q4f, kcT_s[...], precision=HIGHEST, preferred_element_type=F32).reshape(N_HEADS, t, n_chunk)
    n_idx = lax.broadcasted_iota(jnp.int32, (1, n_chunk), 1)
    cvalid = ((n_idx * NSA_CMP_STRIDE + (NSA_CMP_BLOCK - 1) <= pos_r) & (n_idx < n_cmp))[None]
    sc = jnp.where(cvalid, sc, NEG_INF)
    e = jnp.exp(sc - jnp.max(sc, axis=-1, keepdims=True))
    p = jnp.where(cvalid, e / jnp.sum(e, axis=-1, keepdims=True), 0.0)
    o_cmp = jnp.dot(p.reshape(N_HEADS * t, n_chunk).astype(BF16), vc_s[...],
                    preferred_element_type=F32).reshape(N_HEADS, t, 2 * HEAD_DIM)
    pg = ((p[0] + p[1]) + p[2]) + p[3]

    p_slc = lax.dot_general(cmat_ref[...], pg, nt, precision=HIGHEST, preferred_element_type=F32)
    blk = lax.broadcasted_iota(jnp.int32, (n_blk, 1), 0)
    cur = lax.shift_right_logical(qs + lax.broadcasted_iota(jnp.int32, (1, t), 1), 6)
    forced = (blk == 0) | ((blk <= cur) & (blk > cur - NSA_N_LOCAL))
    score = jnp.where(blk > cur, NEG_INF, jnp.where(forced, BIG, p_slc))
    cnt = jnp.zeros((n_blk, t), F32)
    for i in range(n_blk):
        ri = score[i:i + 1, :]
        beats = jnp.where(ri > score, 1.0, jnp.where((ri == score) & (blk > i), 1.0, 0.0))
        cnt = cnt + beats
    sel_t = jnp.where(cnt < float(NSA_N_SELECT), 1.0, 0.0).astype(BF16)
    eye = (lax.broadcasted_iota(jnp.int32, (t, t), 0) == lax.broadcasted_iota(jnp.int32, (t, t), 1))
    sel = lax.dot_general(eye.astype(BF16), sel_t, nt, preferred_element_type=F32).astype(BF16)

    def init():
        return (jnp.full((N_HEADS, t, 1), NEG_INF, F32), jnp.zeros((N_HEADS, t, 1), F32),
                jnp.zeros((N_HEADS, t, 2 * HEAD_DIM), F32))

    def slc_body(j, carry):
        off = pl.multiple_of(j * t, t)
        s = jnp.dot(q4, k2T_s[0:HEAD_DIM, pl.ds(off, t)], preferred_element_type=F32).reshape(N_HEADS, t, t)
        bmask = jnp.dot(sel, emat_ref[:, pl.ds(off, t)], preferred_element_type=F32)
        kpos = off + lax.broadcasted_iota(jnp.int32, (1, t), 1)
        valid = (bmask > 0.5) & (kpos <= pos_r)
        s = jnp.where(valid[None], s, NEG_INF)
        return _online_update(carry, s, v2_s[pl.ds(off, t), :])

    _, l_s, acc_s = lax.fori_loop(0, qi + 1, slc_body, init())

    def win_body(j, carry):
        off = pl.multiple_of(j * t, t)
        s = jnp.dot(q4, k2T_s[HEAD_DIM:2 * HEAD_DIM, pl.ds(off, t)],
                    preferred_element_type=F32).reshape(N_HEADS, t, t)
        kpos = off + lax.broadcasted_iota(jnp.int32, (1, t), 1)
        valid = (kpos <= pos_r) & (kpos > pos_r - NSA_WINDOW)
        s = jnp.where(valid[None], s, NEG_INF)
        return _online_update(carry, s, v2_s[pl.ds(off, t), :])

    _, l_w, acc_w = lax.fori_loop(jnp.maximum(qi - win_tiles, 0), qi + 1, win_body, init())

    g = jax.nn.sigmoid(gl_ref[0])
    outs = []
    for h in range(N_HEADS):
        oc = o_cmp[h][:, HEAD_DIM:]
        osl = (acc_s[h] / l_s[h])[:, :HEAD_DIM]
        ow = (acc_w[h] / l_w[h])[:, HEAD_DIM:]
        outs.append(g[:, h:h + 1] * oc + g[:, N_HEADS + h:N_HEADS + h + 1] * osl
                    + g[:, 2 * N_HEADS + h:2 * N_HEADS + h + 1] * ow)
    y = jnp.concatenate(outs, axis=-1) * _silu(z_ref[0])
    y_ref[0] = y.astype(BF16)


def _nsa_call(q, kcvc, k2, v2, gl, z, qk_gain, cmp_pe, w_cmp, tabs64, t=128):
    b, seq, _ = q.shape
    cos_f, sin_a, sin_b, cos_t, sin_t = tabs64
    n_chunk = seq // NSA_CMP_STRIDE
    n_blk = seq // NSA_SLC_BLOCK
    half_blk = NSA_CMP_BLOCK // 2
    cw = half_blk * 2 * HEAD_DIM

    wk = w_cmp[0].reshape(NSA_CMP_BLOCK, HEAD_DIM, HEAD_DIM)
    wv = w_cmp[1].reshape(NSA_CMP_BLOCK, HEAD_DIM, HEAD_DIM)
    zero = jnp.zeros_like(wk)
    w_all = jnp.concatenate([jnp.concatenate([wk, zero], axis=2),
                             jnp.concatenate([zero, wv], axis=2)], axis=1)
    w_lo = w_all[:half_blk].reshape(cw, 2 * HEAD_DIM)
    w_hi = w_all[half_blk:].reshape(cw, 2 * HEAD_DIM)
    pe_all = jnp.concatenate([cmp_pe[0], cmp_pe[1]], axis=1)
    pe_lo = pe_all[:half_blk].reshape(1, cw)
    pe_hi = pe_all[half_blk:].reshape(1, cw)

    cmp_end = jnp.arange(n_chunk, dtype=jnp.int32) * NSA_CMP_STRIDE + (NSA_CMP_BLOCK - 1)
    _, _, _, cos_c, sin_c = _rope_tables(cmp_end, HEAD_DIM, HEAD_DIM)

    ratio = NSA_SLC_BLOCK // NSA_CMP_STRIDE
    coef = np.convolve(np.ones(ratio), np.ones(NSA_CMP_BLOCK // NSA_CMP_STRIDE))
    cmat = np.zeros((n_blk, n_chunk), np.float32)
    for j in range(n_blk):
        for i, c in enumerate(coef):
            if ratio * j + i < n_chunk - 1:
                cmat[j, ratio * j + i] = c
    emat = (np.arange(seq)[None, :] // NSA_SLC_BLOCK == np.arange(n_blk)[:, None]).astype(np.float32)

    tile = pl.BlockSpec((1, t, GROUP_WIDTH), lambda bi, qi: (bi, qi, 0))
    const = lambda shape: pl.BlockSpec(shape, lambda bi, qi: tuple(0 for _ in shape))
    full128 = pl.BlockSpec((1, seq, LANES), lambda bi, qi: (bi, 0, 0))
    tab = pl.BlockSpec((t, GROUP_WIDTH), lambda bi, qi: (qi, 0))
    return pl.pallas_call(
        functools.partial(_nsa_kernel, seq, t),
        grid=(b, seq // t),
        in_specs=[tile,
                  pl.BlockSpec((1, n_chunk, cw), lambda bi, qi: (bi, 0, 0)),
                  full128, full128,
                  pl.BlockSpec((1, t, LANES), lambda bi, qi: (bi, qi, 0)),
                  tile,
                  const((1, GROUP_WIDTH)), const((HEAD_DIM, 1)), const((2 * HEAD_DIM, 1)),
                  const((1, cw)), const((1, cw)), const((cw, 2 * HEAD_DIM)), const((cw, 2 * HEAD_DIM)),
                  const((2 * HEAD_DIM, cw)), const((2 * HEAD_DIM, cw)),
                  tab, tab, tab, const((HEAD_DIM // 2, seq)), const((HEAD_DIM // 2, seq)),
                  const((HEAD_DIM // 2, n_chunk)), const((HEAD_DIM // 2, n_chunk)),
                  const((n_blk, n_chunk)), const((n_blk, seq))],
        out_specs=tile,
        out_shape=jax.ShapeDtypeStruct((b, seq, GROUP_WIDTH), BF16),
        scratch_shapes=[pltpu.VMEM((2 * HEAD_DIM, seq), BF16),
                        pltpu.VMEM((seq, 2 * HEAD_DIM), BF16),
                        pltpu.VMEM((HEAD_DIM, n_chunk), F32),
                        pltpu.VMEM((n_chunk, 2 * HEAD_DIM), BF16)],
        compiler_params=_cparams(("parallel", "arbitrary")),
        name="nsa_attn",
    )(q, kcvc.reshape(b, n_chunk, cw), k2, v2, gl, z,
      jnp.tile(qk_gain[0], N_HEADS)[None, :], qk_gain[1][:, None],
      jnp.concatenate([qk_gain[2], qk_gain[3]])[:, None],
      pe_lo, pe_hi, w_lo, w_hi, w_lo.T, w_hi.T,
      cos_f, sin_a, sin_b, cos_t, sin_t, cos_c, sin_c,
      jnp.asarray(cmat), jnp.asarray(emat, dtype=BF16))


def _rope_tables(pos, dim, width):
    half = dim // 2
    inv_freq = ROPE_THETA ** (-jnp.arange(half, dtype=F32) / half)
    ang = pos.astype(F32)[:, None] * inv_freq[None, :]
    cos, sin = jnp.cos(ang), jnp.sin(ang)
    zero = jnp.zeros_like(sin)
    reps = width // dim
    cos_f = jnp.tile(jnp.concatenate([cos, cos], axis=-1), (1, reps))
    sin_a = jnp.tile(jnp.concatenate([-sin, zero], axis=-1), (1, reps))
    sin_b = jnp.tile(jnp.concatenate([zero, sin], axis=-1), (1, reps))
    return cos_f, sin_a, sin_b, cos.T, sin.T


def _layer(x, mem, layer_idx, tabs32, tabs64, norm_gain, w_in, w_out, nsa_qk_gain, nsa_cmp_pe, nsa_w_cmp,
           diff_qk_gain, diff_lambda, diff_subln_gain, mla_cq_gain, mla_ckv_gain, mla_w_uq, mla_w_ukv,
           mla_qk_gain, mem_norm_gain, mem_w_kv, mem_qk_gain):
    b, seq, d = x.shape
    x2 = x.reshape(b * seq, d)
    w_pad = jnp.where(jnp.asarray(_IN_VALID)[None, :], jnp.take(w_in, jnp.asarray(_IN_COLS), axis=1), 0.0)
    u = _in_proj(x2, norm_gain[None, :], w_pad.astype(BF16))
    u = {name: a.reshape(b, seq, a.shape[-1]) for (name, _), a in zip(_SEGS, u)}

    y_nsa = _nsa_call(u["nsa_q"], u["nsa_kcvc"], u["nsa_k2"], u["nsa_v2"], u["nsa_gl"], u["nsa_z"],
                      nsa_qk_gain, nsa_cmp_pe, nsa_w_cmp, tabs64)
    lambda_init = 0.8 - 0.6 * math.exp(-0.3 * layer_idx)
    y_diff = _diff_call(u["diff_q"], u["diff_k"], u["diff_v"], u["diff_z"], diff_qk_gain, diff_lambda,
                        diff_subln_gain, lambda_init, tabs32)
    y_mla = _mla_call(u["mla_cq"], u["mla_ckv"], u["mla_kr"], u["mla_z"], mla_cq_gain, mla_ckv_gain,
                      mla_w_uq, mla_w_ukv, mla_qk_gain, tabs32)
    y_mem = _mem_call(u["mem_q"], mem, u["mem_z"], mem_norm_gain, mem_w_kv, mem_qk_gain)

    ys = [y.reshape(b * seq, GROUP_WIDTH) for y in (y_nsa, y_diff, y_mla, y_mem)]
    return _out_proj(x2, ys, w_out.astype(BF16)).reshape(b, seq, d)


def kernel(x, mem, norm_gain, w_in, w_out, nsa_qk_gain, nsa_cmp_pe, nsa_w_cmp, diff_qk_gain, diff_lambda,
           diff_subln_gain, mla_cq_gain, mla_ckv_gain, mla_w_uq, mla_w_ukv, mla_qk_gain, mem_norm_gain,
           mem_w_kv, mem_qk_gain):
    seq = x.shape[1]
    pos = jnp.arange(seq, dtype=jnp.int32)
    tabs32 = _rope_tables(pos, DIFF_D, GROUP_WIDTH)
    tabs64 = _rope_tables(pos, HEAD_DIM, GROUP_WIDTH)
    for l in range(DEPTH):
        x = _layer(x, mem, l, tabs32, tabs64, norm_gain[l], w_in[l], w_out[l], nsa_qk_gain[l], nsa_cmp_pe[l],
                   nsa_w_cmp[l], diff_qk_gain[l], diff_lambda[l], diff_subln_gain[l], mla_cq_gain[l],
                   mla_ckv_gain[l], mla_w_uq[l], mla_w_ukv[l], mla_qk_gain[l], mem_norm_gain[l], mem_w_kv[l],
                   mem_qk_gain[l])
    return x
```

```python
import functools
import math

import numpy as np
import jax
import jax.numpy as jnp
from jax import lax
from jax.experimental import pallas as pl
from jax.experimental.pallas import tpu as pltpu

F32 = jnp.float32
BF16 = jnp.bfloat16
HIGHEST = lax.Precision.HIGHEST

D_MODEL = 1024
DEPTH = 2
N_HEADS = 4
HEAD_DIM = 64
GROUP_WIDTH = N_HEADS * HEAD_DIM
ROPE_THETA = 10000.0
EPS = 1e-6
NEG_INF = -1e30
BIG = 1e30
LOG2E = 1.4426950408889634

NSA_CMP_BLOCK = 32
NSA_CMP_STRIDE = 16
NSA_SLC_BLOCK = 64
NSA_N_SELECT = 16
NSA_N_LOCAL = 2
NSA_WINDOW = 512
DIFF_D = HEAD_DIM // 2
MLA_Q_RANK = 256
MLA_KV_RANK = 128
MLA_NOPE = 64
MLA_ROPE = 32
MLA_QK = MLA_NOPE + MLA_ROPE

VMEM_LIMIT_BYTES = 48 * 1024 * 1024
LANES = 128
PREP_ROWS = 512

_SEGS = (
    ("nsa_q", 256), ("nsa_kcvc", 128), ("nsa_k2", 128), ("nsa_v2", 128), ("nsa_gl", 128), ("nsa_z", 256),
    ("diff_q", 256), ("diff_k", 256), ("diff_v", 256), ("diff_z", 256),
    ("mla_cq", 256), ("mla_ckv", 128), ("mla_kr", 128), ("mla_z", 256),
    ("mem_q", 256), ("mem_z", 256),
)
_SEG_W = tuple(w for _, w in _SEGS)
_D_IN_PAD = sum(_SEG_W)


def _in_col_map():
    r = lambda a, b: list(range(a, b))
    pad = lambda n: [-1] * n
    cols = []
    cols += r(0, 256)
    cols += r(256, 384)
    cols += r(384, 448) + r(512, 576)
    cols += r(448, 512) + r(576, 640)
    cols += r(640, 652) + pad(116)
    cols += r(652, 908)
    cols += r(908, 1164) + r(1164, 1420) + r(1420, 1676) + r(1676, 1932)
    cols += r(1932, 2188) + r(2188, 2316) + r(2316, 2348) + pad(96) + r(2348, 2604)
    cols += r(2604, 2860) + r(2860, 3116)
    cols = np.asarray(cols, np.int32)
    assert cols.shape[0] == _D_IN_PAD
    return np.maximum(cols, 0), (cols >= 0)


_IN_COLS, _IN_VALID = _in_col_map()


def _cparams(sem):
    return pltpu.CompilerParams(dimension_semantics=sem, vmem_limit_bytes=VMEM_LIMIT_BYTES)


def _group_rsqrt(x, group, denom):
    rows, width = x.shape
    lane = lax.broadcasted_iota(jnp.int32, (1, width), 1)
    sq = x * x
    rs = jnp.zeros_like(x)
    for g in range(width // group):
        m = (lane >= g * group) & (lane < (g + 1) * group)
        ms = jnp.sum(jnp.where(m, sq, 0.0), axis=-1, keepdims=True) / denom
        rs = jnp.where(m, lax.rsqrt(ms + EPS), rs)
    return rs


def _rope_rows(x, cos, sin_a, sin_b, half):
    width = x.shape[-1]
    return x * cos + pltpu.roll(x, width - half, 1) * sin_a + pltpu.roll(x, half, 1) * sin_b


def _silu(z):
    return z * jax.nn.sigmoid(z)


def _flash_step_t(s, v_t, m_ref, l_ref, acc_ref, i):
    m_old = m_ref[i:i + 1, :]
    m_new = jnp.maximum(m_old, jnp.max(s, axis=0, keepdims=True))
    alpha = jnp.exp2(m_old - m_new)
    p = jnp.exp2(s - m_new)
    l_ref[i:i + 1, :] = alpha * l_ref[i:i + 1, :] + jnp.sum(p, axis=0, keepdims=True)
    acc_ref[i] = alpha * acc_ref[i] + jnp.dot(v_t, p.astype(BF16), preferred_element_type=F32)
    m_ref[i:i + 1, :] = m_new


def _flash_reset(m_ref, l_ref, acc_ref):
    m_ref[...] = jnp.full(m_ref.shape, NEG_INF, F32)
    l_ref[...] = jnp.zeros(l_ref.shape, F32)
    acc_ref[...] = jnp.zeros(acc_ref.shape, F32)


def _causal_flash_t(n, k_tile, q_t, v_tile, qi, t, m_ref, l_ref, acc_ref):
    _flash_reset(m_ref, l_ref, acc_ref)

    def tile_step(off, causal):
        ss = [jnp.dot(k_tile(i, off), q_t(i), preferred_element_type=F32) for i in range(n)]
        for i in range(n):
            s = ss[i] if causal is None else jnp.where(causal, ss[i], NEG_INF)
            _flash_step_t(s, v_tile(i, off), m_ref, l_ref, acc_ref, i)

    def body(j, carry):
        tile_step(pl.multiple_of(j * t, t), None)
        return carry

    lax.fori_loop(0, qi, body, 0)
    krow = lax.broadcasted_iota(jnp.int32, (t, t), 0)
    qcol = lax.broadcasted_iota(jnp.int32, (t, t), 1)
    tile_step(pl.multiple_of(qi * t, t), krow <= qcol)


def _in_proj_kernel(x_ref, g_ref, w_ref, *out_refs):
    x = x_ref[...]
    ms = jnp.mean(x * x, axis=-1, keepdims=True)
    h = (x * lax.rsqrt(ms + EPS) * g_ref[...]).astype(BF16)
    off = 0
    for o_ref, w in zip(out_refs, _SEG_W):
        o_ref[...] = jnp.dot(h, w_ref[:, off:off + w], preferred_element_type=F32)
        off += w


def _in_proj(x2, gain, w_pad, tm=256):
    n = x2.shape[0]
    return pl.pallas_call(
        _in_proj_kernel,
        grid=(n // tm,),
        in_specs=[
            pl.BlockSpec((tm, D_MODEL), lambda i: (i, 0)),
            pl.BlockSpec((1, D_MODEL), lambda i: (0, 0)),
            pl.BlockSpec((D_MODEL, _D_IN_PAD), lambda i: (0, 0)),
        ],
        out_specs=[pl.BlockSpec((tm, w), lambda i: (i, 0)) for w in _SEG_W],
        out_shape=[jax.ShapeDtypeStruct((n, w), F32) for w in _SEG_W],
        compiler_params=_cparams(("parallel",)),
        name="in_proj",
    )(x2, gain, w_pad)


def _out_proj_kernel(x_ref, y0_ref, y1_ref, y2_ref, y3_ref, w_ref, o_ref):
    acc = x_ref[...]
    for g, y_ref in enumerate((y0_ref, y1_ref, y2_ref, y3_ref)):
        acc = acc + jnp.dot(y_ref[...], w_ref[g * GROUP_WIDTH:(g + 1) * GROUP_WIDTH, :],
                            preferred_element_type=F32)
    o_ref[...] = acc


def _out_proj(x2, ys, w_out_b, tm=512):
    n = x2.shape[0]
    yspec = pl.BlockSpec((tm, GROUP_WIDTH), lambda i: (i, 0))
    return pl.pallas_call(
        _out_proj_kernel,
        grid=(n // tm,),
        in_specs=[pl.BlockSpec((tm, D_MODEL), lambda i: (i, 0)), yspec, yspec, yspec, yspec,
                  pl.BlockSpec((D_MODEL, D_MODEL), lambda i: (0, 0))],
        out_specs=pl.BlockSpec((tm, D_MODEL), lambda i: (i, 0)),
        out_shape=jax.ShapeDtypeStruct((n, D_MODEL), F32),
        compiler_params=_cparams(("parallel",)),
        name="out_proj",
    )(x2, *ys, w_out_b)


def _diff_kernel(lambda_init, seq, t,
                 q_ref, k_ref, v_ref, z_ref, qg_ref, kg_ref, lam_ref, sg_ref,
                 cosq_ref, sina_ref, sinb_ref, cosk_ref, sinak_ref, sinbk_ref,
                 y_ref, k_s, vT_s, m_s, l_s, acc_s):
    qi = pl.program_id(1)
    n_maps = 2 * N_HEADS
    half = DIFF_D // 2

    @pl.when(qi == 0)
    def _prep():
        for c0 in range(0, seq, PREP_ROWS):
            rows = slice(c0, c0 + PREP_ROWS)
            k = k_ref[0, rows, :]
            kn = k * _group_rsqrt(k, DIFF_D, float(DIFF_D)) * kg_ref[...]
            kr = _rope_rows(kn, cosk_ref[rows, :], sinak_ref[rows, :], sinbk_ref[rows, :], half)
            for i in range(n_maps):
                k_s[i, rows, :] = kr[:, i * DIFF_D:(i + 1) * DIFF_D].astype(BF16)
            vT_s[:, rows] = v_ref[0, rows, :].T.astype(BF16)

    q = q_ref[0]
    qn = q * _group_rsqrt(q, DIFF_D, float(DIFF_D)) * qg_ref[...]
    qr = _rope_rows(qn, cosq_ref[...], sina_ref[...], sinb_ref[...], half)
    qT = (qr * (DIFF_D ** -0.5 * LOG2E)).T.astype(BF16)

    _causal_flash_t(
        n_maps,
        lambda i, off: k_s[i, pl.ds(off, t), :],
        lambda i: qT[i * DIFF_D:(i + 1) * DIFF_D],
        lambda i, off: vT_s[(i // 2) * HEAD_DIM:(i // 2 + 1) * HEAD_DIM, pl.ds(off, t)],
        qi, t, m_s, l_s, acc_s)

    lam = lam_ref[...]
    lmbda = (jnp.exp(jnp.sum(lam[0:1] * lam[1:2], axis=-1, keepdims=True))
             - jnp.exp(jnp.sum(lam[2:3] * lam[3:4], axis=-1, keepdims=True)) + lambda_init)
    outs = []
    for h in range(N_HEADS):
        o1 = acc_s[2 * h] / l_s[2 * h:2 * h + 1, :]
        o2 = acc_s[2 * h + 1] / l_s[2 * h + 1:2 * h + 2, :]
        d = o1 - lmbda * o2
        ms = jnp.mean(d * d, axis=0, keepdims=True)
        outs.append(d * lax.rsqrt(ms + EPS) * sg_ref[...] * (1.0 - lambda_init))
    y = jnp.concatenate(outs, axis=0).T * _silu(z_ref[0])
    y_ref[0] = y.astype(BF16)


def _diff_call(q, k, v, z, qk_gain, lam, subln_gain, lambda_init, tabs32, t=256):
    b, seq, _ = q.shape
    cos_f, sin_a, sin_b = tabs32[:3]
    tile = pl.BlockSpec((1, t, GROUP_WIDTH), lambda bi, qi: (bi, qi, 0))
    full = pl.BlockSpec((1, seq, GROUP_WIDTH), lambda bi, qi: (bi, 0, 0))
    const = lambda shape: pl.BlockSpec(shape, lambda bi, qi: tuple(0 for _ in shape))
    tab = pl.BlockSpec((t, GROUP_WIDTH), lambda bi, qi: (qi, 0))
    return pl.pallas_call(
        functools.partial(_diff_kernel, lambda_init, seq, t),
        grid=(b, seq // t),
        in_specs=[tile, full, full, tile,
                  const((1, GROUP_WIDTH)), const((1, GROUP_WIDTH)), const((4, DIFF_D)), const((HEAD_DIM, 1)),
                  tab, tab, tab, const((seq, GROUP_WIDTH)), const((seq, GROUP_WIDTH)), const((seq, GROUP_WIDTH))],
        out_specs=tile,
        out_shape=jax.ShapeDtypeStruct((b, seq, GROUP_WIDTH), BF16),
        scratch_shapes=[pltpu.VMEM((2 * N_HEADS, seq, DIFF_D), BF16),
                        pltpu.VMEM((GROUP_WIDTH, seq), BF16),
                        pltpu.VMEM((2 * N_HEADS, t), F32),
                        pltpu.VMEM((2 * N_HEADS, t), F32),
                        pltpu.VMEM((2 * N_HEADS, HEAD_DIM, t), F32)],
        compiler_params=_cparams(("parallel", "arbitrary")),
        name="diff_attn",
    )(q, k, v, z,
      jnp.tile(qk_gain[0], 2 * N_HEADS)[None, :], jnp.tile(qk_gain[1], 2 * N_HEADS)[None, :], lam,
      subln_gain[:, None],
      cos_f, sin_a, sin_b, cos_f, sin_a, sin_b)


def _mla_kernel(seq, t,
                cq_ref, ckv_ref, kr_ref, z_ref, cqg_ref, ckvg_ref, wuq_ref, wk_ref, wvT_ref,
                qgn_ref, qgr_ref, kgn_ref, kgr_ref,
                cosq_ref, sina_ref, sinb_ref, cosk_ref, sinak_ref, sinbk_ref,
                y_ref, k_s, vT_s, m_s, l_s, acc_s):
    qi = pl.program_id(1)
    half = MLA_ROPE // 2
    nt = (((1,), (1,)), ((), ()))

    def head_rsqrt(xn_h, xr_h):
        ms = (jnp.sum(xn_h * xn_h, axis=-1, keepdims=True)
              + jnp.sum(xr_h * xr_h, axis=-1, keepdims=True)) / float(MLA_QK)
        return lax.rsqrt(ms + EPS)

    @pl.when(qi == 0)
    def _prep():
        for c0 in range(0, seq, PREP_ROWS):
            rows = slice(c0, c0 + PREP_ROWS)
            ckv = ckv_ref[0, rows, :]
            ms = jnp.mean(ckv * ckv, axis=-1, keepdims=True)
            cb = (ckv * lax.rsqrt(ms + EPS) * ckvg_ref[...]).astype(BF16)
            kn = jnp.dot(cb, wk_ref[...], preferred_element_type=F32)
            kr = _rope_rows(kr_ref[0, rows, :], cosk_ref[rows, :], sinak_ref[rows, :], sinbk_ref[rows, :],
                            half)[:, :MLA_ROPE]
            for h in range(N_HEADS):
                kn_h = kn[:, h * MLA_NOPE:(h + 1) * MLA_NOPE]
                rs = head_rsqrt(kn_h, kr)
                k_s[h, rows, :] = jnp.concatenate(
                    [kn_h * rs * kgn_ref[...], kr * rs * kgr_ref[...]], axis=-1).astype(BF16)
            vT_s[:, rows] = lax.dot_general(wvT_ref[...], cb, nt, preferred_element_type=F32).astype(BF16)

    cq = cq_ref[0]
    ms = jnp.mean(cq * cq, axis=-1, keepdims=True)
    cqb = (cq * lax.rsqrt(ms + EPS) * cqg_ref[...]).astype(BF16)
    qa = jnp.dot(cqb, wuq_ref[...], preferred_element_type=F32)
    qn = qa[:, :N_HEADS * MLA_NOPE]
    qr = _rope_rows(qa[:, N_HEADS * MLA_NOPE:], cosq_ref[...], sina_ref[...], sinb_ref[...], half)
    scale = MLA_QK ** -0.5 * LOG2E
    pad = jnp.zeros((t, LANES - MLA_QK), F32)
    parts = []
    for h in range(N_HEADS):
        qn_h = qn[:, h * MLA_NOPE:(h + 1) * MLA_NOPE]
        qr_h = qr[:, h * MLA_ROPE:(h + 1) * MLA_ROPE]
        rs = head_rsqrt(qn_h, qr_h) * scale
        parts += [qn_h * rs * qgn_ref[...], qr_h * rs * qgr_ref[...], pad]
    qT = jnp.concatenate(parts, axis=-1).T.astype(BF16)

    _causal_flash_t(
        N_HEADS,
        lambda h, off: k_s[h, pl.ds(off, t), :],
        lambda h: qT[h * LANES:h * LANES + MLA_QK],
        lambda h, off: vT_s[h * HEAD_DIM:(h + 1) * HEAD_DIM, pl.ds(off, t)],
        qi, t, m_s, l_s, acc_s)

    outs = [acc_s[h] / l_s[h:h + 1, :] for h in range(N_HEADS)]
    y = jnp.concatenate(outs, axis=0).T * _silu(z_ref[0])
    y_ref[0] = y.astype(BF16)


def _mla_call(cq, ckv, kr, z, cq_gain, ckv_gain, w_uq, w_ukv, qk_gain, tabs32, t=256):
    b, seq, _ = cq.shape
    cos_f, sin_a, sin_b = (a[:, :N_HEADS * MLA_ROPE] for a in tabs32[:3])
    uq = w_uq.reshape(MLA_Q_RANK, N_HEADS, MLA_QK)
    w_uq_r = jnp.concatenate([uq[:, :, :MLA_NOPE].reshape(MLA_Q_RANK, -1),
                              uq[:, :, MLA_NOPE:].reshape(MLA_Q_RANK, -1)], axis=1).astype(BF16)
    ukv = w_ukv.reshape(MLA_KV_RANK, N_HEADS, MLA_NOPE + HEAD_DIM)
    w_k = ukv[:, :, :MLA_NOPE].reshape(MLA_KV_RANK, -1).astype(BF16)
    w_vT = ukv[:, :, MLA_NOPE:].reshape(MLA_KV_RANK, -1).T.astype(BF16)
    tile = pl.BlockSpec((1, t, GROUP_WIDTH), lambda bi, qi: (bi, qi, 0))
    const = lambda shape: pl.BlockSpec(shape, lambda bi, qi: tuple(0 for _ in shape))
    tab = pl.BlockSpec((t, N_HEADS * MLA_ROPE), lambda bi, qi: (qi, 0))
    ktab = const((seq, N_HEADS * MLA_ROPE))
    return pl.pallas_call(
        functools.partial(_mla_kernel, seq, t),
        grid=(b, seq // t),
        in_specs=[tile,
                  pl.BlockSpec((1, seq, MLA_KV_RANK), lambda bi, qi: (bi, 0, 0)),
                  pl.BlockSpec((1, seq, LANES), lambda bi, qi: (bi, 0, 0)),
                  tile,
                  const((1, MLA_Q_RANK)), const((1, MLA_KV_RANK)),
                  const((MLA_Q_RANK, N_HEADS * MLA_QK)), const((MLA_KV_RANK, N_HEADS * MLA_NOPE)),
                  const((GROUP_WIDTH, MLA_KV_RANK)),
                  const((1, MLA_NOPE)), const((1, MLA_ROPE)), const((1, MLA_NOPE)), const((1, MLA_ROPE)),
                  tab, tab, tab, ktab, ktab, ktab],
        out_specs=tile,
        out_shape=jax.ShapeDtypeStruct((b, seq, GROUP_WIDTH), BF16),
        scratch_shapes=[pltpu.VMEM((N_HEADS, seq, MLA_QK), BF16),
                        pltpu.VMEM((GROUP_WIDTH, seq), BF16),
                        pltpu.VMEM((N_HEADS, t), F32),
                        pltpu.VMEM((N_HEADS, t), F32),
                        pltpu.VMEM((N_HEADS, HEAD_DIM, t), F32)],
        compiler_params=_cparams(("parallel", "arbitrary")),
        name="mla_attn",
    )(cq, ckv, kr, z, cq_gain[None, :], ckv_gain[None, :], w_uq_r, w_k, w_vT,
      qk_gain[0, :MLA_NOPE][None, :], qk_gain[0, MLA_NOPE:][None, :],
      qk_gain[1, :MLA_NOPE][None, :], qk_gain[1, MLA_NOPE:][None, :],
      cos_f, sin_a, sin_b, cos_f, sin_a, sin_b)


def _mem_kernel(q_ref, mem_ref, z_ref, mg_ref, wkT_ref, wv_ref, qg_ref, kg_ref, y_ref, kT_s, vh_s):
    qi = pl.program_id(1)
    nt = (((1,), (1,)), ((), ()))
    m_len = mem_ref.shape[1]

    @pl.when(qi == 0)
    def _prep():
        mem = mem_ref[0]
        ms = jnp.mean(mem * mem, axis=-1, keepdims=True)
        mb = (mem * lax.rsqrt(ms + EPS) * mg_ref[...]).astype(BF16)
        kT = lax.dot_general(wkT_ref[...], mb, nt, preferred_element_type=F32)
        k3 = kT.reshape(N_HEADS, HEAD_DIM, m_len)
        ms = jnp.mean(k3 * k3, axis=1, keepdims=True)
        kn = k3 * lax.rsqrt(ms + EPS) * kg_ref[...].reshape(1, HEAD_DIM, 1)
        kT_s[...] = kn.reshape(GROUP_WIDTH, m_len).astype(BF16)
        v = jnp.dot(mb, wv_ref[...], preferred_element_type=F32)
        for h in range(N_HEADS):
            vh_s[h] = v[:, h * HEAD_DIM:(h + 1) * HEAD_DIM].astype(BF16)

    q = q_ref[0]
    qn = q * _group_rsqrt(q, HEAD_DIM, float(HEAD_DIM)) * qg_ref[...]
    qf = qn * (HEAD_DIM ** -0.5)
    outs = []
    for h in range(N_HEADS):
        s = jnp.dot(qf[:, h * HEAD_DIM:(h + 1) * HEAD_DIM].astype(BF16), kT_s[h * HEAD_DIM:(h + 1) * HEAD_DIM, :],
                    preferred_element_type=F32)
        m = jnp.max(s, axis=-1, keepdims=True)
        e = jnp.exp(s - m)
        l = jnp.sum(e, axis=-1, keepdims=True)
        outs.append(jnp.dot(e.astype(BF16), vh_s[h], preferred_element_type=F32) / l)
    y = jnp.concatenate(outs, axis=-1) * _silu(z_ref[0])
    y_ref[0] = y.astype(BF16)


def _mem_call(q, mem, z, mem_gain, w_kv, qk_gain, t=256):
    b, seq, _ = q.shape
    m_len = mem.shape[1]
    w_kT = w_kv[:, :GROUP_WIDTH].T.astype(BF16)
    w_v = w_kv[:, GROUP_WIDTH:].astype(BF16)
    tile = pl.BlockSpec((1, t, GROUP_WIDTH), lambda bi, qi: (bi, qi, 0))
    const = lambda shape: pl.BlockSpec(shape, lambda bi, qi: tuple(0 for _ in shape))
    return pl.pallas_call(
        _mem_kernel,
        grid=(b, seq // t),
        in_specs=[tile, pl.BlockSpec((1, m_len, D_MODEL), lambda bi, qi: (bi, 0, 0)), tile,
                  const((1, D_MODEL)), const((GROUP_WIDTH, D_MODEL)), const((D_MODEL, GROUP_WIDTH)),
                  const((1, GROUP_WIDTH)), const((HEAD_DIM, 1))],
        out_specs=tile,
        out_shape=jax.ShapeDtypeStruct((b, seq, GROUP_WIDTH), BF16),
        scratch_shapes=[pltpu.VMEM((GROUP_WIDTH, m_len), BF16),
                        pltpu.VMEM((N_HEADS, m_len, HEAD_DIM), BF16)],
        compiler_params=_cparams(("parallel", "arbitrary")),
        name="mem_attn",
    )(q, mem, z, mem_gain[None, :], w_kT, w_v, jnp.tile(qk_gain[0], N_HEADS)[None, :], qk_gain[1][:, None])


def _nsa_kernel(seq, t,
                q_ref, kcvc_ref, k2_ref, v2_ref, gl_ref, z_ref,
                qg_ref, cg_ref, k2g_ref, pelo_ref, pehi_ref, wlo_ref, whi_ref, wloT_ref, whiT_ref,
                cosq_ref, sina_ref, sinb_ref, cosk_ref, sinak_ref, sinbk_ref, cosc_ref, sinc_ref,
                cmat_ref,
                y_ref, k2_s, v2T_s, kc_s, vcT_s, bias_s, m_s, l_s, acc_s):
    qi = pl.program_id(1)
    half = HEAD_DIM // 2
    nt = (((1,), (1,)), ((), ()))
    n_chunk = seq // NSA_CMP_STRIDE
    n_cmp = n_chunk - 1
    n_blk = seq // NSA_SLC_BLOCK
    blk_per_tile = t // NSA_SLC_BLOCK
    win_tiles = NSA_WINDOW // t
    w4 = N_HEADS * t
    SLC, WIN = 0, 1

    @pl.when(qi == 0)
    def _prep():
        ch = kcvc_ref[0]
        chl = ch + pelo_ref[...]
        chh = ch + pehi_ref[...]
        a = jnp.dot(chl, wlo_ref[...], precision=HIGHEST, preferred_element_type=F32)
        bm = jnp.dot(chh, whi_ref[...], precision=HIGHEST, preferred_element_type=F32)
        kc = (a + pltpu.roll(bm, n_chunk - 1, 0))[:, :HEAD_DIM]
        ms = jnp.mean(kc * kc, axis=-1, keepdims=True)
        kc = kc * lax.rsqrt(ms + EPS) * cg_ref[...]
        x1, x2 = kc[:, :half], kc[:, half:]
        c, s = cosc_ref[...], sinc_ref[...]
        kc_s[...] = jnp.concatenate([x1 * c - x2 * s, x2 * c + x1 * s], axis=-1)
        at = lax.dot_general(wloT_ref[...], chl, nt, precision=HIGHEST, preferred_element_type=F32)
        bt = lax.dot_general(whiT_ref[...], chh, nt, precision=HIGHEST, preferred_element_type=F32)
        vcT_s[...] = (at + pltpu.roll(bt, n_chunk - 1, 1))[HEAD_DIM:].astype(BF16)
        for c0 in range(0, seq, PREP_ROWS):
            rows = slice(c0, c0 + PREP_ROWS)
            k2 = k2_ref[0, rows, :]
            kn = k2 * _group_rsqrt(k2, HEAD_DIM, float(HEAD_DIM)) * k2g_ref[...]
            kr = _rope_rows(kn, cosk_ref[rows, :], sinak_ref[rows, :], sinbk_ref[rows, :], half)
            k2_s[0, rows, :] = kr[:, :HEAD_DIM].astype(BF16)
            k2_s[1, rows, :] = kr[:, HEAD_DIM:].astype(BF16)
            v2T_s[:, rows] = v2_ref[0, rows, :].T.astype(BF16)

    qs = qi * t
    q = q_ref[0]
    qn = q * _group_rsqrt(q, HEAD_DIM, float(HEAD_DIM)) * qg_ref[...]
    qr = _rope_rows(qn, cosq_ref[...], sina_ref[...], sinb_ref[...], half) * (HEAD_DIM ** -0.5 * LOG2E)
    qT = qr.T
    q4f = jnp.concatenate([qT[h * HEAD_DIM:(h + 1) * HEAD_DIM] for h in range(N_HEADS)], axis=1)
    q4 = q4f.astype(BF16)
    qcol = lax.broadcasted_iota(jnp.int32, (1, w4), 1) & (t - 1)
    pos_c = qs + qcol

    sc = jnp.dot(kc_s[...], q4f, precision=HIGHEST, preferred_element_type=F32)
    n_idx = lax.broadcasted_iota(jnp.int32, (n_chunk, 1), 0)
    cvalid = (n_idx * NSA_CMP_STRIDE + (NSA_CMP_BLOCK - 1) <= pos_c) & (n_idx < n_cmp)
    sc = jnp.where(cvalid, sc, NEG_INF)
    e = jnp.exp2(sc - jnp.max(sc, axis=0, keepdims=True))
    p = jnp.where(cvalid, e / jnp.sum(e, axis=0, keepdims=True), 0.0)
    o_cmp = jnp.dot(vcT_s[...], p.astype(BF16), preferred_element_type=F32)
    pg = ((p[:, 0:t] + p[:, t:2 * t]) + p[:, 2 * t:3 * t]) + p[:, 3 * t:4 * t]

    p_slc = jnp.dot(cmat_ref[...], pg, precision=HIGHEST, preferred_element_type=F32)
    blk = lax.broadcasted_iota(jnp.int32, (n_blk, 1), 0)
    cur = lax.shift_right_logical(pos_c[:, :t], NSA_SLC_BLOCK.bit_length() - 1)
    forced = (blk == 0) | ((blk <= cur) & (blk > cur - NSA_N_LOCAL))
    score = jnp.where(blk > cur, NEG_INF, jnp.where(forced, BIG, p_slc))
    cnt = jnp.zeros((n_blk, t), F32)
    for i in range(n_blk):
        ri = score[i:i + 1, :]
        cnt = cnt + jnp.where(ri > score, 1.0, jnp.where((ri == score) & (blk > i), 1.0, 0.0))
    bias = jnp.where(cnt < float(NSA_N_SELECT), 0.0, NEG_INF)
    bias = jnp.concatenate([bias] * N_HEADS, axis=1)
    for r in range(n_blk // blk_per_tile):
        bias_s[r] = bias[r * blk_per_tile:(r + 1) * blk_per_tile, :]

    _flash_reset(m_s, l_s, acc_s)
    krow = lax.broadcasted_iota(jnp.int32, (t, 1), 0)
    causal = krow <= qcol
    beyond = krow > qcol

    def slc_scores(j, off):
        s = jnp.dot(k2_s[0, pl.ds(off, t), :], q4, preferred_element_type=F32)
        s = s.reshape(blk_per_tile, NSA_SLC_BLOCK, w4) + bias_s[j][:, None, :]
        return s.reshape(t, w4)

    def win_scores(off):
        return jnp.dot(k2_s[1, pl.ds(off, t), :], q4, preferred_element_type=F32)

    def step(j, with_win, mask):
        off = pl.multiple_of(j * t, t)
        s_slc = slc_scores(j, off)
        s_win = win_scores(off) if with_win else None
        if mask is not None:
            s_win = jnp.where(mask, s_win, NEG_INF)
            if mask is causal:
                s_slc = jnp.where(mask, s_slc, NEG_INF)
        _flash_step_t(s_slc, v2T_s[0:HEAD_DIM, pl.ds(off, t)], m_s, l_s, acc_s, SLC)
        if with_win:
            _flash_step_t(s_win, v2T_s[HEAD_DIM:2 * HEAD_DIM, pl.ds(off, t)], m_s, l_s, acc_s, WIN)

    lax.fori_loop(0, jnp.maximum(qi - win_tiles, 0), lambda j, c: (step(j, False, None), c)[1], 0)

    @pl.when(qi >= win_tiles)
    def _oldest_window_tile():
        step(qi - win_tiles, True, beyond)

    lax.fori_loop(jnp.maximum(qi - win_tiles + 1, 0), qi, lambda j, c: (step(j, True, None), c)[1], 0)
    step(qi, True, causal)

    g = jax.nn.sigmoid(gl_ref[0]).T
    outs = []
    for h in range(N_HEADS):
        cols = slice(h * t, (h + 1) * t)
        o_s = acc_s[SLC, :, cols] / l_s[SLC:SLC + 1, cols]
        o_w = acc_s[WIN, :, cols] / l_s[WIN:WIN + 1, cols]
        outs.append(g[h:h + 1, :] * o_cmp[:, cols] + g[N_HEADS + h:N_HEADS + h + 1, :] * o_s
                    + g[2 * N_HEADS + h:2 * N_HEADS + h + 1, :] * o_w)
    y = jnp.concatenate(outs, axis=0).T * _silu(z_ref[0])
    y_ref[0] = y.astype(BF16)


def _nsa_call(q, kcvc, k2, v2, gl, z, qk_gain, cmp_pe, w_cmp, tabs64, t=256):
    b, seq, _ = q.shape
    cos_f, sin_a, sin_b = tabs64[:3]
    n_chunk = seq // NSA_CMP_STRIDE
    n_blk = seq // NSA_SLC_BLOCK
    half_blk = NSA_CMP_BLOCK // 2
    cw = half_blk * 2 * HEAD_DIM
    assert t % NSA_SLC_BLOCK == 0 and NSA_WINDOW % t == 0 and t & (t - 1) == 0

    wk = w_cmp[0].reshape(NSA_CMP_BLOCK, HEAD_DIM, HEAD_DIM)
    wv = w_cmp[1].reshape(NSA_CMP_BLOCK, HEAD_DIM, HEAD_DIM)
    zero = jnp.zeros_like(wk)
    w_all = jnp.concatenate([jnp.concatenate([wk, zero], axis=2),
                             jnp.concatenate([zero, wv], axis=2)], axis=1)
    w_lo = w_all[:half_blk].reshape(cw, 2 * HEAD_DIM)
    w_hi = w_all[half_blk:].reshape(cw, 2 * HEAD_DIM)
    pe_all = jnp.concatenate([cmp_pe[0], cmp_pe[1]], axis=1)
    pe_lo = pe_all[:half_blk].reshape(1, cw)
    pe_hi = pe_all[half_blk:].reshape(1, cw)

    cmp_end = jnp.arange(n_chunk, dtype=jnp.int32) * NSA_CMP_STRIDE + (NSA_CMP_BLOCK - 1)
    cos_c, sin_c = _rope_tables(cmp_end, HEAD_DIM, HEAD_DIM)[3:]

    ratio = NSA_SLC_BLOCK // NSA_CMP_STRIDE
    coef = np.convolve(np.ones(ratio), np.ones(NSA_CMP_BLOCK // NSA_CMP_STRIDE))
    cmat = np.zeros((n_blk, n_chunk), np.float32)
    for j in range(n_blk):
        for i, c in enumerate(coef):
            if ratio * j + i < n_chunk - 1:
                cmat[j, ratio * j + i] = c

    tile = pl.BlockSpec((1, t, GROUP_WIDTH), lambda bi, qi: (bi, qi, 0))
    const = lambda shape: pl.BlockSpec(shape, lambda bi, qi: tuple(0 for _ in shape))
    full128 = pl.BlockSpec((1, seq, LANES), lambda bi, qi: (bi, 0, 0))
    tab = pl.BlockSpec((t, GROUP_WIDTH), lambda bi, qi: (qi, 0))
    ktab = const((seq, LANES))
    w4 = N_HEADS * t
    return pl.pallas_call(
        functools.partial(_nsa_kernel, seq, t),
        grid=(b, seq // t),
        in_specs=[tile,
                  pl.BlockSpec((1, n_chunk, cw), lambda bi, qi: (bi, 0, 0)),
                  full128, full128,
                  pl.BlockSpec((1, t, LANES), lambda bi, qi: (bi, qi, 0)),
                  tile,
                  const((1, GROUP_WIDTH)), const((1, HEAD_DIM)), const((1, 2 * HEAD_DIM)),
                  const((1, cw)), const((1, cw)), const((cw, 2 * HEAD_DIM)), const((cw, 2 * HEAD_DIM)),
                  const((2 * HEAD_DIM, cw)), const((2 * HEAD_DIM, cw)),
                  tab, tab, tab, ktab, ktab, ktab,
                  const((n_chunk, HEAD_DIM // 2)), const((n_chunk, HEAD_DIM // 2)),
                  const((n_blk, n_chunk))],
        out_specs=tile,
        out_shape=jax.ShapeDtypeStruct((b, seq, GROUP_WIDTH), BF16),
        scratch_shapes=[pltpu.VMEM((2, seq, HEAD_DIM), BF16),
                        pltpu.VMEM((2 * HEAD_DIM, seq), BF16),
                        pltpu.VMEM((n_chunk, HEAD_DIM), F32),
                        pltpu.VMEM((HEAD_DIM, n_chunk), BF16),
                        pltpu.VMEM((seq // t, t // NSA_SLC_BLOCK, w4), F32),
                        pltpu.VMEM((2, w4), F32), pltpu.VMEM((2, w4), F32),
                        pltpu.VMEM((2, HEAD_DIM, w4), F32)],
        compiler_params=_cparams(("parallel", "arbitrary")),
        name="nsa_attn",
    )(q, kcvc.reshape(b, n_chunk, cw), k2, v2, gl, z,
      jnp.tile(qk_gain[0], N_HEADS)[None, :], qk_gain[1][None, :],
      jnp.concatenate([qk_gain[2], qk_gain[3]])[None, :],
      pe_lo, pe_hi, w_lo, w_hi, w_lo.T, w_hi.T,
      cos_f, sin_a, sin_b, cos_f[:, :LANES], sin_a[:, :LANES], sin_b[:, :LANES], cos_c, sin_c,
      jnp.asarray(cmat))


def _rope_tables(pos, dim, width):
    half = dim // 2
    inv_freq = ROPE_THETA ** (-jnp.arange(half, dtype=F32) / half)
    ang = pos.astype(F32)[:, None] * inv_freq[None, :]
    cos, sin = jnp.cos(ang), jnp.sin(ang)
    zero = jnp.zeros_like(sin)
    reps = width // dim
    cos_f = jnp.tile(jnp.concatenate([cos, cos], axis=-1), (1, reps))
    sin_a = jnp.tile(jnp.concatenate([-sin, zero], axis=-1), (1, reps))
    sin_b = jnp.tile(jnp.concatenate([zero, sin], axis=-1), (1, reps))
    return cos_f, sin_a, sin_b, cos, sin


def _layer(x, mem, layer_idx, tabs32, tabs64, norm_gain, w_in, w_out, nsa_qk_gain, nsa_cmp_pe, nsa_w_cmp,
           diff_qk_gain, diff_lambda, diff_subln_gain, mla_cq_gain, mla_ckv_gain, mla_w_uq, mla_w_ukv,
           mla_qk_gain, mem_norm_gain, mem_w_kv, mem_qk_gain):
    b, seq, d = x.shape
    x2 = x.reshape(b * seq, d)
    w_pad = jnp.where(jnp.asarray(_IN_VALID)[None, :], jnp.take(w_in, jnp.asarray(_IN_COLS), axis=1), 0.0)
    u = _in_proj(x2, norm_gain[None, :], w_pad.astype(BF16))
    u = {name: a.reshape(b, seq, a.shape[-1]) for (name, _), a in zip(_SEGS, u)}

    y_nsa = _nsa_call(u["nsa_q"], u["nsa_kcvc"], u["nsa_k2"], u["nsa_v2"], u["nsa_gl"], u["nsa_z"],
                      nsa_qk_gain, nsa_cmp_pe, nsa_w_cmp, tabs64)
    lambda_init = 0.8 - 0.6 * math.exp(-0.3 * layer_idx)
    y_diff = _diff_call(u["diff_q"], u["diff_k"], u["diff_v"], u["diff_z"], diff_qk_gain, diff_lambda,
                        diff_subln_gain, lambda_init, tabs32)
    y_mla = _mla_call(u["mla_cq"], u["mla_ckv"], u["mla_kr"], u["mla_z"], mla_cq_gain, mla_ckv_gain,
                      mla_w_uq, mla_w_ukv, mla_qk_gain, tabs32)
    y_mem = _mem_call(u["mem_q"], mem, u["mem_z"], mem_norm_gain, mem_w_kv, mem_qk_gain)

    ys = [y.reshape(b * seq, GROUP_WIDTH) for y in (y_nsa, y_diff, y_mla, y_mem)]
    return _out_proj(x2, ys, w_out.astype(BF16)).reshape(b, seq, d)


def kernel(x, mem, norm_gain, w_in, w_out, nsa_qk_gain, nsa_cmp_pe, nsa_w_cmp, diff_qk_gain, diff_lambda,
           diff_subln_gain, mla_cq_gain, mla_ckv_gain, mla_w_uq, mla_w_ukv, mla_qk_gain, mem_norm_gain,
           mem_w_kv, mem_qk_gain):
    seq = x.shape[1]
    pos = jnp.arange(seq, dtype=jnp.int32)
    tabs32 = _rope_tables(pos, DIFF_D, GROUP_WIDTH)
    tabs64 = _rope_tables(pos, HEAD_DIM, GROUP_WIDTH)
    for l in range(DEPTH):
        x = _layer(x, mem, l, tabs32, tabs64, norm_gain[l], w_in[l], w_out[l], nsa_qk_gain[l], nsa_cmp_pe[l],
                   nsa_w_cmp[l], diff_qk_gain[l], diff_lambda[l], diff_subln_gain[l], mla_cq_gain[l],
                   mla_ckv_gain[l], mla_w_uq[l], mla_w_ukv[l], mla_qk_gain[l], mem_norm_gain[l], mem_w_kv[l],
                   mem_qk_gain[l])
    return x
```

```python
import functools
import math

import numpy as np
import jax
import jax.numpy as jnp
from jax import lax
from jax.experimental import pallas as pl
from jax.experimental.pallas import tpu as pltpu

F32 = jnp.float32
BF16 = jnp.bfloat16
HIGHEST = lax.Precision.HIGHEST

D_MODEL = 1024
DEPTH = 2
N_HEADS = 4
HEAD_DIM = 64
GROUP_WIDTH = N_HEADS * HEAD_DIM
ROPE_THETA = 10000.0
EPS = 1e-6
NEG_INF = -1e30
BIG = 1e30
LOG2E = 1.4426950408889634

NSA_CMP_BLOCK = 32
NSA_CMP_STRIDE = 16
NSA_SLC_BLOCK = 64
NSA_N_SELECT = 16
NSA_N_LOCAL = 2
NSA_WINDOW = 512
DIFF_D = HEAD_DIM // 2
MLA_Q_RANK = 256
MLA_KV_RANK = 128
MLA_NOPE = 64
MLA_ROPE = 32
MLA_QK = MLA_NOPE + MLA_ROPE

VMEM_LIMIT_BYTES = 48 * 1024 * 1024
LANES = 128
PREP_ROWS = 512
KEY_TILE = 256
VT_ROWS = 80

_SEGS = (
    ("nsa_q", 256), ("nsa_kcvc", 128), ("nsa_k2", 128), ("nsa_v2", 128), ("nsa_gl", 128), ("nsa_z", 256),
    ("diff_q", 256), ("diff_k", 256), ("diff_v", 256), ("diff_z", 256),
    ("mla_cq", 256), ("mla_ckv", 128), ("mla_kr", 128), ("mla_z", 256),
    ("mem_q", 256), ("mem_z", 256),
)
_SEG_W = tuple(w for _, w in _SEGS)
_D_IN_PAD = sum(_SEG_W)


def _in_col_map():
    r = lambda a, b: list(range(a, b))
    pad = lambda n: [-1] * n
    cols = []
    cols += r(0, 256)
    cols += r(256, 384)
    cols += r(384, 448) + r(512, 576)
    cols += r(448, 512) + r(576, 640)
    cols += r(640, 652) + pad(116)
    cols += r(652, 908)
    cols += r(908, 1164) + r(1164, 1420) + r(1420, 1676) + r(1676, 1932)
    cols += r(1932, 2188) + r(2188, 2316) + pad(64) + r(2316, 2348) + pad(32) + r(2348, 2604)
    cols += r(2604, 2860) + r(2860, 3116)
    cols = np.asarray(cols, np.int32)
    assert cols.shape[0] == _D_IN_PAD
    return np.maximum(cols, 0), (cols >= 0)


_IN_COLS, _IN_VALID = _in_col_map()


def _cparams(sem):
    return pltpu.CompilerParams(dimension_semantics=sem, vmem_limit_bytes=VMEM_LIMIT_BYTES)


def _group_ones(width, group):
    g = np.arange(width) // group
    return jnp.asarray(g[:, None] == g[None, :], dtype=BF16)


def _group_rsqrt(x, gmat, denom):
    sq = x * x
    hi = sq.astype(BF16)
    lo = (sq - hi.astype(F32)).astype(BF16)
    ss = jnp.dot(hi, gmat, preferred_element_type=F32) + jnp.dot(lo, gmat, preferred_element_type=F32)
    return lax.rsqrt(ss / denom + EPS)


def _rope_rows(x, cos, sin_a, sin_b, half):
    width = x.shape[-1]
    return x * cos + pltpu.roll(x, width - half, 1) * sin_a + pltpu.roll(x, half, 1) * sin_b


def _silu(z):
    return z * jax.nn.sigmoid(z)


def _flash_step_t(s, v_t, m_ref, acc_ref, i):
    m_old = m_ref[i:i + 1, :]
    m_new = jnp.maximum(m_old, jnp.max(s, axis=0, keepdims=True))
    alpha = jnp.exp2(m_old - m_new)
    p = jnp.exp2(s - m_new).astype(BF16)
    acc_ref[i] = alpha * acc_ref[i] + jnp.dot(v_t, p, preferred_element_type=F32)
    m_ref[i:i + 1, :] = m_new


def _flash_reset(m_ref, acc_ref):
    m_ref[...] = jnp.full(m_ref.shape, NEG_INF, F32)
    acc_ref[...] = jnp.zeros(acc_ref.shape, F32)


def _flash_out(acc_ref, i, cols=slice(None)):
    return acc_ref[i, 0:HEAD_DIM, cols] / acc_ref[i, HEAD_DIM:HEAD_DIM + 1, cols]


def _store_vt(vt_ref, i, cols, v_t):
    vt_ref[i, 0:HEAD_DIM, cols] = v_t.astype(BF16)
    r = lax.broadcasted_iota(jnp.int32, (VT_ROWS - HEAD_DIM, v_t.shape[1]), 0)
    vt_ref[i, HEAD_DIM:VT_ROWS, cols] = jnp.where(r == 0, 1.0, 0.0).astype(BF16)


def _causal_flash_t(n, k_tile, q_t, v_tile, qi, tq, tk, m_ref, acc_ref, s_ref):
    _flash_reset(m_ref, acc_ref)
    n_diag = tq // tk

    def scores(i, off):
        return jnp.dot(k_tile(i, off), q_t(i), preferred_element_type=F32)

    def step(off, off_next, mask):
        for i in range(n):
            s = s_ref[i]
            if off_next is not None:
                s_ref[i] = scores(i, off_next)
            if mask is not None:
                s = jnp.where(mask, s, NEG_INF)
            _flash_step_t(s, v_tile(i, off), m_ref, acc_ref, i)

    for i in range(n):
        s_ref[i] = scores(i, 0)

    def body(j, carry):
        step(pl.multiple_of(j * tk, tk), pl.multiple_of(j * tk + tk, tk), None)
        return carry

    lax.fori_loop(0, qi * n_diag, body, 0)
    krow = lax.broadcasted_iota(jnp.int32, (tk, tq), 0)
    qcol = lax.broadcasted_iota(jnp.int32, (tk, tq), 1)
    for d in range(n_diag):
        off = pl.multiple_of(qi * tq + d * tk, tk)
        off_next = pl.multiple_of(qi * tq + (d + 1) * tk, tk) if d + 1 < n_diag else None
        step(off, off_next, krow + d * tk <= qcol)


def _in_proj_kernel(x_ref, g_ref, w_ref, *out_refs):
    x = x_ref[...]
    ms = jnp.mean(x * x, axis=-1, keepdims=True)
    h = (x * lax.rsqrt(ms + EPS) * g_ref[...]).astype(BF16)
    off = 0
    for o_ref, w in zip(out_refs, _SEG_W):
        o_ref[...] = jnp.dot(h, w_ref[:, off:off + w], preferred_element_type=F32)
        off += w


def _in_proj(x2, gain, w_pad, tm=512):
    n = x2.shape[0]
    return pl.pallas_call(
        _in_proj_kernel,
        grid=(n // tm,),
        in_specs=[
            pl.BlockSpec((tm, D_MODEL), lambda i: (i, 0)),
            pl.BlockSpec((1, D_MODEL), lambda i: (0, 0)),
            pl.BlockSpec((D_MODEL, _D_IN_PAD), lambda i: (0, 0)),
        ],
        out_specs=[pl.BlockSpec((tm, w), lambda i: (i, 0)) for w in _SEG_W],
        out_shape=[jax.ShapeDtypeStruct((n, w), F32) for w in _SEG_W],
        compiler_params=_cparams(("parallel",)),
        name="in_proj",
    )(x2, gain, w_pad)


def _out_proj_kernel(x_ref, y0_ref, y1_ref, y2_ref, y3_ref, w_ref, o_ref):
    acc = x_ref[...]
    for g, y_ref in enumerate((y0_ref, y1_ref, y2_ref, y3_ref)):
        acc = acc + jnp.dot(y_ref[...], w_ref[g * GROUP_WIDTH:(g + 1) * GROUP_WIDTH, :],
                            preferred_element_type=F32)
    o_ref[...] = acc


def _out_proj(x2, ys, w_out_b, tm=512):
    n = x2.shape[0]
    yspec = pl.BlockSpec((tm, GROUP_WIDTH), lambda i: (i, 0))
    return pl.pallas_call(
        _out_proj_kernel,
        grid=(n // tm,),
        in_specs=[pl.BlockSpec((tm, D_MODEL), lambda i: (i, 0)), yspec, yspec, yspec, yspec,
                  pl.BlockSpec((D_MODEL, D_MODEL), lambda i: (0, 0))],
        out_specs=pl.BlockSpec((tm, D_MODEL), lambda i: (i, 0)),
        out_shape=jax.ShapeDtypeStruct((n, D_MODEL), F32),
        compiler_params=_cparams(("parallel",)),
        name="out_proj",
    )(x2, *ys, w_out_b)


def _diff_kernel(lambda_init, seq, t,
                 q_ref, k_ref, v_ref, z_ref, qg_ref, kg_ref, lam_ref, sg_ref, gm_ref,
                 cosq_ref, sina_ref, sinb_ref, cosk_ref, sinak_ref, sinbk_ref,
                 y_ref, k_s, vT_s, m_s, acc_s, s_s):
    qi = pl.program_id(1)
    n_maps = 2 * N_HEADS
    half = DIFF_D // 2

    @pl.when(qi == 0)
    def _prep():
        for c0 in range(0, seq, PREP_ROWS):
            rows = slice(c0, c0 + PREP_ROWS)
            k = k_ref[0, rows, :]
            kn = k * _group_rsqrt(k, gm_ref[...], float(DIFF_D)) * kg_ref[...]
            kr = _rope_rows(kn, cosk_ref[rows, :], sinak_ref[rows, :], sinbk_ref[rows, :], half)
            for i in range(n_maps):
                k_s[i, rows, :] = kr[:, i * DIFF_D:(i + 1) * DIFF_D].astype(BF16)
            vT = v_ref[0, rows, :].T
            for h in range(N_HEADS):
                _store_vt(vT_s, h, rows, vT[h * HEAD_DIM:(h + 1) * HEAD_DIM])

    q = q_ref[0]
    qn = q * _group_rsqrt(q, gm_ref[...], float(DIFF_D)) * qg_ref[...]
    qr = _rope_rows(qn, cosq_ref[...], sina_ref[...], sinb_ref[...], half)
    qT = (qr * (DIFF_D ** -0.5 * LOG2E)).T.astype(BF16)

    _causal_flash_t(
        n_maps,
        lambda i, off: k_s[i, pl.ds(off, KEY_TILE), :],
        lambda i: qT[i * DIFF_D:(i + 1) * DIFF_D],
        lambda i, off: vT_s[i // 2, :, pl.ds(off, KEY_TILE)],
        qi, t, KEY_TILE, m_s, acc_s, s_s)

    lam = lam_ref[...]
    lmbda = (jnp.exp(jnp.sum(lam[0:1] * lam[1:2], axis=-1, keepdims=True))
             - jnp.exp(jnp.sum(lam[2:3] * lam[3:4], axis=-1, keepdims=True)) + lambda_init)
    outs = []
    for h in range(N_HEADS):
        d = _flash_out(acc_s, 2 * h) - lmbda * _flash_out(acc_s, 2 * h + 1)
        ms = jnp.mean(d * d, axis=0, keepdims=True)
        outs.append(d * lax.rsqrt(ms + EPS) * sg_ref[...] * (1.0 - lambda_init))
    y = jnp.concatenate(outs, axis=0).T * _silu(z_ref[0])
    y_ref[0] = y.astype(BF16)


def _diff_call(q, k, v, z, qk_gain, lam, subln_gain, lambda_init, tabs32, t=512):
    b, seq, _ = q.shape
    cos_f, sin_a, sin_b = tabs32[:3]
    tile = pl.BlockSpec((1, t, GROUP_WIDTH), lambda bi, qi: (bi, qi, 0))
    full = pl.BlockSpec((1, seq, GROUP_WIDTH), lambda bi, qi: (bi, 0, 0))
    const = lambda shape: pl.BlockSpec(shape, lambda bi, qi: tuple(0 for _ in shape))
    tab = pl.BlockSpec((t, GROUP_WIDTH), lambda bi, qi: (qi, 0))
    return pl.pallas_call(
        functools.partial(_diff_kernel, lambda_init, seq, t),
        grid=(b, seq // t),
        in_specs=[tile, full, full, tile,
                  const((1, GROUP_WIDTH)), const((1, GROUP_WIDTH)), const((4, DIFF_D)), const((HEAD_DIM, 1)),
                  const((GROUP_WIDTH, GROUP_WIDTH)),
                  tab, tab, tab, const((seq, GROUP_WIDTH)), const((seq, GROUP_WIDTH)), const((seq, GROUP_WIDTH))],
        out_specs=tile,
        out_shape=jax.ShapeDtypeStruct((b, seq, GROUP_WIDTH), BF16),
        scratch_shapes=[pltpu.VMEM((2 * N_HEADS, seq, DIFF_D), BF16),
                        pltpu.VMEM((N_HEADS, VT_ROWS, seq), BF16),
                        pltpu.VMEM((2 * N_HEADS, t), F32),
                        pltpu.VMEM((2 * N_HEADS, VT_ROWS, t), F32),
                        pltpu.VMEM((2 * N_HEADS, KEY_TILE, t), F32)],
        compiler_params=_cparams(("parallel", "arbitrary")),
        name="diff_attn",
    )(q, k, v, z,
      jnp.tile(qk_gain[0], 2 * N_HEADS)[None, :], jnp.tile(qk_gain[1], 2 * N_HEADS)[None, :], lam,
      subln_gain[:, None], _group_ones(GROUP_WIDTH, DIFF_D),
      cos_f, sin_a, sin_b, cos_f, sin_a, sin_b)


def _mla_kernel(seq, t,
                cq_ref, ckv_ref, kr_ref, z_ref, cqg_ref, ckvg_ref, wuq_ref, wk_ref, wvT_ref,
                qg_ref, kg_ref, gm_ref, cosq_ref, sina_ref, sinb_ref, cosk_ref, sinak_ref, sinbk_ref,
                y_ref, k_s, vT_s, m_s, acc_s, s_s):
    qi = pl.program_id(1)
    half = MLA_ROPE // 2
    nt = (((1,), (1,)), ((), ()))

    def head_norm(x_h, gain):
        return x_h * _group_rsqrt(x_h, gm_ref[...], float(MLA_QK)) * gain

    @pl.when(qi == 0)
    def _prep():
        for c0 in range(0, seq, PREP_ROWS):
            rows = slice(c0, c0 + PREP_ROWS)
            ckv = ckv_ref[0, rows, :]
            ms = jnp.mean(ckv * ckv, axis=-1, keepdims=True)
            cb = (ckv * lax.rsqrt(ms + EPS) * ckvg_ref[...]).astype(BF16)
            kn = jnp.dot(cb, wk_ref[...], preferred_element_type=F32)
            kr = _rope_rows(kr_ref[0, rows, :], cosk_ref[rows, :], sinak_ref[rows, :], sinbk_ref[rows, :], half)
            for h in range(N_HEADS):
                k_h = kn[:, h * LANES:(h + 1) * LANES] + kr
                k_s[h, rows, :] = head_norm(k_h, kg_ref[...]).astype(BF16)
            vT = lax.dot_general(wvT_ref[...], cb, nt, preferred_element_type=F32)
            for h in range(N_HEADS):
                _store_vt(vT_s, h, rows, vT[h * HEAD_DIM:(h + 1) * HEAD_DIM])

    cq = cq_ref[0]
    ms = jnp.mean(cq * cq, axis=-1, keepdims=True)
    cqb = (cq * lax.rsqrt(ms + EPS) * cqg_ref[...]).astype(BF16)
    qa = jnp.dot(cqb, wuq_ref[...], preferred_element_type=F32)
    qr = _rope_rows(qa, cosq_ref[...], sina_ref[...], sinb_ref[...], half)
    qg = qg_ref[...] * (MLA_QK ** -0.5 * LOG2E)
    qT = jnp.concatenate([head_norm(qr[:, h * LANES:(h + 1) * LANES], qg) for h in range(N_HEADS)],
                         axis=-1).T.astype(BF16)

    _causal_flash_t(
        N_HEADS,
        lambda h, off: k_s[h, pl.ds(off, KEY_TILE), :],
        lambda h: qT[h * LANES:(h + 1) * LANES],
        lambda h, off: vT_s[h, :, pl.ds(off, KEY_TILE)],
        qi, t, KEY_TILE, m_s, acc_s, s_s)

    y = jnp.concatenate([_flash_out(acc_s, h) for h in range(N_HEADS)], axis=0).T * _silu(z_ref[0])
    y_ref[0] = y.astype(BF16)


def _mla_call(cq, ckv, kr, z, cq_gain, ckv_gain, w_uq, w_ukv, qk_gain, tabs32, t=512):
    b, seq, _ = cq.shape
    npad = LANES - MLA_QK
    one = jnp.ones((seq, MLA_NOPE), F32)
    grp = lambda tab, fill: jnp.concatenate([fill, tab[:, :MLA_ROPE], fill[:, :npad]], axis=1)
    cos_g, sina_g, sinb_g = grp(tabs32[0], one), grp(tabs32[1], 0.0 * one), grp(tabs32[2], 0.0 * one)
    cos_q, sina_q, sinb_q = (jnp.tile(a, (1, N_HEADS)) for a in (cos_g, sina_g, sinb_g))
    uq = jnp.pad(w_uq.reshape(MLA_Q_RANK, N_HEADS, MLA_QK), ((0, 0), (0, 0), (0, npad)))
    w_uq_p = uq.reshape(MLA_Q_RANK, N_HEADS * LANES).astype(BF16)
    ukv = w_ukv.reshape(MLA_KV_RANK, N_HEADS, MLA_NOPE + HEAD_DIM)
    w_k = jnp.pad(ukv[:, :, :MLA_NOPE], ((0, 0), (0, 0), (0, LANES - MLA_NOPE)))
    w_k = w_k.reshape(MLA_KV_RANK, N_HEADS * LANES).astype(BF16)
    w_vT = ukv[:, :, MLA_NOPE:].reshape(MLA_KV_RANK, -1).T.astype(BF16)
    gain = lambda g: jnp.pad(g, (0, npad))[None, :]
    tile = pl.BlockSpec((1, t, GROUP_WIDTH), lambda bi, qi: (bi, qi, 0))
    const = lambda shape: pl.BlockSpec(shape, lambda bi, qi: tuple(0 for _ in shape))
    tab = pl.BlockSpec((t, N_HEADS * LANES), lambda bi, qi: (qi, 0))
    ktab = const((seq, LANES))
    return pl.pallas_call(
        functools.partial(_mla_kernel, seq, t),
        grid=(b, seq // t),
        in_specs=[tile,
                  pl.BlockSpec((1, seq, MLA_KV_RANK), lambda bi, qi: (bi, 0, 0)),
                  pl.BlockSpec((1, seq, LANES), lambda bi, qi: (bi, 0, 0)),
                  tile,
                  const((1, MLA_Q_RANK)), const((1, MLA_KV_RANK)),
                  const((MLA_Q_RANK, N_HEADS * LANES)), const((MLA_KV_RANK, N_HEADS * LANES)),
                  const((GROUP_WIDTH, MLA_KV_RANK)),
                  const((1, LANES)), const((1, LANES)), const((LANES, LANES)),
                  tab, tab, tab, ktab, ktab, ktab],
        out_specs=tile,
        out_shape=jax.ShapeDtypeStruct((b, seq, GROUP_WIDTH), BF16),
        scratch_shapes=[pltpu.VMEM((N_HEADS, seq, LANES), BF16),
                        pltpu.VMEM((N_HEADS, VT_ROWS, seq), BF16),
                        pltpu.VMEM((N_HEADS, t), F32),
                        pltpu.VMEM((N_HEADS, VT_ROWS, t), F32),
                        pltpu.VMEM((N_HEADS, KEY_TILE, t), F32)],
        compiler_params=_cparams(("parallel", "arbitrary")),
        name="mla_attn",
    )(cq, ckv, kr, z, cq_gain[None, :], ckv_gain[None, :], w_uq_p, w_k, w_vT,
      gain(qk_gain[0]), gain(qk_gain[1]), _group_ones(LANES, LANES),
      cos_q, sina_q, sinb_q, cos_g, sina_g, sinb_g)


def _mem_kernel(q_ref, mem_ref, z_ref, mg_ref, wk_ref, wvT_ref, qg_ref, kg_ref, gm_ref,
                y_ref, k_s, vT_s, m_s, acc_s):
    qi = pl.program_id(1)
    nt = (((1,), (1,)), ((), ()))
    all_keys = slice(None)

    @pl.when(qi == 0)
    def _prep():
        mem = mem_ref[0]
        ms = jnp.mean(mem * mem, axis=-1, keepdims=True)
        mb = (mem * lax.rsqrt(ms + EPS) * mg_ref[...]).astype(BF16)
        k = jnp.dot(mb, wk_ref[...], preferred_element_type=F32)
        kn = k * _group_rsqrt(k, gm_ref[...], float(HEAD_DIM)) * kg_ref[...]
        vT = lax.dot_general(wvT_ref[...], mb, nt, preferred_element_type=F32)
        for h in range(N_HEADS):
            k_s[h] = kn[:, h * HEAD_DIM:(h + 1) * HEAD_DIM].astype(BF16)
            _store_vt(vT_s, h, all_keys, vT[h * HEAD_DIM:(h + 1) * HEAD_DIM])

    q = q_ref[0]
    qn = q * _group_rsqrt(q, gm_ref[...], float(HEAD_DIM)) * qg_ref[...]
    qT = (qn * (HEAD_DIM ** -0.5 * LOG2E)).T.astype(BF16)
    _flash_reset(m_s, acc_s)
    ss = [jnp.dot(k_s[h], qT[h * HEAD_DIM:(h + 1) * HEAD_DIM], preferred_element_type=F32)
          for h in range(N_HEADS)]
    for h in range(N_HEADS):
        _flash_step_t(ss[h], vT_s[h], m_s, acc_s, h)
    y = jnp.concatenate([_flash_out(acc_s, h) for h in range(N_HEADS)], axis=0).T * _silu(z_ref[0])
    y_ref[0] = y.astype(BF16)


def _mem_call(q, mem, z, mem_gain, w_kv, qk_gain, t=512):
    b, seq, _ = q.shape
    m_len = mem.shape[1]
    w_k = w_kv[:, :GROUP_WIDTH].astype(BF16)
    w_vT = w_kv[:, GROUP_WIDTH:].T.astype(BF16)
    tile = pl.BlockSpec((1, t, GROUP_WIDTH), lambda bi, qi: (bi, qi, 0))
    const = lambda shape: pl.BlockSpec(shape, lambda bi, qi: tuple(0 for _ in shape))
    return pl.pallas_call(
        _mem_kernel,
        grid=(b, seq // t),
        in_specs=[tile, pl.BlockSpec((1, m_len, D_MODEL), lambda bi, qi: (bi, 0, 0)), tile,
                  const((1, D_MODEL)), const((D_MODEL, GROUP_WIDTH)), const((GROUP_WIDTH, D_MODEL)),
                  const((1, GROUP_WIDTH)), const((1, GROUP_WIDTH)), const((GROUP_WIDTH, GROUP_WIDTH))],
        out_specs=tile,
        out_shape=jax.ShapeDtypeStruct((b, seq, GROUP_WIDTH), BF16),
        scratch_shapes=[pltpu.VMEM((N_HEADS, m_len, HEAD_DIM), BF16),
                        pltpu.VMEM((N_HEADS, VT_ROWS, m_len), BF16),
                        pltpu.VMEM((N_HEADS, t), F32),
                        pltpu.VMEM((N_HEADS, VT_ROWS, t), F32)],
        compiler_params=_cparams(("parallel", "arbitrary")),
        name="mem_attn",
    )(q, mem, z, mem_gain[None, :], w_k, w_vT, jnp.tile(qk_gain[0], N_HEADS)[None, :],
      jnp.tile(qk_gain[1], N_HEADS)[None, :], _group_ones(GROUP_WIDTH, HEAD_DIM))


def _nsa_kernel(seq, t,
                q_ref, kcvc_ref, k2_ref, v2_ref, gl_ref, z_ref,
                qg_ref, cg_ref, k2g_ref, gm_ref, pelo_ref, pehi_ref, wlo_ref, whi_ref, wloT_ref, whiT_ref,
                cosq_ref, sina_ref, sinb_ref, cosk_ref, sinak_ref, sinbk_ref, cosc_ref, sinc_ref,
                cmat_ref,
                y_ref, k2_s, v2T_s, kc_s, vcT_s, bias_s, m_s, acc_s, s_s):
    qi = pl.program_id(1)
    half = HEAD_DIM // 2
    nt = (((1,), (1,)), ((), ()))
    n_chunk = seq // NSA_CMP_STRIDE
    n_cmp = n_chunk - 1
    n_blk = seq // NSA_SLC_BLOCK
    blk_per_tile = t // NSA_SLC_BLOCK
    win_tiles = NSA_WINDOW // t
    w4 = N_HEADS * t
    SLC, WIN = 0, 1

    @pl.when(qi == 0)
    def _prep():
        ch = kcvc_ref[0]
        chl = ch + pelo_ref[...]
        chh = ch + pehi_ref[...]
        a = jnp.dot(chl, wlo_ref[...], precision=HIGHEST, preferred_element_type=F32)
        bm = jnp.dot(chh, whi_ref[...], precision=HIGHEST, preferred_element_type=F32)
        kc = (a + pltpu.roll(bm, n_chunk - 1, 0))[:, :HEAD_DIM]
        ms = jnp.mean(kc * kc, axis=-1, keepdims=True)
        kc = kc * lax.rsqrt(ms + EPS) * cg_ref[...]
        x1, x2 = kc[:, :half], kc[:, half:]
        c, s = cosc_ref[...], sinc_ref[...]
        kc_s[...] = jnp.concatenate([x1 * c - x2 * s, x2 * c + x1 * s], axis=-1)
        at = lax.dot_general(wloT_ref[...], chl, nt, precision=HIGHEST, preferred_element_type=F32)
        bt = lax.dot_general(whiT_ref[...], chh, nt, precision=HIGHEST, preferred_element_type=F32)
        vcT_s[...] = (at + pltpu.roll(bt, n_chunk - 1, 1))[HEAD_DIM:].astype(BF16)
        for c0 in range(0, seq, PREP_ROWS):
            rows = slice(c0, c0 + PREP_ROWS)
            k2 = k2_ref[0, rows, :]
            kn = k2 * _group_rsqrt(k2, gm_ref[0:LANES, 0:LANES], float(HEAD_DIM)) * k2g_ref[...]
            kr = _rope_rows(kn, cosk_ref[rows, :], sinak_ref[rows, :], sinbk_ref[rows, :], half)
            k2_s[0, rows, :] = kr[:, :HEAD_DIM].astype(BF16)
            k2_s[1, rows, :] = kr[:, HEAD_DIM:].astype(BF16)
            v2T = v2_ref[0, rows, :].T
            _store_vt(v2T_s, SLC, rows, v2T[:HEAD_DIM])
            _store_vt(v2T_s, WIN, rows, v2T[HEAD_DIM:])

    qs = qi * t
    q = q_ref[0]
    qn = q * _group_rsqrt(q, gm_ref[...], float(HEAD_DIM)) * qg_ref[...]
    qr = _rope_rows(qn, cosq_ref[...], sina_ref[...], sinb_ref[...], half) * (HEAD_DIM ** -0.5 * LOG2E)
    qT = qr.T
    q4f = jnp.concatenate([qT[h * HEAD_DIM:(h + 1) * HEAD_DIM] for h in range(N_HEADS)], axis=1)
    q4 = q4f.astype(BF16)
    qcol = lax.broadcasted_iota(jnp.int32, (1, w4), 1) & (t - 1)
    pos_c = qs + qcol

    sc = jnp.dot(kc_s[...], q4f, precision=HIGHEST, preferred_element_type=F32)
    n_idx = lax.broadcasted_iota(jnp.int32, (n_chunk, 1), 0)
    cvalid = (n_idx * NSA_CMP_STRIDE + (NSA_CMP_BLOCK - 1) <= pos_c) & (n_idx < n_cmp)
    sc = jnp.where(cvalid, sc, NEG_INF)
    e = jnp.exp2(sc - jnp.max(sc, axis=0, keepdims=True))
    p = jnp.where(cvalid, e / jnp.sum(e, axis=0, keepdims=True), 0.0)
    o_cmp = jnp.dot(vcT_s[...], p.astype(BF16), preferred_element_type=F32)
    pg = ((p[:, 0:t] + p[:, t:2 * t]) + p[:, 2 * t:3 * t]) + p[:, 3 * t:4 * t]

    p_slc = jnp.dot(cmat_ref[...], pg, precision=HIGHEST, preferred_element_type=F32)
    blk = lax.broadcasted_iota(jnp.int32, (n_blk, 1), 0)
    cur = lax.shift_right_logical(pos_c[:, :t], NSA_SLC_BLOCK.bit_length() - 1)
    forced = (blk == 0) | ((blk <= cur) & (blk > cur - NSA_N_LOCAL))
    score = jnp.where(blk > cur, NEG_INF, jnp.where(forced, BIG, p_slc))
    cnt = jnp.zeros((n_blk, t), F32)
    for i in range(n_blk):
        ri = score[i:i + 1, :]
        cnt = cnt + jnp.where(ri > score, 1.0, jnp.where((ri == score) & (blk > i), 1.0, 0.0))
    bias = jnp.where(cnt < float(NSA_N_SELECT), 0.0, NEG_INF)
    bias = jnp.concatenate([bias] * N_HEADS, axis=1)
    for r in range(n_blk // blk_per_tile):
        bias_s[r] = bias[r * blk_per_tile:(r + 1) * blk_per_tile, :]

    _flash_reset(m_s, acc_s)
    krow = lax.broadcasted_iota(jnp.int32, (t, 1), 0)
    causal = krow <= qcol
    beyond = krow > qcol

    def slc_scores(j):
        off = pl.multiple_of(j * t, t)
        s = jnp.dot(k2_s[0, pl.ds(off, t), :], q4, preferred_element_type=F32)
        s = s.reshape(blk_per_tile, NSA_SLC_BLOCK, w4) + bias_s[j][:, None, :]
        return s.reshape(t, w4)

    def win_scores(j):
        off = pl.multiple_of(j * t, t)
        return jnp.dot(k2_s[1, pl.ds(off, t), :], q4, preferred_element_type=F32)

    def flash(chain, s, j):
        off = pl.multiple_of(j * t, t)
        _flash_step_t(s, v2T_s[chain, :, pl.ds(off, t)], m_s, acc_s, chain)

    s_s[SLC] = slc_scores(0)

    @pl.when(qi < win_tiles)
    def _first_tile_is_in_window():
        s_s[WIN] = win_scores(0)

    def old_body(j, carry):
        s = s_s[SLC]
        s_s[SLC] = slc_scores(j + 1)
        flash(SLC, s, j)
        return carry

    lax.fori_loop(0, jnp.maximum(qi - win_tiles, 0), old_body, 0)

    @pl.when(qi >= win_tiles)
    def _oldest_window_tile():
        j = qi - win_tiles
        s_slc = s_s[SLC]
        s_win = jnp.where(beyond, win_scores(j), NEG_INF)
        s_s[SLC] = slc_scores(j + 1)
        s_s[WIN] = win_scores(j + 1)
        flash(SLC, s_slc, j)
        flash(WIN, s_win, j)

    def win_body(j, carry):
        s_slc = s_s[SLC]
        s_win = s_s[WIN]
        s_s[SLC] = slc_scores(j + 1)
        s_s[WIN] = win_scores(j + 1)
        flash(SLC, s_slc, j)
        flash(WIN, s_win, j)
        return carry

    lax.fori_loop(jnp.maximum(qi - win_tiles + 1, 0), qi, win_body, 0)
    flash(SLC, jnp.where(causal, s_s[SLC], NEG_INF), qi)
    flash(WIN, jnp.where(causal, s_s[WIN], NEG_INF), qi)

    g = jax.nn.sigmoid(gl_ref[0]).T
    outs = []
    for h in range(N_HEADS):
        cols = slice(h * t, (h + 1) * t)
        o_s = _flash_out(acc_s, SLC, cols)
        o_w = _flash_out(acc_s, WIN, cols)
        outs.append(g[h:h + 1, :] * o_cmp[:, cols] + g[N_HEADS + h:N_HEADS + h + 1, :] * o_s
                    + g[2 * N_HEADS + h:2 * N_HEADS + h + 1, :] * o_w)
    y = jnp.concatenate(outs, axis=0).T * _silu(z_ref[0])
    y_ref[0] = y.astype(BF16)


def _nsa_call(q, kcvc, k2, v2, gl, z, qk_gain, cmp_pe, w_cmp, tabs64, t=256):
    b, seq, _ = q.shape
    cos_f, sin_a, sin_b = tabs64[:3]
    n_chunk = seq // NSA_CMP_STRIDE
    n_blk = seq // NSA_SLC_BLOCK
    half_blk = NSA_CMP_BLOCK // 2
    cw = half_blk * 2 * HEAD_DIM
    assert t % NSA_SLC_BLOCK == 0 and NSA_WINDOW % t == 0 and t & (t - 1) == 0

    wk = w_cmp[0].reshape(NSA_CMP_BLOCK, HEAD_DIM, HEAD_DIM)
    wv = w_cmp[1].reshape(NSA_CMP_BLOCK, HEAD_DIM, HEAD_DIM)
    zero = jnp.zeros_like(wk)
    w_all = jnp.concatenate([jnp.concatenate([wk, zero], axis=2),
                             jnp.concatenate([zero, wv], axis=2)], axis=1)
    w_lo = w_all[:half_blk].reshape(cw, 2 * HEAD_DIM)
    w_hi = w_all[half_blk:].reshape(cw, 2 * HEAD_DIM)
    pe_all = jnp.concatenate([cmp_pe[0], cmp_pe[1]], axis=1)
    pe_lo = pe_all[:half_blk].reshape(1, cw)
    pe_hi = pe_all[half_blk:].reshape(1, cw)

    cmp_end = jnp.arange(n_chunk, dtype=jnp.int32) * NSA_CMP_STRIDE + (NSA_CMP_BLOCK - 1)
    cos_c, sin_c = _rope_tables(cmp_end, HEAD_DIM, HEAD_DIM)[3:]

    ratio = NSA_SLC_BLOCK // NSA_CMP_STRIDE
    coef = np.convolve(np.ones(ratio), np.ones(NSA_CMP_BLOCK // NSA_CMP_STRIDE))
    cmat = np.zeros((n_blk, n_chunk), np.float32)
    for j in range(n_blk):
        for i, c in enumerate(coef):
            if ratio * j + i < n_chunk - 1:
                cmat[j, ratio * j + i] = c

    tile = pl.BlockSpec((1, t, GROUP_WIDTH), lambda bi, qi: (bi, qi, 0))
    const = lambda shape: pl.BlockSpec(shape, lambda bi, qi: tuple(0 for _ in shape))
    full128 = pl.BlockSpec((1, seq, LANES), lambda bi, qi: (bi, 0, 0))
    tab = pl.BlockSpec((t, GROUP_WIDTH), lambda bi, qi: (qi, 0))
    ktab = const((seq, LANES))
    w4 = N_HEADS * t
    return pl.pallas_call(
        functools.partial(_nsa_kernel, seq, t),
        grid=(b, seq // t),
        in_specs=[tile,
                  pl.BlockSpec((1, n_chunk, cw), lambda bi, qi: (bi, 0, 0)),
                  full128, full128,
                  pl.BlockSpec((1, t, LANES), lambda bi, qi: (bi, qi, 0)),
                  tile,
                  const((1, GROUP_WIDTH)), const((1, HEAD_DIM)), const((1, 2 * HEAD_DIM)),
                  const((GROUP_WIDTH, GROUP_WIDTH)),
                  const((1, cw)), const((1, cw)), const((cw, 2 * HEAD_DIM)), const((cw, 2 * HEAD_DIM)),
                  const((2 * HEAD_DIM, cw)), const((2 * HEAD_DIM, cw)),
                  tab, tab, tab, ktab, ktab, ktab,
                  const((n_chunk, HEAD_DIM // 2)), const((n_chunk, HEAD_DIM // 2)),
                  const((n_blk, n_chunk))],
        out_specs=tile,
        out_shape=jax.ShapeDtypeStruct((b, seq, GROUP_WIDTH), BF16),
        scratch_shapes=[pltpu.VMEM((2, seq, HEAD_DIM), BF16),
                        pltpu.VMEM((2, VT_ROWS, seq), BF16),
                        pltpu.VMEM((n_chunk, HEAD_DIM), F32),
                        pltpu.VMEM((HEAD_DIM, n_chunk), BF16),
                        pltpu.VMEM((seq // t, t // NSA_SLC_BLOCK, w4), F32),
                        pltpu.VMEM((2, w4), F32),
                        pltpu.VMEM((2, VT_ROWS, w4), F32),
                        pltpu.VMEM((2, t, w4), F32)],
        compiler_params=_cparams(("parallel", "arbitrary")),
        name="nsa_attn",
    )(q, kcvc.reshape(b, n_chunk, cw), k2, v2, gl, z,
      jnp.tile(qk_gain[0], N_HEADS)[None, :], qk_gain[1][None, :],
      jnp.concatenate([qk_gain[2], qk_gain[3]])[None, :], _group_ones(GROUP_WIDTH, HEAD_DIM),
      pe_lo, pe_hi, w_lo, w_hi, w_lo.T, w_hi.T,
      cos_f, sin_a, sin_b, cos_f[:, :LANES], sin_a[:, :LANES], sin_b[:, :LANES], cos_c, sin_c,
      jnp.asarray(cmat))


def _rope_tables(pos, dim, width):
    half = dim // 2
    inv_freq = ROPE_THETA ** (-jnp.arange(half, dtype=F32) / half)
    ang = pos.astype(F32)[:, None] * inv_freq[None, :]
    cos, sin = jnp.cos(ang), jnp.sin(ang)
    zero = jnp.zeros_like(sin)
    reps = width // dim
    cos_f = jnp.tile(jnp.concatenate([cos, cos], axis=-1), (1, reps))
    sin_a = jnp.tile(jnp.concatenate([-sin, zero], axis=-1), (1, reps))
    sin_b = jnp.tile(jnp.concatenate([zero, sin], axis=-1), (1, reps))
    return cos_f, sin_a, sin_b, cos, sin


def _layer(x, mem, layer_idx, tabs32, tabs64, norm_gain, w_in, w_out, nsa_qk_gain, nsa_cmp_pe, nsa_w_cmp,
           diff_qk_gain, diff_lambda, diff_subln_gain, mla_cq_gain, mla_ckv_gain, mla_w_uq, mla_w_ukv,
           mla_qk_gain, mem_norm_gain, mem_w_kv, mem_qk_gain):
    b, seq, d = x.shape
    x2 = x.reshape(b * seq, d)
    w_pad = jnp.where(jnp.asarray(_IN_VALID)[None, :], jnp.take(w_in, jnp.asarray(_IN_COLS), axis=1), 0.0)
    u = _in_proj(x2, norm_gain[None, :], w_pad.astype(BF16))
    u = {name: a.reshape(b, seq, a.shape[-1]) for (name, _), a in zip(_SEGS, u)}

    y_nsa = _nsa_call(u["nsa_q"], u["nsa_kcvc"], u["nsa_k2"], u["nsa_v2"], u["nsa_gl"], u["nsa_z"],
                      nsa_qk_gain, nsa_cmp_pe, nsa_w_cmp, tabs64)
    lambda_init = 0.8 - 0.6 * math.exp(-0.3 * layer_idx)
    y_diff = _diff_call(u["diff_q"], u["diff_k"], u["diff_v"], u["diff_z"], diff_qk_gain, diff_lambda,
                        diff_subln_gain, lambda_init, tabs32)
    y_mla = _mla_call(u["mla_cq"], u["mla_ckv"], u["mla_kr"], u["mla_z"], mla_cq_gain, mla_ckv_gain,
                      mla_w_uq, mla_w_ukv, mla_qk_gain, tabs32)
    y_mem = _mem_call(u["mem_q"], mem, u["mem_z"], mem_norm_gain, mem_w_kv, mem_qk_gain)

    ys = [y.reshape(b * seq, GROUP_WIDTH) for y in (y_nsa, y_diff, y_mla, y_mem)]
    return _out_proj(x2, ys, w_out.astype(BF16)).reshape(b, seq, d)


def kernel(x, mem, norm_gain, w_in, w_out, nsa_qk_gain, nsa_cmp_pe, nsa_w_cmp, diff_qk_gain, diff_lambda,
           diff_subln_gain, mla_cq_gain, mla_ckv_gain, mla_w_uq, mla_w_ukv, mla_qk_gain, mem_norm_gain,
           mem_w_kv, mem_qk_gain):
    seq = x.shape[1]
    pos = jnp.arange(seq, dtype=jnp.int32)
    tabs32 = _rope_tables(pos, DIFF_D, GROUP_WIDTH)
    tabs64 = _rope_tables(pos, HEAD_DIM, GROUP_WIDTH)
    for l in range(DEPTH):
        x = _layer(x, mem, l, tabs32, tabs64, norm_gain[l], w_in[l], w_out[l], nsa_qk_gain[l], nsa_cmp_pe[l],
                   nsa_w_cmp[l], diff_qk_gain[l], diff_lambda[l], diff_subln_gain[l], mla_cq_gain[l],
                   mla_ckv_gain[l], mla_w_uq[l], mla_w_ukv[l], mla_qk_gain[l], mem_norm_gain[l], mem_w_kv[l],
                   mem_qk_gain[l])
    return x
```

```python
import functools
import math

import numpy as np
import jax
import jax.numpy as jnp
from jax import lax
from jax.experimental import pallas as pl
from jax.experimental.pallas import tpu as pltpu

F32 = jnp.float32
BF16 = jnp.bfloat16
HIGHEST = lax.Precision.HIGHEST

D_MODEL = 1024
DEPTH = 2
N_HEADS = 4
HEAD_DIM = 64
GROUP_WIDTH = N_HEADS * HEAD_DIM
ROPE_THETA = 10000.0
EPS = 1e-6
NEG_INF = -1e30
BIG = 1e30
LOG2E = 1.4426950408889634

NSA_CMP_BLOCK = 32
NSA_CMP_STRIDE = 16
NSA_SLC_BLOCK = 64
NSA_N_SELECT = 16
NSA_N_LOCAL = 2
NSA_WINDOW = 512
DIFF_D = HEAD_DIM // 2
MLA_Q_RANK = 256
MLA_KV_RANK = 128
MLA_NOPE = 64
MLA_ROPE = 32
MLA_QK = MLA_NOPE + MLA_ROPE

VMEM_LIMIT_BYTES = 48 * 1024 * 1024
LANES = 128
KEY_TILE = 256
VT_ROWS = 80


def _cparams(sem):
    return pltpu.CompilerParams(dimension_semantics=sem, vmem_limit_bytes=VMEM_LIMIT_BYTES)


def _group_ones(width, group):
    g = np.arange(width) // group
    return jnp.asarray(g[:, None] == g[None, :], dtype=BF16)


def _group_rsqrt(x, gmat, denom):
    sq = x * x
    hi = sq.astype(BF16)
    lo = (sq - hi.astype(F32)).astype(BF16)
    ss = jnp.dot(hi, gmat, preferred_element_type=F32) + jnp.dot(lo, gmat, preferred_element_type=F32)
    return lax.rsqrt(ss / denom + EPS)


def _norm_rope_t(u_t, groups, dim, gain, cos=None, sin=None):
    x = u_t.reshape(groups, dim, u_t.shape[-1])
    ms = jnp.mean(x * x, axis=1, keepdims=True)
    x = x * lax.rsqrt(ms + EPS) * gain
    if cos is not None:
        half = dim // 2
        x1, x2 = x[:, :half], x[:, half:]
        x = jnp.concatenate([x1 * cos - x2 * sin, x2 * cos + x1 * sin], axis=1)
    return x.reshape(groups * dim, u_t.shape[-1])


def _silu(z):
    z = z.astype(F32)
    return z * jax.nn.sigmoid(z)


def _ones_rows(n):
    r = lax.broadcasted_iota(jnp.int32, (VT_ROWS - HEAD_DIM, n), 0)
    return jnp.where(r == 0, 1.0, 0.0).astype(BF16)


def _flash_step_t(s, v_t, m_ref, acc_ref, i):
    m_old = m_ref[i:i + 1, :]
    m_new = jnp.maximum(m_old, jnp.max(s, axis=0, keepdims=True))
    alpha = jnp.exp2(m_old - m_new)
    p = jnp.exp2(s - m_new).astype(BF16)
    acc_ref[i] = alpha * acc_ref[i] + jnp.dot(v_t, p, preferred_element_type=F32)
    m_ref[i:i + 1, :] = m_new


def _flash_reset(m_ref, acc_ref):
    m_ref[...] = jnp.full(m_ref.shape, NEG_INF, F32)
    acc_ref[...] = jnp.zeros(acc_ref.shape, F32)


def _flash_out(acc_ref, i, cols=slice(None)):
    return acc_ref[i, 0:HEAD_DIM, cols] / acc_ref[i, HEAD_DIM:HEAD_DIM + 1, cols]


def _causal_flash_t(n, k_tile, q_t, v_tile, qi, tq, tk, m_ref, acc_ref, s_ref):
    _flash_reset(m_ref, acc_ref)
    n_diag = tq // tk

    def scores(i, off):
        return jnp.dot(k_tile(i, off), q_t(i), preferred_element_type=F32)

    def step(off, off_next, mask):
        for i in range(n):
            s = s_ref[i]
            if off_next is not None:
                s_ref[i] = scores(i, off_next)
            if mask is not None:
                s = jnp.where(mask, s, NEG_INF)
            _flash_step_t(s, v_tile(i, off), m_ref, acc_ref, i)

    for i in range(n):
        s_ref[i] = scores(i, 0)

    def body(j, carry):
        step(pl.multiple_of(j * tk, tk), pl.multiple_of(j * tk + tk, tk), None)
        return carry

    lax.fori_loop(0, qi * n_diag, body, 0)
    krow = lax.broadcasted_iota(jnp.int32, (tk, tq), 0)
    qcol = lax.broadcasted_iota(jnp.int32, (tk, tq), 1)
    for d in range(n_diag):
        off = pl.multiple_of(qi * tq + d * tk, tk)
        off_next = pl.multiple_of(qi * tq + (d + 1) * tk, tk) if d + 1 < n_diag else None
        step(off, off_next, krow + d * tk <= qcol)


_ROW_SEGS = (("nsa_kcvc", 128), ("nsa_gl", 128), ("nsa_z", 256), ("diff_z", 256), ("mla_z", 256),
             ("mem_z", 256))
_COL_SEGS = (("nsa_q", 256), ("diff_q", 256), ("mem_q", 256), ("diff_k", 256), ("nsa_k2", 128),
             ("diff_v", 256), ("nsa_v2", 128), ("mla_cq", 256), ("mla_ckv", 128), ("mla_kr", 32))


def _seg_offsets(segs):
    out, off = {}, 0
    for name, w in segs:
        out[name] = (off, w)
        off += w
    return out, off


_ROW_OFF, _ROW_W = _seg_offsets(_ROW_SEGS)
_COL_OFF, _COL_W = _seg_offsets(_COL_SEGS)


def _split_w_in(w_in):
    c = lambda a, b: w_in[:, a:b]
    z = lambda n: jnp.zeros((w_in.shape[0], n), w_in.dtype)
    w_row = jnp.concatenate([
        c(256, 384),
        c(640, 652), z(116),
        c(652, 908), c(1676, 1932), c(2348, 2604), c(2860, 3116),
    ], axis=1)
    w_col = jnp.concatenate([
        c(0, 256), c(908, 1164), c(2604, 2860),
        c(1164, 1420),
        c(384, 448), c(512, 576),
        c(1420, 1676),
        c(448, 512), c(576, 640),
        c(1932, 2188), c(2188, 2316), c(2316, 2348),
    ], axis=1)
    return w_row.astype(BF16), w_col.T.astype(BF16)


def _mla_head_norm_t(x_t, gain):
    ms = jnp.sum(x_t * x_t, axis=0, keepdims=True) / float(MLA_QK)
    return x_t * lax.rsqrt(ms + EPS) * gain


def _rope_t(x_t, cos, sin):
    half = x_t.shape[0] // 2
    x1, x2 = x_t[:half], x_t[half:]
    return jnp.concatenate([x1 * cos - x2 * sin, x2 * cos + x1 * sin], axis=0)


def _in_proj_kernel(x_ref, g_ref, wrow_ref, wcol_ref, nqg_ref, dqg_ref, dkg_ref, k2g_ref, mqg_ref,
                    cqg_ref, ckvg_ref, wuq_ref, wuk_ref, wuv_ref, lqg_ref, lkg_ref,
                    c32_ref, s32_ref, c64_ref, s64_ref,
                    kcvc_o, gl_o, nz_o, dz_o, mz_o, ez_o, dk_o, k2_o,
                    nq_o, dq_o, mq_o, dv_o, v2_o, lq_o, lk_o, lv_o):
    x = x_ref[...]
    tm = x.shape[0]
    ms = jnp.mean(x * x, axis=-1, keepdims=True)
    h = x * lax.rsqrt(ms + EPS) * g_ref[...]
    hb = h.astype(BF16)
    h_t = h.T.astype(BF16)

    def row(name):
        off, w = _ROW_OFF[name]
        return jnp.dot(hb, wrow_ref[:, off:off + w], preferred_element_type=F32)

    def col(name):
        off, w = _COL_OFF[name]
        return jnp.dot(wcol_ref[off:off + w, :], h_t, preferred_element_type=F32)

    kcvc_o[...] = row("nsa_kcvc")
    gl_o[...] = row("nsa_gl")
    nz_o[...] = row("nsa_z").astype(BF16)
    dz_o[...] = row("diff_z").astype(BF16)
    mz_o[...] = row("mla_z").astype(BF16)
    ez_o[...] = row("mem_z").astype(BF16)

    c32, s32, c64, s64 = c32_ref[...], s32_ref[...], c64_ref[...], s64_ref[...]
    g3 = lambda ref, groups: ref[...].reshape(groups, -1, 1)
    nq_o[0] = _norm_rope_t(col("nsa_q"), N_HEADS, HEAD_DIM, g3(nqg_ref, 1), c64, s64) * (HEAD_DIM ** -0.5 * LOG2E)
    dq = _norm_rope_t(col("diff_q"), 2 * N_HEADS, DIFF_D, g3(dqg_ref, 1), c32, s32) * (DIFF_D ** -0.5 * LOG2E)
    dq_o[0] = dq.astype(BF16)
    mq_o[0] = (_norm_rope_t(col("mem_q"), N_HEADS, HEAD_DIM, g3(mqg_ref, 1)) * (HEAD_DIM ** -0.5 * LOG2E)).astype(BF16)
    dk_o[...] = _norm_rope_t(col("diff_k"), 2 * N_HEADS, DIFF_D, g3(dkg_ref, 1), c32, s32).T.astype(BF16)
    k2_o[...] = _norm_rope_t(col("nsa_k2"), 2, HEAD_DIM, g3(k2g_ref, 2), c64, s64).T.astype(BF16)
    ones = _ones_rows(tm)
    dv = col("diff_v")
    for hd in range(N_HEADS):
        dv_o[0, hd, 0:HEAD_DIM, :] = dv[hd * HEAD_DIM:(hd + 1) * HEAD_DIM].astype(BF16)
        dv_o[0, hd, HEAD_DIM:VT_ROWS, :] = ones
    v2 = col("nsa_v2")
    for br in range(2):
        v2_o[0, br, 0:HEAD_DIM, :] = v2[br * HEAD_DIM:(br + 1) * HEAD_DIM].astype(BF16)
        v2_o[0, br, HEAD_DIM:VT_ROWS, :] = ones

    def latent(name, gain_ref):
        c = col(name)
        return (c * lax.rsqrt(jnp.mean(c * c, axis=0, keepdims=True) + EPS) * gain_ref[...]).astype(BF16)

    rope_rows = slice(MLA_NOPE, MLA_QK)
    qa = jnp.dot(wuq_ref[...], latent("mla_cq", cqg_ref), preferred_element_type=F32)
    lqg = lqg_ref[...] * (MLA_QK ** -0.5 * LOG2E)
    ckv = latent("mla_ckv", ckvg_ref)
    kn = jnp.dot(wuk_ref[...], ckv, preferred_element_type=F32)
    kr = _rope_t(col("mla_kr"), c32, s32)
    zpad = jnp.zeros((LANES - MLA_QK, tm), F32)
    lv = jnp.dot(wuv_ref[...], ckv, preferred_element_type=F32)
    for hd in range(N_HEADS):
        q_h = qa[hd * LANES:(hd + 1) * LANES]
        q_h = jnp.concatenate([q_h[:MLA_NOPE], _rope_t(q_h[rope_rows], c32, s32), q_h[MLA_QK:]], axis=0)
        lq_o[0, hd * LANES:(hd + 1) * LANES, :] = _mla_head_norm_t(q_h, lqg).astype(BF16)
        k_h = jnp.concatenate([kn[hd * LANES:hd * LANES + MLA_NOPE], kr, zpad], axis=0)
        lk_o[0, hd] = _mla_head_norm_t(k_h, lkg_ref[...]).T.astype(BF16)
        lv_o[0, hd, 0:HEAD_DIM, :] = lv[hd * HEAD_DIM:(hd + 1) * HEAD_DIM].astype(BF16)
        lv_o[0, hd, HEAD_DIM:VT_ROWS, :] = ones


def _in_proj(x, gain, w_in, nsa_qk_gain, diff_qk_gain, mem_qk_gain, mla_cq_gain, mla_ckv_gain, mla_w_uq,
             mla_w_ukv, mla_qk_gain, tabs_t, tm=512):
    b, seq, d = x.shape
    n = b * seq
    nb = seq // tm
    w_row, w_col = _split_w_in(w_in)
    c32, s32, c64, s64 = tabs_t
    npad = LANES - MLA_QK
    uq = jnp.pad(mla_w_uq.reshape(MLA_Q_RANK, N_HEADS, MLA_QK), ((0, 0), (0, 0), (0, npad)))
    w_uq_t = uq.reshape(MLA_Q_RANK, N_HEADS * LANES).T.astype(BF16)
    ukv = mla_w_ukv.reshape(MLA_KV_RANK, N_HEADS, MLA_NOPE + HEAD_DIM)
    uk = jnp.pad(ukv[:, :, :MLA_NOPE], ((0, 0), (0, 0), (0, LANES - MLA_NOPE)))
    w_uk_t = uk.reshape(MLA_KV_RANK, N_HEADS * LANES).T.astype(BF16)
    w_uv_t = ukv[:, :, MLA_NOPE:].reshape(MLA_KV_RANK, GROUP_WIDTH).T.astype(BF16)
    pad_gain = lambda g: jnp.pad(g, (0, npad))[:, None]

    rowspec = lambda w: pl.BlockSpec((tm, w), lambda i: (i, 0))
    colspec = lambda r: pl.BlockSpec((1, r, tm), lambda i: (i // nb, 0, i % nb))
    vtspec = lambda c: pl.BlockSpec((1, c, VT_ROWS, tm), lambda i: (i // nb, 0, 0, i % nb))
    const = lambda shape: pl.BlockSpec(shape, lambda i: tuple(0 for _ in shape))
    tabspec = lambda r: pl.BlockSpec((r, tm), lambda i: (0, i % nb))
    row_out = lambda w, dt: jax.ShapeDtypeStruct((n, w), dt)
    col_out = lambda r, dt: jax.ShapeDtypeStruct((b, r, seq), dt)
    vt_out = lambda c: jax.ShapeDtypeStruct((b, c, VT_ROWS, seq), BF16)
    outs = pl.pallas_call(
        _in_proj_kernel,
        grid=(n // tm,),
        in_specs=[rowspec(d), const((1, d)), const((d, _ROW_W)), const((_COL_W, d)),
                  const((HEAD_DIM, 1)), const((DIFF_D, 1)), const((DIFF_D, 1)), const((2 * HEAD_DIM, 1)),
                  const((HEAD_DIM, 1)),
                  const((MLA_Q_RANK, 1)), const((MLA_KV_RANK, 1)),
                  const((N_HEADS * LANES, MLA_Q_RANK)), const((N_HEADS * LANES, MLA_KV_RANK)),
                  const((GROUP_WIDTH, MLA_KV_RANK)), const((LANES, 1)), const((LANES, 1)),
                  tabspec(DIFF_D // 2), tabspec(DIFF_D // 2), tabspec(HEAD_DIM // 2), tabspec(HEAD_DIM // 2)],
        out_specs=[rowspec(128), rowspec(128), rowspec(256), rowspec(256), rowspec(256), rowspec(256),
                   rowspec(256), rowspec(128),
                   colspec(256), colspec(256), colspec(256), vtspec(N_HEADS), vtspec(2),
                   colspec(N_HEADS * LANES),
                   pl.BlockSpec((1, N_HEADS, tm, LANES), lambda i: (i // nb, 0, i % nb, 0)),
                   vtspec(N_HEADS)],
        out_shape=[row_out(128, F32), row_out(128, F32), row_out(256, BF16), row_out(256, BF16),
                   row_out(256, BF16), row_out(256, BF16), row_out(256, BF16), row_out(128, BF16),
                   col_out(256, F32), col_out(256, BF16), col_out(256, BF16), vt_out(N_HEADS), vt_out(2),
                   col_out(N_HEADS * LANES, BF16),
                   jax.ShapeDtypeStruct((b, N_HEADS, seq, LANES), BF16),
                   vt_out(N_HEADS)],
        compiler_params=_cparams(("parallel",)),
        name="in_proj",
    )(x.reshape(n, d), gain[None, :], w_row, w_col,
      nsa_qk_gain[0][:, None], diff_qk_gain[0][:, None], diff_qk_gain[1][:, None],
      jnp.concatenate([nsa_qk_gain[2], nsa_qk_gain[3]])[:, None], mem_qk_gain[0][:, None],
      mla_cq_gain[:, None], mla_ckv_gain[:, None], w_uq_t, w_uk_t, w_uv_t,
      pad_gain(mla_qk_gain[0]), pad_gain(mla_qk_gain[1]),
      c32, s32, c64, s64)
    names = ("nsa_kcvc", "nsa_gl", "nsa_z", "diff_z", "mla_z", "mem_z", "diff_k", "nsa_k2",
             "nsa_qT", "diff_qT", "mem_qT", "diff_vT", "nsa_v2T", "mla_qT", "mla_k", "mla_vT")
    u = dict(zip(names, outs))
    for name in names[:8]:
        u[name] = u[name].reshape(b, seq, -1)
    return u


def _out_proj_kernel(x_ref, y0_ref, y1_ref, y2_ref, y3_ref, w_ref, o_ref):
    acc = x_ref[...]
    for g, y_ref in enumerate((y0_ref, y1_ref, y2_ref, y3_ref)):
        acc = acc + jnp.dot(y_ref[...], w_ref[g * GROUP_WIDTH:(g + 1) * GROUP_WIDTH, :],
                            preferred_element_type=F32)
    o_ref[...] = acc


def _out_proj(x2, ys, w_out_b, tm=512):
    n = x2.shape[0]
    yspec = pl.BlockSpec((tm, GROUP_WIDTH), lambda i: (i, 0))
    return pl.pallas_call(
        _out_proj_kernel,
        grid=(n // tm,),
        in_specs=[pl.BlockSpec((tm, D_MODEL), lambda i: (i, 0)), yspec, yspec, yspec, yspec,
                  pl.BlockSpec((D_MODEL, D_MODEL), lambda i: (0, 0))],
        out_specs=pl.BlockSpec((tm, D_MODEL), lambda i: (i, 0)),
        out_shape=jax.ShapeDtypeStruct((n, D_MODEL), F32),
        compiler_params=_cparams(("parallel",)),
        name="out_proj",
    )(x2, *ys, w_out_b)


def _diff_kernel(lambda_init, t, qT_ref, k_ref, vT_ref, z_ref, lam_ref, sg_ref, y_ref, m_s, acc_s, s_s):
    qi = pl.program_id(1)
    n_maps = 2 * N_HEADS
    qT = qT_ref[0]
    frow = lax.broadcasted_iota(jnp.int32, (GROUP_WIDTH, 1), 0) // DIFF_D
    qT_map = [jnp.where(frow == i, qT, jnp.zeros_like(qT)) for i in range(n_maps)]

    _causal_flash_t(
        n_maps,
        lambda i, off: k_ref[0, pl.ds(off, KEY_TILE), :],
        lambda i: qT_map[i],
        lambda i, off: vT_ref[0, i // 2, :, pl.ds(off, KEY_TILE)],
        qi, t, KEY_TILE, m_s, acc_s, s_s)

    lam = lam_ref[...]
    lmbda = (jnp.exp(jnp.sum(lam[0:1] * lam[1:2], axis=-1, keepdims=True))
             - jnp.exp(jnp.sum(lam[2:3] * lam[3:4], axis=-1, keepdims=True)) + lambda_init)
    outs = []
    for h in range(N_HEADS):
        d = _flash_out(acc_s, 2 * h) - lmbda * _flash_out(acc_s, 2 * h + 1)
        ms = jnp.mean(d * d, axis=0, keepdims=True)
        outs.append(d * lax.rsqrt(ms + EPS) * sg_ref[...] * (1.0 - lambda_init))
    y = jnp.concatenate(outs, axis=0).T * _silu(z_ref[0])
    y_ref[0] = y.astype(BF16)


def _diff_call(qT, k, vT, z, lam, subln_gain, lambda_init, t=512):
    b, seq, _ = k.shape
    const = lambda shape: pl.BlockSpec(shape, lambda bi, qi: tuple(0 for _ in shape))
    tile = pl.BlockSpec((1, t, GROUP_WIDTH), lambda bi, qi: (bi, qi, 0))
    return pl.pallas_call(
        functools.partial(_diff_kernel, lambda_init, t),
        grid=(b, seq // t),
        in_specs=[pl.BlockSpec((1, GROUP_WIDTH, t), lambda bi, qi: (bi, 0, qi)),
                  pl.BlockSpec((1, seq, GROUP_WIDTH), lambda bi, qi: (bi, 0, 0)),
                  pl.BlockSpec((1, N_HEADS, VT_ROWS, seq), lambda bi, qi: (bi, 0, 0, 0)),
                  tile, const((4, DIFF_D)), const((HEAD_DIM, 1))],
        out_specs=tile,
        out_shape=jax.ShapeDtypeStruct((b, seq, GROUP_WIDTH), BF16),
        scratch_shapes=[pltpu.VMEM((2 * N_HEADS, t), F32),
                        pltpu.VMEM((2 * N_HEADS, VT_ROWS, t), F32),
                        pltpu.VMEM((2 * N_HEADS, KEY_TILE, t), F32)],
        compiler_params=_cparams(("parallel", "parallel")),
        name="diff_attn",
    )(qT, k, vT, z, lam, subln_gain[:, None])


def _mla_kernel(t, qT_ref, k_ref, vT_ref, z_ref, y_ref, m_s, acc_s, s_s):
    qi = pl.program_id(1)
    qT = qT_ref[0]
    _causal_flash_t(
        N_HEADS,
        lambda h, off: k_ref[0, h, pl.ds(off, KEY_TILE), :],
        lambda h: qT[h * LANES:(h + 1) * LANES],
        lambda h, off: vT_ref[0, h, :, pl.ds(off, KEY_TILE)],
        qi, t, KEY_TILE, m_s, acc_s, s_s)
    y = jnp.concatenate([_flash_out(acc_s, h) for h in range(N_HEADS)], axis=0).T * _silu(z_ref[0])
    y_ref[0] = y.astype(BF16)


def _mla_call(qT, k, vT, z, t=512):
    b, _, seq, _ = k.shape
    tile = pl.BlockSpec((1, t, GROUP_WIDTH), lambda bi, qi: (bi, qi, 0))
    return pl.pallas_call(
        functools.partial(_mla_kernel, t),
        grid=(b, seq // t),
        in_specs=[pl.BlockSpec((1, N_HEADS * LANES, t), lambda bi, qi: (bi, 0, qi)),
                  pl.BlockSpec((1, N_HEADS, seq, LANES), lambda bi, qi: (bi, 0, 0, 0)),
                  pl.BlockSpec((1, N_HEADS, VT_ROWS, seq), lambda bi, qi: (bi, 0, 0, 0)),
                  tile],
        out_specs=tile,
        out_shape=jax.ShapeDtypeStruct((b, seq, GROUP_WIDTH), BF16),
        scratch_shapes=[pltpu.VMEM((N_HEADS, t), F32),
                        pltpu.VMEM((N_HEADS, VT_ROWS, t), F32),
                        pltpu.VMEM((N_HEADS, KEY_TILE, t), F32)],
        compiler_params=_cparams(("parallel", "parallel")),
        name="mla_attn",
    )(qT, k, vT, z)


def _mem_kernel(qT_ref, mem_ref, z_ref, mg_ref, wk_ref, wvT_ref, kg_ref, gm_ref,
                y_ref, k_s, vT_s, m_s, acc_s):
    qi = pl.program_id(1)
    nt = (((1,), (1,)), ((), ()))
    m_len = mem_ref.shape[1]

    @pl.when(qi == 0)
    def _prep():
        mem = mem_ref[0]
        ms = jnp.mean(mem * mem, axis=-1, keepdims=True)
        mb = (mem * lax.rsqrt(ms + EPS) * mg_ref[...]).astype(BF16)
        k = jnp.dot(mb, wk_ref[...], preferred_element_type=F32)
        k_s[...] = (k * _group_rsqrt(k, gm_ref[...], float(HEAD_DIM)) * kg_ref[...]).astype(BF16)
        vT = lax.dot_general(wvT_ref[...], mb, nt, preferred_element_type=F32)
        for h in range(N_HEADS):
            vT_s[h, 0:HEAD_DIM, :] = vT[h * HEAD_DIM:(h + 1) * HEAD_DIM].astype(BF16)
            vT_s[h, HEAD_DIM:VT_ROWS, :] = _ones_rows(m_len)

    qT = qT_ref[0]
    frow = lax.broadcasted_iota(jnp.int32, (GROUP_WIDTH, 1), 0) // HEAD_DIM
    _flash_reset(m_s, acc_s)
    ss = [jnp.dot(k_s[...], jnp.where(frow == h, qT, jnp.zeros_like(qT)), preferred_element_type=F32)
          for h in range(N_HEADS)]
    for h in range(N_HEADS):
        _flash_step_t(ss[h], vT_s[h], m_s, acc_s, h)
    y = jnp.concatenate([_flash_out(acc_s, h) for h in range(N_HEADS)], axis=0).T * _silu(z_ref[0])
    y_ref[0] = y.astype(BF16)


def _mem_call(qT, mem, z, mem_gain, w_kv, qk_gain, t=512):
    b, m_len, _ = mem.shape
    seq = qT.shape[2]
    w_k = w_kv[:, :GROUP_WIDTH].astype(BF16)
    w_vT = w_kv[:, GROUP_WIDTH:].T.astype(BF16)
    tile = pl.BlockSpec((1, t, GROUP_WIDTH), lambda bi, qi: (bi, qi, 0))
    const = lambda shape: pl.BlockSpec(shape, lambda bi, qi: tuple(0 for _ in shape))
    return pl.pallas_call(
        _mem_kernel,
        grid=(b, seq // t),
        in_specs=[pl.BlockSpec((1, GROUP_WIDTH, t), lambda bi, qi: (bi, 0, qi)),
                  pl.BlockSpec((1, m_len, D_MODEL), lambda bi, qi: (bi, 0, 0)), tile,
                  const((1, D_MODEL)), const((D_MODEL, GROUP_WIDTH)), const((GROUP_WIDTH, D_MODEL)),
                  const((1, GROUP_WIDTH)), const((GROUP_WIDTH, GROUP_WIDTH))],
        out_specs=tile,
        out_shape=jax.ShapeDtypeStruct((b, seq, GROUP_WIDTH), BF16),
        scratch_shapes=[pltpu.VMEM((m_len, GROUP_WIDTH), BF16),
                        pltpu.VMEM((N_HEADS, VT_ROWS, m_len), BF16),
                        pltpu.VMEM((N_HEADS, t), F32),
                        pltpu.VMEM((N_HEADS, VT_ROWS, t), F32)],
        compiler_params=_cparams(("parallel", "arbitrary")),
        name="mem_attn",
    )(qT, mem, z, mem_gain[None, :], w_k, w_vT, jnp.tile(qk_gain[1], N_HEADS)[None, :],
      _group_ones(GROUP_WIDTH, HEAD_DIM))


def _nsa_kernel(seq, t,
                qT_ref, kcvc_ref, k2_ref, v2T_ref, gl_ref, z_ref,
                cg_ref, pelo_ref, pehi_ref, wlo_ref, whi_ref, cosc_ref, sinc_ref,
                cmat_ref,
                y_ref, kc_s, vcT_s, bias_s, m_s, acc_s, s_s):
    qi = pl.program_id(1)
    half = HEAD_DIM // 2
    n_chunk = seq // NSA_CMP_STRIDE
    n_cmp = n_chunk - 1
    n_blk = seq // NSA_SLC_BLOCK
    blk_per_tile = t // NSA_SLC_BLOCK
    win_tiles = NSA_WINDOW // t
    w4 = N_HEADS * t
    SLC, WIN = 0, 1

    @pl.when(qi == 0)
    def _prep():
        ch = kcvc_ref[0]
        chl = ch + pelo_ref[...]
        chh = ch + pehi_ref[...]
        a = jnp.dot(chl, wlo_ref[...], precision=HIGHEST, preferred_element_type=F32)
        bm = jnp.dot(chh, whi_ref[...], precision=HIGHEST, preferred_element_type=F32)
        cmp = a + pltpu.roll(bm, n_chunk - 1, 0)
        kc = cmp[:, :HEAD_DIM]
        ms = jnp.mean(kc * kc, axis=-1, keepdims=True)
        kc = kc * lax.rsqrt(ms + EPS) * cg_ref[...]
        x1, x2 = kc[:, :half], kc[:, half:]
        c, s = cosc_ref[...], sinc_ref[...]
        kc_s[...] = jnp.concatenate([x1 * c - x2 * s, x2 * c + x1 * s], axis=-1)
        vcT_s[...] = cmp.T[HEAD_DIM:].astype(BF16)

    qs = qi * t
    qT = qT_ref[0]
    q4f = jnp.concatenate([qT[h * HEAD_DIM:(h + 1) * HEAD_DIM] for h in range(N_HEADS)], axis=1)
    q4 = q4f.astype(BF16)
    zq = jnp.zeros_like(q4)
    q_br = (jnp.concatenate([q4, zq], axis=0), jnp.concatenate([zq, q4], axis=0))
    qcol = lax.broadcasted_iota(jnp.int32, (1, w4), 1) & (t - 1)
    pos_c = qs + qcol

    sc = jnp.dot(kc_s[...], q4f, precision=HIGHEST, preferred_element_type=F32)
    n_idx = lax.broadcasted_iota(jnp.int32, (n_chunk, 1), 0)
    cvalid = (n_idx * NSA_CMP_STRIDE + (NSA_CMP_BLOCK - 1) <= pos_c) & (n_idx < n_cmp)
    sc = jnp.where(cvalid, sc, NEG_INF)
    e = jnp.exp2(sc - jnp.max(sc, axis=0, keepdims=True))
    p = jnp.where(cvalid, e / jnp.sum(e, axis=0, keepdims=True), 0.0)
    o_cmp = jnp.dot(vcT_s[...], p.astype(BF16), preferred_element_type=F32)
    pg = ((p[:, 0:t] + p[:, t:2 * t]) + p[:, 2 * t:3 * t]) + p[:, 3 * t:4 * t]

    p_slc = jnp.dot(cmat_ref[...], pg, precision=HIGHEST, preferred_element_type=F32)
    blk = lax.broadcasted_iota(jnp.int32, (n_blk, 1), 0)
    cur = lax.shift_right_logical(pos_c[:, :t], NSA_SLC_BLOCK.bit_length() - 1)
    forced = (blk == 0) | ((blk <= cur) & (blk > cur - NSA_N_LOCAL))
    score = jnp.where(blk > cur, NEG_INF, jnp.where(forced, BIG, p_slc))
    cnt = jnp.zeros((n_blk, t), F32)
    for i in range(n_blk):
        ri = score[i:i + 1, :]
        cnt = cnt + jnp.where(ri > score, 1.0, jnp.where((ri == score) & (blk > i), 1.0, 0.0))
    bias = jnp.where(cnt < float(NSA_N_SELECT), 0.0, NEG_INF)
    bias = jnp.concatenate([bias] * N_HEADS, axis=1)
    for r in range(n_blk // blk_per_tile):
        bias_s[r] = bias[r * blk_per_tile:(r + 1) * blk_per_tile, :]

    _flash_reset(m_s, acc_s)
    krow = lax.broadcasted_iota(jnp.int32, (t, 1), 0)
    causal = krow <= qcol
    beyond = krow > qcol

    def scores(br, j):
        off = pl.multiple_of(j * t, t)
        return jnp.dot(k2_ref[0, pl.ds(off, t), :], q_br[br], preferred_element_type=F32)

    def slc_scores(j):
        s = scores(SLC, j).reshape(blk_per_tile, NSA_SLC_BLOCK, w4) + bias_s[j][:, None, :]
        return s.reshape(t, w4)

    def flash(br, s, j):
        off = pl.multiple_of(j * t, t)
        _flash_step_t(s, v2T_ref[0, br, :, pl.ds(off, t)], m_s, acc_s, br)

    s_s[SLC] = slc_scores(0)

    @pl.when(qi < win_tiles)
    def _first_tile_is_in_window():
        s_s[WIN] = scores(WIN, 0)

    def old_body(j, carry):
        s = s_s[SLC]
        s_s[SLC] = slc_scores(j + 1)
        flash(SLC, s, j)
        return carry

    lax.fori_loop(0, jnp.maximum(qi - win_tiles, 0), old_body, 0)

    @pl.when(qi >= win_tiles)
    def _oldest_window_tile():
        j = qi - win_tiles
        s_slc = s_s[SLC]
        s_win = jnp.where(beyond, scores(WIN, j), NEG_INF)
        s_s[SLC] = slc_scores(j + 1)
        s_s[WIN] = scores(WIN, j + 1)
        flash(SLC, s_slc, j)
        flash(WIN, s_win, j)

    def win_body(j, carry):
        s_slc = s_s[SLC]
        s_win = s_s[WIN]
        s_s[SLC] = slc_scores(j + 1)
        s_s[WIN] = scores(WIN, j + 1)
        flash(SLC, s_slc, j)
        flash(WIN, s_win, j)
        return carry

    lax.fori_loop(jnp.maximum(qi - win_tiles + 1, 0), qi, win_body, 0)
    flash(SLC, jnp.where(causal, s_s[SLC], NEG_INF), qi)
    flash(WIN, jnp.where(causal, s_s[WIN], NEG_INF), qi)

    g = jax.nn.sigmoid(gl_ref[0]).T
    outs = []
    for h in range(N_HEADS):
        cols = slice(h * t, (h + 1) * t)
        o_s = _flash_out(acc_s, SLC, cols)
        o_w = _flash_out(acc_s, WIN, cols)
        outs.append(g[h:h + 1, :] * o_cmp[:, cols] + g[N_HEADS + h:N_HEADS + h + 1, :] * o_s
                    + g[2 * N_HEADS + h:2 * N_HEADS + h + 1, :] * o_w)
    y = jnp.concatenate(outs, axis=0).T * _silu(z_ref[0])
    y_ref[0] = y.astype(BF16)


def _nsa_call(qT, kcvc, k2, v2T, gl, z, qk_gain, cmp_pe, w_cmp, t=256):
    b, seq, _ = k2.shape
    n_chunk = seq // NSA_CMP_STRIDE
    n_blk = seq // NSA_SLC_BLOCK
    half_blk = NSA_CMP_BLOCK // 2
    cw = half_blk * 2 * HEAD_DIM
    assert t % NSA_SLC_BLOCK == 0 and NSA_WINDOW % t == 0 and t & (t - 1) == 0

    wk = w_cmp[0].reshape(NSA_CMP_BLOCK, HEAD_DIM, HEAD_DIM)
    wv = w_cmp[1].reshape(NSA_CMP_BLOCK, HEAD_DIM, HEAD_DIM)
    zero = jnp.zeros_like(wk)
    w_all = jnp.concatenate([jnp.concatenate([wk, zero], axis=2),
                             jnp.concatenate([zero, wv], axis=2)], axis=1)
    w_lo = w_all[:half_blk].reshape(cw, 2 * HEAD_DIM)
    w_hi = w_all[half_blk:].reshape(cw, 2 * HEAD_DIM)
    pe_all = jnp.concatenate([cmp_pe[0], cmp_pe[1]], axis=1)
    pe_lo = pe_all[:half_blk].reshape(1, cw)
    pe_hi = pe_all[half_blk:].reshape(1, cw)

    cmp_end = jnp.arange(n_chunk, dtype=jnp.int32) * NSA_CMP_STRIDE + (NSA_CMP_BLOCK - 1)
    cos_c, sin_c = _rope_cos_sin(cmp_end, HEAD_DIM)

    ratio = NSA_SLC_BLOCK // NSA_CMP_STRIDE
    coef = np.convolve(np.ones(ratio), np.ones(NSA_CMP_BLOCK // NSA_CMP_STRIDE))
    cmat = np.zeros((n_blk, n_chunk), np.float32)
    for j in range(n_blk):
        for i, c in enumerate(coef):
            if ratio * j + i < n_chunk - 1:
                cmat[j, ratio * j + i] = c

    tile = pl.BlockSpec((1, t, GROUP_WIDTH), lambda bi, qi: (bi, qi, 0))
    const = lambda shape: pl.BlockSpec(shape, lambda bi, qi: tuple(0 for _ in shape))
    w4 = N_HEADS * t
    return pl.pallas_call(
        functools.partial(_nsa_kernel, seq, t),
        grid=(b, seq // t),
        in_specs=[pl.BlockSpec((1, GROUP_WIDTH, t), lambda bi, qi: (bi, 0, qi)),
                  pl.BlockSpec((1, n_chunk, cw), lambda bi, qi: (bi, 0, 0)),
                  pl.BlockSpec((1, seq, LANES), lambda bi, qi: (bi, 0, 0)),
                  pl.BlockSpec((1, 2, VT_ROWS, seq), lambda bi, qi: (bi, 0, 0, 0)),
                  pl.BlockSpec((1, t, LANES), lambda bi, qi: (bi, qi, 0)),
                  tile,
                  const((1, HEAD_DIM)),
                  const((1, cw)), const((1, cw)), const((cw, 2 * HEAD_DIM)), const((cw, 2 * HEAD_DIM)),
                  const((n_chunk, HEAD_DIM // 2)), const((n_chunk, HEAD_DIM // 2)),
                  const((n_blk, n_chunk))],
        out_specs=tile,
        out_shape=jax.ShapeDtypeStruct((b, seq, GROUP_WIDTH), BF16),
        scratch_shapes=[pltpu.VMEM((n_chunk, HEAD_DIM), F32),
                        pltpu.VMEM((HEAD_DIM, n_chunk), BF16),
                        pltpu.VMEM((seq // t, t // NSA_SLC_BLOCK, w4), F32),
                        pltpu.VMEM((2, w4), F32),
                        pltpu.VMEM((2, VT_ROWS, w4), F32),
                        pltpu.VMEM((2, t, w4), F32)],
        compiler_params=_cparams(("parallel", "arbitrary")),
        name="nsa_attn",
    )(qT, kcvc.reshape(b, n_chunk, cw), k2, v2T, gl, z,
      qk_gain[1][None, :],
      pe_lo, pe_hi, w_lo, w_hi, cos_c, sin_c,
      jnp.asarray(cmat))


def _rope_cos_sin(pos, dim):
    half = dim // 2
    inv_freq = ROPE_THETA ** (-jnp.arange(half, dtype=F32) / half)
    ang = pos.astype(F32)[:, None] * inv_freq[None, :]
    return jnp.cos(ang), jnp.sin(ang)


def _layer(x, mem, layer_idx, tabs_t, norm_gain, w_in, w_out, nsa_qk_gain, nsa_cmp_pe, nsa_w_cmp,
           diff_qk_gain, diff_lambda, diff_subln_gain, mla_cq_gain, mla_ckv_gain, mla_w_uq, mla_w_ukv,
           mla_qk_gain, mem_norm_gain, mem_w_kv, mem_qk_gain):
    b, seq, d = x.shape
    u = _in_proj(x, norm_gain, w_in, nsa_qk_gain, diff_qk_gain, mem_qk_gain, mla_cq_gain, mla_ckv_gain,
                 mla_w_uq, mla_w_ukv, mla_qk_gain, tabs_t)

    y_nsa = _nsa_call(u["nsa_qT"], u["nsa_kcvc"], u["nsa_k2"], u["nsa_v2T"], u["nsa_gl"], u["nsa_z"],
                      nsa_qk_gain, nsa_cmp_pe, nsa_w_cmp)
    lambda_init = 0.8 - 0.6 * math.exp(-0.3 * layer_idx)
    y_diff = _diff_call(u["diff_qT"], u["diff_k"], u["diff_vT"], u["diff_z"], diff_lambda, diff_subln_gain,
                        lambda_init)
    y_mla = _mla_call(u["mla_qT"], u["mla_k"], u["mla_vT"], u["mla_z"])
    y_mem = _mem_call(u["mem_qT"], mem, u["mem_z"], mem_norm_gain, mem_w_kv, mem_qk_gain)

    ys = [y.reshape(b * seq, GROUP_WIDTH) for y in (y_nsa, y_diff, y_mla, y_mem)]
    return _out_proj(x.reshape(b * seq, d), ys, w_out.astype(BF16)).reshape(b, seq, d)


def kernel(x, mem, norm_gain, w_in, w_out, nsa_qk_gain, nsa_cmp_pe, nsa_w_cmp, diff_qk_gain, diff_lambda,
           diff_subln_gain, mla_cq_gain, mla_ckv_gain, mla_w_uq, mla_w_ukv, mla_qk_gain, mem_norm_gain,
           mem_w_kv, mem_qk_gain):
    seq = x.shape[1]
    pos = jnp.arange(seq, dtype=jnp.int32)
    cos32, sin32 = _rope_cos_sin(pos, DIFF_D)
    cos64, sin64 = _rope_cos_sin(pos, HEAD_DIM)
    tabs_t = (cos32.T, sin32.T, cos64.T, sin64.T)
    for l in range(DEPTH):
        x = _layer(x, mem, l, tabs_t, norm_gain[l], w_in[l], w_out[l], nsa_qk_gain[l], nsa_cmp_pe[l],
                   nsa_w_cmp[l], diff_qk_gain[l], diff_lambda[l], diff_subln_gain[l], mla_cq_gain[l],
                   mla_ckv_gain[l], mla_w_uq[l], mla_w_ukv[l], mla_qk_gain[l], mem_norm_gain[l], mem_w_kv[l],
                   mem_qk_gain[l])
    return x
```

```python
import functools
import math

import numpy as np
import jax
import jax.numpy as jnp
from jax import lax
from jax.experimental import pallas as pl
from jax.experimental.pallas import tpu as pltpu

F32 = jnp.float32
BF16 = jnp.bfloat16
HIGHEST = lax.Precision.HIGHEST

D_MODEL = 1024
DEPTH = 2
N_HEADS = 4
HEAD_DIM = 64
GROUP_WIDTH = N_HEADS * HEAD_DIM
ROPE_THETA = 10000.0
EPS = 1e-6
NEG_INF = -1e30
BIG = 1e30
LOG2E = 1.4426950408889634

NSA_CMP_BLOCK = 32
NSA_CMP_STRIDE = 16
NSA_SLC_BLOCK = 64
NSA_N_SELECT = 16
NSA_N_LOCAL = 2
NSA_WINDOW = 512
DIFF_D = HEAD_DIM // 2
MLA_Q_RANK = 256
MLA_KV_RANK = 128
MLA_NOPE = 64
MLA_ROPE = 32
MLA_QK = MLA_NOPE + MLA_ROPE

VMEM_LIMIT_BYTES = 48 * 1024 * 1024
LANES = 128
KEY_TILE = 256
VT_ROWS = 80


def _cparams(sem):
    return pltpu.CompilerParams(dimension_semantics=sem, vmem_limit_bytes=VMEM_LIMIT_BYTES)


def _group_ones(width, group):
    g = np.arange(width) // group
    return jnp.asarray(g[:, None] == g[None, :], dtype=BF16)


def _group_rsqrt(x, gmat, denom):
    sq = x * x
    hi = sq.astype(BF16)
    lo = (sq - hi.astype(F32)).astype(BF16)
    ss = jnp.dot(hi, gmat, preferred_element_type=F32) + jnp.dot(lo, gmat, preferred_element_type=F32)
    return lax.rsqrt(ss / denom + EPS)


def _norm_rope_t(u_t, groups, dim, gain, cos=None, sin=None):
    x = u_t.reshape(groups, dim, u_t.shape[-1])
    ms = jnp.mean(x * x, axis=1, keepdims=True)
    x = x * lax.rsqrt(ms + EPS) * gain
    if cos is not None:
        half = dim // 2
        x1, x2 = x[:, :half], x[:, half:]
        x = jnp.concatenate([x1 * cos - x2 * sin, x2 * cos + x1 * sin], axis=1)
    return x.reshape(groups * dim, u_t.shape[-1])


def _silu(z):
    z = z.astype(F32)
    return z * jax.nn.sigmoid(z)


def _ones_rows(n):
    r = lax.broadcasted_iota(jnp.int32, (VT_ROWS - HEAD_DIM, n), 0)
    return jnp.where(r == 0, 1.0, 0.0).astype(BF16)


def _split_bf16(x):
    hi = x.astype(BF16)
    return hi, (x - hi.astype(F32)).astype(BF16)


def _dot_x3(a_hi, a_lo, b_hi, b_lo):
    dot = lambda a, b: jnp.dot(a, b, preferred_element_type=F32)
    return dot(a_hi, b_hi) + dot(a_hi, b_lo) + dot(a_lo, b_hi)


def _flash_step_t(s, v_t, m_ref, acc_ref, i, cols=slice(None)):
    m_old = m_ref[i:i + 1, cols]
    m_new = jnp.maximum(m_old, jnp.max(s, axis=0, keepdims=True))
    alpha = jnp.exp2(m_old - m_new)
    p = jnp.exp2(s - m_new).astype(BF16)
    acc_ref[i, :, cols] = alpha * acc_ref[i, :, cols] + jnp.dot(v_t, p, preferred_element_type=F32)
    m_ref[i:i + 1, cols] = m_new


def _flash_reset(m_ref, acc_ref):
    m_ref[...] = jnp.full(m_ref.shape, NEG_INF, F32)
    acc_ref[...] = jnp.zeros(acc_ref.shape, F32)


def _flash_out(acc_ref, i, cols=slice(None)):
    return acc_ref[i, 0:HEAD_DIM, cols] / acc_ref[i, HEAD_DIM:HEAD_DIM + 1, cols]


def _causal_flash_t(n, k_tile, q_t, v_tile, qi, tq, tk, m_ref, acc_ref, s_ref):
    _flash_reset(m_ref, acc_ref)
    n_diag = tq // tk

    all_q = slice(0, tq)

    def scores(i, off, cols):
        return jnp.dot(k_tile(i, off), q_t(i)[:, cols], preferred_element_type=F32)

    def step(off, cols, off_next, cols_next, mask):
        for i in range(n):
            s = s_ref[i, :, cols]
            if off_next is not None:
                s_ref[i, :, cols_next] = scores(i, off_next, cols_next)
            if mask is not None:
                s = jnp.where(mask, s, NEG_INF)
            _flash_step_t(s, v_tile(i, off), m_ref, acc_ref, i, cols)

    for i in range(n):
        s_ref[i] = scores(i, 0, all_q)

    def body(j, carry):
        step(pl.multiple_of(j * tk, tk), all_q, pl.multiple_of(j * tk + tk, tk), all_q, None)
        return carry

    lax.fori_loop(0, qi * n_diag, body, 0)
    krow = lax.broadcasted_iota(jnp.int32, (tk, tq), 0)
    qcol = lax.broadcasted_iota(jnp.int32, (tk, tq), 1)
    causal = krow <= qcol
    diag_cols = [slice(d * tk, tq) for d in range(n_diag)]
    for d in range(n_diag):
        off = pl.multiple_of(qi * tq + d * tk, tk)
        last = d + 1 == n_diag
        off_next = None if last else pl.multiple_of(qi * tq + (d + 1) * tk, tk)
        step(off, diag_cols[d], off_next, None if last else diag_cols[d + 1], causal[:, 0:tq - d * tk])


_ROW_SEGS = (("nsa_kcvc", 128), ("nsa_gl", 128), ("nsa_z", 256), ("diff_z", 256), ("mla_z", 256),
             ("mem_z", 256))
_COL_SEGS = (("nsa_q", 256), ("diff_q", 256), ("mem_q", 256), ("diff_k", 256), ("nsa_k2", 128),
             ("diff_v", 256), ("nsa_v2", 128), ("mla_cq", 256), ("mla_ckv", 128), ("mla_kr", 32))


def _seg_offsets(segs):
    out, off = {}, 0
    for name, w in segs:
        out[name] = (off, w)
        off += w
    return out, off


_ROW_OFF, _ROW_W = _seg_offsets(_ROW_SEGS)
_COL_OFF, _COL_W = _seg_offsets(_COL_SEGS)


def _split_w_in(w_in):
    c = lambda a, b: w_in[:, a:b]
    z = lambda n: jnp.zeros((w_in.shape[0], n), w_in.dtype)
    w_row = jnp.concatenate([
        c(256, 384),
        c(640, 652), z(116),
        c(652, 908), c(1676, 1932), c(2348, 2604), c(2860, 3116),
    ], axis=1)
    w_col = jnp.concatenate([
        c(0, 256), c(908, 1164), c(2604, 2860),
        c(1164, 1420),
        c(384, 448), c(512, 576),
        c(1420, 1676),
        c(448, 512), c(576, 640),
        c(1932, 2188), c(2188, 2316), c(2316, 2348),
    ], axis=1)
    return w_row.astype(BF16), w_col.T.astype(BF16)


def _mla_head_norm_t(x_t, gain):
    ms = jnp.sum(x_t * x_t, axis=0, keepdims=True) / float(MLA_QK)
    return x_t * lax.rsqrt(ms + EPS) * gain


def _rope_t(x_t, cos, sin):
    half = x_t.shape[0] // 2
    x1, x2 = x_t[:half], x_t[half:]
    return jnp.concatenate([x1 * cos - x2 * sin, x2 * cos + x1 * sin], axis=0)


def _in_proj_kernel(x_ref, g_ref, wrow_ref, wcol_ref, nqg_ref, dqg_ref, dkg_ref, k2g_ref, mqg_ref,
                    cqg_ref, ckvg_ref, wuq_ref, wuk_ref, wuv_ref, lqg_ref, lkg_ref,
                    c32_ref, s32_ref, c64_ref, s64_ref,
                    kcvc_o, gl_o, nz_o, dz_o, mz_o, ez_o, dk_o, k2_o,
                    nq_o, dq_o, mq_o, dv_o, v2_o, lq_o, lk_o, lv_o, kcvc_s):
    x = x_ref[...]
    tm = x.shape[0]
    ms = jnp.mean(x * x, axis=-1, keepdims=True)
    h = x * lax.rsqrt(ms + EPS) * g_ref[...]
    hb = h.astype(BF16)
    h_t = h.T.astype(BF16)

    def row(name):
        off, w = _ROW_OFF[name]
        return jnp.dot(hb, wrow_ref[:, off:off + w], preferred_element_type=F32)

    def col(name):
        off, w = _COL_OFF[name]
        return jnp.dot(wcol_ref[off:off + w, :], h_t, preferred_element_type=F32)

    kcvc_s[...] = row("nsa_kcvc")
    n_chunk_rows = tm // NSA_CMP_STRIDE
    for tok in range(NSA_CMP_STRIDE):
        kcvc_o[0, :, tok * LANES:(tok + 1) * LANES] = kcvc_s[pl.ds(tok, n_chunk_rows, stride=NSA_CMP_STRIDE), :]
    gl_o[...] = row("nsa_gl")
    nz_o[...] = row("nsa_z").astype(BF16)
    dz_o[...] = row("diff_z").astype(BF16)
    mz_o[...] = row("mla_z").astype(BF16)
    ez_o[...] = row("mem_z").astype(BF16)

    c32, s32, c64, s64 = c32_ref[...], s32_ref[...], c64_ref[...], s64_ref[...]
    g3 = lambda ref, groups: ref[...].reshape(groups, -1, 1)
    nq_o[0] = _norm_rope_t(col("nsa_q"), N_HEADS, HEAD_DIM, g3(nqg_ref, 1), c64, s64) * (HEAD_DIM ** -0.5 * LOG2E)
    dq = _norm_rope_t(col("diff_q"), 2 * N_HEADS, DIFF_D, g3(dqg_ref, 1), c32, s32) * (DIFF_D ** -0.5 * LOG2E)
    dq_o[0] = dq.astype(BF16)
    mq_o[0] = (_norm_rope_t(col("mem_q"), N_HEADS, HEAD_DIM, g3(mqg_ref, 1)) * (HEAD_DIM ** -0.5 * LOG2E)).astype(BF16)
    dk_o[...] = _norm_rope_t(col("diff_k"), 2 * N_HEADS, DIFF_D, g3(dkg_ref, 1), c32, s32).T.astype(BF16)
    k2_o[...] = _norm_rope_t(col("nsa_k2"), 2, HEAD_DIM, g3(k2g_ref, 2), c64, s64).T.astype(BF16)
    ones = _ones_rows(tm)
    dv = col("diff_v")
    for hd in range(N_HEADS):
        dv_o[0, hd, 0:HEAD_DIM, :] = dv[hd * HEAD_DIM:(hd + 1) * HEAD_DIM].astype(BF16)
        dv_o[0, hd, HEAD_DIM:VT_ROWS, :] = ones
    v2 = col("nsa_v2")
    for br in range(2):
        v2_o[0, br, 0:HEAD_DIM, :] = v2[br * HEAD_DIM:(br + 1) * HEAD_DIM].astype(BF16)
        v2_o[0, br, HEAD_DIM:VT_ROWS, :] = ones

    def latent(name, gain_ref):
        c = col(name)
        return (c * lax.rsqrt(jnp.mean(c * c, axis=0, keepdims=True) + EPS) * gain_ref[...]).astype(BF16)

    rope_rows = slice(MLA_NOPE, MLA_QK)
    qa = jnp.dot(wuq_ref[...], latent("mla_cq", cqg_ref), preferred_element_type=F32)
    lqg = lqg_ref[...] * (MLA_QK ** -0.5 * LOG2E)
    ckv = latent("mla_ckv", ckvg_ref)
    kn = jnp.dot(wuk_ref[...], ckv, preferred_element_type=F32)
    kr = _rope_t(col("mla_kr"), c32, s32)
    zpad = jnp.zeros((LANES - MLA_QK, tm), F32)
    lv = jnp.dot(wuv_ref[...], ckv, preferred_element_type=F32)
    for hd in range(N_HEADS):
        q_h = qa[hd * LANES:(hd + 1) * LANES]
        q_h = jnp.concatenate([q_h[:MLA_NOPE], _rope_t(q_h[rope_rows], c32, s32), q_h[MLA_QK:]], axis=0)
        lq_o[0, hd * LANES:(hd + 1) * LANES, :] = _mla_head_norm_t(q_h, lqg).astype(BF16)
        k_h = jnp.concatenate([kn[hd * LANES:hd * LANES + MLA_NOPE], kr, zpad], axis=0)
        lk_o[0, hd] = _mla_head_norm_t(k_h, lkg_ref[...]).T.astype(BF16)
        lv_o[0, hd, 0:HEAD_DIM, :] = lv[hd * HEAD_DIM:(hd + 1) * HEAD_DIM].astype(BF16)
        lv_o[0, hd, HEAD_DIM:VT_ROWS, :] = ones


def _in_proj(x, gain, w_in, nsa_qk_gain, diff_qk_gain, mem_qk_gain, mla_cq_gain, mla_ckv_gain, mla_w_uq,
             mla_w_ukv, mla_qk_gain, tabs_t, tm=512):
    b, seq, d = x.shape
    n = b * seq
    nb = seq // tm
    w_row, w_col = _split_w_in(w_in)
    c32, s32, c64, s64 = tabs_t
    npad = LANES - MLA_QK
    uq = jnp.pad(mla_w_uq.reshape(MLA_Q_RANK, N_HEADS, MLA_QK), ((0, 0), (0, 0), (0, npad)))
    w_uq_t = uq.reshape(MLA_Q_RANK, N_HEADS * LANES).T.astype(BF16)
    ukv = mla_w_ukv.reshape(MLA_KV_RANK, N_HEADS, MLA_NOPE + HEAD_DIM)
    uk = jnp.pad(ukv[:, :, :MLA_NOPE], ((0, 0), (0, 0), (0, LANES - MLA_NOPE)))
    w_uk_t = uk.reshape(MLA_KV_RANK, N_HEADS * LANES).T.astype(BF16)
    w_uv_t = ukv[:, :, MLA_NOPE:].reshape(MLA_KV_RANK, GROUP_WIDTH).T.astype(BF16)
    pad_gain = lambda g: jnp.pad(g, (0, npad))[:, None]

    rowspec = lambda w: pl.BlockSpec((tm, w), lambda i: (i, 0))
    colspec = lambda r: pl.BlockSpec((1, r, tm), lambda i: (i // nb, 0, i % nb))
    vtspec = lambda c: pl.BlockSpec((1, c, VT_ROWS, tm), lambda i: (i // nb, 0, 0, i % nb))
    const = lambda shape: pl.BlockSpec(shape, lambda i: tuple(0 for _ in shape))
    tabspec = lambda r: pl.BlockSpec((r, tm), lambda i: (0, i % nb))
    row_out = lambda w, dt: jax.ShapeDtypeStruct((n, w), dt)
    col_out = lambda r, dt: jax.ShapeDtypeStruct((b, r, seq), dt)
    vt_out = lambda c: jax.ShapeDtypeStruct((b, c, VT_ROWS, seq), BF16)
    outs = pl.pallas_call(
        _in_proj_kernel,
        grid=(n // tm,),
        in_specs=[rowspec(d), const((1, d)), const((d, _ROW_W)), const((_COL_W, d)),
                  const((HEAD_DIM, 1)), const((DIFF_D, 1)), const((DIFF_D, 1)), const((2 * HEAD_DIM, 1)),
                  const((HEAD_DIM, 1)),
                  const((MLA_Q_RANK, 1)), const((MLA_KV_RANK, 1)),
                  const((N_HEADS * LANES, MLA_Q_RANK)), const((N_HEADS * LANES, MLA_KV_RANK)),
                  const((GROUP_WIDTH, MLA_KV_RANK)), const((LANES, 1)), const((LANES, 1)),
                  tabspec(DIFF_D // 2), tabspec(DIFF_D // 2), tabspec(HEAD_DIM // 2), tabspec(HEAD_DIM // 2)],
        out_specs=[pl.BlockSpec((1, tm // NSA_CMP_STRIDE, NSA_CMP_STRIDE * LANES), lambda i: (i // nb, i % nb, 0)),
                   rowspec(128), rowspec(256), rowspec(256), rowspec(256), rowspec(256),
                   rowspec(256), rowspec(128),
                   colspec(256), colspec(256), colspec(256), vtspec(N_HEADS), vtspec(2),
                   colspec(N_HEADS * LANES),
                   pl.BlockSpec((1, N_HEADS, tm, LANES), lambda i: (i // nb, 0, i % nb, 0)),
                   vtspec(N_HEADS)],
        out_shape=[jax.ShapeDtypeStruct((b, seq // NSA_CMP_STRIDE, NSA_CMP_STRIDE * LANES), F32),
                   row_out(128, F32), row_out(256, BF16), row_out(256, BF16),
                   row_out(256, BF16), row_out(256, BF16), row_out(256, BF16), row_out(128, BF16),
                   col_out(256, F32), col_out(256, BF16), col_out(256, BF16), vt_out(N_HEADS), vt_out(2),
                   col_out(N_HEADS * LANES, BF16),
                   jax.ShapeDtypeStruct((b, N_HEADS, seq, LANES), BF16),
                   vt_out(N_HEADS)],
        scratch_shapes=[pltpu.VMEM((tm, LANES), F32)],
        compiler_params=_cparams(("parallel",)),
        name="in_proj",
    )(x.reshape(n, d), gain[None, :], w_row, w_col,
      nsa_qk_gain[0][:, None], diff_qk_gain[0][:, None], diff_qk_gain[1][:, None],
      jnp.concatenate([nsa_qk_gain[2], nsa_qk_gain[3]])[:, None], mem_qk_gain[0][:, None],
      mla_cq_gain[:, None], mla_ckv_gain[:, None], w_uq_t, w_uk_t, w_uv_t,
      pad_gain(mla_qk_gain[0]), pad_gain(mla_qk_gain[1]),
      c32, s32, c64, s64)
    names = ("nsa_kcvc", "nsa_gl", "nsa_z", "diff_z", "mla_z", "mem_z", "diff_k", "nsa_k2",
             "nsa_qT", "diff_qT", "mem_qT", "diff_vT", "nsa_v2T", "mla_qT", "mla_k", "mla_vT")
    u = dict(zip(names, outs))
    for name in names[1:8]:
        u[name] = u[name].reshape(b, seq, -1)
    return u


def _out_proj_kernel(x_ref, y0_ref, y1_ref, y2_ref, y3_ref, w_ref, o_ref):
    acc = x_ref[...]
    for g, y_ref in enumerate((y0_ref, y1_ref, y2_ref, y3_ref)):
        acc = acc + jnp.dot(y_ref[...], w_ref[g * GROUP_WIDTH:(g + 1) * GROUP_WIDTH, :],
                            preferred_element_type=F32)
    o_ref[...] = acc


def _out_proj(x2, ys, w_out_b, tm=512):
    n = x2.shape[0]
    yspec = pl.BlockSpec((tm, GROUP_WIDTH), lambda i: (i, 0))
    return pl.pallas_call(
        _out_proj_kernel,
        grid=(n // tm,),
        in_specs=[pl.BlockSpec((tm, D_MODEL), lambda i: (i, 0)), yspec, yspec, yspec, yspec,
                  pl.BlockSpec((D_MODEL, D_MODEL), lambda i: (0, 0))],
        out_specs=pl.BlockSpec((tm, D_MODEL), lambda i: (i, 0)),
        out_shape=jax.ShapeDtypeStruct((n, D_MODEL), F32),
        compiler_params=_cparams(("parallel",)),
        name="out_proj",
    )(x2, *ys, w_out_b)


def _diff_kernel(lambda_init, t, qT_ref, k_ref, vT_ref, z_ref, lam_ref, sg_ref, y_ref, m_s, acc_s, s_s):
    qi = pl.program_id(1)
    n_maps = 2 * N_HEADS
    qT = qT_ref[0]
    frow = lax.broadcasted_iota(jnp.int32, (GROUP_WIDTH, 1), 0) // DIFF_D
    qT_map = [jnp.where(frow == i, qT, jnp.zeros_like(qT)) for i in range(n_maps)]

    _causal_flash_t(
        n_maps,
        lambda i, off: k_ref[0, pl.ds(off, KEY_TILE), :],
        lambda i: qT_map[i],
        lambda i, off: vT_ref[0, i // 2, :, pl.ds(off, KEY_TILE)],
        qi, t, KEY_TILE, m_s, acc_s, s_s)

    lam = lam_ref[...]
    lmbda = (jnp.exp(jnp.sum(lam[0:1] * lam[1:2], axis=-1, keepdims=True))
             - jnp.exp(jnp.sum(lam[2:3] * lam[3:4], axis=-1, keepdims=True)) + lambda_init)
    outs = []
    for h in range(N_HEADS):
        d = _flash_out(acc_s, 2 * h) - lmbda * _flash_out(acc_s, 2 * h + 1)
        ms = jnp.mean(d * d, axis=0, keepdims=True)
        outs.append(d * lax.rsqrt(ms + EPS) * sg_ref[...] * (1.0 - lambda_init))
    y = jnp.concatenate(outs, axis=0).T * _silu(z_ref[0])
    y_ref[0] = y.astype(BF16)


def _diff_call(qT, k, vT, z, lam, subln_gain, lambda_init, t=512):
    b, seq, _ = k.shape
    const = lambda shape: pl.BlockSpec(shape, lambda bi, qi: tuple(0 for _ in shape))
    tile = pl.BlockSpec((1, t, GROUP_WIDTH), lambda bi, qi: (bi, qi, 0))
    return pl.pallas_call(
        functools.partial(_diff_kernel, lambda_init, t),
        grid=(b, seq // t),
        in_specs=[pl.BlockSpec((1, GROUP_WIDTH, t), lambda bi, qi: (bi, 0, qi)),
                  pl.BlockSpec((1, seq, GROUP_WIDTH), lambda bi, qi: (bi, 0, 0)),
                  pl.BlockSpec((1, N_HEADS, VT_ROWS, seq), lambda bi, qi: (bi, 0, 0, 0)),
                  tile, const((4, DIFF_D)), const((HEAD_DIM, 1))],
        out_specs=tile,
        out_shape=jax.ShapeDtypeStruct((b, seq, GROUP_WIDTH), BF16),
        scratch_shapes=[pltpu.VMEM((2 * N_HEADS, t), F32),
                        pltpu.VMEM((2 * N_HEADS, VT_ROWS, t), F32),
                        pltpu.VMEM((2 * N_HEADS, KEY_TILE, t), F32)],
        compiler_params=_cparams(("parallel", "parallel")),
        name="diff_attn",
    )(qT, k, vT, z, lam, subln_gain[:, None])


def _mla_kernel(t, qT_ref, k_ref, vT_ref, z_ref, y_ref, m_s, acc_s, s_s):
    qi = pl.program_id(1)
    qT = qT_ref[0]
    _causal_flash_t(
        N_HEADS,
        lambda h, off: k_ref[0, h, pl.ds(off, KEY_TILE), :],
        lambda h: qT[h * LANES:(h + 1) * LANES],
        lambda h, off: vT_ref[0, h, :, pl.ds(off, KEY_TILE)],
        qi, t, KEY_TILE, m_s, acc_s, s_s)
    y = jnp.concatenate([_flash_out(acc_s, h) for h in range(N_HEADS)], axis=0).T * _silu(z_ref[0])
    y_ref[0] = y.astype(BF16)


def _mla_call(qT, k, vT, z, t=512):
    b, _, seq, _ = k.shape
    tile = pl.BlockSpec((1, t, GROUP_WIDTH), lambda bi, qi: (bi, qi, 0))
    return pl.pallas_call(
        functools.partial(_mla_kernel, t),
        grid=(b, seq // t),
        in_specs=[pl.BlockSpec((1, N_HEADS * LANES, t), lambda bi, qi: (bi, 0, qi)),
                  pl.BlockSpec((1, N_HEADS, seq, LANES), lambda bi, qi: (bi, 0, 0, 0)),
                  pl.BlockSpec((1, N_HEADS, VT_ROWS, seq), lambda bi, qi: (bi, 0, 0, 0)),
                  tile],
        out_specs=tile,
        out_shape=jax.ShapeDtypeStruct((b, seq, GROUP_WIDTH), BF16),
        scratch_shapes=[pltpu.VMEM((N_HEADS, t), F32),
                        pltpu.VMEM((N_HEADS, VT_ROWS, t), F32),
                        pltpu.VMEM((N_HEADS, KEY_TILE, t), F32)],
        compiler_params=_cparams(("parallel", "parallel")),
        name="mla_attn",
    )(qT, k, vT, z)


def _mem_kernel(qT_ref, mem_ref, z_ref, mg_ref, wk_ref, wvT_ref, kg_ref, gm_ref,
                y_ref, k_s, vT_s, m_s, acc_s):
    qi = pl.program_id(1)
    nt = (((1,), (1,)), ((), ()))
    m_len = mem_ref.shape[1]

    @pl.when(qi == 0)
    def _prep():
        mem = mem_ref[0]
        ms = jnp.mean(mem * mem, axis=-1, keepdims=True)
        mb = (mem * lax.rsqrt(ms + EPS) * mg_ref[...]).astype(BF16)
        k = jnp.dot(mb, wk_ref[...], preferred_element_type=F32)
        k_s[...] = (k * _group_rsqrt(k, gm_ref[...], float(HEAD_DIM)) * kg_ref[...]).astype(BF16)
        vT = lax.dot_general(wvT_ref[...], mb, nt, preferred_element_type=F32)
        for h in range(N_HEADS):
            vT_s[h, 0:HEAD_DIM, :] = vT[h * HEAD_DIM:(h + 1) * HEAD_DIM].astype(BF16)
            vT_s[h, HEAD_DIM:VT_ROWS, :] = _ones_rows(m_len)

    qT = qT_ref[0]
    frow = lax.broadcasted_iota(jnp.int32, (GROUP_WIDTH, 1), 0) // HEAD_DIM
    _flash_reset(m_s, acc_s)
    ss = [jnp.dot(k_s[...], jnp.where(frow == h, qT, jnp.zeros_like(qT)), preferred_element_type=F32)
          for h in range(N_HEADS)]
    for h in range(N_HEADS):
        _flash_step_t(ss[h], vT_s[h], m_s, acc_s, h)
    y = jnp.concatenate([_flash_out(acc_s, h) for h in range(N_HEADS)], axis=0).T * _silu(z_ref[0])
    y_ref[0] = y.astype(BF16)


def _mem_call(qT, mem, z, mem_gain, w_kv, qk_gain, t=512):
    b, m_len, _ = mem.shape
    seq = qT.shape[2]
    w_k = w_kv[:, :GROUP_WIDTH].astype(BF16)
    w_vT = w_kv[:, GROUP_WIDTH:].T.astype(BF16)
    tile = pl.BlockSpec((1, t, GROUP_WIDTH), lambda bi, qi: (bi, qi, 0))
    const = lambda shape: pl.BlockSpec(shape, lambda bi, qi: tuple(0 for _ in shape))
    return pl.pallas_call(
        _mem_kernel,
        grid=(b, seq // t),
        in_specs=[pl.BlockSpec((1, GROUP_WIDTH, t), lambda bi, qi: (bi, 0, qi)),
                  pl.BlockSpec((1, m_len, D_MODEL), lambda bi, qi: (bi, 0, 0)), tile,
                  const((1, D_MODEL)), const((D_MODEL, GROUP_WIDTH)), const((GROUP_WIDTH, D_MODEL)),
                  const((1, GROUP_WIDTH)), const((GROUP_WIDTH, GROUP_WIDTH))],
        out_specs=tile,
        out_shape=jax.ShapeDtypeStruct((b, seq, GROUP_WIDTH), BF16),
        scratch_shapes=[pltpu.VMEM((m_len, GROUP_WIDTH), BF16),
                        pltpu.VMEM((N_HEADS, VT_ROWS, m_len), BF16),
                        pltpu.VMEM((N_HEADS, t), F32),
                        pltpu.VMEM((N_HEADS, VT_ROWS, t), F32)],
        compiler_params=_cparams(("parallel", "arbitrary")),
        name="mem_attn",
    )(qT, mem, z, mem_gain[None, :], w_k, w_vT, jnp.tile(qk_gain[1], N_HEADS)[None, :],
      _group_ones(GROUP_WIDTH, HEAD_DIM))


def _nsa_kernel(seq, t,
                qT_ref, kcvc_ref, k2_ref, v2T_ref, gl_ref, z_ref,
                cg_ref, pelo_ref, pehi_ref, wlo_ref, whi_ref, cosc_ref, sinc_ref,
                cmat_ref,
                y_ref, kc_s, vcT_s, bias_s, m_s, acc_s, s_s):
    qi = pl.program_id(1)
    half = HEAD_DIM // 2
    n_chunk = seq // NSA_CMP_STRIDE
    n_cmp = n_chunk - 1
    n_blk = seq // NSA_SLC_BLOCK
    blk_per_tile = t // NSA_SLC_BLOCK
    win_tiles = NSA_WINDOW // t
    w4 = N_HEADS * t
    SLC, WIN = 0, 1

    @pl.when(qi == 0)
    def _prep():
        ch = kcvc_ref[0]
        a = _dot_x3(*_split_bf16(ch + pelo_ref[...]), wlo_ref[0], wlo_ref[1])
        bm = _dot_x3(*_split_bf16(ch + pehi_ref[...]), whi_ref[0], whi_ref[1])
        cmp = a + pltpu.roll(bm, n_chunk - 1, 0)
        kc = cmp[:, :HEAD_DIM]
        ms = jnp.mean(kc * kc, axis=-1, keepdims=True)
        kc = kc * lax.rsqrt(ms + EPS) * cg_ref[...]
        x1, x2 = kc[:, :half], kc[:, half:]
        c, s = cosc_ref[...], sinc_ref[...]
        kc_hi, kc_lo = _split_bf16(jnp.concatenate([x1 * c - x2 * s, x2 * c + x1 * s], axis=-1))
        kc_s[0] = kc_hi
        kc_s[1] = kc_lo
        vcT_s[...] = cmp.T[HEAD_DIM:].astype(BF16)

    qs = qi * t
    qT = qT_ref[0]
    q4f = jnp.concatenate([qT[h * HEAD_DIM:(h + 1) * HEAD_DIM] for h in range(N_HEADS)], axis=1)
    q4 = q4f.astype(BF16)
    zq = jnp.zeros_like(q4)
    q_br = (jnp.concatenate([q4, zq], axis=0), jnp.concatenate([zq, q4], axis=0))
    qcol = lax.broadcasted_iota(jnp.int32, (1, w4), 1) & (t - 1)
    pos_c = qs + qcol

    sc = _dot_x3(kc_s[0], kc_s[1], q4, (q4f - q4.astype(F32)).astype(BF16))
    n_idx = lax.broadcasted_iota(jnp.int32, (n_chunk, 1), 0)
    cvalid = (n_idx * NSA_CMP_STRIDE + (NSA_CMP_BLOCK - 1) <= pos_c) & (n_idx < n_cmp)
    sc = jnp.where(cvalid, sc, NEG_INF)
    e = jnp.exp2(sc - jnp.max(sc, axis=0, keepdims=True))
    p = jnp.where(cvalid, e / jnp.sum(e, axis=0, keepdims=True), 0.0)
    o_cmp = jnp.dot(vcT_s[...], p.astype(BF16), preferred_element_type=F32)
    pg = ((p[:, 0:t] + p[:, t:2 * t]) + p[:, 2 * t:3 * t]) + p[:, 3 * t:4 * t]

    p_slc = jnp.dot(cmat_ref[...], pg, precision=HIGHEST, preferred_element_type=F32)
    blk = lax.broadcasted_iota(jnp.int32, (n_blk, 1), 0)
    cur = lax.shift_right_logical(pos_c[:, :t], NSA_SLC_BLOCK.bit_length() - 1)
    forced = (blk == 0) | ((blk <= cur) & (blk > cur - NSA_N_LOCAL))
    score = jnp.where(blk > cur, NEG_INF, jnp.where(forced, BIG, p_slc))
    cnt = jnp.zeros((n_blk, t), F32)
    for i in range(n_blk):
        ri = score[i:i + 1, :]
        cnt = cnt + jnp.where(ri > score, 1.0, jnp.where((ri == score) & (blk > i), 1.0, 0.0))
    bias = jnp.where(cnt < float(NSA_N_SELECT), 0.0, NEG_INF)
    bias = jnp.concatenate([bias] * N_HEADS, axis=1)
    for r in range(n_blk // blk_per_tile):
        bias_s[r] = bias[r * blk_per_tile:(r + 1) * blk_per_tile, :]

    _flash_reset(m_s, acc_s)
    krow = lax.broadcasted_iota(jnp.int32, (t, 1), 0)
    causal = krow <= qcol
    beyond = krow > qcol

    def scores(br, j):
        off = pl.multiple_of(j * t, t)
        return jnp.dot(k2_ref[0, pl.ds(off, t), :], q_br[br], preferred_element_type=F32)

    def slc_scores(j):
        s = scores(SLC, j).reshape(blk_per_tile, NSA_SLC_BLOCK, w4) + bias_s[j][:, None, :]
        return s.reshape(t, w4)

    def flash(br, s, j):
        off = pl.multiple_of(j * t, t)
        _flash_step_t(s, v2T_ref[0, br, :, pl.ds(off, t)], m_s, acc_s, br)

    s_s[SLC] = slc_scores(0)

    @pl.when(qi < win_tiles)
    def _first_tile_is_in_window():
        s_s[WIN] = scores(WIN, 0)

    def old_body(j, carry):
        s = s_s[SLC]
        s_s[SLC] = slc_scores(j + 1)
        flash(SLC, s, j)
        return carry

    lax.fori_loop(0, jnp.maximum(qi - win_tiles, 0), old_body, 0)

    @pl.when(qi >= win_tiles)
    def _oldest_window_tile():
        j = qi - win_tiles
        s_slc = s_s[SLC]
        s_win = jnp.where(beyond, scores(WIN, j), NEG_INF)
        s_s[SLC] = slc_scores(j + 1)
        s_s[WIN] = scores(WIN, j + 1)
        flash(SLC, s_slc, j)
        flash(WIN, s_win, j)

    def win_body(j, carry):
        s_slc = s_s[SLC]
        s_win = s_s[WIN]
        s_s[SLC] = slc_scores(j + 1)
        s_s[WIN] = scores(WIN, j + 1)
        flash(SLC, s_slc, j)
        flash(WIN, s_win, j)
        return carry

    lax.fori_loop(jnp.maximum(qi - win_tiles + 1, 0), qi, win_body, 0)
    flash(SLC, jnp.where(causal, s_s[SLC], NEG_INF), qi)
    flash(WIN, jnp.where(causal, s_s[WIN], NEG_INF), qi)

    g = jax.nn.sigmoid(gl_ref[0]).T
    outs = []
    for h in range(N_HEADS):
        cols = slice(h * t, (h + 1) * t)
        o_s = _flash_out(acc_s, SLC, cols)
        o_w = _flash_out(acc_s, WIN, cols)
        outs.append(g[h:h + 1, :] * o_cmp[:, cols] + g[N_HEADS + h:N_HEADS + h + 1, :] * o_s
                    + g[2 * N_HEADS + h:2 * N_HEADS + h + 1, :] * o_w)
    y = jnp.concatenate(outs, axis=0).T * _silu(z_ref[0])
    y_ref[0] = y.astype(BF16)


def _nsa_call(qT, kcvc, k2, v2T, gl, z, qk_gain, cmp_pe, w_cmp, t=256):
    b, seq, _ = k2.shape
    n_chunk = seq // NSA_CMP_STRIDE
    n_blk = seq // NSA_SLC_BLOCK
    half_blk = NSA_CMP_BLOCK // 2
    cw = half_blk * 2 * HEAD_DIM
    assert t % NSA_SLC_BLOCK == 0 and NSA_WINDOW % t == 0 and t & (t - 1) == 0

    wk = w_cmp[0].reshape(NSA_CMP_BLOCK, HEAD_DIM, HEAD_DIM)
    wv = w_cmp[1].reshape(NSA_CMP_BLOCK, HEAD_DIM, HEAD_DIM)
    zero = jnp.zeros_like(wk)
    w_all = jnp.concatenate([jnp.concatenate([wk, zero], axis=2),
                             jnp.concatenate([zero, wv], axis=2)], axis=1)
    hi_lo = lambda w: jnp.stack([w.astype(BF16), (w - w.astype(BF16).astype(F32)).astype(BF16)])
    w_lo = hi_lo(w_all[:half_blk].reshape(cw, 2 * HEAD_DIM))
    w_hi = hi_lo(w_all[half_blk:].reshape(cw, 2 * HEAD_DIM))
    pe_all = jnp.concatenate([cmp_pe[0], cmp_pe[1]], axis=1)
    pe_lo = pe_all[:half_blk].reshape(1, cw)
    pe_hi = pe_all[half_blk:].reshape(1, cw)

    cmp_end = jnp.arange(n_chunk, dtype=jnp.int32) * NSA_CMP_STRIDE + (NSA_CMP_BLOCK - 1)
    cos_c, sin_c = _rope_cos_sin(cmp_end, HEAD_DIM)

    ratio = NSA_SLC_BLOCK // NSA_CMP_STRIDE
    coef = np.convolve(np.ones(ratio), np.ones(NSA_CMP_BLOCK // NSA_CMP_STRIDE))
    cmat = np.zeros((n_blk, n_chunk), np.float32)
    for j in range(n_blk):
        for i, c in enumerate(coef):
            if ratio * j + i < n_chunk - 1:
                cmat[j, ratio * j + i] = c

    tile = pl.BlockSpec((1, t, GROUP_WIDTH), lambda bi, qi: (bi, qi, 0))
    const = lambda shape: pl.BlockSpec(shape, lambda bi, qi: tuple(0 for _ in shape))
    w4 = N_HEADS * t
    return pl.pallas_call(
        functools.partial(_nsa_kernel, seq, t),
        grid=(b, seq // t),
        in_specs=[pl.BlockSpec((1, GROUP_WIDTH, t), lambda bi, qi: (bi, 0, qi)),
                  pl.BlockSpec((1, n_chunk, cw), lambda bi, qi: (bi, 0, 0)),
                  pl.BlockSpec((1, seq, LANES), lambda bi, qi: (bi, 0, 0)),
                  pl.BlockSpec((1, 2, VT_ROWS, seq), lambda bi, qi: (bi, 0, 0, 0)),
                  pl.BlockSpec((1, t, LANES), lambda bi, qi: (bi, qi, 0)),
                  tile,
                  const((1, HEAD_DIM)),
                  const((1, cw)), const((1, cw)), const((2, cw, 2 * HEAD_DIM)), const((2, cw, 2 * HEAD_DIM)),
                  const((n_chunk, HEAD_DIM // 2)), const((n_chunk, HEAD_DIM // 2)),
                  const((n_blk, n_chunk))],
        out_specs=tile,
        out_shape=jax.ShapeDtypeStruct((b, seq, GROUP_WIDTH), BF16),
        scratch_shapes=[pltpu.VMEM((2, n_chunk, HEAD_DIM), BF16),
                        pltpu.VMEM((HEAD_DIM, n_chunk), BF16),
                        pltpu.VMEM((seq // t, t // NSA_SLC_BLOCK, w4), F32),
                        pltpu.VMEM((2, w4), F32),
                        pltpu.VMEM((2, VT_ROWS, w4), F32),
                        pltpu.VMEM((2, t, w4), F32)],
        compiler_params=_cparams(("parallel", "arbitrary")),
        name="nsa_attn",
    )(qT, kcvc, k2, v2T, gl, z,
      qk_gain[1][None, :],
      pe_lo, pe_hi, w_lo, w_hi, cos_c, sin_c,
      jnp.asarray(cmat))


def _rope_cos_sin(pos, dim):
    half = dim // 2
    inv_freq = ROPE_THETA ** (-jnp.arange(half, dtype=F32) / half)
    ang = pos.astype(F32)[:, None] * inv_freq[None, :]
    return jnp.cos(ang), jnp.sin(ang)


def _layer(x, mem, layer_idx, tabs_t, norm_gain, w_in, w_out, nsa_qk_gain, nsa_cmp_pe, nsa_w_cmp,
           diff_qk_gain, diff_lambda, diff_subln_gain, mla_cq_gain, mla_ckv_gain, mla_w_uq, mla_w_ukv,
           mla_qk_gain, mem_norm_gain, mem_w_kv, mem_qk_gain):
    b, seq, d = x.shape
    u = _in_proj(x, norm_gain, w_in, nsa_qk_gain, diff_qk_gain, mem_qk_gain, mla_cq_gain, mla_ckv_gain,
                 mla_w_uq, mla_w_ukv, mla_qk_gain, tabs_t)

    y_nsa = _nsa_call(u["nsa_qT"], u["nsa_kcvc"], u["nsa_k2"], u["nsa_v2T"], u["nsa_gl"], u["nsa_z"],
                      nsa_qk_gain, nsa_cmp_pe, nsa_w_cmp)
    lambda_init = 0.8 - 0.6 * math.exp(-0.3 * layer_idx)
    y_diff = _diff_call(u["diff_qT"], u["diff_k"], u["diff_vT"], u["diff_z"], diff_lambda, diff_subln_gain,
                        lambda_init)
    y_mla = _mla_call(u["mla_qT"], u["mla_k"], u["mla_vT"], u["mla_z"])
    y_mem = _mem_call(u["mem_qT"], mem, u["mem_z"], mem_norm_gain, mem_w_kv, mem_qk_gain)

    ys = [y.reshape(b * seq, GROUP_WIDTH) for y in (y_nsa, y_diff, y_mla, y_mem)]
    return _out_proj(x.reshape(b * seq, d), ys, w_out.astype(BF16)).reshape(b, seq, d)


def kernel(x, mem, norm_gain, w_in, w_out, nsa_qk_gain, nsa_cmp_pe, nsa_w_cmp, diff_qk_gain, diff_lambda,
           diff_subln_gain, mla_cq_gain, mla_ckv_gain, mla_w_uq, mla_w_ukv, mla_qk_gain, mem_norm_gain,
           mem_w_kv, mem_qk_gain):
    seq = x.shape[1]
    pos = jnp.arange(seq, dtype=jnp.int32)
    cos32, sin32 = _rope_cos_sin(pos, DIFF_D)
    cos64, sin64 = _rope_cos_sin(pos, HEAD_DIM)
    tabs_t = (cos32.T, sin32.T, cos64.T, sin64.T)
    for l in range(DEPTH):
        x = _layer(x, mem, l, tabs_t, norm_gain[l], w_in[l], w_out[l], nsa_qk_gain[l], nsa_cmp_pe[l],
                   nsa_w_cmp[l], diff_qk_gain[l], diff_lambda[l], diff_subln_gain[l], mla_cq_gain[l],
                   mla_ckv_gain[l], mla_w_uq[l], mla_w_ukv[l], mla_qk_gain[l], mem_norm_gain[l], mem_w_kv[l],
                   mem_qk_gain[l])
    return x
```

```python
import functools
import math

import numpy as np
import jax
import jax.numpy as jnp
from jax import lax
from jax.experimental import pallas as pl
from jax.experimental.pallas import tpu as pltpu

F32 = jnp.float32
BF16 = jnp.bfloat16
HIGHEST = lax.Precision.HIGHEST

D_MODEL = 1024
DEPTH = 2
N_HEADS = 4
HEAD_DIM = 64
GROUP_WIDTH = N_HEADS * HEAD_DIM
ROPE_THETA = 10000.0
EPS = 1e-6
NEG_INF = -1e30
BIG = 1e30
LOG2E = 1.4426950408889634

NSA_CMP_BLOCK = 32
NSA_CMP_STRIDE = 16
NSA_SLC_BLOCK = 64
NSA_N_SELECT = 16
NSA_N_LOCAL = 2
NSA_WINDOW = 512
DIFF_D = HEAD_DIM // 2
MLA_Q_RANK = 256
MLA_KV_RANK = 128
MLA_NOPE = 64
MLA_ROPE = 32
MLA_QK = MLA_NOPE + MLA_ROPE

VMEM_LIMIT_BYTES = 48 * 1024 * 1024
LANES = 128
KEY_TILE = 256
IN_PROJ_SUBTILES = 1
VT_ROWS = 80


def _cparams(sem):
    return pltpu.CompilerParams(dimension_semantics=sem, vmem_limit_bytes=VMEM_LIMIT_BYTES)


def _group_ones(width, group):
    g = np.arange(width) // group
    return jnp.asarray(g[:, None] == g[None, :], dtype=BF16)


def _group_rsqrt(x, gmat, denom):
    sq = x * x
    hi = sq.astype(BF16)
    lo = (sq - hi.astype(F32)).astype(BF16)
    ss = jnp.dot(hi, gmat, preferred_element_type=F32) + jnp.dot(lo, gmat, preferred_element_type=F32)
    return lax.rsqrt(ss / denom + EPS)


def _norm_rope_t(u_t, groups, dim, gain, cos=None, sin=None):
    x = u_t.reshape(groups, dim, u_t.shape[-1])
    ms = jnp.mean(x * x, axis=1, keepdims=True)
    x = x * lax.rsqrt(ms + EPS) * gain
    if cos is not None:
        half = dim // 2
        x1, x2 = x[:, :half], x[:, half:]
        x = jnp.concatenate([x1 * cos - x2 * sin, x2 * cos + x1 * sin], axis=1)
    return x.reshape(groups * dim, u_t.shape[-1])


def _silu(z):
    z = z.astype(F32)
    return z * jax.nn.sigmoid(z)


def _ones_rows(n):
    r = lax.broadcasted_iota(jnp.int32, (VT_ROWS - HEAD_DIM, n), 0)
    return jnp.where(r == 0, 1.0, 0.0).astype(BF16)


def _split_bf16(x):
    hi = x.astype(BF16)
    return hi, (x - hi.astype(F32)).astype(BF16)


def _dot_x3(a_hi, a_lo, b_hi, b_lo):
    dot = lambda a, b: jnp.dot(a, b, preferred_element_type=F32)
    return dot(a_hi, b_hi) + dot(a_hi, b_lo) + dot(a_lo, b_hi)


def _flash_step_t(s, v_t, m_ref, acc_ref, i, cols=slice(None)):
    m_old = m_ref[i:i + 1, cols]
    m_new = jnp.maximum(m_old, jnp.max(s, axis=0, keepdims=True))
    alpha = jnp.exp2(m_old - m_new)
    p = jnp.exp2(s - m_new).astype(BF16)
    acc_ref[i, :, cols] = alpha * acc_ref[i, :, cols] + jnp.dot(v_t, p, preferred_element_type=F32)
    m_ref[i:i + 1, cols] = m_new


def _flash_reset(m_ref, acc_ref):
    m_ref[...] = jnp.full(m_ref.shape, NEG_INF, F32)
    acc_ref[...] = jnp.zeros(acc_ref.shape, F32)


def _flash_out(acc_ref, i, cols=slice(None)):
    return acc_ref[i, 0:HEAD_DIM, cols] / acc_ref[i, HEAD_DIM:HEAD_DIM + 1, cols]


def _causal_flash_t(n, k_tile, q_t, v_tile, qi, tq, tk, m_ref, acc_ref, s_ref):
    _flash_reset(m_ref, acc_ref)
    n_diag = tq // tk

    all_q = slice(0, tq)

    def scores(i, off, cols):
        return jnp.dot(k_tile(i, off), q_t(i)[:, cols], preferred_element_type=F32)

    def step(off, cols, off_next, cols_next, mask):
        for i in range(n):
            s = s_ref[i, :, cols]
            if off_next is not None:
                s_ref[i, :, cols_next] = scores(i, off_next, cols_next)
            if mask is not None:
                s = jnp.where(mask, s, NEG_INF)
            _flash_step_t(s, v_tile(i, off), m_ref, acc_ref, i, cols)

    for i in range(n):
        s_ref[i] = scores(i, 0, all_q)

    def body(j, carry):
        step(pl.multiple_of(j * tk, tk), all_q, pl.multiple_of(j * tk + tk, tk), all_q, None)
        return carry

    lax.fori_loop(0, qi * n_diag, body, 0)
    krow = lax.broadcasted_iota(jnp.int32, (tk, tq), 0)
    qcol = lax.broadcasted_iota(jnp.int32, (tk, tq), 1)
    causal = krow <= qcol
    diag_cols = [slice(d * tk, tq) for d in range(n_diag)]
    for d in range(n_diag):
        off = pl.multiple_of(qi * tq + d * tk, tk)
        last = d + 1 == n_diag
        off_next = None if last else pl.multiple_of(qi * tq + (d + 1) * tk, tk)
        step(off, diag_cols[d], off_next, None if last else diag_cols[d + 1], causal[:, 0:tq - d * tk])


_ROW_SEGS = (("nsa_kcvc", 128), ("nsa_gl", 128), ("nsa_z", 256), ("diff_z", 256), ("mla_z", 256),
             ("mem_z", 256))
_COL_SEGS = (("nsa_q", 256), ("diff_q", 256), ("mem_q", 256), ("diff_k", 256), ("nsa_k2", 128),
             ("diff_v", 256), ("nsa_v2", 128), ("mla_cq", 256), ("mla_ckv", 128), ("mla_kr", 32))


def _seg_offsets(segs):
    out, off = {}, 0
    for name, w in segs:
        out[name] = (off, w)
        off += w
    return out, off


_ROW_OFF, _ROW_W = _seg_offsets(_ROW_SEGS)
_COL_OFF, _COL_W = _seg_offsets(_COL_SEGS)


def _split_w_in(w_in):
    c = lambda a, b: w_in[:, a:b]
    z = lambda n: jnp.zeros((w_in.shape[0], n), w_in.dtype)
    w_row = jnp.concatenate([
        c(256, 384),
        c(640, 652), z(116),
        c(652, 908), c(1676, 1932), c(2348, 2604), c(2860, 3116),
    ], axis=1)
    w_col = jnp.concatenate([
        c(0, 256), c(908, 1164), c(2604, 2860),
        c(1164, 1420),
        c(384, 448), c(512, 576),
        c(1420, 1676),
        c(448, 512), c(576, 640),
        c(1932, 2188), c(2188, 2316), c(2316, 2348),
    ], axis=1)
    return w_row.astype(BF16), w_col.T.astype(BF16)


def _mla_head_norm_t(x_t, gain):
    ms = jnp.sum(x_t * x_t, axis=0, keepdims=True) / float(MLA_QK)
    return x_t * lax.rsqrt(ms + EPS) * gain


def _rope_t(x_t, cos, sin):
    half = x_t.shape[0] // 2
    x1, x2 = x_t[:half], x_t[half:]
    return jnp.concatenate([x1 * cos - x2 * sin, x2 * cos + x1 * sin], axis=0)


def _in_proj_kernel(x_ref, g_ref, wrow_ref, wcol_ref, nqg_ref, dqg_ref, dkg_ref, k2g_ref, mqg_ref,
                    cqg_ref, ckvg_ref, wuq_ref, wuk_ref, wuv_ref, lqg_ref, lkg_ref,
                    c32_ref, s32_ref, c64_ref, s64_ref,
                    kcvc_o, gl_o, nz_o, dz_o, mz_o, ez_o, dk_o, k2_o,
                    nq_o, dq_o, mq_o, dv_o, v2_o, lq_o, lk_o, lv_o, kcvc_s):
    sub = x_ref.shape[0] // IN_PROJ_SUBTILES
    for part in range(IN_PROJ_SUBTILES):
        _in_proj_subtile(slice(part * sub, (part + 1) * sub), part * (sub // NSA_CMP_STRIDE),
                         x_ref, g_ref, wrow_ref, wcol_ref, nqg_ref, dqg_ref, dkg_ref, k2g_ref, mqg_ref,
                         cqg_ref, ckvg_ref, wuq_ref, wuk_ref, wuv_ref, lqg_ref, lkg_ref,
                         c32_ref, s32_ref, c64_ref, s64_ref,
                         kcvc_o, gl_o, nz_o, dz_o, mz_o, ez_o, dk_o, k2_o,
                         nq_o, dq_o, mq_o, dv_o, v2_o, lq_o, lk_o, lv_o, kcvc_s)


def _in_proj_subtile(rows, chunk0,
                     x_ref, g_ref, wrow_ref, wcol_ref, nqg_ref, dqg_ref, dkg_ref, k2g_ref, mqg_ref,
                     cqg_ref, ckvg_ref, wuq_ref, wuk_ref, wuv_ref, lqg_ref, lkg_ref,
                     c32_ref, s32_ref, c64_ref, s64_ref,
                     kcvc_o, gl_o, nz_o, dz_o, mz_o, ez_o, dk_o, k2_o,
                     nq_o, dq_o, mq_o, dv_o, v2_o, lq_o, lk_o, lv_o, kcvc_s):
    x = x_ref[rows, :]
    tm = x.shape[0]
    ms = jnp.mean(x * x, axis=-1, keepdims=True)
    h = x * lax.rsqrt(ms + EPS) * g_ref[...]
    hb = h.astype(BF16)
    h_t = h.T.astype(BF16)

    def row(name):
        off, w = _ROW_OFF[name]
        return jnp.dot(hb, wrow_ref[:, off:off + w], preferred_element_type=F32)

    u_t = jnp.dot(wcol_ref[...], h_t, preferred_element_type=F32)

    def col(name):
        off, w = _COL_OFF[name]
        return u_t[off:off + w]

    kcvc_s[rows, :] = row("nsa_kcvc")
    n_chunk_rows = tm // NSA_CMP_STRIDE
    for tok in range(NSA_CMP_STRIDE):
        kcvc_o[0, chunk0:chunk0 + n_chunk_rows, tok * LANES:(tok + 1) * LANES] = (
            kcvc_s[pl.ds(rows.start + tok, n_chunk_rows, stride=NSA_CMP_STRIDE), :])
    gl_o[rows, :] = row("nsa_gl")
    nz_o[rows, :] = row("nsa_z").astype(BF16)
    dz_o[rows, :] = row("diff_z").astype(BF16)
    mz_o[rows, :] = row("mla_z").astype(BF16)
    ez_o[rows, :] = row("mem_z").astype(BF16)

    c32, s32, c64, s64 = c32_ref[:, rows], s32_ref[:, rows], c64_ref[:, rows], s64_ref[:, rows]
    g3 = lambda ref, groups: ref[...].reshape(groups, -1, 1)
    nq_o[0, :, rows] = (_norm_rope_t(col("nsa_q"), N_HEADS, HEAD_DIM, g3(nqg_ref, 1), c64, s64)
                        * (HEAD_DIM ** -0.5 * LOG2E))
    dq = _norm_rope_t(col("diff_q"), 2 * N_HEADS, DIFF_D, g3(dqg_ref, 1), c32, s32) * (DIFF_D ** -0.5 * LOG2E)
    dq_o[0, :, rows] = dq.astype(BF16)
    mq = _norm_rope_t(col("mem_q"), N_HEADS, HEAD_DIM, g3(mqg_ref, 1)) * (HEAD_DIM ** -0.5 * LOG2E)
    mq_o[0, :, rows] = mq.astype(BF16)
    dk = _norm_rope_t(col("diff_k"), 2 * N_HEADS, DIFF_D, g3(dkg_ref, 1), c32, s32).T
    for mp in range(2 * N_HEADS):
        dk_o[0, mp, rows, :] = dk[:, mp * DIFF_D:(mp + 1) * DIFF_D].astype(BF16)
    k2 = _norm_rope_t(col("nsa_k2"), 2, HEAD_DIM, g3(k2g_ref, 2), c64, s64).T
    for br in range(2):
        k2_o[0, br, rows, :] = k2[:, br * HEAD_DIM:(br + 1) * HEAD_DIM].astype(BF16)
    ones = _ones_rows(tm)
    dv = col("diff_v")
    for hd in range(N_HEADS):
        dv_o[0, hd, 0:HEAD_DIM, rows] = dv[hd * HEAD_DIM:(hd + 1) * HEAD_DIM].astype(BF16)
        dv_o[0, hd, HEAD_DIM:VT_ROWS, rows] = ones
    v2 = col("nsa_v2")
    for br in range(2):
        v2_o[0, br, 0:HEAD_DIM, rows] = v2[br * HEAD_DIM:(br + 1) * HEAD_DIM].astype(BF16)
        v2_o[0, br, HEAD_DIM:VT_ROWS, rows] = ones

    def latent(name, gain_ref):
        c = col(name)
        return (c * lax.rsqrt(jnp.mean(c * c, axis=0, keepdims=True) + EPS) * gain_ref[...]).astype(BF16)

    rope_rows = slice(MLA_NOPE, MLA_QK)
    qa = jnp.dot(wuq_ref[...], latent("mla_cq", cqg_ref), preferred_element_type=F32)
    lqg = lqg_ref[...] * (MLA_QK ** -0.5 * LOG2E)
    ckv = latent("mla_ckv", ckvg_ref)
    kn = jnp.dot(wuk_ref[...], ckv, preferred_element_type=F32)
    kr = _rope_t(col("mla_kr"), c32, s32)
    zpad = jnp.zeros((LANES - MLA_QK, tm), F32)
    lv = jnp.dot(wuv_ref[...], ckv, preferred_element_type=F32)
    for hd in range(N_HEADS):
        q_h = qa[hd * LANES:(hd + 1) * LANES]
        q_h = jnp.concatenate([q_h[:MLA_NOPE], _rope_t(q_h[rope_rows], c32, s32), q_h[MLA_QK:]], axis=0)
        lq_o[0, hd * LANES:(hd + 1) * LANES, rows] = _mla_head_norm_t(q_h, lqg).astype(BF16)
        k_h = jnp.concatenate([kn[hd * LANES:hd * LANES + MLA_NOPE], kr, zpad], axis=0)
        lk_o[0, hd, rows, :] = _mla_head_norm_t(k_h, lkg_ref[...]).T.astype(BF16)
        lv_o[0, hd, 0:HEAD_DIM, rows] = lv[hd * HEAD_DIM:(hd + 1) * HEAD_DIM].astype(BF16)
        lv_o[0, hd, HEAD_DIM:VT_ROWS, rows] = ones


def _in_proj(x, gain, w_in, nsa_qk_gain, diff_qk_gain, mem_qk_gain, mla_cq_gain, mla_ckv_gain, mla_w_uq,
             mla_w_ukv, mla_qk_gain, tabs_t, tm=512):
    b, seq, d = x.shape
    n = b * seq
    nb = seq // tm
    w_row, w_col = _split_w_in(w_in)
    c32, s32, c64, s64 = tabs_t
    npad = LANES - MLA_QK
    uq = jnp.pad(mla_w_uq.reshape(MLA_Q_RANK, N_HEADS, MLA_QK), ((0, 0), (0, 0), (0, npad)))
    w_uq_t = uq.reshape(MLA_Q_RANK, N_HEADS * LANES).T.astype(BF16)
    ukv = mla_w_ukv.reshape(MLA_KV_RANK, N_HEADS, MLA_NOPE + HEAD_DIM)
    uk = jnp.pad(ukv[:, :, :MLA_NOPE], ((0, 0), (0, 0), (0, LANES - MLA_NOPE)))
    w_uk_t = uk.reshape(MLA_KV_RANK, N_HEADS * LANES).T.astype(BF16)
    w_uv_t = ukv[:, :, MLA_NOPE:].reshape(MLA_KV_RANK, GROUP_WIDTH).T.astype(BF16)
    pad_gain = lambda g: jnp.pad(g, (0, npad))[:, None]

    rowspec = lambda w: pl.BlockSpec((tm, w), lambda i: (i, 0))
    colspec = lambda r: pl.BlockSpec((1, r, tm), lambda i: (i // nb, 0, i % nb))
    vtspec = lambda c: pl.BlockSpec((1, c, VT_ROWS, tm), lambda i: (i // nb, 0, 0, i % nb))
    const = lambda shape: pl.BlockSpec(shape, lambda i: tuple(0 for _ in shape))
    tabspec = lambda r: pl.BlockSpec((r, tm), lambda i: (0, i % nb))
    row_out = lambda w, dt: jax.ShapeDtypeStruct((n, w), dt)
    col_out = lambda r, dt: jax.ShapeDtypeStruct((b, r, seq), dt)
    vt_out = lambda c: jax.ShapeDtypeStruct((b, c, VT_ROWS, seq), BF16)
    slabspec = lambda c, w: pl.BlockSpec((1, c, tm, w), lambda i: (i // nb, 0, i % nb, 0))
    slab_out = lambda c, w: jax.ShapeDtypeStruct((b, c, seq, w), BF16)
    outs = pl.pallas_call(
        _in_proj_kernel,
        grid=(n // tm,),
        in_specs=[rowspec(d), const((1, d)), const((d, _ROW_W)), const((_COL_W, d)),
                  const((HEAD_DIM, 1)), const((DIFF_D, 1)), const((DIFF_D, 1)), const((2 * HEAD_DIM, 1)),
                  const((HEAD_DIM, 1)),
                  const((MLA_Q_RANK, 1)), const((MLA_KV_RANK, 1)),
                  const((N_HEADS * LANES, MLA_Q_RANK)), const((N_HEADS * LANES, MLA_KV_RANK)),
                  const((GROUP_WIDTH, MLA_KV_RANK)), const((LANES, 1)), const((LANES, 1)),
                  tabspec(DIFF_D // 2), tabspec(DIFF_D // 2), tabspec(HEAD_DIM // 2), tabspec(HEAD_DIM // 2)],
        out_specs=[pl.BlockSpec((1, tm // NSA_CMP_STRIDE, NSA_CMP_STRIDE * LANES), lambda i: (i // nb, i % nb, 0)),
                   rowspec(128), rowspec(256), rowspec(256), rowspec(256), rowspec(256),
                   slabspec(2 * N_HEADS, DIFF_D), slabspec(2, HEAD_DIM),
                   colspec(256), colspec(256), colspec(256), vtspec(N_HEADS), vtspec(2),
                   colspec(N_HEADS * LANES), slabspec(N_HEADS, LANES), vtspec(N_HEADS)],
        out_shape=[jax.ShapeDtypeStruct((b, seq // NSA_CMP_STRIDE, NSA_CMP_STRIDE * LANES), F32),
                   row_out(128, F32), row_out(256, BF16), row_out(256, BF16),
                   row_out(256, BF16), row_out(256, BF16), slab_out(2 * N_HEADS, DIFF_D), slab_out(2, HEAD_DIM),
                   col_out(256, F32), col_out(256, BF16), col_out(256, BF16), vt_out(N_HEADS), vt_out(2),
                   col_out(N_HEADS * LANES, BF16), slab_out(N_HEADS, LANES), vt_out(N_HEADS)],
        scratch_shapes=[pltpu.VMEM((tm, LANES), F32)],
        compiler_params=_cparams(("parallel",)),
        name="in_proj",
    )(x.reshape(n, d), gain[None, :], w_row, w_col,
      nsa_qk_gain[0][:, None], diff_qk_gain[0][:, None], diff_qk_gain[1][:, None],
      jnp.concatenate([nsa_qk_gain[2], nsa_qk_gain[3]])[:, None], mem_qk_gain[0][:, None],
      mla_cq_gain[:, None], mla_ckv_gain[:, None], w_uq_t, w_uk_t, w_uv_t,
      pad_gain(mla_qk_gain[0]), pad_gain(mla_qk_gain[1]),
      c32, s32, c64, s64)
    names = ("nsa_kcvc", "nsa_gl", "nsa_z", "diff_z", "mla_z", "mem_z", "diff_k", "nsa_k2",
             "nsa_qT", "diff_qT", "mem_qT", "diff_vT", "nsa_v2T", "mla_qT", "mla_k", "mla_vT")
    u = dict(zip(names, outs))
    for name in names[1:6]:
        u[name] = u[name].reshape(b, seq, -1)
    return u


def _out_proj_kernel(x_ref, y0_ref, y1_ref, y2_ref, y3_ref, w_ref, o_ref):
    acc = x_ref[...]
    for g, y_ref in enumerate((y0_ref, y1_ref, y2_ref, y3_ref)):
        acc = acc + jnp.dot(y_ref[...], w_ref[g * GROUP_WIDTH:(g + 1) * GROUP_WIDTH, :],
                            preferred_element_type=F32)
    o_ref[...] = acc


def _out_proj(x2, ys, w_out_b, tm=512):
    n = x2.shape[0]
    yspec = pl.BlockSpec((tm, GROUP_WIDTH), lambda i: (i, 0))
    return pl.pallas_call(
        _out_proj_kernel,
        grid=(n // tm,),
        in_specs=[pl.BlockSpec((tm, D_MODEL), lambda i: (i, 0)), yspec, yspec, yspec, yspec,
                  pl.BlockSpec((D_MODEL, D_MODEL), lambda i: (0, 0))],
        out_specs=pl.BlockSpec((tm, D_MODEL), lambda i: (i, 0)),
        out_shape=jax.ShapeDtypeStruct((n, D_MODEL), F32),
        compiler_params=_cparams(("parallel",)),
        name="out_proj",
    )(x2, *ys, w_out_b)


def _diff_kernel(lambda_init, t, qT_ref, k_ref, vT_ref, z_ref, lam_ref, sg_ref, y_ref, m_s, acc_s, s_s):
    qi = pl.program_id(1)
    n_maps = 2 * N_HEADS
    qT = qT_ref[0]
    _causal_flash_t(
        n_maps,
        lambda i, off: k_ref[0, i, pl.ds(off, KEY_TILE), :],
        lambda i: qT[i * DIFF_D:(i + 1) * DIFF_D],
        lambda i, off: vT_ref[0, i // 2, :, pl.ds(off, KEY_TILE)],
        qi, t, KEY_TILE, m_s, acc_s, s_s)

    lam = lam_ref[...]
    lmbda = (jnp.exp(jnp.sum(lam[0:1] * lam[1:2], axis=-1, keepdims=True))
             - jnp.exp(jnp.sum(lam[2:3] * lam[3:4], axis=-1, keepdims=True)) + lambda_init)
    outs = []
    for h in range(N_HEADS):
        d = _flash_out(acc_s, 2 * h) - lmbda * _flash_out(acc_s, 2 * h + 1)
        ms = jnp.mean(d * d, axis=0, keepdims=True)
        outs.append(d * lax.rsqrt(ms + EPS) * sg_ref[...] * (1.0 - lambda_init))
    y = jnp.concatenate(outs, axis=0).T * _silu(z_ref[0])
    y_ref[0] = y.astype(BF16)


def _diff_call(qT, k, vT, z, lam, subln_gain, lambda_init, t=512):
    b, _, seq, _ = k.shape
    const = lambda shape: pl.BlockSpec(shape, lambda bi, qi: tuple(0 for _ in shape))
    tile = pl.BlockSpec((1, t, GROUP_WIDTH), lambda bi, qi: (bi, qi, 0))
    return pl.pallas_call(
        functools.partial(_diff_kernel, lambda_init, t),
        grid=(b, seq // t),
        in_specs=[pl.BlockSpec((1, GROUP_WIDTH, t), lambda bi, qi: (bi, 0, qi)),
                  pl.BlockSpec((1, 2 * N_HEADS, seq, DIFF_D), lambda bi, qi: (bi, 0, 0, 0)),
                  pl.BlockSpec((1, N_HEADS, VT_ROWS, seq), lambda bi, qi: (bi, 0, 0, 0)),
                  tile, const((4, DIFF_D)), const((HEAD_DIM, 1))],
        out_specs=tile,
        out_shape=jax.ShapeDtypeStruct((b, seq, GROUP_WIDTH), BF16),
        scratch_shapes=[pltpu.VMEM((2 * N_HEADS, t), F32),
                        pltpu.VMEM((2 * N_HEADS, VT_ROWS, t), F32),
                        pltpu.VMEM((2 * N_HEADS, KEY_TILE, t), F32)],
        compiler_params=_cparams(("parallel", "parallel")),
        name="diff_attn",
    )(qT, k, vT, z, lam, subln_gain[:, None])


def _mla_kernel(t, qT_ref, k_ref, vT_ref, z_ref, y_ref, m_s, acc_s, s_s):
    qi = pl.program_id(1)
    qT = qT_ref[0]
    _causal_flash_t(
        N_HEADS,
        lambda h, off: k_ref[0, h, pl.ds(off, KEY_TILE), :],
        lambda h: qT[h * LANES:(h + 1) * LANES],
        lambda h, off: vT_ref[0, h, :, pl.ds(off, KEY_TILE)],
        qi, t, KEY_TILE, m_s, acc_s, s_s)
    y = jnp.concatenate([_flash_out(acc_s, h) for h in range(N_HEADS)], axis=0).T * _silu(z_ref[0])
    y_ref[0] = y.astype(BF16)


def _mla_call(qT, k, vT, z, t=512):
    b, _, seq, _ = k.shape
    tile = pl.BlockSpec((1, t, GROUP_WIDTH), lambda bi, qi: (bi, qi, 0))
    return pl.pallas_call(
        functools.partial(_mla_kernel, t),
        grid=(b, seq // t),
        in_specs=[pl.BlockSpec((1, N_HEADS * LANES, t), lambda bi, qi: (bi, 0, qi)),
                  pl.BlockSpec((1, N_HEADS, seq, LANES), lambda bi, qi: (bi, 0, 0, 0)),
                  pl.BlockSpec((1, N_HEADS, VT_ROWS, seq), lambda bi, qi: (bi, 0, 0, 0)),
                  tile],
        out_specs=tile,
        out_shape=jax.ShapeDtypeStruct((b, seq, GROUP_WIDTH), BF16),
        scratch_shapes=[pltpu.VMEM((N_HEADS, t), F32),
                        pltpu.VMEM((N_HEADS, VT_ROWS, t), F32),
                        pltpu.VMEM((N_HEADS, KEY_TILE, t), F32)],
        compiler_params=_cparams(("parallel", "parallel")),
        name="mla_attn",
    )(qT, k, vT, z)


def _mem_kernel(qT_ref, mem_ref, z_ref, mg_ref, wk_ref, wvT_ref, kg_ref, gm_ref,
                y_ref, k_s, vT_s, m_s, acc_s):
    qi = pl.program_id(1)
    nt = (((1,), (1,)), ((), ()))
    m_len = mem_ref.shape[1]

    @pl.when(qi == 0)
    def _prep():
        mem = mem_ref[0]
        ms = jnp.mean(mem * mem, axis=-1, keepdims=True)
        mb = (mem * lax.rsqrt(ms + EPS) * mg_ref[...]).astype(BF16)
        k = jnp.dot(mb, wk_ref[...], preferred_element_type=F32)
        kn = k * _group_rsqrt(k, gm_ref[...], float(HEAD_DIM)) * kg_ref[...]
        vT = lax.dot_general(wvT_ref[...], mb, nt, preferred_element_type=F32)
        for h in range(N_HEADS):
            k_s[h] = kn[:, h * HEAD_DIM:(h + 1) * HEAD_DIM].astype(BF16)
            vT_s[h, 0:HEAD_DIM, :] = vT[h * HEAD_DIM:(h + 1) * HEAD_DIM].astype(BF16)
            vT_s[h, HEAD_DIM:VT_ROWS, :] = _ones_rows(m_len)

    qT = qT_ref[0]
    _flash_reset(m_s, acc_s)
    ss = [jnp.dot(k_s[h], qT[h * HEAD_DIM:(h + 1) * HEAD_DIM], preferred_element_type=F32)
          for h in range(N_HEADS)]
    for h in range(N_HEADS):
        _flash_step_t(ss[h], vT_s[h], m_s, acc_s, h)
    y = jnp.concatenate([_flash_out(acc_s, h) for h in range(N_HEADS)], axis=0).T * _silu(z_ref[0])
    y_ref[0] = y.astype(BF16)


def _mem_call(qT, mem, z, mem_gain, w_kv, qk_gain, t=512):
    b, m_len, _ = mem.shape
    seq = qT.shape[2]
    w_k = w_kv[:, :GROUP_WIDTH].astype(BF16)
    w_vT = w_kv[:, GROUP_WIDTH:].T.astype(BF16)
    tile = pl.BlockSpec((1, t, GROUP_WIDTH), lambda bi, qi: (bi, qi, 0))
    const = lambda shape: pl.BlockSpec(shape, lambda bi, qi: tuple(0 for _ in shape))
    return pl.pallas_call(
        _mem_kernel,
        grid=(b, seq // t),
        in_specs=[pl.BlockSpec((1, GROUP_WIDTH, t), lambda bi, qi: (bi, 0, qi)),
                  pl.BlockSpec((1, m_len, D_MODEL), lambda bi, qi: (bi, 0, 0)), tile,
                  const((1, D_MODEL)), const((D_MODEL, GROUP_WIDTH)), const((GROUP_WIDTH, D_MODEL)),
                  const((1, GROUP_WIDTH)), const((GROUP_WIDTH, GROUP_WIDTH))],
        out_specs=tile,
        out_shape=jax.ShapeDtypeStruct((b, seq, GROUP_WIDTH), BF16),
        scratch_shapes=[pltpu.VMEM((N_HEADS, m_len, HEAD_DIM), BF16),
                        pltpu.VMEM((N_HEADS, VT_ROWS, m_len), BF16),
                        pltpu.VMEM((N_HEADS, t), F32),
                        pltpu.VMEM((N_HEADS, VT_ROWS, t), F32)],
        compiler_params=_cparams(("parallel", "arbitrary")),
        name="mem_attn",
    )(qT, mem, z, mem_gain[None, :], w_k, w_vT, jnp.tile(qk_gain[1], N_HEADS)[None, :],
      _group_ones(GROUP_WIDTH, HEAD_DIM))


def _nsa_kernel(seq, t,
                qT_ref, kcvc_ref, k2_ref, v2T_ref, gl_ref, z_ref,
                cg_ref, pelo_ref, pehi_ref, wlo_ref, whi_ref, cosc_ref, sinc_ref,
                cmat_ref,
                y_ref, kc_s, vcT_s, bias_s, m_s, acc_s, s_s):
    qi = pl.program_id(1)
    half = HEAD_DIM // 2
    n_chunk = seq // NSA_CMP_STRIDE
    n_cmp = n_chunk - 1
    n_blk = seq // NSA_SLC_BLOCK
    blk_per_tile = t // NSA_SLC_BLOCK
    win_tiles = NSA_WINDOW // t
    w4 = N_HEADS * t
    SLC, WIN = 0, 1

    @pl.when(qi == 0)
    def _prep():
        ch = kcvc_ref[0]
        a = _dot_x3(*_split_bf16(ch + pelo_ref[...]), wlo_ref[0], wlo_ref[1])
        bm = _dot_x3(*_split_bf16(ch + pehi_ref[...]), whi_ref[0], whi_ref[1])
        cmp = a + pltpu.roll(bm, n_chunk - 1, 0)
        kc = cmp[:, :HEAD_DIM]
        ms = jnp.mean(kc * kc, axis=-1, keepdims=True)
        kc = kc * lax.rsqrt(ms + EPS) * cg_ref[...]
        x1, x2 = kc[:, :half], kc[:, half:]
        c, s = cosc_ref[...], sinc_ref[...]
        kc_hi, kc_lo = _split_bf16(jnp.concatenate([x1 * c - x2 * s, x2 * c + x1 * s], axis=-1))
        kc_s[0] = kc_hi
        kc_s[1] = kc_lo
        vcT_s[...] = cmp.T[HEAD_DIM:].astype(BF16)

    qs = qi * t
    qT = qT_ref[0]
    q4f = jnp.concatenate([qT[h * HEAD_DIM:(h + 1) * HEAD_DIM] for h in range(N_HEADS)], axis=1)
    q4 = q4f.astype(BF16)
    qcol = lax.broadcasted_iota(jnp.int32, (1, w4), 1) & (t - 1)
    pos_c = qs + qcol

    sc = _dot_x3(kc_s[0], kc_s[1], q4, (q4f - q4.astype(F32)).astype(BF16))
    n_idx = lax.broadcasted_iota(jnp.int32, (n_chunk, 1), 0)
    cvalid = (n_idx * NSA_CMP_STRIDE + (NSA_CMP_BLOCK - 1) <= pos_c) & (n_idx < n_cmp)
    sc = jnp.where(cvalid, sc, NEG_INF)
    e = jnp.exp2(sc - jnp.max(sc, axis=0, keepdims=True))
    p = jnp.where(cvalid, e / jnp.sum(e, axis=0, keepdims=True), 0.0)
    o_cmp = jnp.dot(vcT_s[...], p.astype(BF16), preferred_element_type=F32)
    pg = ((p[:, 0:t] + p[:, t:2 * t]) + p[:, 2 * t:3 * t]) + p[:, 3 * t:4 * t]

    p_slc = jnp.dot(cmat_ref[...], pg, precision=HIGHEST, preferred_element_type=F32)
    blk = lax.broadcasted_iota(jnp.int32, (n_blk, 1), 0)
    cur = lax.shift_right_logical(pos_c[:, :t], NSA_SLC_BLOCK.bit_length() - 1)
    forced = (blk == 0) | ((blk <= cur) & (blk > cur - NSA_N_LOCAL))
    score = jnp.where(blk > cur, NEG_INF, jnp.where(forced, BIG, p_slc))
    cnt = jnp.zeros((n_blk, t), F32)
    for i in range(n_blk):
        ri = score[i:i + 1, :]
        cnt = cnt + jnp.where(ri > score, 1.0, jnp.where((ri == score) & (blk > i), 1.0, 0.0))
    bias = jnp.where(cnt < float(NSA_N_SELECT), 0.0, NEG_INF)
    bias = jnp.concatenate([bias] * N_HEADS, axis=1)
    for r in range(n_blk // blk_per_tile):
        bias_s[r] = bias[r * blk_per_tile:(r + 1) * blk_per_tile, :]

    _flash_reset(m_s, acc_s)
    krow = lax.broadcasted_iota(jnp.int32, (t, 1), 0)
    causal = krow <= qcol
    beyond = krow > qcol

    def scores(br, j):
        off = pl.multiple_of(j * t, t)
        return jnp.dot(k2_ref[0, br, pl.ds(off, t), :], q4, preferred_element_type=F32)

    def slc_scores(j):
        s = scores(SLC, j).reshape(blk_per_tile, NSA_SLC_BLOCK, w4) + bias_s[j][:, None, :]
        return s.reshape(t, w4)

    def flash(br, s, j):
        off = pl.multiple_of(j * t, t)
        _flash_step_t(s, v2T_ref[0, br, :, pl.ds(off, t)], m_s, acc_s, br)

    s_s[SLC] = slc_scores(0)

    @pl.when(qi < win_tiles)
    def _first_tile_is_in_window():
        s_s[WIN] = scores(WIN, 0)

    def old_body(j, carry):
        s = s_s[SLC]
        s_s[SLC] = slc_scores(j + 1)
        flash(SLC, s, j)
        return carry

    lax.fori_loop(0, jnp.maximum(qi - win_tiles, 0), old_body, 0)

    @pl.when(qi >= win_tiles)
    def _oldest_window_tile():
        j = qi - win_tiles
        s_slc = s_s[SLC]
        s_win = jnp.where(beyond, scores(WIN, j), NEG_INF)
        s_s[SLC] = slc_scores(j + 1)
        s_s[WIN] = scores(WIN, j + 1)
        flash(SLC, s_slc, j)
        flash(WIN, s_win, j)

    def win_body(j, carry):
        s_slc = s_s[SLC]
        s_win = s_s[WIN]
        s_s[SLC] = slc_scores(j + 1)
        s_s[WIN] = scores(WIN, j + 1)
        flash(SLC, s_slc, j)
        flash(WIN, s_win, j)
        return carry

    lax.fori_loop(jnp.maximum(qi - win_tiles + 1, 0), qi, win_body, 0)
    flash(SLC, jnp.where(causal, s_s[SLC], NEG_INF), qi)
    flash(WIN, jnp.where(causal, s_s[WIN], NEG_INF), qi)

    g = jax.nn.sigmoid(gl_ref[0]).T
    outs = []
    for h in range(N_HEADS):
        cols = slice(h * t, (h + 1) * t)
        o_s = _flash_out(acc_s, SLC, cols)
        o_w = _flash_out(acc_s, WIN, cols)
        outs.append(g[h:h + 1, :] * o_cmp[:, cols] + g[N_HEADS + h:N_HEADS + h + 1, :] * o_s
                    + g[2 * N_HEADS + h:2 * N_HEADS + h + 1, :] * o_w)
    y = jnp.concatenate(outs, axis=0).T * _silu(z_ref[0])
    y_ref[0] = y.astype(BF16)


def _nsa_call(qT, kcvc, k2, v2T, gl, z, qk_gain, cmp_pe, w_cmp, t=256):
    b, _, seq, _ = k2.shape
    n_chunk = seq // NSA_CMP_STRIDE
    n_blk = seq // NSA_SLC_BLOCK
    half_blk = NSA_CMP_BLOCK // 2
    cw = half_blk * 2 * HEAD_DIM
    assert t % NSA_SLC_BLOCK == 0 and NSA_WINDOW % t == 0 and t & (t - 1) == 0

    wk = w_cmp[0].reshape(NSA_CMP_BLOCK, HEAD_DIM, HEAD_DIM)
    wv = w_cmp[1].reshape(NSA_CMP_BLOCK, HEAD_DIM, HEAD_DIM)
    zero = jnp.zeros_like(wk)
    w_all = jnp.concatenate([jnp.concatenate([wk, zero], axis=2),
                             jnp.concatenate([zero, wv], axis=2)], axis=1)
    hi_lo = lambda w: jnp.stack([w.astype(BF16), (w - w.astype(BF16).astype(F32)).astype(BF16)])
    w_lo = hi_lo(w_all[:half_blk].reshape(cw, 2 * HEAD_DIM))
    w_hi = hi_lo(w_all[half_blk:].reshape(cw, 2 * HEAD_DIM))
    pe_all = jnp.concatenate([cmp_pe[0], cmp_pe[1]], axis=1)
    pe_lo = pe_all[:half_blk].reshape(1, cw)
    pe_hi = pe_all[half_blk:].reshape(1, cw)

    cmp_end = jnp.arange(n_chunk, dtype=jnp.int32) * NSA_CMP_STRIDE + (NSA_CMP_BLOCK - 1)
    cos_c, sin_c = _rope_cos_sin(cmp_end, HEAD_DIM)

    ratio = NSA_SLC_BLOCK // NSA_CMP_STRIDE
    coef = np.convolve(np.ones(ratio), np.ones(NSA_CMP_BLOCK // NSA_CMP_STRIDE))
    cmat = np.zeros((n_blk, n_chunk), np.float32)
    for j in range(n_blk):
        for i, c in enumerate(coef):
            if ratio * j + i < n_chunk - 1:
                cmat[j, ratio * j + i] = c

    tile = pl.BlockSpec((1, t, GROUP_WIDTH), lambda bi, qi: (bi, qi, 0))
    const = lambda shape: pl.BlockSpec(shape, lambda bi, qi: tuple(0 for _ in shape))
    w4 = N_HEADS * t
    return pl.pallas_call(
        functools.partial(_nsa_kernel, seq, t),
        grid=(b, seq // t),
        in_specs=[pl.BlockSpec((1, GROUP_WIDTH, t), lambda bi, qi: (bi, 0, qi)),
                  pl.BlockSpec((1, n_chunk, cw), lambda bi, qi: (bi, 0, 0)),
                  pl.BlockSpec((1, 2, seq, HEAD_DIM), lambda bi, qi: (bi, 0, 0, 0)),
                  pl.BlockSpec((1, 2, VT_ROWS, seq), lambda bi, qi: (bi, 0, 0, 0)),
                  pl.BlockSpec((1, t, LANES), lambda bi, qi: (bi, qi, 0)),
                  tile,
                  const((1, HEAD_DIM)),
                  const((1, cw)), const((1, cw)), const((2, cw, 2 * HEAD_DIM)), const((2, cw, 2 * HEAD_DIM)),
                  const((n_chunk, HEAD_DIM // 2)), const((n_chunk, HEAD_DIM // 2)),
                  const((n_blk, n_chunk))],
        out_specs=tile,
        out_shape=jax.ShapeDtypeStruct((b, seq, GROUP_WIDTH), BF16),
        scratch_shapes=[pltpu.VMEM((2, n_chunk, HEAD_DIM), BF16),
                        pltpu.VMEM((HEAD_DIM, n_chunk), BF16),
                        pltpu.VMEM((seq // t, t // NSA_SLC_BLOCK, w4), F32),
                        pltpu.VMEM((2, w4), F32),
                        pltpu.VMEM((2, VT_ROWS, w4), F32),
                        pltpu.VMEM((2, t, w4), F32)],
        compiler_params=_cparams(("parallel", "arbitrary")),
        name="nsa_attn",
    )(qT, kcvc, k2, v2T, gl, z,
      qk_gain[1][None, :],
      pe_lo, pe_hi, w_lo, w_hi, cos_c, sin_c,
      jnp.asarray(cmat))


def _rope_cos_sin(pos, dim):
    half = dim // 2
    inv_freq = ROPE_THETA ** (-jnp.arange(half, dtype=F32) / half)
    ang = pos.astype(F32)[:, None] * inv_freq[None, :]
    return jnp.cos(ang), jnp.sin(ang)


def _layer(x, mem, layer_idx, tabs_t, norm_gain, w_in, w_out, nsa_qk_gain, nsa_cmp_pe, nsa_w_cmp,
           diff_qk_gain, diff_lambda, diff_subln_gain, mla_cq_gain, mla_ckv_gain, mla_w_uq, mla_w_ukv,
           mla_qk_gain, mem_norm_gain, mem_w_kv, mem_qk_gain):
    b, seq, d = x.shape
    u = _in_proj(x, norm_gain, w_in, nsa_qk_gain, diff_qk_gain, mem_qk_gain, mla_cq_gain, mla_ckv_gain,
                 mla_w_uq, mla_w_ukv, mla_qk_gain, tabs_t)

    y_nsa = _nsa_call(u["nsa_qT"], u["nsa_kcvc"], u["nsa_k2"], u["nsa_v2T"], u["nsa_gl"], u["nsa_z"],
                      nsa_qk_gain, nsa_cmp_pe, nsa_w_cmp)
    lambda_init = 0.8 - 0.6 * math.exp(-0.3 * layer_idx)
    y_diff = _diff_call(u["diff_qT"], u["diff_k"], u["diff_vT"], u["diff_z"], diff_lambda, diff_subln_gain,
                        lambda_init)
    y_mla = _mla_call(u["mla_qT"], u["mla_k"], u["mla_vT"], u["mla_z"])
    y_mem = _mem_call(u["mem_qT"], mem, u["mem_z"], mem_norm_gain, mem_w_kv, mem_qk_gain)

    ys = [y.reshape(b * seq, GROUP_WIDTH) for y in (y_nsa, y_diff, y_mla, y_mem)]
    return _out_proj(x.reshape(b * seq, d), ys, w_out.astype(BF16)).reshape(b, seq, d)


def kernel(x, mem, norm_gain, w_in, w_out, nsa_qk_gain, nsa_cmp_pe, nsa_w_cmp, diff_qk_gain, diff_lambda,
           diff_subln_gain, mla_cq_gain, mla_ckv_gain, mla_w_uq, mla_w_ukv, mla_qk_gain, mem_norm_gain,
           mem_w_kv, mem_qk_gain):
    seq = x.shape[1]
    pos = jnp.arange(seq, dtype=jnp.int32)
    cos32, sin32 = _rope_cos_sin(pos, DIFF_D)
    cos64, sin64 = _rope_cos_sin(pos, HEAD_DIM)
    tabs_t = (cos32.T, sin32.T, cos64.T, sin64.T)
    for l in range(DEPTH):
        x = _layer(x, mem, l, tabs_t, norm_gain[l], w_in[l], w_out[l], nsa_qk_gain[l], nsa_cmp_pe[l],
                   nsa_w_cmp[l], diff_qk_gain[l], diff_lambda[l], diff_subln_gain[l], mla_cq_gain[l],
                   mla_ckv_gain[l], mla_w_uq[l], mla_w_ukv[l], mla_qk_gain[l], mem_norm_gain[l], mem_w_kv[l],
                   mem_qk_gain[l])
    return x
```

```python
import functools
import math

import numpy as np
import jax
import jax.numpy as jnp
from jax import lax
from jax.experimental import pallas as pl
from jax.experimental.pallas import tpu as pltpu

F32 = jnp.float32
BF16 = jnp.bfloat16
HIGHEST = lax.Precision.HIGHEST

D_MODEL = 1024
DEPTH = 2
N_HEADS = 4
HEAD_DIM = 64
GROUP_WIDTH = N_HEADS * HEAD_DIM
ROPE_THETA = 10000.0
EPS = 1e-6
NEG_INF = -1e30
BIG = 1e30
LOG2E = 1.4426950408889634

NSA_CMP_BLOCK = 32
NSA_CMP_STRIDE = 16
NSA_SLC_BLOCK = 64
NSA_N_SELECT = 16
NSA_N_LOCAL = 2
NSA_WINDOW = 512
DIFF_D = HEAD_DIM // 2
MLA_Q_RANK = 256
MLA_KV_RANK = 128
MLA_NOPE = 64
MLA_ROPE = 32
MLA_QK = MLA_NOPE + MLA_ROPE

VMEM_LIMIT_BYTES = 48 * 1024 * 1024
LANES = 128
KEY_TILE = 256
IN_PROJ_SUBTILES = 1
VT_ROWS = 80


def _cparams(sem):
    return pltpu.CompilerParams(dimension_semantics=sem, vmem_limit_bytes=VMEM_LIMIT_BYTES)


def _group_ones(width, group):
    g = np.arange(width) // group
    return jnp.asarray(g[:, None] == g[None, :], dtype=BF16)


def _group_rsqrt(x, gmat, denom):
    sq = x * x
    hi = sq.astype(BF16)
    lo = (sq - hi.astype(F32)).astype(BF16)
    ss = jnp.dot(hi, gmat, preferred_element_type=F32) + jnp.dot(lo, gmat, preferred_element_type=F32)
    return lax.rsqrt(ss / denom + EPS)


def _norm_rope_t(u_t, groups, dim, gain, cos=None, sin=None):
    x = u_t.reshape(groups, dim, u_t.shape[-1])
    ms = jnp.mean(x * x, axis=1, keepdims=True)
    x = x * lax.rsqrt(ms + EPS) * gain
    if cos is not None:
        half = dim // 2
        x1, x2 = x[:, :half], x[:, half:]
        x = jnp.concatenate([x1 * cos - x2 * sin, x2 * cos + x1 * sin], axis=1)
    return x.reshape(groups * dim, u_t.shape[-1])


def _silu(z):
    z = z.astype(F32)
    return z * jax.nn.sigmoid(z)


def _ones_rows(n):
    r = lax.broadcasted_iota(jnp.int32, (VT_ROWS - HEAD_DIM, n), 0)
    return jnp.where(r == 0, 1.0, 0.0).astype(BF16)


def _split_bf16(x):
    hi = x.astype(BF16)
    return hi, (x - hi.astype(F32)).astype(BF16)


def _dot_x3(a_hi, a_lo, b_hi, b_lo):
    dot = lambda a, b: jnp.dot(a, b, preferred_element_type=F32)
    return dot(a_hi, b_hi) + dot(a_hi, b_lo) + dot(a_lo, b_hi)


def _flash_step_t(s, v_t, m_ref, acc_ref, i, cols=slice(None)):
    m_old = m_ref[i:i + 1, cols]
    m_new = jnp.maximum(m_old, jnp.max(s, axis=0, keepdims=True))
    alpha = jnp.exp2(m_old - m_new)
    p = jnp.exp2(s - m_new).astype(BF16)
    acc_ref[i, :, cols] = alpha * acc_ref[i, :, cols] + jnp.dot(v_t, p, preferred_element_type=F32)
    m_ref[i:i + 1, cols] = m_new


def _flash_reset(m_ref, acc_ref):
    m_ref[...] = jnp.full(m_ref.shape, NEG_INF, F32)
    acc_ref[...] = jnp.zeros(acc_ref.shape, F32)


def _flash_out(acc_ref, i, cols=slice(None)):
    return acc_ref[i, 0:HEAD_DIM, cols] / acc_ref[i, HEAD_DIM:HEAD_DIM + 1, cols]


def _causal_flash_t(n, k_tile, q_t, v_tile, qi, tq, tk, m_ref, acc_ref, s_ref):
    _flash_reset(m_ref, acc_ref)
    n_diag = tq // tk

    all_q = slice(0, tq)

    def scores(i, off, cols):
        return jnp.dot(k_tile(i, off), q_t(i)[:, cols], preferred_element_type=F32)

    def step(off, cols, off_next, cols_next, mask):
        for i in range(n):
            s = s_ref[i, :, cols]
            if off_next is not None:
                s_ref[i, :, cols_next] = scores(i, off_next, cols_next)
            if mask is not None:
                s = jnp.where(mask, s, NEG_INF)
            _flash_step_t(s, v_tile(i, off), m_ref, acc_ref, i, cols)

    for i in range(n):
        s_ref[i] = scores(i, 0, all_q)

    def body(j, carry):
        step(pl.multiple_of(j * tk, tk), all_q, pl.multiple_of(j * tk + tk, tk), all_q, None)
        return carry

    lax.fori_loop(0, qi * n_diag, body, 0)
    krow = lax.broadcasted_iota(jnp.int32, (tk, tq), 0)
    qcol = lax.broadcasted_iota(jnp.int32, (tk, tq), 1)
    causal = krow <= qcol
    diag_cols = [slice(d * tk, tq) for d in range(n_diag)]
    for d in range(n_diag):
        off = pl.multiple_of(qi * tq + d * tk, tk)
        last = d + 1 == n_diag
        off_next = None if last else pl.multiple_of(qi * tq + (d + 1) * tk, tk)
        step(off, diag_cols[d], off_next, None if last else diag_cols[d + 1], causal[:, 0:tq - d * tk])


_ROW_SEGS = (("nsa_kcvc", 128), ("nsa_gl", 128), ("nsa_z", 256), ("diff_z", 256), ("mla_z", 256),
             ("mem_z", 256))
_COL_SEGS = (("nsa_q", 256), ("diff_q", 256), ("mem_q", 256), ("diff_k", 256), ("nsa_k2", 128),
             ("diff_v", 256), ("nsa_v2", 128), ("mla_cq", 256), ("mla_ckv", 128), ("mla_kr", 32))


def _seg_offsets(segs):
    out, off = {}, 0
    for name, w in segs:
        out[name] = (off, w)
        off += w
    return out, off


_ROW_OFF, _ROW_W = _seg_offsets(_ROW_SEGS)
_COL_OFF, _COL_W = _seg_offsets(_COL_SEGS)


def _split_w_in(w_in):
    c = lambda a, b: w_in[:, a:b]
    z = lambda n: jnp.zeros((w_in.shape[0], n), w_in.dtype)
    w_row = jnp.concatenate([
        c(256, 384),
        c(640, 652), z(116),
        c(652, 908), c(1676, 1932), c(2348, 2604), c(2860, 3116),
    ], axis=1)
    w_col = jnp.concatenate([
        c(0, 256), c(908, 1164), c(2604, 2860),
        c(1164, 1420),
        c(384, 448), c(512, 576),
        c(1420, 1676),
        c(448, 512), c(576, 640),
        c(1932, 2188), c(2188, 2316), c(2316, 2348),
    ], axis=1)
    return w_row.astype(BF16), w_col.astype(BF16)


def _mla_head_norm_t(x_t, gain):
    ms = jnp.sum(x_t * x_t, axis=0, keepdims=True) / float(MLA_QK)
    return x_t * lax.rsqrt(ms + EPS) * gain


def _rope_t(x_t, cos, sin):
    half = x_t.shape[0] // 2
    x1, x2 = x_t[:half], x_t[half:]
    return jnp.concatenate([x1 * cos - x2 * sin, x2 * cos + x1 * sin], axis=0)


def _in_proj_kernel(x_ref, g_ref, wrow_ref, wcol_ref, nqg_ref, dqg_ref, dkg_ref, k2g_ref, mqg_ref,
                    cqg_ref, ckvg_ref, wuq_ref, wuk_ref, wuv_ref, lqg_ref, lkg_ref,
                    c32_ref, s32_ref, c64_ref, s64_ref,
                    kcvc_o, gl_o, nz_o, dz_o, mz_o, ez_o, dk_o, k2_o,
                    nq_o, dq_o, mq_o, dv_o, v2_o, lq_o, lk_o, lv_o, kcvc_s, wcol_t_s):
    @pl.when(pl.program_id(0) == 0)
    def _transpose_feature_weights():
        for off, w in _COL_OFF.values():
            blk = wcol_ref[:, off:off + w].astype(F32)
            if w % LANES:
                blk = jnp.concatenate([blk, jnp.zeros((blk.shape[0], LANES - w % LANES), F32)], axis=1)
            wcol_t_s[off:off + w, :] = blk.T[:w].astype(BF16)

    sub = x_ref.shape[0] // IN_PROJ_SUBTILES
    for part in range(IN_PROJ_SUBTILES):
        _in_proj_subtile(slice(part * sub, (part + 1) * sub), part * (sub // NSA_CMP_STRIDE),
                         x_ref, g_ref, wrow_ref, wcol_t_s, nqg_ref, dqg_ref, dkg_ref, k2g_ref, mqg_ref,
                         cqg_ref, ckvg_ref, wuq_ref, wuk_ref, wuv_ref, lqg_ref, lkg_ref,
                         c32_ref, s32_ref, c64_ref, s64_ref,
                         kcvc_o, gl_o, nz_o, dz_o, mz_o, ez_o, dk_o, k2_o,
                         nq_o, dq_o, mq_o, dv_o, v2_o, lq_o, lk_o, lv_o, kcvc_s)


def _in_proj_subtile(rows, chunk0,
                     x_ref, g_ref, wrow_ref, wcol_ref, nqg_ref, dqg_ref, dkg_ref, k2g_ref, mqg_ref,
                     cqg_ref, ckvg_ref, wuq_ref, wuk_ref, wuv_ref, lqg_ref, lkg_ref,
                     c32_ref, s32_ref, c64_ref, s64_ref,
                     kcvc_o, gl_o, nz_o, dz_o, mz_o, ez_o, dk_o, k2_o,
                     nq_o, dq_o, mq_o, dv_o, v2_o, lq_o, lk_o, lv_o, kcvc_s):
    x = x_ref[rows, :]
    tm = x.shape[0]
    ms = jnp.mean(x * x, axis=-1, keepdims=True)
    h = x * lax.rsqrt(ms + EPS) * g_ref[...]
    hb = h.astype(BF16)
    h_t = h.T.astype(BF16)

    def row(name):
        off, w = _ROW_OFF[name]
        return jnp.dot(hb, wrow_ref[:, off:off + w], preferred_element_type=F32)

    u_t = jnp.dot(wcol_ref[...], h_t, preferred_element_type=F32)

    def col(name):
        off, w = _COL_OFF[name]
        return u_t[off:off + w]

    kcvc_s[rows, :] = row("nsa_kcvc")
    n_chunk_rows = tm // NSA_CMP_STRIDE
    for tok in range(NSA_CMP_STRIDE):
        kcvc_o[0, chunk0:chunk0 + n_chunk_rows, tok * LANES:(tok + 1) * LANES] = (
            kcvc_s[pl.ds(rows.start + tok, n_chunk_rows, stride=NSA_CMP_STRIDE), :])
    gl_o[rows, :] = row("nsa_gl")
    nz_o[rows, :] = row("nsa_z").astype(BF16)
    dz_o[rows, :] = row("diff_z").astype(BF16)
    mz_o[rows, :] = row("mla_z").astype(BF16)
    ez_o[rows, :] = row("mem_z").astype(BF16)

    c32, s32, c64, s64 = c32_ref[:, rows], s32_ref[:, rows], c64_ref[:, rows], s64_ref[:, rows]
    g3 = lambda ref, groups: ref[...].reshape(groups, -1, 1)
    nq_o[0, :, rows] = (_norm_rope_t(col("nsa_q"), N_HEADS, HEAD_DIM, g3(nqg_ref, 1), c64, s64)
                        * (HEAD_DIM ** -0.5 * LOG2E))
    dq = _norm_rope_t(col("diff_q"), 2 * N_HEADS, DIFF_D, g3(dqg_ref, 1), c32, s32) * (DIFF_D ** -0.5 * LOG2E)
    dq_o[0, :, rows] = dq.astype(BF16)
    mq = _norm_rope_t(col("mem_q"), N_HEADS, HEAD_DIM, g3(mqg_ref, 1)) * (HEAD_DIM ** -0.5 * LOG2E)
    mq_o[0, :, rows] = mq.astype(BF16)
    dk = _norm_rope_t(col("diff_k"), 2 * N_HEADS, DIFF_D, g3(dkg_ref, 1), c32, s32).T
    for mp in range(2 * N_HEADS):
        dk_o[0, mp, rows, :] = dk[:, mp * DIFF_D:(mp + 1) * DIFF_D].astype(BF16)
    k2 = _norm_rope_t(col("nsa_k2"), 2, HEAD_DIM, g3(k2g_ref, 2), c64, s64).T
    for br in range(2):
        k2_o[0, br, rows, :] = k2[:, br * HEAD_DIM:(br + 1) * HEAD_DIM].astype(BF16)
    ones = _ones_rows(tm)
    dv = col("diff_v")
    for hd in range(N_HEADS):
        dv_o[0, hd, 0:HEAD_DIM, rows] = dv[hd * HEAD_DIM:(hd + 1) * HEAD_DIM].astype(BF16)
        dv_o[0, hd, HEAD_DIM:VT_ROWS, rows] = ones
    v2 = col("nsa_v2")
    for br in range(2):
        v2_o[0, br, 0:HEAD_DIM, rows] = v2[br * HEAD_DIM:(br + 1) * HEAD_DIM].astype(BF16)
        v2_o[0, br, HEAD_DIM:VT_ROWS, rows] = ones

    def latent(name, gain_ref):
        c = col(name)
        return (c * lax.rsqrt(jnp.mean(c * c, axis=0, keepdims=True) + EPS) * gain_ref[...]).astype(BF16)

    rope_rows = slice(MLA_NOPE, MLA_QK)
    qa = jnp.dot(wuq_ref[...], latent("mla_cq", cqg_ref), preferred_element_type=F32)
    lqg = lqg_ref[...] * (MLA_QK ** -0.5 * LOG2E)
    ckv = latent("mla_ckv", ckvg_ref)
    kn = jnp.dot(wuk_ref[...], ckv, preferred_element_type=F32)
    kr = _rope_t(col("mla_kr"), c32, s32)
    zpad = jnp.zeros((LANES - MLA_QK, tm), F32)
    lv = jnp.dot(wuv_ref[...], ckv, preferred_element_type=F32)
    for hd in range(N_HEADS):
        q_h = qa[hd * LANES:(hd + 1) * LANES]
        q_h = jnp.concatenate([q_h[:MLA_NOPE], _rope_t(q_h[rope_rows], c32, s32), q_h[MLA_QK:]], axis=0)
        lq_o[0, hd * LANES:(hd + 1) * LANES, rows] = _mla_head_norm_t(q_h, lqg).astype(BF16)
        k_h = jnp.concatenate([kn[hd * LANES:hd * LANES + MLA_NOPE], kr, zpad], axis=0)
        lk_o[0, hd, rows, :] = _mla_head_norm_t(k_h, lkg_ref[...]).T.astype(BF16)
        lv_o[0, hd, 0:HEAD_DIM, rows] = lv[hd * HEAD_DIM:(hd + 1) * HEAD_DIM].astype(BF16)
        lv_o[0, hd, HEAD_DIM:VT_ROWS, rows] = ones


def _in_proj(x, gain, w_in, nsa_qk_gain, diff_qk_gain, mem_qk_gain, mla_cq_gain, mla_ckv_gain, mla_w_uq,
             mla_w_ukv, mla_qk_gain, tabs_t, tm=512):
    b, seq, d = x.shape
    n = b * seq
    nb = seq // tm
    w_row, w_col = _split_w_in(w_in)
    c32, s32, c64, s64 = tabs_t
    npad = LANES - MLA_QK
    uq = jnp.pad(mla_w_uq.reshape(MLA_Q_RANK, N_HEADS, MLA_QK), ((0, 0), (0, 0), (0, npad)))
    w_uq_t = uq.reshape(MLA_Q_RANK, N_HEADS * LANES).T.astype(BF16)
    ukv = mla_w_ukv.reshape(MLA_KV_RANK, N_HEADS, MLA_NOPE + HEAD_DIM)
    uk = jnp.pad(ukv[:, :, :MLA_NOPE], ((0, 0), (0, 0), (0, LANES - MLA_NOPE)))
    w_uk_t = uk.reshape(MLA_KV_RANK, N_HEADS * LANES).T.astype(BF16)
    w_uv_t = ukv[:, :, MLA_NOPE:].reshape(MLA_KV_RANK, GROUP_WIDTH).T.astype(BF16)
    pad_gain = lambda g: jnp.pad(g, (0, npad))[:, None]

    rowspec = lambda w: pl.BlockSpec((tm, w), lambda i: (i, 0))
    colspec = lambda r: pl.BlockSpec((1, r, tm), lambda i: (i // nb, 0, i % nb))
    vtspec = lambda c: pl.BlockSpec((1, c, VT_ROWS, tm), lambda i: (i // nb, 0, 0, i % nb))
    const = lambda shape: pl.BlockSpec(shape, lambda i: tuple(0 for _ in shape))
    tabspec = lambda r: pl.BlockSpec((r, tm), lambda i: (0, i % nb))
    row_out = lambda w, dt: jax.ShapeDtypeStruct((n, w), dt)
    col_out = lambda r, dt: jax.ShapeDtypeStruct((b, r, seq), dt)
    vt_out = lambda c: jax.ShapeDtypeStruct((b, c, VT_ROWS, seq), BF16)
    slabspec = lambda c, w: pl.BlockSpec((1, c, tm, w), lambda i: (i // nb, 0, i % nb, 0))
    slab_out = lambda c, w: jax.ShapeDtypeStruct((b, c, seq, w), BF16)
    outs = pl.pallas_call(
        _in_proj_kernel,
        grid=(n // tm,),
        in_specs=[rowspec(d), const((1, d)), const((d, _ROW_W)), const((d, _COL_W)),
                  const((HEAD_DIM, 1)), const((DIFF_D, 1)), const((DIFF_D, 1)), const((2 * HEAD_DIM, 1)),
                  const((HEAD_DIM, 1)),
                  const((MLA_Q_RANK, 1)), const((MLA_KV_RANK, 1)),
                  const((N_HEADS * LANES, MLA_Q_RANK)), const((N_HEADS * LANES, MLA_KV_RANK)),
                  const((GROUP_WIDTH, MLA_KV_RANK)), const((LANES, 1)), const((LANES, 1)),
                  tabspec(DIFF_D // 2), tabspec(DIFF_D // 2), tabspec(HEAD_DIM // 2), tabspec(HEAD_DIM // 2)],
        out_specs=[pl.BlockSpec((1, tm // NSA_CMP_STRIDE, NSA_CMP_STRIDE * LANES), lambda i: (i // nb, i % nb, 0)),
                   rowspec(128), rowspec(256), rowspec(256), rowspec(256), rowspec(256),
                   slabspec(2 * N_HEADS, DIFF_D), slabspec(2, HEAD_DIM),
                   colspec(256), colspec(256), colspec(256), vtspec(N_HEADS), vtspec(2),
                   colspec(N_HEADS * LANES), slabspec(N_HEADS, LANES), vtspec(N_HEADS)],
        out_shape=[jax.ShapeDtypeStruct((b, seq // NSA_CMP_STRIDE, NSA_CMP_STRIDE * LANES), F32),
                   row_out(128, F32), row_out(256, BF16), row_out(256, BF16),
                   row_out(256, BF16), row_out(256, BF16), slab_out(2 * N_HEADS, DIFF_D), slab_out(2, HEAD_DIM),
                   col_out(256, F32), col_out(256, BF16), col_out(256, BF16), vt_out(N_HEADS), vt_out(2),
                   col_out(N_HEADS * LANES, BF16), slab_out(N_HEADS, LANES), vt_out(N_HEADS)],
        scratch_shapes=[pltpu.VMEM((tm, LANES), F32), pltpu.VMEM((_COL_W, d), BF16)],
        compiler_params=_cparams(("arbitrary",)),
        name="in_proj",
    )(x.reshape(n, d), gain[None, :], w_row, w_col,
      nsa_qk_gain[0][:, None], diff_qk_gain[0][:, None], diff_qk_gain[1][:, None],
      jnp.concatenate([nsa_qk_gain[2], nsa_qk_gain[3]])[:, None], mem_qk_gain[0][:, None],
      mla_cq_gain[:, None], mla_ckv_gain[:, None], w_uq_t, w_uk_t, w_uv_t,
      pad_gain(mla_qk_gain[0]), pad_gain(mla_qk_gain[1]),
      c32, s32, c64, s64)
    names = ("nsa_kcvc", "nsa_gl", "nsa_z", "diff_z", "mla_z", "mem_z", "diff_k", "nsa_k2",
             "nsa_qT", "diff_qT", "mem_qT", "diff_vT", "nsa_v2T", "mla_qT", "mla_k", "mla_vT")
    u = dict(zip(names, outs))
    for name in names[1:6]:
        u[name] = u[name].reshape(b, seq, -1)
    return u


def _out_proj_kernel(x_ref, y0_ref, y1_ref, y2_ref, y3_ref, w_ref, o_ref):
    acc = x_ref[...]
    for g, y_ref in enumerate((y0_ref, y1_ref, y2_ref, y3_ref)):
        acc = acc + jnp.dot(y_ref[...], w_ref[g * GROUP_WIDTH:(g + 1) * GROUP_WIDTH, :],
                            preferred_element_type=F32)
    o_ref[...] = acc


def _out_proj(x2, ys, w_out_b, tm=512):
    n = x2.shape[0]
    yspec = pl.BlockSpec((tm, GROUP_WIDTH), lambda i: (i, 0))
    return pl.pallas_call(
        _out_proj_kernel,
        grid=(n // tm,),
        in_specs=[pl.BlockSpec((tm, D_MODEL), lambda i: (i, 0)), yspec, yspec, yspec, yspec,
                  pl.BlockSpec((D_MODEL, D_MODEL), lambda i: (0, 0))],
        out_specs=pl.BlockSpec((tm, D_MODEL), lambda i: (i, 0)),
        out_shape=jax.ShapeDtypeStruct((n, D_MODEL), F32),
        compiler_params=_cparams(("parallel",)),
        name="out_proj",
    )(x2, *ys, w_out_b)


def _diff_kernel(lambda_init, t, qT_ref, k_ref, vT_ref, z_ref, lam_ref, sg_ref, y_ref, m_s, acc_s, s_s):
    qi = pl.program_id(1)
    n_maps = 2 * N_HEADS
    qT = qT_ref[0]
    _causal_flash_t(
        n_maps,
        lambda i, off: k_ref[0, i, pl.ds(off, KEY_TILE), :],
        lambda i: qT[i * DIFF_D:(i + 1) * DIFF_D],
        lambda i, off: vT_ref[0, i // 2, :, pl.ds(off, KEY_TILE)],
        qi, t, KEY_TILE, m_s, acc_s, s_s)

    lam = lam_ref[...]
    lmbda = (jnp.exp(jnp.sum(lam[0:1] * lam[1:2], axis=-1, keepdims=True))
             - jnp.exp(jnp.sum(lam[2:3] * lam[3:4], axis=-1, keepdims=True)) + lambda_init)
    outs = []
    for h in range(N_HEADS):
        d = _flash_out(acc_s, 2 * h) - lmbda * _flash_out(acc_s, 2 * h + 1)
        ms = jnp.mean(d * d, axis=0, keepdims=True)
        outs.append(d * lax.rsqrt(ms + EPS) * sg_ref[...] * (1.0 - lambda_init))
    y = jnp.concatenate(outs, axis=0).T * _silu(z_ref[0])
    y_ref[0] = y.astype(BF16)


def _diff_call(qT, k, vT, z, lam, subln_gain, lambda_init, t=512):
    b, _, seq, _ = k.shape
    const = lambda shape: pl.BlockSpec(shape, lambda bi, qi: tuple(0 for _ in shape))
    tile = pl.BlockSpec((1, t, GROUP_WIDTH), lambda bi, qi: (bi, qi, 0))
    return pl.pallas_call(
        functools.partial(_diff_kernel, lambda_init, t),
        grid=(b, seq // t),
        in_specs=[pl.BlockSpec((1, GROUP_WIDTH, t), lambda bi, qi: (bi, 0, qi)),
                  pl.BlockSpec((1, 2 * N_HEADS, seq, DIFF_D), lambda bi, qi: (bi, 0, 0, 0)),
                  pl.BlockSpec((1, N_HEADS, VT_ROWS, seq), lambda bi, qi: (bi, 0, 0, 0)),
                  tile, const((4, DIFF_D)), const((HEAD_DIM, 1))],
        out_specs=tile,
        out_shape=jax.ShapeDtypeStruct((b, seq, GROUP_WIDTH), BF16),
        scratch_shapes=[pltpu.VMEM((2 * N_HEADS, t), F32),
                        pltpu.VMEM((2 * N_HEADS, VT_ROWS, t), F32),
                        pltpu.VMEM((2 * N_HEADS, KEY_TILE, t), F32)],
        compiler_params=_cparams(("parallel", "parallel")),
        name="diff_attn",
    )(qT, k, vT, z, lam, subln_gain[:, None])


def _mla_mem_kernel(t, qT_ref, k_ref, vT_ref, z_ref,
                    mqT_ref, mem_ref, mz_ref, mg_ref, wk_ref, wvT_ref, kg_ref, gm_ref,
                    y_ref, ymem_ref, m_s, acc_s, s_s, mk_s, mvT_s, mm_s, macc_s):
    qi = pl.program_id(1)
    nt = (((1,), (1,)), ((), ()))
    m_len = mem_ref.shape[1]

    @pl.when(qi == 0)
    def _prep_memory_kv():
        mem = mem_ref[0]
        ms = jnp.mean(mem * mem, axis=-1, keepdims=True)
        mb = (mem * lax.rsqrt(ms + EPS) * mg_ref[...]).astype(BF16)
        k = jnp.dot(mb, wk_ref[...], preferred_element_type=F32)
        kn = k * _group_rsqrt(k, gm_ref[...], float(HEAD_DIM)) * kg_ref[...]
        vT = lax.dot_general(wvT_ref[...], mb, nt, preferred_element_type=F32)
        for h in range(N_HEADS):
            mk_s[h] = kn[:, h * HEAD_DIM:(h + 1) * HEAD_DIM].astype(BF16)
            mvT_s[h, 0:HEAD_DIM, :] = vT[h * HEAD_DIM:(h + 1) * HEAD_DIM].astype(BF16)
            mvT_s[h, HEAD_DIM:VT_ROWS, :] = _ones_rows(m_len)

    mqT = mqT_ref[0]
    _flash_reset(mm_s, macc_s)
    ss = [jnp.dot(mk_s[h], mqT[h * HEAD_DIM:(h + 1) * HEAD_DIM], preferred_element_type=F32)
          for h in range(N_HEADS)]
    for h in range(N_HEADS):
        _flash_step_t(ss[h], mvT_s[h], mm_s, macc_s, h)

    qT = qT_ref[0]
    _causal_flash_t(
        N_HEADS,
        lambda h, off: k_ref[0, h, pl.ds(off, KEY_TILE), :],
        lambda h: qT[h * LANES:(h + 1) * LANES],
        lambda h, off: vT_ref[0, h, :, pl.ds(off, KEY_TILE)],
        qi, t, KEY_TILE, m_s, acc_s, s_s)

    y_mem = jnp.concatenate([_flash_out(macc_s, h) for h in range(N_HEADS)], axis=0).T * _silu(mz_ref[0])
    ymem_ref[0] = y_mem.astype(BF16)
    y = jnp.concatenate([_flash_out(acc_s, h) for h in range(N_HEADS)], axis=0).T * _silu(z_ref[0])
    y_ref[0] = y.astype(BF16)


def _mla_mem_call(qT, k, vT, z, mem_qT, mem, mem_z, mem_gain, w_kv, mem_qk_gain, t=512):
    b, _, seq, _ = k.shape
    m_len = mem.shape[1]
    w_k = w_kv[:, :GROUP_WIDTH].astype(BF16)
    w_vT = w_kv[:, GROUP_WIDTH:].T.astype(BF16)
    tile = pl.BlockSpec((1, t, GROUP_WIDTH), lambda bi, qi: (bi, qi, 0))
    const = lambda shape: pl.BlockSpec(shape, lambda bi, qi: tuple(0 for _ in shape))
    out = jax.ShapeDtypeStruct((b, seq, GROUP_WIDTH), BF16)
    return pl.pallas_call(
        functools.partial(_mla_mem_kernel, t),
        grid=(b, seq // t),
        in_specs=[pl.BlockSpec((1, N_HEADS * LANES, t), lambda bi, qi: (bi, 0, qi)),
                  pl.BlockSpec((1, N_HEADS, seq, LANES), lambda bi, qi: (bi, 0, 0, 0)),
                  pl.BlockSpec((1, N_HEADS, VT_ROWS, seq), lambda bi, qi: (bi, 0, 0, 0)),
                  tile,
                  pl.BlockSpec((1, GROUP_WIDTH, t), lambda bi, qi: (bi, 0, qi)),
                  pl.BlockSpec((1, m_len, D_MODEL), lambda bi, qi: (bi, 0, 0)), tile,
                  const((1, D_MODEL)), const((D_MODEL, GROUP_WIDTH)), const((GROUP_WIDTH, D_MODEL)),
                  const((1, GROUP_WIDTH)), const((GROUP_WIDTH, GROUP_WIDTH))],
        out_specs=[tile, tile],
        out_shape=[out, out],
        scratch_shapes=[pltpu.VMEM((N_HEADS, t), F32),
                        pltpu.VMEM((N_HEADS, VT_ROWS, t), F32),
                        pltpu.VMEM((N_HEADS, KEY_TILE, t), F32),
                        pltpu.VMEM((N_HEADS, m_len, HEAD_DIM), BF16),
                        pltpu.VMEM((N_HEADS, VT_ROWS, m_len), BF16),
                        pltpu.VMEM((N_HEADS, t), F32),
                        pltpu.VMEM((N_HEADS, VT_ROWS, t), F32)],
        compiler_params=_cparams(("parallel", "arbitrary")),
        name="mla_mem_attn",
    )(qT, k, vT, z, mem_qT, mem, mem_z, mem_gain[None, :], w_k, w_vT,
      jnp.tile(mem_qk_gain[1], N_HEADS)[None, :], _group_ones(GROUP_WIDTH, HEAD_DIM))


def _nsa_kernel(seq, t,
                qT_ref, kcvc_ref, k2_ref, v2T_ref, gl_ref, z_ref,
                cg_ref, pelo_ref, pehi_ref, wlo_ref, whi_ref, cosc_ref, sinc_ref,
                cmat_ref,
                y_ref, kc_s, vcT_s, bias_s, m_s, acc_s, s_s):
    qi = pl.program_id(1)
    half = HEAD_DIM // 2
    n_chunk = seq // NSA_CMP_STRIDE
    n_cmp = n_chunk - 1
    n_blk = seq // NSA_SLC_BLOCK
    blk_per_tile = t // NSA_SLC_BLOCK
    win_tiles = NSA_WINDOW // t
    w4 = N_HEADS * t
    SLC, WIN = 0, 1

    @pl.when(qi == 0)
    def _prep():
        ch = kcvc_ref[0]
        a = _dot_x3(*_split_bf16(ch + pelo_ref[...]), wlo_ref[0], wlo_ref[1])
        bm = _dot_x3(*_split_bf16(ch + pehi_ref[...]), whi_ref[0], whi_ref[1])
        cmp = a + pltpu.roll(bm, n_chunk - 1, 0)
        kc = cmp[:, :HEAD_DIM]
        ms = jnp.mean(kc * kc, axis=-1, keepdims=True)
        kc = kc * lax.rsqrt(ms + EPS) * cg_ref[...]
        x1, x2 = kc[:, :half], kc[:, half:]
        c, s = cosc_ref[...], sinc_ref[...]
        kc_hi, kc_lo = _split_bf16(jnp.concatenate([x1 * c - x2 * s, x2 * c + x1 * s], axis=-1))
        kc_s[0] = kc_hi
        kc_s[1] = kc_lo
        vcT_s[...] = cmp.T[HEAD_DIM:].astype(BF16)

    qs = qi * t
    qT = qT_ref[0]
    q4f = jnp.concatenate([qT[h * HEAD_DIM:(h + 1) * HEAD_DIM] for h in range(N_HEADS)], axis=1)
    q4 = q4f.astype(BF16)
    qcol = lax.broadcasted_iota(jnp.int32, (1, w4), 1) & (t - 1)
    pos_c = qs + qcol

    q_lo = (q4f - q4.astype(F32)).astype(BF16)
    heads = [slice(h * t, (h + 1) * t) for h in range(N_HEADS)]
    scs = [_dot_x3(kc_s[0], kc_s[1], q4[:, c], q_lo[:, c]) for c in heads]
    n_idx = lax.broadcasted_iota(jnp.int32, (n_chunk, 1), 0)
    cvalid = (n_idx * NSA_CMP_STRIDE + (NSA_CMP_BLOCK - 1) <= pos_c[:, :t]) & (n_idx < n_cmp)
    o_cmp, pg = [], None
    for h in range(N_HEADS):
        sc = jnp.where(cvalid, scs[h], NEG_INF)
        e = jnp.exp2(sc - jnp.max(sc, axis=0, keepdims=True))
        p = jnp.where(cvalid, e / jnp.sum(e, axis=0, keepdims=True), 0.0)
        o_cmp.append(jnp.dot(vcT_s[...], p.astype(BF16), preferred_element_type=F32))
        pg = p if pg is None else pg + p

    p_slc = jnp.dot(cmat_ref[...], pg, precision=HIGHEST, preferred_element_type=F32)
    blk = lax.broadcasted_iota(jnp.int32, (n_blk, 1), 0)
    cur = lax.shift_right_logical(pos_c[:, :t], NSA_SLC_BLOCK.bit_length() - 1)
    forced = (blk == 0) | ((blk <= cur) & (blk > cur - NSA_N_LOCAL))
    score = jnp.where(blk > cur, NEG_INF, jnp.where(forced, BIG, p_slc))
    sub = 8
    groups = [score[g * sub:(g + 1) * sub] for g in range(n_blk // sub)]
    cnts = [jnp.zeros((sub, t), F32) for _ in groups]
    for i in range(n_blk):
        ri = score[i:i + 1, :]
        for g, sg in enumerate(groups):
            if g < i // sub:
                beats = ri > sg
            elif g > i // sub:
                beats = ri >= sg
            else:
                beats = (ri > sg) | ((ri == sg) & (blk[g * sub:(g + 1) * sub] > i))
            cnts[g] = cnts[g] + jnp.where(beats, 1.0, 0.0)
    cnt = jnp.concatenate(cnts, axis=0)
    bias = jnp.where(cnt < float(NSA_N_SELECT), 0.0, NEG_INF)
    bias = jnp.concatenate([bias] * N_HEADS, axis=1)
    for r in range(n_blk // blk_per_tile):
        bias_s[r] = bias[r * blk_per_tile:(r + 1) * blk_per_tile, :]

    _flash_reset(m_s, acc_s)
    krow = lax.broadcasted_iota(jnp.int32, (t, 1), 0)
    causal = krow <= qcol
    beyond = krow > qcol

    def scores(br, j):
        off = pl.multiple_of(j * t, t)
        return jnp.dot(k2_ref[0, br, pl.ds(off, t), :], q4, preferred_element_type=F32)

    def slc_scores(j):
        s = scores(SLC, j).reshape(blk_per_tile, NSA_SLC_BLOCK, w4) + bias_s[j][:, None, :]
        return s.reshape(t, w4)

    def flash(br, s, j):
        off = pl.multiple_of(j * t, t)
        _flash_step_t(s, v2T_ref[0, br, :, pl.ds(off, t)], m_s, acc_s, br)

    s_s[SLC] = slc_scores(0)

    @pl.when(qi < win_tiles)
    def _first_tile_is_in_window():
        s_s[WIN] = scores(WIN, 0)

    def old_body(j, carry):
        s = s_s[SLC]
        s_s[SLC] = slc_scores(j + 1)
        flash(SLC, s, j)
        return carry

    lax.fori_loop(0, jnp.maximum(qi - win_tiles, 0), old_body, 0)

    @pl.when(qi >= win_tiles)
    def _oldest_window_tile():
        j = qi - win_tiles
        s_win = jnp.where(beyond, scores(WIN, j), NEG_INF)
        s_slc = s_s[SLC]
        s_s[SLC] = slc_scores(j + 1)
        flash(SLC, s_slc, j)
        s_s[WIN] = scores(WIN, j + 1)
        flash(WIN, s_win, j)

    def win_body(j, carry):
        s_slc = s_s[SLC]
        s_s[SLC] = slc_scores(j + 1)
        flash(SLC, s_slc, j)
        s_win = s_s[WIN]
        s_s[WIN] = scores(WIN, j + 1)
        flash(WIN, s_win, j)
        return carry

    lax.fori_loop(jnp.maximum(qi - win_tiles + 1, 0), qi, win_body, 0)
    flash(SLC, jnp.where(causal, s_s[SLC], NEG_INF), qi)
    flash(WIN, jnp.where(causal, s_s[WIN], NEG_INF), qi)

    g = jax.nn.sigmoid(gl_ref[0]).T
    outs = []
    for h in range(N_HEADS):
        cols = slice(h * t, (h + 1) * t)
        o_s = _flash_out(acc_s, SLC, cols)
        o_w = _flash_out(acc_s, WIN, cols)
        outs.append(g[h:h + 1, :] * o_cmp[h] + g[N_HEADS + h:N_HEADS + h + 1, :] * o_s
                    + g[2 * N_HEADS + h:2 * N_HEADS + h + 1, :] * o_w)
    y = jnp.concatenate(outs, axis=0).T * _silu(z_ref[0])
    y_ref[0] = y.astype(BF16)


def _nsa_call(qT, kcvc, k2, v2T, gl, z, qk_gain, cmp_pe, w_cmp, t=256):
    b, _, seq, _ = k2.shape
    n_chunk = seq // NSA_CMP_STRIDE
    n_blk = seq // NSA_SLC_BLOCK
    half_blk = NSA_CMP_BLOCK // 2
    cw = half_blk * 2 * HEAD_DIM
    assert t % NSA_SLC_BLOCK == 0 and NSA_WINDOW % t == 0 and t & (t - 1) == 0

    wk = w_cmp[0].reshape(NSA_CMP_BLOCK, HEAD_DIM, HEAD_DIM)
    wv = w_cmp[1].reshape(NSA_CMP_BLOCK, HEAD_DIM, HEAD_DIM)
    zero = jnp.zeros_like(wk)
    w_all = jnp.concatenate([jnp.concatenate([wk, zero], axis=2),
                             jnp.concatenate([zero, wv], axis=2)], axis=1)
    hi_lo = lambda w: jnp.stack([w.astype(BF16), (w - w.astype(BF16).astype(F32)).astype(BF16)])
    w_lo = hi_lo(w_all[:half_blk].reshape(cw, 2 * HEAD_DIM))
    w_hi = hi_lo(w_all[half_blk:].reshape(cw, 2 * HEAD_DIM))
    pe_all = jnp.concatenate([cmp_pe[0], cmp_pe[1]], axis=1)
    pe_lo = pe_all[:half_blk].reshape(1, cw)
    pe_hi = pe_all[half_blk:].reshape(1, cw)

    cmp_end = jnp.arange(n_chunk, dtype=jnp.int32) * NSA_CMP_STRIDE + (NSA_CMP_BLOCK - 1)
    cos_c, sin_c = _rope_cos_sin(cmp_end, HEAD_DIM)

    ratio = NSA_SLC_BLOCK // NSA_CMP_STRIDE
    coef = np.convolve(np.ones(ratio), np.ones(NSA_CMP_BLOCK // NSA_CMP_STRIDE))
    cmat = np.zeros((n_blk, n_chunk), np.float32)
    for j in range(n_blk):
        for i, c in enumerate(coef):
            if ratio * j + i < n_chunk - 1:
                cmat[j, ratio * j + i] = c

    tile = pl.BlockSpec((1, t, GROUP_WIDTH), lambda bi, qi: (bi, qi, 0))
    const = lambda shape: pl.BlockSpec(shape, lambda bi, qi: tuple(0 for _ in shape))
    w4 = N_HEADS * t
    return pl.pallas_call(
        functools.partial(_nsa_kernel, seq, t),
        grid=(b, seq // t),
        in_specs=[pl.BlockSpec((1, GROUP_WIDTH, t), lambda bi, qi: (bi, 0, qi)),
                  pl.BlockSpec((1, n_chunk, cw), lambda bi, qi: (bi, 0, 0)),
                  pl.BlockSpec((1, 2, seq, HEAD_DIM), lambda bi, qi: (bi, 0, 0, 0)),
                  pl.BlockSpec((1, 2, VT_ROWS, seq), lambda bi, qi: (bi, 0, 0, 0)),
                  pl.BlockSpec((1, t, LANES), lambda bi, qi: (bi, qi, 0)),
                  tile,
                  const((1, HEAD_DIM)),
                  const((1, cw)), const((1, cw)), const((2, cw, 2 * HEAD_DIM)), const((2, cw, 2 * HEAD_DIM)),
                  const((n_chunk, HEAD_DIM // 2)), const((n_chunk, HEAD_DIM // 2)),
                  const((n_blk, n_chunk))],
        out_specs=tile,
        out_shape=jax.ShapeDtypeStruct((b, seq, GROUP_WIDTH), BF16),
        scratch_shapes=[pltpu.VMEM((2, n_chunk, HEAD_DIM), BF16),
                        pltpu.VMEM((HEAD_DIM, n_chunk), BF16),
                        pltpu.VMEM((seq // t, t // NSA_SLC_BLOCK, w4), F32),
                        pltpu.VMEM((2, w4), F32),
                        pltpu.VMEM((2, VT_ROWS, w4), F32),
                        pltpu.VMEM((2, t, w4), F32)],
        compiler_params=_cparams(("parallel", "arbitrary")),
        name="nsa_attn",
    )(qT, kcvc, k2, v2T, gl, z,
      qk_gain[1][None, :],
      pe_lo, pe_hi, w_lo, w_hi, cos_c, sin_c,
      jnp.asarray(cmat))


def _rope_cos_sin(pos, dim):
    half = dim // 2
    inv_freq = ROPE_THETA ** (-jnp.arange(half, dtype=F32) / half)
    ang = pos.astype(F32)[:, None] * inv_freq[None, :]
    return jnp.cos(ang), jnp.sin(ang)


def _layer(x, mem, layer_idx, tabs_t, norm_gain, w_in, w_out, nsa_qk_gain, nsa_cmp_pe, nsa_w_cmp,
           diff_qk_gain, diff_lambda, diff_subln_gain, mla_cq_gain, mla_ckv_gain, mla_w_uq, mla_w_ukv,
           mla_qk_gain, mem_norm_gain, mem_w_kv, mem_qk_gain):
    b, seq, d = x.shape
    u = _in_proj(x, norm_gain, w_in, nsa_qk_gain, diff_qk_gain, mem_qk_gain, mla_cq_gain, mla_ckv_gain,
                 mla_w_uq, mla_w_ukv, mla_qk_gain, tabs_t)

    y_nsa = _nsa_call(u["nsa_qT"], u["nsa_kcvc"], u["nsa_k2"], u["nsa_v2T"], u["nsa_gl"], u["nsa_z"],
                      nsa_qk_gain, nsa_cmp_pe, nsa_w_cmp)
    lambda_init = 0.8 - 0.6 * math.exp(-0.3 * layer_idx)
    y_diff = _diff_call(u["diff_qT"], u["diff_k"], u["diff_vT"], u["diff_z"], diff_lambda, diff_subln_gain,
                        lambda_init)
    y_mla, y_mem = _mla_mem_call(u["mla_qT"], u["mla_k"], u["mla_vT"], u["mla_z"],
                                 u["mem_qT"], mem, u["mem_z"], mem_norm_gain, mem_w_kv, mem_qk_gain)

    ys = [y.reshape(b * seq, GROUP_WIDTH) for y in (y_nsa, y_diff, y_mla, y_mem)]
    return _out_proj(x.reshape(b * seq, d), ys, w_out.astype(BF16)).reshape(b, seq, d)


def kernel(x, mem, norm_gain, w_in, w_out, nsa_qk_gain, nsa_cmp_pe, nsa_w_cmp, diff_qk_gain, diff_lambda,
           diff_subln_gain, mla_cq_gain, mla_ckv_gain, mla_w_uq, mla_w_ukv, mla_qk_gain, mem_norm_gain,
           mem_w_kv, mem_qk_gain):
    seq = x.shape[1]
    pos = jnp.arange(seq, dtype=jnp.int32)
    cos32, sin32 = _rope_cos_sin(pos, DIFF_D)
    cos64, sin64 = _rope_cos_sin(pos, HEAD_DIM)
    tabs_t = (cos32.T, sin32.T, cos64.T, sin64.T)
    for l in range(DEPTH):
        x = _layer(x, mem, l, tabs_t, norm_gain[l], w_in[l], w_out[l], nsa_qk_gain[l], nsa_cmp_pe[l],
                   nsa_w_cmp[l], diff_qk_gain[l], diff_lambda[l], diff_subln_gain[l], mla_cq_gain[l],
                   mla_ckv_gain[l], mla_w_uq[l], mla_w_ukv[l], mla_qk_gain[l], mem_norm_gain[l], mem_w_kv[l],
                   mem_qk_gain[l])
    return x
```

```python
import functools
import math

import numpy as np
import jax
import jax.numpy as jnp
from jax import lax
from jax.experimental import pallas as pl
from jax.experimental.pallas import tpu as pltpu

F32 = jnp.float32
BF16 = jnp.bfloat16
HIGHEST = lax.Precision.HIGHEST

D_MODEL = 1024
DEPTH = 2
N_HEADS = 4
HEAD_DIM = 64
GROUP_WIDTH = N_HEADS * HEAD_DIM
ROPE_THETA = 10000.0
EPS = 1e-6
NEG_INF = -1e30
BIG = 1e30
LOG2E = 1.4426950408889634

NSA_CMP_BLOCK = 32
NSA_CMP_STRIDE = 16
NSA_SLC_BLOCK = 64
NSA_N_SELECT = 16
NSA_N_LOCAL = 2
NSA_WINDOW = 512
DIFF_D = HEAD_DIM // 2
MLA_Q_RANK = 256
MLA_KV_RANK = 128
MLA_NOPE = 64
MLA_ROPE = 32
MLA_QK = MLA_NOPE + MLA_ROPE

VMEM_LIMIT_BYTES = 48 * 1024 * 1024
LANES = 128
KEY_TILE = 256
IN_PROJ_SUBTILES = 1
VT_ROWS = 80


def _cparams(sem):
    return pltpu.CompilerParams(dimension_semantics=sem, vmem_limit_bytes=VMEM_LIMIT_BYTES)


def _group_ones(width, group):
    g = np.arange(width) // group
    return jnp.asarray(g[:, None] == g[None, :], dtype=BF16)


def _group_rsqrt(x, gmat, denom):
    sq = x * x
    hi = sq.astype(BF16)
    lo = (sq - hi.astype(F32)).astype(BF16)
    ss = jnp.dot(hi, gmat, preferred_element_type=F32) + jnp.dot(lo, gmat, preferred_element_type=F32)
    return lax.rsqrt(ss / denom + EPS)


def _norm_rope_t(u_t, groups, dim, gain, cos=None, sin=None):
    x = u_t.reshape(groups, dim, u_t.shape[-1])
    ms = jnp.mean(x * x, axis=1, keepdims=True)
    x = x * lax.rsqrt(ms + EPS) * gain
    if cos is not None:
        half = dim // 2
        x1, x2 = x[:, :half], x[:, half:]
        x = jnp.concatenate([x1 * cos - x2 * sin, x2 * cos + x1 * sin], axis=1)
    return x.reshape(groups * dim, u_t.shape[-1])


def _silu(z):
    z = z.astype(F32)
    return z * jax.nn.sigmoid(z)


def _ones_rows(n):
    r = lax.broadcasted_iota(jnp.int32, (VT_ROWS - HEAD_DIM, n), 0)
    return jnp.where(r == 0, 1.0, 0.0).astype(BF16)


def _split_bf16(x):
    hi = x.astype(BF16)
    return hi, (x - hi.astype(F32)).astype(BF16)


def _dot_x3(a_hi, a_lo, b_hi, b_lo):
    dot = lambda a, b: jnp.dot(a, b, preferred_element_type=F32)
    return dot(a_hi, b_hi) + dot(a_hi, b_lo) + dot(a_lo, b_hi)


def _flash_step_t(s, v_t, m_ref, acc_ref, i, cols=slice(None)):
    m_old = m_ref[i:i + 1, cols]
    m_new = jnp.maximum(m_old, jnp.max(s, axis=0, keepdims=True))
    alpha = jnp.exp2(m_old - m_new)
    p = jnp.exp2(s - m_new).astype(BF16)
    acc_ref[i, :, cols] = alpha * acc_ref[i, :, cols] + jnp.dot(v_t, p, preferred_element_type=F32)
    m_ref[i:i + 1, cols] = m_new


def _flash_reset(m_ref, acc_ref):
    m_ref[...] = jnp.full(m_ref.shape, NEG_INF, F32)
    acc_ref[...] = jnp.zeros(acc_ref.shape, F32)


def _flash_out(acc_ref, i, cols=slice(None)):
    return acc_ref[i, 0:HEAD_DIM, cols] / acc_ref[i, HEAD_DIM:HEAD_DIM + 1, cols]


def _causal_flash_t(n, k_tile, q_t, v_tile, qi, tq, tk, m_ref, acc_ref, s_ref):
    _flash_reset(m_ref, acc_ref)
    n_diag = tq // tk

    all_q = slice(0, tq)

    def scores(i, off, cols):
        return jnp.dot(k_tile(i, off), q_t(i)[:, cols], preferred_element_type=F32)

    def step(off, cols, off_next, cols_next, mask):
        for i in range(n):
            s = s_ref[i, :, cols]
            if off_next is not None:
                s_ref[i, :, cols_next] = scores(i, off_next, cols_next)
            if mask is not None:
                s = jnp.where(mask, s, NEG_INF)
            _flash_step_t(s, v_tile(i, off), m_ref, acc_ref, i, cols)

    for i in range(n):
        s_ref[i] = scores(i, 0, all_q)

    def body(j, carry):
        step(pl.multiple_of(j * tk, tk), all_q, pl.multiple_of(j * tk + tk, tk), all_q, None)
        return carry

    lax.fori_loop(0, qi * n_diag, body, 0)
    krow = lax.broadcasted_iota(jnp.int32, (tk, tq), 0)
    qcol = lax.broadcasted_iota(jnp.int32, (tk, tq), 1)
    causal = krow <= qcol
    diag_cols = [slice(d * tk, tq) for d in range(n_diag)]
    for d in range(n_diag):
        off = pl.multiple_of(qi * tq + d * tk, tk)
        last = d + 1 == n_diag
        off_next = None if last else pl.multiple_of(qi * tq + (d + 1) * tk, tk)
        step(off, diag_cols[d], off_next, None if last else diag_cols[d + 1], causal[:, 0:tq - d * tk])


_ROW_SEGS = (("nsa_kcvc", 128), ("nsa_gl", 128), ("nsa_z", 256), ("diff_z", 256), ("mla_z", 256),
             ("mem_z", 256))
_COL_SEGS = (("nsa_q", 256), ("diff_q", 256), ("mem_q", 256), ("diff_k", 256), ("nsa_k2", 128),
             ("diff_v", 256), ("nsa_v2", 128), ("mla_cq", 256), ("mla_ckv", 128), ("mla_kr", 32))


def _seg_offsets(segs):
    out, off = {}, 0
    for name, w in segs:
        out[name] = (off, w)
        off += w
    return out, off


_ROW_OFF, _ROW_W = _seg_offsets(_ROW_SEGS)
_COL_OFF, _COL_W = _seg_offsets(_COL_SEGS)


def _split_w_in(w_in):
    c = lambda a, b: w_in[:, a:b]
    z = lambda n: jnp.zeros((w_in.shape[0], n), w_in.dtype)
    w_row = jnp.concatenate([
        c(256, 384),
        c(640, 652), z(116),
        c(652, 908), c(1676, 1932), c(2348, 2604), c(2860, 3116),
    ], axis=1)
    w_col = jnp.concatenate([
        c(0, 256), c(908, 1164), c(2604, 2860),
        c(1164, 1420),
        c(384, 448), c(512, 576),
        c(1420, 1676),
        c(448, 512), c(576, 640),
        c(1932, 2188), c(2188, 2316), c(2316, 2348),
    ], axis=1)
    return w_row.astype(BF16), w_col.T.astype(BF16)


def _mla_head_norm_t(x_t, gain):
    ms = jnp.sum(x_t * x_t, axis=0, keepdims=True) / float(MLA_QK)
    return x_t * lax.rsqrt(ms + EPS) * gain


def _rope_t(x_t, cos, sin):
    half = x_t.shape[0] // 2
    x1, x2 = x_t[:half], x_t[half:]
    return jnp.concatenate([x1 * cos - x2 * sin, x2 * cos + x1 * sin], axis=0)


def _in_proj_kernel(x_ref, g_ref, wrow_ref, wcol_ref, nqg_ref, dqg_ref, dkg_ref, k2g_ref, mqg_ref,
                    cqg_ref, ckvg_ref, wuq_ref, wuk_ref, wuv_ref, lqg_ref, lkg_ref,
                    c32_ref, s32_ref, c64_ref, s64_ref,
                    kcvc_o, gl_o, nz_o, dz_o, mz_o, ez_o, dk_o, k2_o,
                    nq_o, dq_o, mq_o, dv_o, v2_o, lq_o, lk_o, lv_o, kcvc_s):
    sub = x_ref.shape[0] // IN_PROJ_SUBTILES
    for part in range(IN_PROJ_SUBTILES):
        _in_proj_subtile(slice(part * sub, (part + 1) * sub), part * (sub // NSA_CMP_STRIDE),
                         x_ref, g_ref, wrow_ref, wcol_ref, nqg_ref, dqg_ref, dkg_ref, k2g_ref, mqg_ref,
                         cqg_ref, ckvg_ref, wuq_ref, wuk_ref, wuv_ref, lqg_ref, lkg_ref,
                         c32_ref, s32_ref, c64_ref, s64_ref,
                         kcvc_o, gl_o, nz_o, dz_o, mz_o, ez_o, dk_o, k2_o,
                         nq_o, dq_o, mq_o, dv_o, v2_o, lq_o, lk_o, lv_o, kcvc_s)


def _in_proj_subtile(rows, chunk0,
                     x_ref, g_ref, wrow_ref, wcol_ref, nqg_ref, dqg_ref, dkg_ref, k2g_ref, mqg_ref,
                     cqg_ref, ckvg_ref, wuq_ref, wuk_ref, wuv_ref, lqg_ref, lkg_ref,
                     c32_ref, s32_ref, c64_ref, s64_ref,
                     kcvc_o, gl_o, nz_o, dz_o, mz_o, ez_o, dk_o, k2_o,
                     nq_o, dq_o, mq_o, dv_o, v2_o, lq_o, lk_o, lv_o, kcvc_s):
    x = x_ref[rows, :]
    tm = x.shape[0]
    ms = jnp.mean(x * x, axis=-1, keepdims=True)
    h = x * lax.rsqrt(ms + EPS) * g_ref[...]
    hb = h.astype(BF16)
    h_t = h.T.astype(BF16)

    def row(name):
        off, w = _ROW_OFF[name]
        return jnp.dot(hb, wrow_ref[:, off:off + w], preferred_element_type=F32)

    u_t = jnp.dot(wcol_ref[...], h_t, preferred_element_type=F32)

    def col(name):
        off, w = _COL_OFF[name]
        return u_t[off:off + w]

    kcvc_s[rows, :] = row("nsa_kcvc")
    n_chunk_rows = tm // NSA_CMP_STRIDE
    for tok in range(NSA_CMP_STRIDE):
        kcvc_o[0, chunk0:chunk0 + n_chunk_rows, tok * LANES:(tok + 1) * LANES] = (
            kcvc_s[pl.ds(rows.start + tok, n_chunk_rows, stride=NSA_CMP_STRIDE), :])
    gl_o[rows, :] = row("nsa_gl")
    nz_o[rows, :] = row("nsa_z").astype(BF16)
    dz_o[rows, :] = row("diff_z").astype(BF16)
    mz_o[rows, :] = row("mla_z").astype(BF16)
    ez_o[rows, :] = row("mem_z").astype(BF16)

    c32, s32, c64, s64 = c32_ref[:, rows], s32_ref[:, rows], c64_ref[:, rows], s64_ref[:, rows]
    g3 = lambda ref, groups: ref[...].reshape(groups, -1, 1)
    nq_o[0, :, rows] = (_norm_rope_t(col("nsa_q"), N_HEADS, HEAD_DIM, g3(nqg_ref, 1), c64, s64)
                        * (HEAD_DIM ** -0.5 * LOG2E))
    dq = _norm_rope_t(col("diff_q"), 2 * N_HEADS, DIFF_D, g3(dqg_ref, 1), c32, s32) * (DIFF_D ** -0.5 * LOG2E)
    dq_o[0, :, rows] = dq.astype(BF16)
    mq = _norm_rope_t(col("mem_q"), N_HEADS, HEAD_DIM, g3(mqg_ref, 1)) * (HEAD_DIM ** -0.5 * LOG2E)
    mq_o[0, :, rows] = mq.astype(BF16)
    dk = _norm_rope_t(col("diff_k"), 2 * N_HEADS, DIFF_D, g3(dkg_ref, 1), c32, s32).T
    for mp in range(2 * N_HEADS):
        dk_o[0, mp, rows, :] = dk[:, mp * DIFF_D:(mp + 1) * DIFF_D].astype(BF16)
    k2 = _norm_rope_t(col("nsa_k2"), 2, HEAD_DIM, g3(k2g_ref, 2), c64, s64).T
    for br in range(2):
        k2_o[0, br, rows, :] = k2[:, br * HEAD_DIM:(br + 1) * HEAD_DIM].astype(BF16)
    ones = _ones_rows(tm)
    dv = col("diff_v")
    for hd in range(N_HEADS):
        dv_o[0, hd, 0:HEAD_DIM, rows] = dv[hd * HEAD_DIM:(hd + 1) * HEAD_DIM].astype(BF16)
        dv_o[0, hd, HEAD_DIM:VT_ROWS, rows] = ones
    v2 = col("nsa_v2")
    for br in range(2):
        v2_o[0, br, 0:HEAD_DIM, rows] = v2[br * HEAD_DIM:(br + 1) * HEAD_DIM].astype(BF16)
        v2_o[0, br, HEAD_DIM:VT_ROWS, rows] = ones

    def latent(name, gain_ref):
        c = col(name)
        return (c * lax.rsqrt(jnp.mean(c * c, axis=0, keepdims=True) + EPS) * gain_ref[...]).astype(BF16)

    rope_rows = slice(MLA_NOPE, MLA_QK)
    qa = jnp.dot(wuq_ref[...], latent("mla_cq", cqg_ref), preferred_element_type=F32)
    lqg = lqg_ref[...] * (MLA_QK ** -0.5 * LOG2E)
    ckv = latent("mla_ckv", ckvg_ref)
    kn = jnp.dot(wuk_ref[...], ckv, preferred_element_type=F32)
    kr = _rope_t(col("mla_kr"), c32, s32)
    zpad = jnp.zeros((LANES - MLA_QK, tm), F32)
    lv = jnp.dot(wuv_ref[...], ckv, preferred_element_type=F32)
    for hd in range(N_HEADS):
        q_h = qa[hd * LANES:(hd + 1) * LANES]
        q_h = jnp.concatenate([q_h[:MLA_NOPE], _rope_t(q_h[rope_rows], c32, s32), q_h[MLA_QK:]], axis=0)
        lq_o[0, hd * LANES:(hd + 1) * LANES, rows] = _mla_head_norm_t(q_h, lqg).astype(BF16)
        k_h = jnp.concatenate([kn[hd * LANES:hd * LANES + MLA_NOPE], kr, zpad], axis=0)
        lk_o[0, hd, rows, :] = _mla_head_norm_t(k_h, lkg_ref[...]).T.astype(BF16)
        lv_o[0, hd, 0:HEAD_DIM, rows] = lv[hd * HEAD_DIM:(hd + 1) * HEAD_DIM].astype(BF16)
        lv_o[0, hd, HEAD_DIM:VT_ROWS, rows] = ones


def _in_proj(x, gain, w_in, nsa_qk_gain, diff_qk_gain, mem_qk_gain, mla_cq_gain, mla_ckv_gain, mla_w_uq,
             mla_w_ukv, mla_qk_gain, tabs_t, tm=512):
    b, seq, d = x.shape
    n = b * seq
    nb = seq // tm
    w_row, w_col = _split_w_in(w_in)
    c32, s32, c64, s64 = tabs_t
    npad = LANES - MLA_QK
    uq = jnp.pad(mla_w_uq.reshape(MLA_Q_RANK, N_HEADS, MLA_QK), ((0, 0), (0, 0), (0, npad)))
    w_uq_t = uq.reshape(MLA_Q_RANK, N_HEADS * LANES).T.astype(BF16)
    ukv = mla_w_ukv.reshape(MLA_KV_RANK, N_HEADS, MLA_NOPE + HEAD_DIM)
    uk = jnp.pad(ukv[:, :, :MLA_NOPE], ((0, 0), (0, 0), (0, LANES - MLA_NOPE)))
    w_uk_t = uk.reshape(MLA_KV_RANK, N_HEADS * LANES).T.astype(BF16)
    w_uv_t = ukv[:, :, MLA_NOPE:].reshape(MLA_KV_RANK, GROUP_WIDTH).T.astype(BF16)
    pad_gain = lambda g: jnp.pad(g, (0, npad))[:, None]

    rowspec = lambda w: pl.BlockSpec((tm, w), lambda i: (i, 0))
    colspec = lambda r: pl.BlockSpec((1, r, tm), lambda i: (i // nb, 0, i % nb))
    vtspec = lambda c: pl.BlockSpec((1, c, VT_ROWS, tm), lambda i: (i // nb, 0, 0, i % nb))
    const = lambda shape: pl.BlockSpec(shape, lambda i: tuple(0 for _ in shape))
    tabspec = lambda r: pl.BlockSpec((r, tm), lambda i: (0, i % nb))
    row_out = lambda w, dt: jax.ShapeDtypeStruct((n, w), dt)
    col_out = lambda r, dt: jax.ShapeDtypeStruct((b, r, seq), dt)
    vt_out = lambda c: jax.ShapeDtypeStruct((b, c, VT_ROWS, seq), BF16)
    slabspec = lambda c, w: pl.BlockSpec((1, c, tm, w), lambda i: (i // nb, 0, i % nb, 0))
    slab_out = lambda c, w: jax.ShapeDtypeStruct((b, c, seq, w), BF16)
    outs = pl.pallas_call(
        _in_proj_kernel,
        grid=(n // tm,),
        in_specs=[rowspec(d), const((1, d)), const((d, _ROW_W)), const((_COL_W, d)),
                  const((HEAD_DIM, 1)), const((DIFF_D, 1)), const((DIFF_D, 1)), const((2 * HEAD_DIM, 1)),
                  const((HEAD_DIM, 1)),
                  const((MLA_Q_RANK, 1)), const((MLA_KV_RANK, 1)),
                  const((N_HEADS * LANES, MLA_Q_RANK)), const((N_HEADS * LANES, MLA_KV_RANK)),
                  const((GROUP_WIDTH, MLA_KV_RANK)), const((LANES, 1)), const((LANES, 1)),
                  tabspec(DIFF_D // 2), tabspec(DIFF_D // 2), tabspec(HEAD_DIM // 2), tabspec(HEAD_DIM // 2)],
        out_specs=[pl.BlockSpec((1, tm // NSA_CMP_STRIDE, NSA_CMP_STRIDE * LANES), lambda i: (i // nb, i % nb, 0)),
                   rowspec(128), rowspec(256), rowspec(256), rowspec(256), rowspec(256),
                   slabspec(2 * N_HEADS, DIFF_D), slabspec(2, HEAD_DIM),
                   colspec(256), colspec(256), colspec(256), vtspec(N_HEADS), vtspec(2),
                   colspec(N_HEADS * LANES), slabspec(N_HEADS, LANES), vtspec(N_HEADS)],
        out_shape=[jax.ShapeDtypeStruct((b, seq // NSA_CMP_STRIDE, NSA_CMP_STRIDE * LANES), F32),
                   row_out(128, F32), row_out(256, BF16), row_out(256, BF16),
                   row_out(256, BF16), row_out(256, BF16), slab_out(2 * N_HEADS, DIFF_D), slab_out(2, HEAD_DIM),
                   col_out(256, F32), col_out(256, BF16), col_out(256, BF16), vt_out(N_HEADS), vt_out(2),
                   col_out(N_HEADS * LANES, BF16), slab_out(N_HEADS, LANES), vt_out(N_HEADS)],
        scratch_shapes=[pltpu.VMEM((tm, LANES), F32)],
        compiler_params=_cparams(("parallel",)),
        name="in_proj",
    )(x.reshape(n, d), gain[None, :], w_row, w_col,
      nsa_qk_gain[0][:, None], diff_qk_gain[0][:, None], diff_qk_gain[1][:, None],
      jnp.concatenate([nsa_qk_gain[2], nsa_qk_gain[3]])[:, None], mem_qk_gain[0][:, None],
      mla_cq_gain[:, None], mla_ckv_gain[:, None], w_uq_t, w_uk_t, w_uv_t,
      pad_gain(mla_qk_gain[0]), pad_gain(mla_qk_gain[1]),
      c32, s32, c64, s64)
    names = ("nsa_kcvc", "nsa_gl", "nsa_z", "diff_z", "mla_z", "mem_z", "diff_k", "nsa_k2",
             "nsa_qT", "diff_qT", "mem_qT", "diff_vT", "nsa_v2T", "mla_qT", "mla_k", "mla_vT")
    u = dict(zip(names, outs))
    for name in names[1:6]:
        u[name] = u[name].reshape(b, seq, -1)
    return u


def _out_proj_kernel(x_ref, y0_ref, y1_ref, y2_ref, y3_ref, w_ref, o_ref):
    acc = x_ref[...]
    for g, y_ref in enumerate((y0_ref, y1_ref, y2_ref, y3_ref)):
        acc = acc + jnp.dot(y_ref[...], w_ref[g * GROUP_WIDTH:(g + 1) * GROUP_WIDTH, :],
                            preferred_element_type=F32)
    o_ref[...] = acc


def _out_proj(x2, ys, w_out_b, tm=512):
    n = x2.shape[0]
    yspec = pl.BlockSpec((tm, GROUP_WIDTH), lambda i: (i, 0))
    return pl.pallas_call(
        _out_proj_kernel,
        grid=(n // tm,),
        in_specs=[pl.BlockSpec((tm, D_MODEL), lambda i: (i, 0)), yspec, yspec, yspec, yspec,
                  pl.BlockSpec((D_MODEL, D_MODEL), lambda i: (0, 0))],
        out_specs=pl.BlockSpec((tm, D_MODEL), lambda i: (i, 0)),
        out_shape=jax.ShapeDtypeStruct((n, D_MODEL), F32),
        compiler_params=_cparams(("parallel",)),
        name="out_proj",
    )(x2, *ys, w_out_b)


def _diff_kernel(lambda_init, t, qT_ref, k_ref, vT_ref, z_ref, lam_ref, sg_ref, y_ref, m_s, acc_s, s_s):
    qi = pl.program_id(1)
    n_maps = 2 * N_HEADS
    qT = qT_ref[0]
    _causal_flash_t(
        n_maps,
        lambda i, off: k_ref[0, i, pl.ds(off, KEY_TILE), :],
        lambda i: qT[i * DIFF_D:(i + 1) * DIFF_D],
        lambda i, off: vT_ref[0, i // 2, :, pl.ds(off, KEY_TILE)],
        qi, t, KEY_TILE, m_s, acc_s, s_s)

    lam = lam_ref[...]
    lmbda = (jnp.exp(jnp.sum(lam[0:1] * lam[1:2], axis=-1, keepdims=True))
             - jnp.exp(jnp.sum(lam[2:3] * lam[3:4], axis=-1, keepdims=True)) + lambda_init)
    outs = []
    for h in range(N_HEADS):
        d = _flash_out(acc_s, 2 * h) - lmbda * _flash_out(acc_s, 2 * h + 1)
        ms = jnp.mean(d * d, axis=0, keepdims=True)
        outs.append(d * lax.rsqrt(ms + EPS) * sg_ref[...] * (1.0 - lambda_init))
    y = jnp.concatenate(outs, axis=0).T * _silu(z_ref[0])
    y_ref[0] = y.astype(BF16)


def _diff_call(qT, k, vT, z, lam, subln_gain, lambda_init, t=512):
    b, _, seq, _ = k.shape
    const = lambda shape: pl.BlockSpec(shape, lambda bi, qi: tuple(0 for _ in shape))
    tile = pl.BlockSpec((1, t, GROUP_WIDTH), lambda bi, qi: (bi, qi, 0))
    return pl.pallas_call(
        functools.partial(_diff_kernel, lambda_init, t),
        grid=(b, seq // t),
        in_specs=[pl.BlockSpec((1, GROUP_WIDTH, t), lambda bi, qi: (bi, 0, qi)),
                  pl.BlockSpec((1, 2 * N_HEADS, seq, DIFF_D), lambda bi, qi: (bi, 0, 0, 0)),
                  pl.BlockSpec((1, N_HEADS, VT_ROWS, seq), lambda bi, qi: (bi, 0, 0, 0)),
                  tile, const((4, DIFF_D)), const((HEAD_DIM, 1))],
        out_specs=tile,
        out_shape=jax.ShapeDtypeStruct((b, seq, GROUP_WIDTH), BF16),
        scratch_shapes=[pltpu.VMEM((2 * N_HEADS, t), F32),
                        pltpu.VMEM((2 * N_HEADS, VT_ROWS, t), F32),
                        pltpu.VMEM((2 * N_HEADS, KEY_TILE, t), F32)],
        compiler_params=_cparams(("parallel", "parallel")),
        name="diff_attn",
    )(qT, k, vT, z, lam, subln_gain[:, None])


def _mla_mem_kernel(t, qT_ref, k_ref, vT_ref, z_ref,
                    mqT_ref, mem_ref, mz_ref, mg_ref, wk_ref, wvT_ref, kg_ref, gm_ref,
                    y_ref, ymem_ref, m_s, acc_s, s_s, mk_s, mvT_s, mm_s, macc_s):
    qi = pl.program_id(1)
    nt = (((1,), (1,)), ((), ()))
    m_len = mem_ref.shape[1]

    @pl.when(qi == 0)
    def _prep_memory_kv():
        mem = mem_ref[0]
        ms = jnp.mean(mem * mem, axis=-1, keepdims=True)
        mb = (mem * lax.rsqrt(ms + EPS) * mg_ref[...]).astype(BF16)
        k = jnp.dot(mb, wk_ref[...], preferred_element_type=F32)
        kn = k * _group_rsqrt(k, gm_ref[...], float(HEAD_DIM)) * kg_ref[...]
        vT = lax.dot_general(wvT_ref[...], mb, nt, preferred_element_type=F32)
        for h in range(N_HEADS):
            mk_s[h] = kn[:, h * HEAD_DIM:(h + 1) * HEAD_DIM].astype(BF16)
            mvT_s[h, 0:HEAD_DIM, :] = vT[h * HEAD_DIM:(h + 1) * HEAD_DIM].astype(BF16)
            mvT_s[h, HEAD_DIM:VT_ROWS, :] = _ones_rows(m_len)

    mqT = mqT_ref[0]
    _flash_reset(mm_s, macc_s)
    ss = [jnp.dot(mk_s[h], mqT[h * HEAD_DIM:(h + 1) * HEAD_DIM], preferred_element_type=F32)
          for h in range(N_HEADS)]
    for h in range(N_HEADS):
        _flash_step_t(ss[h], mvT_s[h], mm_s, macc_s, h)

    qT = qT_ref[0]
    _causal_flash_t(
        N_HEADS,
        lambda h, off: k_ref[0, h, pl.ds(off, KEY_TILE), :],
        lambda h: qT[h * LANES:(h + 1) * LANES],
        lambda h, off: vT_ref[0, h, :, pl.ds(off, KEY_TILE)],
        qi, t, KEY_TILE, m_s, acc_s, s_s)

    y_mem = jnp.concatenate([_flash_out(macc_s, h) for h in range(N_HEADS)], axis=0).T * _silu(mz_ref[0])
    ymem_ref[0] = y_mem.astype(BF16)
    y = jnp.concatenate([_flash_out(acc_s, h) for h in range(N_HEADS)], axis=0).T * _silu(z_ref[0])
    y_ref[0] = y.astype(BF16)


def _mla_mem_call(qT, k, vT, z, mem_qT, mem, mem_z, mem_gain, w_kv, mem_qk_gain, t=512):
    b, _, seq, _ = k.shape
    m_len = mem.shape[1]
    w_k = w_kv[:, :GROUP_WIDTH].astype(BF16)
    w_vT = w_kv[:, GROUP_WIDTH:].T.astype(BF16)
    tile = pl.BlockSpec((1, t, GROUP_WIDTH), lambda bi, qi: (bi, qi, 0))
    const = lambda shape: pl.BlockSpec(shape, lambda bi, qi: tuple(0 for _ in shape))
    out = jax.ShapeDtypeStruct((b, seq, GROUP_WIDTH), BF16)
    return pl.pallas_call(
        functools.partial(_mla_mem_kernel, t),
        grid=(b, seq // t),
        in_specs=[pl.BlockSpec((1, N_HEADS * LANES, t), lambda bi, qi: (bi, 0, qi)),
                  pl.BlockSpec((1, N_HEADS, seq, LANES), lambda bi, qi: (bi, 0, 0, 0)),
                  pl.BlockSpec((1, N_HEADS, VT_ROWS, seq), lambda bi, qi: (bi, 0, 0, 0)),
                  tile,
                  pl.BlockSpec((1, GROUP_WIDTH, t), lambda bi, qi: (bi, 0, qi)),
                  pl.BlockSpec((1, m_len, D_MODEL), lambda bi, qi: (bi, 0, 0)), tile,
                  const((1, D_MODEL)), const((D_MODEL, GROUP_WIDTH)), const((GROUP_WIDTH, D_MODEL)),
                  const((1, GROUP_WIDTH)), const((GROUP_WIDTH, GROUP_WIDTH))],
        out_specs=[tile, tile],
        out_shape=[out, out],
        scratch_shapes=[pltpu.VMEM((N_HEADS, t), F32),
                        pltpu.VMEM((N_HEADS, VT_ROWS, t), F32),
                        pltpu.VMEM((N_HEADS, KEY_TILE, t), F32),
                        pltpu.VMEM((N_HEADS, m_len, HEAD_DIM), BF16),
                        pltpu.VMEM((N_HEADS, VT_ROWS, m_len), BF16),
                        pltpu.VMEM((N_HEADS, t), F32),
                        pltpu.VMEM((N_HEADS, VT_ROWS, t), F32)],
        compiler_params=_cparams(("parallel", "arbitrary")),
        name="mla_mem_attn",
    )(qT, k, vT, z, mem_qT, mem, mem_z, mem_gain[None, :], w_k, w_vT,
      jnp.tile(mem_qk_gain[1], N_HEADS)[None, :], _group_ones(GROUP_WIDTH, HEAD_DIM))


def _nsa_kernel(seq, t,
                qT_ref, kcvc_ref, k2_ref, v2T_ref, gl_ref, z_ref,
                cg_ref, pelo_ref, pehi_ref, wlo_ref, whi_ref, cosc_ref, sinc_ref,
                cmat_ref,
                y_ref, kc_s, vcT_s, bias_s, m_s, acc_s, s_s):
    qi = pl.program_id(1)
    half = HEAD_DIM // 2
    n_chunk = seq // NSA_CMP_STRIDE
    n_cmp = n_chunk - 1
    n_blk = seq // NSA_SLC_BLOCK
    blk_per_tile = t // NSA_SLC_BLOCK
    win_tiles = NSA_WINDOW // t
    w4 = N_HEADS * t
    SLC, WIN = 0, 1

    @pl.when(qi == 0)
    def _prep():
        ch = kcvc_ref[0]
        a = _dot_x3(*_split_bf16(ch + pelo_ref[...]), wlo_ref[0], wlo_ref[1])
        bm = _dot_x3(*_split_bf16(ch + pehi_ref[...]), whi_ref[0], whi_ref[1])
        cmp = a + pltpu.roll(bm, n_chunk - 1, 0)
        kc = cmp[:, :HEAD_DIM]
        ms = jnp.mean(kc * kc, axis=-1, keepdims=True)
        kc = kc * lax.rsqrt(ms + EPS) * cg_ref[...]
        x1, x2 = kc[:, :half], kc[:, half:]
        c, s = cosc_ref[...], sinc_ref[...]
        kc_hi, kc_lo = _split_bf16(jnp.concatenate([x1 * c - x2 * s, x2 * c + x1 * s], axis=-1))
        kc_s[0] = kc_hi
        kc_s[1] = kc_lo
        vcT_s[...] = cmp.T[HEAD_DIM:].astype(BF16)

    qs = qi * t
    qT = qT_ref[0]
    q4f = jnp.concatenate([qT[h * HEAD_DIM:(h + 1) * HEAD_DIM] for h in range(N_HEADS)], axis=1)
    q4 = q4f.astype(BF16)
    qcol = lax.broadcasted_iota(jnp.int32, (1, w4), 1) & (t - 1)
    pos_c = qs + qcol

    q_lo = (q4f - q4.astype(F32)).astype(BF16)
    heads = [slice(h * t, (h + 1) * t) for h in range(N_HEADS)]
    scs = [_dot_x3(kc_s[0], kc_s[1], q4[:, c], q_lo[:, c]) for c in heads]
    n_idx = lax.broadcasted_iota(jnp.int32, (n_chunk, 1), 0)
    cvalid = (n_idx * NSA_CMP_STRIDE + (NSA_CMP_BLOCK - 1) <= pos_c[:, :t]) & (n_idx < n_cmp)
    o_cmp, pg = [], None
    for h in range(N_HEADS):
        sc = jnp.where(cvalid, scs[h], NEG_INF)
        e = jnp.exp2(sc - jnp.max(sc, axis=0, keepdims=True))
        p = jnp.where(cvalid, e / jnp.sum(e, axis=0, keepdims=True), 0.0)
        o_cmp.append(jnp.dot(vcT_s[...], p.astype(BF16), preferred_element_type=F32))
        pg = p if pg is None else pg + p

    p_slc = jnp.dot(cmat_ref[...], pg, precision=HIGHEST, preferred_element_type=F32)
    blk = lax.broadcasted_iota(jnp.int32, (n_blk, 1), 0)
    cur = lax.shift_right_logical(pos_c[:, :t], NSA_SLC_BLOCK.bit_length() - 1)
    forced = (blk == 0) | ((blk <= cur) & (blk > cur - NSA_N_LOCAL))
    score = jnp.where(blk > cur, NEG_INF, jnp.where(forced, BIG, p_slc))
    sub = 8
    groups = [score[g * sub:(g + 1) * sub] for g in range(n_blk // sub)]
    cnts = [jnp.zeros((sub, t), F32) for _ in groups]
    for i in range(n_blk):
        ri = score[i:i + 1, :]
        for g, sg in enumerate(groups):
            if g < i // sub:
                beats = ri > sg
            elif g > i // sub:
                beats = ri >= sg
            else:
                beats = (ri > sg) | ((ri == sg) & (blk[g * sub:(g + 1) * sub] > i))
            cnts[g] = cnts[g] + jnp.where(beats, 1.0, 0.0)
    cnt = jnp.concatenate(cnts, axis=0)
    bias = jnp.where(cnt < float(NSA_N_SELECT), 0.0, NEG_INF)
    bias = jnp.concatenate([bias] * N_HEADS, axis=1)
    for r in range(n_blk // blk_per_tile):
        bias_s[r] = bias[r * blk_per_tile:(r + 1) * blk_per_tile, :]

    _flash_reset(m_s, acc_s)
    krow = lax.broadcasted_iota(jnp.int32, (t, 1), 0)
    causal = krow <= qcol
    beyond = krow > qcol

    def scores(br, j):
        off = pl.multiple_of(j * t, t)
        return jnp.dot(k2_ref[0, br, pl.ds(off, t), :], q4, preferred_element_type=F32)

    def slc_scores(j):
        s = scores(SLC, j).reshape(blk_per_tile, NSA_SLC_BLOCK, w4) + bias_s[j][:, None, :]
        return s.reshape(t, w4)

    def flash(br, s, j):
        off = pl.multiple_of(j * t, t)
        _flash_step_t(s, v2T_ref[0, br, :, pl.ds(off, t)], m_s, acc_s, br)

    s_s[SLC] = slc_scores(0)

    @pl.when(qi < win_tiles)
    def _first_tile_is_in_window():
        s_s[WIN] = scores(WIN, 0)

    def old_body(j, carry):
        s = s_s[SLC]
        s_s[SLC] = slc_scores(j + 1)
        flash(SLC, s, j)
        return carry

    lax.fori_loop(0, jnp.maximum(qi - win_tiles, 0), old_body, 0)

    @pl.when(qi >= win_tiles)
    def _oldest_window_tile():
        j = qi - win_tiles
        s_win = jnp.where(beyond, scores(WIN, j), NEG_INF)
        s_slc = s_s[SLC]
        s_s[SLC] = slc_scores(j + 1)
        flash(SLC, s_slc, j)
        s_s[WIN] = scores(WIN, j + 1)
        flash(WIN, s_win, j)

    def win_body(j, carry):
        s_slc = s_s[SLC]
        s_s[SLC] = slc_scores(j + 1)
        flash(SLC, s_slc, j)
        s_win = s_s[WIN]
        s_s[WIN] = scores(WIN, j + 1)
        flash(WIN, s_win, j)
        return carry

    lax.fori_loop(jnp.maximum(qi - win_tiles + 1, 0), qi, win_body, 0)
    flash(SLC, jnp.where(causal, s_s[SLC], NEG_INF), qi)
    flash(WIN, jnp.where(causal, s_s[WIN], NEG_INF), qi)

    g = jax.nn.sigmoid(gl_ref[0]).T
    outs = []
    for h in range(N_HEADS):
        cols = slice(h * t, (h + 1) * t)
        o_s = _flash_out(acc_s, SLC, cols)
        o_w = _flash_out(acc_s, WIN, cols)
        outs.append(g[h:h + 1, :] * o_cmp[h] + g[N_HEADS + h:N_HEADS + h + 1, :] * o_s
                    + g[2 * N_HEADS + h:2 * N_HEADS + h + 1, :] * o_w)
    y = jnp.concatenate(outs, axis=0).T * _silu(z_ref[0])
    y_ref[0] = y.astype(BF16)


def _nsa_call(qT, kcvc, k2, v2T, gl, z, qk_gain, cmp_pe, w_cmp, t=256):
    b, _, seq, _ = k2.shape
    n_chunk = seq // NSA_CMP_STRIDE
    n_blk = seq // NSA_SLC_BLOCK
    half_blk = NSA_CMP_BLOCK // 2
    cw = half_blk * 2 * HEAD_DIM
    assert t % NSA_SLC_BLOCK == 0 and NSA_WINDOW % t == 0 and t & (t - 1) == 0

    wk = w_cmp[0].reshape(NSA_CMP_BLOCK, HEAD_DIM, HEAD_DIM)
    wv = w_cmp[1].reshape(NSA_CMP_BLOCK, HEAD_DIM, HEAD_DIM)
    zero = jnp.zeros_like(wk)
    w_all = jnp.concatenate([jnp.concatenate([wk, zero], axis=2),
                             jnp.concatenate([zero, wv], axis=2)], axis=1)
    hi_lo = lambda w: jnp.stack([w.astype(BF16), (w - w.astype(BF16).astype(F32)).astype(BF16)])
    w_lo = hi_lo(w_all[:half_blk].reshape(cw, 2 * HEAD_DIM))
    w_hi = hi_lo(w_all[half_blk:].reshape(cw, 2 * HEAD_DIM))
    pe_all = jnp.concatenate([cmp_pe[0], cmp_pe[1]], axis=1)
    pe_lo = pe_all[:half_blk].reshape(1, cw)
    pe_hi = pe_all[half_blk:].reshape(1, cw)

    cmp_end = jnp.arange(n_chunk, dtype=jnp.int32) * NSA_CMP_STRIDE + (NSA_CMP_BLOCK - 1)
    cos_c, sin_c = _rope_cos_sin(cmp_end, HEAD_DIM)

    ratio = NSA_SLC_BLOCK // NSA_CMP_STRIDE
    coef = np.convolve(np.ones(ratio), np.ones(NSA_CMP_BLOCK // NSA_CMP_STRIDE))
    cmat = np.zeros((n_blk, n_chunk), np.float32)
    for j in range(n_blk):
        for i, c in enumerate(coef):
            if ratio * j + i < n_chunk - 1:
                cmat[j, ratio * j + i] = c

    tile = pl.BlockSpec((1, t, GROUP_WIDTH), lambda bi, qi: (bi, qi, 0))
    const = lambda shape: pl.BlockSpec(shape, lambda bi, qi: tuple(0 for _ in shape))
    w4 = N_HEADS * t
    return pl.pallas_call(
        functools.partial(_nsa_kernel, seq, t),
        grid=(b, seq // t),
        in_specs=[pl.BlockSpec((1, GROUP_WIDTH, t), lambda bi, qi: (bi, 0, qi)),
                  pl.BlockSpec((1, n_chunk, cw), lambda bi, qi: (bi, 0, 0)),
                  pl.BlockSpec((1, 2, seq, HEAD_DIM), lambda bi, qi: (bi, 0, 0, 0)),
                  pl.BlockSpec((1, 2, VT_ROWS, seq), lambda bi, qi: (bi, 0, 0, 0)),
                  pl.BlockSpec((1, t, LANES), lambda bi, qi: (bi, qi, 0)),
                  tile,
                  const((1, HEAD_DIM)),
                  const((1, cw)), const((1, cw)), const((2, cw, 2 * HEAD_DIM)), const((2, cw, 2 * HEAD_DIM)),
                  const((n_chunk, HEAD_DIM // 2)), const((n_chunk, HEAD_DIM // 2)),
                  const((n_blk, n_chunk))],
        out_specs=tile,
        out_shape=jax.ShapeDtypeStruct((b, seq, GROUP_WIDTH), BF16),
        scratch_shapes=[pltpu.VMEM((2, n_chunk, HEAD_DIM), BF16),
                        pltpu.VMEM((HEAD_DIM, n_chunk), BF16),
                        pltpu.VMEM((seq // t, t // NSA_SLC_BLOCK, w4), F32),
                        pltpu.VMEM((2, w4), F32),
                        pltpu.VMEM((2, VT_ROWS, w4), F32),
                        pltpu.VMEM((2, t, w4), F32)],
        compiler_params=_cparams(("parallel", "arbitrary")),
        name="nsa_attn",
    )(qT, kcvc, k2, v2T, gl, z,
      qk_gain[1][None, :],
      pe_lo, pe_hi, w_lo, w_hi, cos_c, sin_c,
      jnp.asarray(cmat))


def _rope_cos_sin(pos, dim):
    half = dim // 2
    inv_freq = ROPE_THETA ** (-jnp.arange(half, dtype=F32) / half)
    ang = pos.astype(F32)[:, None] * inv_freq[None, :]
    return jnp.cos(ang), jnp.sin(ang)


def _layer(x, mem, layer_idx, tabs_t, norm_gain, w_in, w_out, nsa_qk_gain, nsa_cmp_pe, nsa_w_cmp,
           diff_qk_gain, diff_lambda, diff_subln_gain, mla_cq_gain, mla_ckv_gain, mla_w_uq, mla_w_ukv,
           mla_qk_gain, mem_norm_gain, mem_w_kv, mem_qk_gain):
    b, seq, d = x.shape
    u = _in_proj(x, norm_gain, w_in, nsa_qk_gain, diff_qk_gain, mem_qk_gain, mla_cq_gain, mla_ckv_gain,
                 mla_w_uq, mla_w_ukv, mla_qk_gain, tabs_t)

    y_nsa = _nsa_call(u["nsa_qT"], u["nsa_kcvc"], u["nsa_k2"], u["nsa_v2T"], u["nsa_gl"], u["nsa_z"],
                      nsa_qk_gain, nsa_cmp_pe, nsa_w_cmp)
    lambda_init = 0.8 - 0.6 * math.exp(-0.3 * layer_idx)
    y_diff = _diff_call(u["diff_qT"], u["diff_k"], u["diff_vT"], u["diff_z"], diff_lambda, diff_subln_gain,
                        lambda_init)
    y_mla, y_mem = _mla_mem_call(u["mla_qT"], u["mla_k"], u["mla_vT"], u["mla_z"],
                                 u["mem_qT"], mem, u["mem_z"], mem_norm_gain, mem_w_kv, mem_qk_gain)

    ys = [y.reshape(b * seq, GROUP_WIDTH) for y in (y_nsa, y_diff, y_mla, y_mem)]
    return _out_proj(x.reshape(b * seq, d), ys, w_out.astype(BF16)).reshape(b, seq, d)


def kernel(x, mem, norm_gain, w_in, w_out, nsa_qk_gain, nsa_cmp_pe, nsa_w_cmp, diff_qk_gain, diff_lambda,
           diff_subln_gain, mla_cq_gain, mla_ckv_gain, mla_w_uq, mla_w_ukv, mla_qk_gain, mem_norm_gain,
           mem_w_kv, mem_qk_gain):
    seq = x.shape[1]
    pos = jnp.arange(seq, dtype=jnp.int32)
    cos32, sin32 = _rope_cos_sin(pos, DIFF_D)
    cos64, sin64 = _rope_cos_sin(pos, HEAD_DIM)
    tabs_t = (cos32.T, sin32.T, cos64.T, sin64.T)
    for l in range(DEPTH):
        x = _layer(x, mem, l, tabs_t, norm_gain[l], w_in[l], w_out[l], nsa_qk_gain[l], nsa_cmp_pe[l],
                   nsa_w_cmp[l], diff_qk_gain[l], diff_lambda[l], diff_subln_gain[l], mla_cq_gain[l],
                   mla_ckv_gain[l], mla_w_uq[l], mla_w_ukv[l], mla_qk_gain[l], mem_norm_gain[l], mem_w_kv[l],
                   mem_qk_gain[l])
    return x
```

```python
import functools
import math

import numpy as np
import jax
import jax.numpy as jnp
from jax import lax
from jax.experimental import pallas as pl
from jax.experimental.pallas import tpu as pltpu

F32 = jnp.float32
BF16 = jnp.bfloat16
HIGHEST = lax.Precision.HIGHEST

D_MODEL = 1024
DEPTH = 2
N_HEADS = 4
HEAD_DIM = 64
GROUP_WIDTH = N_HEADS * HEAD_DIM
ROPE_THETA = 10000.0
EPS = 1e-6
NEG_INF = -1e30
BIG = 1e30
LOG2E = 1.4426950408889634

NSA_CMP_BLOCK = 32
NSA_CMP_STRIDE = 16
NSA_SLC_BLOCK = 64
NSA_N_SELECT = 16
NSA_N_LOCAL = 2
NSA_WINDOW = 512
DIFF_D = HEAD_DIM // 2
MLA_Q_RANK = 256
MLA_KV_RANK = 128
MLA_NOPE = 64
MLA_ROPE = 32
MLA_QK = MLA_NOPE + MLA_ROPE

VMEM_LIMIT_BYTES = 48 * 1024 * 1024
LANES = 128
KEY_TILE = 256
VT_ROWS = 80


def _cparams(sem):
    return pltpu.CompilerParams(dimension_semantics=sem, vmem_limit_bytes=VMEM_LIMIT_BYTES)


def _group_ones(width, group):
    g = np.arange(width) // group
    return jnp.asarray(g[:, None] == g[None, :], dtype=BF16)


def _group_rsqrt(x, gmat, denom):
    sq = x * x
    hi = sq.astype(BF16)
    lo = (sq - hi.astype(F32)).astype(BF16)
    ss = jnp.dot(hi, gmat, preferred_element_type=F32) + jnp.dot(lo, gmat, preferred_element_type=F32)
    return lax.rsqrt(ss / denom + EPS)


def _norm_rope_t(u_t, groups, dim, gain, cos=None, sin=None):
    x = u_t.reshape(groups, dim, u_t.shape[-1])
    ms = jnp.mean(x * x, axis=1, keepdims=True)
    x = x * lax.rsqrt(ms + EPS) * gain
    if cos is not None:
        half = dim // 2
        x1, x2 = x[:, :half], x[:, half:]
        x = jnp.concatenate([x1 * cos - x2 * sin, x2 * cos + x1 * sin], axis=1)
    return x.reshape(groups * dim, u_t.shape[-1])


def _silu(z):
    z = z.astype(F32)
    return z * jax.nn.sigmoid(z)


def _ones_rows(n):
    r = lax.broadcasted_iota(jnp.int32, (VT_ROWS - HEAD_DIM, n), 0)
    return jnp.where(r == 0, 1.0, 0.0).astype(BF16)


def _split_bf16(x):
    hi = x.astype(BF16)
    return hi, (x - hi.astype(F32)).astype(BF16)


def _dot_x3(a_hi, a_lo, b_hi, b_lo):
    dot = lambda a, b: jnp.dot(a, b, preferred_element_type=F32)
    return dot(a_hi, b_hi) + dot(a_hi, b_lo) + dot(a_lo, b_hi)


def _flash_step_t(s, v_t, m_ref, acc_ref, i, cols=slice(None)):
    m_old = m_ref[i:i + 1, cols]
    m_new = jnp.maximum(m_old, jnp.max(s, axis=0, keepdims=True))
    alpha = jnp.exp2(m_old - m_new)
    p = jnp.exp2(s - m_new).astype(BF16)
    acc_ref[i, :, cols] = alpha * acc_ref[i, :, cols] + jnp.dot(v_t, p, preferred_element_type=F32)
    m_ref[i:i + 1, cols] = m_new


def _flash_reset(m_ref, acc_ref):
    m_ref[...] = jnp.full(m_ref.shape, NEG_INF, F32)
    acc_ref[...] = jnp.zeros(acc_ref.shape, F32)


def _flash_out(acc_ref, i, cols=slice(None)):
    return acc_ref[i, 0:HEAD_DIM, cols] / acc_ref[i, HEAD_DIM:HEAD_DIM + 1, cols]


def _causal_flash_t(n, k_tile, q_t, v_tile, qi, tq, tk, m_ref, acc_ref, s_ref):
    _flash_reset(m_ref, acc_ref)
    n_diag = tq // tk

    all_q = slice(0, tq)

    def scores(i, off, cols):
        return jnp.dot(k_tile(i, off), q_t(i)[:, cols], preferred_element_type=F32)

    def step(off, cols, off_next, cols_next, mask):
        for i in range(n):
            s = s_ref[i, :, cols]
            if off_next is not None:
                s_ref[i, :, cols_next] = scores(i, off_next, cols_next)
            if mask is not None:
                s = jnp.where(mask, s, NEG_INF)
            _flash_step_t(s, v_tile(i, off), m_ref, acc_ref, i, cols)

    for i in range(n):
        s_ref[i] = scores(i, 0, all_q)

    def body(j, carry):
        step(pl.multiple_of(j * tk, tk), all_q, pl.multiple_of(j * tk + tk, tk), all_q, None)
        return carry

    lax.fori_loop(0, qi * n_diag, body, 0)
    krow = lax.broadcasted_iota(jnp.int32, (tk, tq), 0)
    qcol = lax.broadcasted_iota(jnp.int32, (tk, tq), 1)
    causal = krow <= qcol
    diag_cols = [slice(d * tk, tq) for d in range(n_diag)]
    for d in range(n_diag):
        off = pl.multiple_of(qi * tq + d * tk, tk)
        last = d + 1 == n_diag
        off_next = None if last else pl.multiple_of(qi * tq + (d + 1) * tk, tk)
        step(off, diag_cols[d], off_next, None if last else diag_cols[d + 1], causal[:, 0:tq - d * tk])


_ROW_SEGS = (("nsa_kcvc", 128), ("nsa_gl", 128), ("nsa_z", 256), ("diff_z", 256), ("mla_z", 256),
             ("mem_z", 256))
_COL_SEGS = (("nsa_q", 256), ("diff_q", 256), ("mem_q", 256), ("diff_k", 256), ("nsa_k2", 128),
             ("diff_v", 256), ("nsa_v2", 128), ("mla_cq", 256), ("mla_ckv", 128), ("mla_kr", 32))


def _seg_offsets(segs):
    out, off = {}, 0
    for name, w in segs:
        out[name] = (off, w)
        off += w
    return out, off


_ROW_OFF, _ROW_W = _seg_offsets(_ROW_SEGS)
_COL_OFF, _COL_W = _seg_offsets(_COL_SEGS)


def _split_w_in(w_in):
    c = lambda a, b: w_in[:, a:b]
    z = lambda n: jnp.zeros((w_in.shape[0], n), w_in.dtype)
    w_row = jnp.concatenate([
        c(256, 384),
        c(640, 652), z(116),
        c(652, 908), c(1676, 1932), c(2348, 2604), c(2860, 3116),
    ], axis=1)
    w_col = jnp.concatenate([
        c(0, 256), c(908, 1164), c(2604, 2860),
        c(1164, 1420),
        c(384, 448), c(512, 576),
        c(1420, 1676),
        c(448, 512), c(576, 640),
        c(1932, 2188), c(2188, 2316), c(2316, 2348),
    ], axis=1)
    return w_row.astype(BF16), w_col.T.astype(BF16)


def _mla_head_norm_t(x_t, gain):
    ms = jnp.sum(x_t * x_t, axis=0, keepdims=True) / float(MLA_QK)
    return x_t * lax.rsqrt(ms + EPS) * gain


def _rope_t(x_t, cos, sin):
    half = x_t.shape[0] // 2
    x1, x2 = x_t[:half], x_t[half:]
    return jnp.concatenate([x1 * cos - x2 * sin, x2 * cos + x1 * sin], axis=0)


def _in_proj_kernel(x_ref, xn_ref, g_ref, *refs):
    *refs, hb_a, ht_a, hb_b, ht_b = refs
    i = pl.program_id(0)

    def normalise(src_ref, hb_dst, ht_dst):
        x = src_ref[...]
        ms = jnp.mean(x * x, axis=-1, keepdims=True)
        h = x * lax.rsqrt(ms + EPS) * g_ref[...]
        hb_dst[...] = h.astype(BF16)
        ht_dst[...] = h.T.astype(BF16)

    @pl.when(i == 0)
    def _first_tile():
        normalise(x_ref, hb_a, ht_a)

    even = lax.rem(i, 2) == 0

    @pl.when(even)
    def _even_step():
        _in_proj_tile(hb_a, ht_a, lambda: normalise(xn_ref, hb_b, ht_b), *refs)

    @pl.when(jnp.logical_not(even))
    def _odd_step():
        _in_proj_tile(hb_b, ht_b, lambda: normalise(xn_ref, hb_a, ht_a), *refs)


def _in_proj_tile(hb_ref, ht_ref, prepare_next,
                  wrow_ref, wcol_ref, nqg_ref, dqg_ref, dkg_ref, k2g_ref, mqg_ref,
                  cqg_ref, ckvg_ref, wuq_ref, wuk_ref, wuv_ref, lqg_ref, lkg_ref,
                  c32_ref, s32_ref, c64_ref, s64_ref,
                  kcvc_o, gl_o, nz_o, dz_o, mz_o, ez_o, dk_o, k2_o,
                  nq_o, dq_o, mq_o, dv_o, v2_o, lq_o, lk_o, lv_o, kcvc_s):
    hb = hb_ref[...]
    h_t = ht_ref[...]
    tm = hb.shape[0]

    def row(name):
        off, w = _ROW_OFF[name]
        return jnp.dot(hb, wrow_ref[:, off:off + w], preferred_element_type=F32)

    u_t = jnp.dot(wcol_ref[...], h_t, preferred_element_type=F32)
    prepare_next()

    def col(name):
        off, w = _COL_OFF[name]
        return u_t[off:off + w]

    c32, s32, c64, s64 = c32_ref[...], s32_ref[...], c64_ref[...], s64_ref[...]
    g3 = lambda ref, groups: ref[...].reshape(groups, -1, 1)

    rows_out = {name: row(name) for name, _ in _ROW_SEGS}

    kcvc_s[...] = rows_out["nsa_kcvc"]
    n_chunk_rows = tm // NSA_CMP_STRIDE
    for tok in range(NSA_CMP_STRIDE):
        kcvc_o[0, :, tok * LANES:(tok + 1) * LANES] = kcvc_s[pl.ds(tok, n_chunk_rows, stride=NSA_CMP_STRIDE), :]
    gl_o[...] = rows_out["nsa_gl"]
    nz_o[...] = rows_out["nsa_z"].astype(BF16)
    dz_o[...] = rows_out["diff_z"].astype(BF16)
    mz_o[...] = rows_out["mla_z"].astype(BF16)
    ez_o[...] = rows_out["mem_z"].astype(BF16)

    nq_o[0] = _norm_rope_t(col("nsa_q"), N_HEADS, HEAD_DIM, g3(nqg_ref, 1), c64, s64) * (HEAD_DIM ** -0.5 * LOG2E)
    dq = _norm_rope_t(col("diff_q"), 2 * N_HEADS, DIFF_D, g3(dqg_ref, 1), c32, s32) * (DIFF_D ** -0.5 * LOG2E)
    dq_o[0] = dq.astype(BF16)
    mq = _norm_rope_t(col("mem_q"), N_HEADS, HEAD_DIM, g3(mqg_ref, 1)) * (HEAD_DIM ** -0.5 * LOG2E)
    mq_o[0] = mq.astype(BF16)
    dk = _norm_rope_t(col("diff_k"), 2 * N_HEADS, DIFF_D, g3(dkg_ref, 1), c32, s32).T
    for mp in range(2 * N_HEADS):
        dk_o[0, mp] = dk[:, mp * DIFF_D:(mp + 1) * DIFF_D].astype(BF16)
    k2 = _norm_rope_t(col("nsa_k2"), 2, HEAD_DIM, g3(k2g_ref, 2), c64, s64).T
    for br in range(2):
        k2_o[0, br] = k2[:, br * HEAD_DIM:(br + 1) * HEAD_DIM].astype(BF16)
    ones = _ones_rows(tm)
    dv = col("diff_v")
    for hd in range(N_HEADS):
        dv_o[0, hd, 0:HEAD_DIM, :] = dv[hd * HEAD_DIM:(hd + 1) * HEAD_DIM].astype(BF16)
        dv_o[0, hd, HEAD_DIM:VT_ROWS, :] = ones
    v2 = col("nsa_v2")
    for br in range(2):
        v2_o[0, br, 0:HEAD_DIM, :] = v2[br * HEAD_DIM:(br + 1) * HEAD_DIM].astype(BF16)
        v2_o[0, br, HEAD_DIM:VT_ROWS, :] = ones

    def latent(name, gain_ref):
        c = col(name)
        return (c * lax.rsqrt(jnp.mean(c * c, axis=0, keepdims=True) + EPS) * gain_ref[...]).astype(BF16)

    qa = jnp.dot(wuq_ref[...], latent("mla_cq", cqg_ref), preferred_element_type=F32)
    ckv = latent("mla_ckv", ckvg_ref)
    kn = jnp.dot(wuk_ref[...], ckv, preferred_element_type=F32)
    lv = jnp.dot(wuv_ref[...], ckv, preferred_element_type=F32)
    rope_rows = slice(MLA_NOPE, MLA_QK)
    lqg = lqg_ref[...] * (MLA_QK ** -0.5 * LOG2E)
    kr = _rope_t(col("mla_kr"), c32, s32)
    zpad = jnp.zeros((LANES - MLA_QK, tm), F32)
    for hd in range(N_HEADS):
        q_h = qa[hd * LANES:(hd + 1) * LANES]
        q_h = jnp.concatenate([q_h[:MLA_NOPE], _rope_t(q_h[rope_rows], c32, s32), q_h[MLA_QK:]], axis=0)
        lq_o[0, hd * LANES:(hd + 1) * LANES, :] = _mla_head_norm_t(q_h, lqg).astype(BF16)
        k_h = jnp.concatenate([kn[hd * LANES:hd * LANES + MLA_NOPE], kr, zpad], axis=0)
        lk_o[0, hd] = _mla_head_norm_t(k_h, lkg_ref[...]).T.astype(BF16)
        lv_o[0, hd, 0:HEAD_DIM, :] = lv[hd * HEAD_DIM:(hd + 1) * HEAD_DIM].astype(BF16)
        lv_o[0, hd, HEAD_DIM:VT_ROWS, :] = ones


def _in_proj(x, gain, w_in, nsa_qk_gain, diff_qk_gain, mem_qk_gain, mla_cq_gain, mla_ckv_gain, mla_w_uq,
             mla_w_ukv, mla_qk_gain, tabs_t, tm=512):
    b, seq, d = x.shape
    n = b * seq
    nb = seq // tm
    w_row, w_col = _split_w_in(w_in)
    c32, s32, c64, s64 = tabs_t
    npad = LANES - MLA_QK
    uq = jnp.pad(mla_w_uq.reshape(MLA_Q_RANK, N_HEADS, MLA_QK), ((0, 0), (0, 0), (0, npad)))
    w_uq_t = uq.reshape(MLA_Q_RANK, N_HEADS * LANES).T.astype(BF16)
    ukv = mla_w_ukv.reshape(MLA_KV_RANK, N_HEADS, MLA_NOPE + HEAD_DIM)
    uk = jnp.pad(ukv[:, :, :MLA_NOPE], ((0, 0), (0, 0), (0, LANES - MLA_NOPE)))
    w_uk_t = uk.reshape(MLA_KV_RANK, N_HEADS * LANES).T.astype(BF16)
    w_uv_t = ukv[:, :, MLA_NOPE:].reshape(MLA_KV_RANK, GROUP_WIDTH).T.astype(BF16)
    pad_gain = lambda g: jnp.pad(g, (0, npad))[:, None]

    rowspec = lambda w: pl.BlockSpec((tm, w), lambda i: (i, 0))
    colspec = lambda r: pl.BlockSpec((1, r, tm), lambda i: (i // nb, 0, i % nb))
    vtspec = lambda c: pl.BlockSpec((1, c, VT_ROWS, tm), lambda i: (i // nb, 0, 0, i % nb))
    const = lambda shape: pl.BlockSpec(shape, lambda i: tuple(0 for _ in shape))
    tabspec = lambda r: pl.BlockSpec((r, tm), lambda i: (0, i % nb))
    row_out = lambda w, dt: jax.ShapeDtypeStruct((n, w), dt)
    col_out = lambda r, dt: jax.ShapeDtypeStruct((b, r, seq), dt)
    vt_out = lambda c: jax.ShapeDtypeStruct((b, c, VT_ROWS, seq), BF16)
    slabspec = lambda c, w: pl.BlockSpec((1, c, tm, w), lambda i: (i // nb, 0, i % nb, 0))
    slab_out = lambda c, w: jax.ShapeDtypeStruct((b, c, seq, w), BF16)
    outs = pl.pallas_call(
        _in_proj_kernel,
        grid=(n // tm,),
        in_specs=[rowspec(d), pl.BlockSpec((tm, d), lambda i: (jnp.minimum(i + 1, n // tm - 1), 0)),
                  const((1, d)), const((d, _ROW_W)), const((_COL_W, d)),
                  const((HEAD_DIM, 1)), const((DIFF_D, 1)), const((DIFF_D, 1)), const((2 * HEAD_DIM, 1)),
                  const((HEAD_DIM, 1)),
                  const((MLA_Q_RANK, 1)), const((MLA_KV_RANK, 1)),
                  const((N_HEADS * LANES, MLA_Q_RANK)), const((N_HEADS * LANES, MLA_KV_RANK)),
                  const((GROUP_WIDTH, MLA_KV_RANK)), const((LANES, 1)), const((LANES, 1)),
                  tabspec(DIFF_D // 2), tabspec(DIFF_D // 2), tabspec(HEAD_DIM // 2), tabspec(HEAD_DIM // 2)],
        out_specs=[pl.BlockSpec((1, tm // NSA_CMP_STRIDE, NSA_CMP_STRIDE * LANES), lambda i: (i // nb, i % nb, 0)),
                   rowspec(128), rowspec(256), rowspec(256), rowspec(256), rowspec(256),
                   slabspec(2 * N_HEADS, DIFF_D), slabspec(2, HEAD_DIM),
                   colspec(256), colspec(256), colspec(256), vtspec(N_HEADS), vtspec(2),
                   colspec(N_HEADS * LANES), slabspec(N_HEADS, LANES), vtspec(N_HEADS)],
        out_shape=[jax.ShapeDtypeStruct((b, seq // NSA_CMP_STRIDE, NSA_CMP_STRIDE * LANES), F32),
                   row_out(128, F32), row_out(256, BF16), row_out(256, BF16),
                   row_out(256, BF16), row_out(256, BF16), slab_out(2 * N_HEADS, DIFF_D), slab_out(2, HEAD_DIM),
                   col_out(256, F32), col_out(256, BF16), col_out(256, BF16), vt_out(N_HEADS), vt_out(2),
                   col_out(N_HEADS * LANES, BF16), slab_out(N_HEADS, LANES), vt_out(N_HEADS)],
        scratch_shapes=[pltpu.VMEM((tm, LANES), F32),
                        pltpu.VMEM((tm, d), BF16), pltpu.VMEM((d, tm), BF16),
                        pltpu.VMEM((tm, d), BF16), pltpu.VMEM((d, tm), BF16)],
        compiler_params=_cparams(("arbitrary",)),
        name="in_proj",
    )(x.reshape(n, d), x.reshape(n, d), gain[None, :], w_row, w_col,
      nsa_qk_gain[0][:, None], diff_qk_gain[0][:, None], diff_qk_gain[1][:, None],
      jnp.concatenate([nsa_qk_gain[2], nsa_qk_gain[3]])[:, None], mem_qk_gain[0][:, None],
      mla_cq_gain[:, None], mla_ckv_gain[:, None], w_uq_t, w_uk_t, w_uv_t,
      pad_gain(mla_qk_gain[0]), pad_gain(mla_qk_gain[1]),
      c32, s32, c64, s64)
    names = ("nsa_kcvc", "nsa_gl", "nsa_z", "diff_z", "mla_z", "mem_z", "diff_k", "nsa_k2",
             "nsa_qT", "diff_qT", "mem_qT", "diff_vT", "nsa_v2T", "mla_qT", "mla_k", "mla_vT")
    u = dict(zip(names, outs))
    for name in names[1:6]:
        u[name] = u[name].reshape(b, seq, -1)
    return u


def _out_proj_kernel(x_ref, y0_ref, y1_ref, y2_ref, y3_ref, w_ref, o_ref):
    acc = x_ref[...]
    for g, y_ref in enumerate((y0_ref, y1_ref, y2_ref, y3_ref)):
        acc = acc + jnp.dot(y_ref[...], w_ref[g * GROUP_WIDTH:(g + 1) * GROUP_WIDTH, :],
                            preferred_element_type=F32)
    o_ref[...] = acc


def _out_proj(x2, ys, w_out_b, tm=512):
    n = x2.shape[0]
    yspec = pl.BlockSpec((tm, GROUP_WIDTH), lambda i: (i, 0))
    return pl.pallas_call(
        _out_proj_kernel,
        grid=(n // tm,),
        in_specs=[pl.BlockSpec((tm, D_MODEL), lambda i: (i, 0)), yspec, yspec, yspec, yspec,
                  pl.BlockSpec((D_MODEL, D_MODEL), lambda i: (0, 0))],
        out_specs=pl.BlockSpec((tm, D_MODEL), lambda i: (i, 0)),
        out_shape=jax.ShapeDtypeStruct((n, D_MODEL), F32),
        compiler_params=_cparams(("parallel",)),
        name="out_proj",
    )(x2, *ys, w_out_b)


def _diff_kernel(lambda_init, t, qT_ref, k_ref, vT_ref, z_ref, lam_ref, sg_ref, y_ref, m_s, acc_s, s_s):
    qi = pl.program_id(1)
    n_maps = 2 * N_HEADS
    qT = qT_ref[0]
    _causal_flash_t(
        n_maps,
        lambda i, off: k_ref[0, i, pl.ds(off, KEY_TILE), :],
        lambda i: qT[i * DIFF_D:(i + 1) * DIFF_D],
        lambda i, off: vT_ref[0, i // 2, :, pl.ds(off, KEY_TILE)],
        qi, t, KEY_TILE, m_s, acc_s, s_s)

    lam = lam_ref[...]
    lmbda = (jnp.exp(jnp.sum(lam[0:1] * lam[1:2], axis=-1, keepdims=True))
             - jnp.exp(jnp.sum(lam[2:3] * lam[3:4], axis=-1, keepdims=True)) + lambda_init)
    outs = []
    for h in range(N_HEADS):
        d = _flash_out(acc_s, 2 * h) - lmbda * _flash_out(acc_s, 2 * h + 1)
        ms = jnp.mean(d * d, axis=0, keepdims=True)
        outs.append(d * lax.rsqrt(ms + EPS) * sg_ref[...] * (1.0 - lambda_init))
    y = jnp.concatenate(outs, axis=0).T * _silu(z_ref[0])
    y_ref[0] = y.astype(BF16)


def _diff_call(qT, k, vT, z, lam, subln_gain, lambda_init, t=512):
    b, _, seq, _ = k.shape
    const = lambda shape: pl.BlockSpec(shape, lambda bi, qi: tuple(0 for _ in shape))
    tile = pl.BlockSpec((1, t, GROUP_WIDTH), lambda bi, qi: (bi, qi, 0))
    return pl.pallas_call(
        functools.partial(_diff_kernel, lambda_init, t),
        grid=(b, seq // t),
        in_specs=[pl.BlockSpec((1, GROUP_WIDTH, t), lambda bi, qi: (bi, 0, qi)),
                  pl.BlockSpec((1, 2 * N_HEADS, seq, DIFF_D), lambda bi, qi: (bi, 0, 0, 0)),
                  pl.BlockSpec((1, N_HEADS, VT_ROWS, seq), lambda bi, qi: (bi, 0, 0, 0)),
                  tile, const((4, DIFF_D)), const((HEAD_DIM, 1))],
        out_specs=tile,
        out_shape=jax.ShapeDtypeStruct((b, seq, GROUP_WIDTH), BF16),
        scratch_shapes=[pltpu.VMEM((2 * N_HEADS, t), F32),
                        pltpu.VMEM((2 * N_HEADS, VT_ROWS, t), F32),
                        pltpu.VMEM((2 * N_HEADS, KEY_TILE, t), F32)],
        compiler_params=_cparams(("parallel", "parallel")),
        name="diff_attn",
    )(qT, k, vT, z, lam, subln_gain[:, None])


def _mla_mem_kernel(t, qT_ref, k_ref, vT_ref, z_ref,
                    mqT_ref, mem_ref, mz_ref, mg_ref, wk_ref, wvT_ref, kg_ref, gm_ref,
                    y_ref, ymem_ref, m_s, acc_s, s_s, mk_s, mvT_s, mm_s, macc_s):
    qi = pl.program_id(1)
    nt = (((1,), (1,)), ((), ()))
    m_len = mem_ref.shape[1]

    @pl.when(qi == 0)
    def _prep_memory_kv():
        mem = mem_ref[0]
        ms = jnp.mean(mem * mem, axis=-1, keepdims=True)
        mb = (mem * lax.rsqrt(ms + EPS) * mg_ref[...]).astype(BF16)
        k = jnp.dot(mb, wk_ref[...], preferred_element_type=F32)
        kn = k * _group_rsqrt(k, gm_ref[...], float(HEAD_DIM)) * kg_ref[...]
        vT = lax.dot_general(wvT_ref[...], mb, nt, preferred_element_type=F32)
        for h in range(N_HEADS):
            mk_s[h] = kn[:, h * HEAD_DIM:(h + 1) * HEAD_DIM].astype(BF16)
            mvT_s[h, 0:HEAD_DIM, :] = vT[h * HEAD_DIM:(h + 1) * HEAD_DIM].astype(BF16)
            mvT_s[h, HEAD_DIM:VT_ROWS, :] = _ones_rows(m_len)

    mqT = mqT_ref[0]
    _flash_reset(mm_s, macc_s)
    ss = [jnp.dot(mk_s[h], mqT[h * HEAD_DIM:(h + 1) * HEAD_DIM], preferred_element_type=F32)
          for h in range(N_HEADS)]
    for h in range(N_HEADS):
        _flash_step_t(ss[h], mvT_s[h], mm_s, macc_s, h)

    qT = qT_ref[0]
    _causal_flash_t(
        N_HEADS,
        lambda h, off: k_ref[0, h, pl.ds(off, KEY_TILE), :],
        lambda h: qT[h * LANES:(h + 1) * LANES],
        lambda h, off: vT_ref[0, h, :, pl.ds(off, KEY_TILE)],
        qi, t, KEY_TILE, m_s, acc_s, s_s)

    y_mem = jnp.concatenate([_flash_out(macc_s, h) for h in range(N_HEADS)], axis=0).T * _silu(mz_ref[0])
    ymem_ref[0] = y_mem.astype(BF16)
    y = jnp.concatenate([_flash_out(acc_s, h) for h in range(N_HEADS)], axis=0).T * _silu(z_ref[0])
    y_ref[0] = y.astype(BF16)


def _mla_mem_call(qT, k, vT, z, mem_qT, mem, mem_z, mem_gain, w_kv, mem_qk_gain, t=512):
    b, _, seq, _ = k.shape
    m_len = mem.shape[1]
    w_k = w_kv[:, :GROUP_WIDTH].astype(BF16)
    w_vT = w_kv[:, GROUP_WIDTH:].T.astype(BF16)
    tile = pl.BlockSpec((1, t, GROUP_WIDTH), lambda bi, qi: (bi, qi, 0))
    const = lambda shape: pl.BlockSpec(shape, lambda bi, qi: tuple(0 for _ in shape))
    out = jax.ShapeDtypeStruct((b, seq, GROUP_WIDTH), BF16)
    return pl.pallas_call(
        functools.partial(_mla_mem_kernel, t),
        grid=(b, seq // t),
        in_specs=[pl.BlockSpec((1, N_HEADS * LANES, t), lambda bi, qi: (bi, 0, qi)),
                  pl.BlockSpec((1, N_HEADS, seq, LANES), lambda bi, qi: (bi, 0, 0, 0)),
                  pl.BlockSpec((1, N_HEADS, VT_ROWS, seq), lambda bi, qi: (bi, 0, 0, 0)),
                  tile,
                  pl.BlockSpec((1, GROUP_WIDTH, t), lambda bi, qi: (bi, 0, qi)),
                  pl.BlockSpec((1, m_len, D_MODEL), lambda bi, qi: (bi, 0, 0)), tile,
                  const((1, D_MODEL)), const((D_MODEL, GROUP_WIDTH)), const((GROUP_WIDTH, D_MODEL)),
                  const((1, GROUP_WIDTH)), const((GROUP_WIDTH, GROUP_WIDTH))],
        out_specs=[tile, tile],
        out_shape=[out, out],
        scratch_shapes=[pltpu.VMEM((N_HEADS, t), F32),
                        pltpu.VMEM((N_HEADS, VT_ROWS, t), F32),
                        pltpu.VMEM((N_HEADS, KEY_TILE, t), F32),
                        pltpu.VMEM((N_HEADS, m_len, HEAD_DIM), BF16),
                        pltpu.VMEM((N_HEADS, VT_ROWS, m_len), BF16),
                        pltpu.VMEM((N_HEADS, t), F32),
                        pltpu.VMEM((N_HEADS, VT_ROWS, t), F32)],
        compiler_params=_cparams(("parallel", "arbitrary")),
        name="mla_mem_attn",
    )(qT, k, vT, z, mem_qT, mem, mem_z, mem_gain[None, :], w_k, w_vT,
      jnp.tile(mem_qk_gain[1], N_HEADS)[None, :], _group_ones(GROUP_WIDTH, HEAD_DIM))


def _nsa_kernel(seq, t,
                qT_ref, kcvc_ref, k2_ref, v2T_ref, gl_ref, z_ref,
                cg_ref, pelo_ref, pehi_ref, wlo_ref, whi_ref, cosc_ref, sinc_ref,
                cmat_ref,
                y_ref, kc_s, vcT_s, bias_s, m_s, acc_s, s_s):
    qi = pl.program_id(1)
    half = HEAD_DIM // 2
    n_chunk = seq // NSA_CMP_STRIDE
    n_cmp = n_chunk - 1
    n_blk = seq // NSA_SLC_BLOCK
    blk_per_tile = t // NSA_SLC_BLOCK
    win_tiles = NSA_WINDOW // t
    w4 = N_HEADS * t
    SLC, WIN = 0, 1

    @pl.when(qi == 0)
    def _prep():
        ch = kcvc_ref[0]
        a = _dot_x3(*_split_bf16(ch + pelo_ref[...]), wlo_ref[0], wlo_ref[1])
        bm = _dot_x3(*_split_bf16(ch + pehi_ref[...]), whi_ref[0], whi_ref[1])
        cmp = a + pltpu.roll(bm, n_chunk - 1, 0)
        kc = cmp[:, :HEAD_DIM]
        ms = jnp.mean(kc * kc, axis=-1, keepdims=True)
        kc = kc * lax.rsqrt(ms + EPS) * cg_ref[...]
        x1, x2 = kc[:, :half], kc[:, half:]
        c, s = cosc_ref[...], sinc_ref[...]
        kc_hi, kc_lo = _split_bf16(jnp.concatenate([x1 * c - x2 * s, x2 * c + x1 * s], axis=-1))
        kc_s[0] = kc_hi
        kc_s[1] = kc_lo
        vcT_s[...] = cmp.T[HEAD_DIM:].astype(BF16)

    qs = qi * t
    qT = qT_ref[0]
    q4f = jnp.concatenate([qT[h * HEAD_DIM:(h + 1) * HEAD_DIM] for h in range(N_HEADS)], axis=1)
    q4 = q4f.astype(BF16)

    def scores(br, j):
        off = pl.multiple_of(j * t, t)
        return jnp.dot(k2_ref[0, br, pl.ds(off, t), :], q4, preferred_element_type=F32)

    s_s[WIN] = scores(WIN, jnp.maximum(qi - win_tiles, 0))
    qcol = lax.broadcasted_iota(jnp.int32, (1, w4), 1) & (t - 1)
    pos_c = qs + qcol

    q_lo = (q4f - q4.astype(F32)).astype(BF16)
    heads = [slice(h * t, (h + 1) * t) for h in range(N_HEADS)]
    scs = [_dot_x3(kc_s[0], kc_s[1], q4[:, c], q_lo[:, c]) for c in heads]
    n_idx = lax.broadcasted_iota(jnp.int32, (n_chunk, 1), 0)
    cvalid = (n_idx * NSA_CMP_STRIDE + (NSA_CMP_BLOCK - 1) <= pos_c[:, :t]) & (n_idx < n_cmp)
    o_cmp, pg = [], None
    for h in range(N_HEADS):
        sc = jnp.where(cvalid, scs[h], NEG_INF)
        e = jnp.exp2(sc - jnp.max(sc, axis=0, keepdims=True))
        p = jnp.where(cvalid, e / jnp.sum(e, axis=0, keepdims=True), 0.0)
        o_cmp.append(jnp.dot(vcT_s[...], p.astype(BF16), preferred_element_type=F32))
        pg = p if pg is None else pg + p

    p_slc = jnp.dot(cmat_ref[...], pg, precision=HIGHEST, preferred_element_type=F32)
    blk = lax.broadcasted_iota(jnp.int32, (n_blk, 1), 0)
    cur = lax.shift_right_logical(pos_c[:, :t], NSA_SLC_BLOCK.bit_length() - 1)
    forced = (blk == 0) | ((blk <= cur) & (blk > cur - NSA_N_LOCAL))
    score = jnp.where(blk > cur, NEG_INF, jnp.where(forced, BIG, p_slc))
    sub = 8
    groups = [score[g * sub:(g + 1) * sub] for g in range(n_blk // sub)]
    cnts = [jnp.zeros((sub, t), F32) for _ in groups]
    for i in range(n_blk):
        ri = score[i:i + 1, :]
        for g, sg in enumerate(groups):
            if g < i // sub:
                beats = ri > sg
            elif g > i // sub:
                beats = ri >= sg
            else:
                beats = (ri > sg) | ((ri == sg) & (blk[g * sub:(g + 1) * sub] > i))
            cnts[g] = cnts[g] + jnp.where(beats, 1.0, 0.0)
    cnt = jnp.concatenate(cnts, axis=0)
    bias = jnp.where(cnt < float(NSA_N_SELECT), 0.0, NEG_INF)
    bias = jnp.concatenate([bias] * N_HEADS, axis=1)
    for r in range(n_blk // blk_per_tile):
        bias_s[r] = bias[r * blk_per_tile:(r + 1) * blk_per_tile, :]

    _flash_reset(m_s, acc_s)
    krow = lax.broadcasted_iota(jnp.int32, (t, 1), 0)
    causal = krow <= qcol
    beyond = krow > qcol

    def slc_scores(j):
        s = scores(SLC, j).reshape(blk_per_tile, NSA_SLC_BLOCK, w4) + bias_s[j][:, None, :]
        return s.reshape(t, w4)

    def flash(br, s, j):
        off = pl.multiple_of(j * t, t)
        _flash_step_t(s, v2T_ref[0, br, :, pl.ds(off, t)], m_s, acc_s, br)

    s_s[SLC] = slc_scores(0)

    def old_body(j, carry):
        s = s_s[SLC]
        s_s[SLC] = slc_scores(j + 1)
        flash(SLC, s, j)
        return carry

    lax.fori_loop(0, jnp.maximum(qi - win_tiles, 0), old_body, 0)

    def win_step(j, win_mask):
        s_slc = s_s[SLC]
        s_s[SLC] = slc_scores(j + 1)
        flash(SLC, s_slc, j)
        s_win = s_s[WIN]
        if win_mask is not None:
            s_win = jnp.where(win_mask, s_win, NEG_INF)
        s_s[WIN] = scores(WIN, j + 1)
        flash(WIN, s_win, j)

    @pl.when(qi >= win_tiles)
    def _oldest_window_tile():
        win_step(qi - win_tiles, beyond)

    def win_body(j, carry):
        win_step(j, None)
        return carry

    lax.fori_loop(jnp.maximum(qi - win_tiles + 1, 0), qi, win_body, 0)
    flash(SLC, jnp.where(causal, s_s[SLC], NEG_INF), qi)
    flash(WIN, jnp.where(causal, s_s[WIN], NEG_INF), qi)

    g = jax.nn.sigmoid(gl_ref[0]).T
    outs = []
    for h in range(N_HEADS):
        cols = slice(h * t, (h + 1) * t)
        o_s = _flash_out(acc_s, SLC, cols)
        o_w = _flash_out(acc_s, WIN, cols)
        outs.append(g[h:h + 1, :] * o_cmp[h] + g[N_HEADS + h:N_HEADS + h + 1, :] * o_s
                    + g[2 * N_HEADS + h:2 * N_HEADS + h + 1, :] * o_w)
    y = jnp.concatenate(outs, axis=0).T * _silu(z_ref[0])
    y_ref[0] = y.astype(BF16)


def _nsa_call(qT, kcvc, k2, v2T, gl, z, qk_gain, cmp_pe, w_cmp, t=256):
    b, _, seq, _ = k2.shape
    n_chunk = seq // NSA_CMP_STRIDE
    n_blk = seq // NSA_SLC_BLOCK
    half_blk = NSA_CMP_BLOCK // 2
    cw = half_blk * 2 * HEAD_DIM
    assert t % NSA_SLC_BLOCK == 0 and NSA_WINDOW % t == 0 and t & (t - 1) == 0

    wk = w_cmp[0].reshape(NSA_CMP_BLOCK, HEAD_DIM, HEAD_DIM)
    wv = w_cmp[1].reshape(NSA_CMP_BLOCK, HEAD_DIM, HEAD_DIM)
    zero = jnp.zeros_like(wk)
    w_all = jnp.concatenate([jnp.concatenate([wk, zero], axis=2),
                             jnp.concatenate([zero, wv], axis=2)], axis=1)
    hi_lo = lambda w: jnp.stack([w.astype(BF16), (w - w.astype(BF16).astype(F32)).astype(BF16)])
    w_lo = hi_lo(w_all[:half_blk].reshape(cw, 2 * HEAD_DIM))
    w_hi = hi_lo(w_all[half_blk:].reshape(cw, 2 * HEAD_DIM))
    pe_all = jnp.concatenate([cmp_pe[0], cmp_pe[1]], axis=1)
    pe_lo = pe_all[:half_blk].reshape(1, cw)
    pe_hi = pe_all[half_blk:].reshape(1, cw)

    cmp_end = jnp.arange(n_chunk, dtype=jnp.int32) * NSA_CMP_STRIDE + (NSA_CMP_BLOCK - 1)
    cos_c, sin_c = _rope_cos_sin(cmp_end, HEAD_DIM)

    ratio = NSA_SLC_BLOCK // NSA_CMP_STRIDE
    coef = np.convolve(np.ones(ratio), np.ones(NSA_CMP_BLOCK // NSA_CMP_STRIDE))
    cmat = np.zeros((n_blk, n_chunk), np.float32)
    for j in range(n_blk):
        for i, c in enumerate(coef):
            if ratio * j + i < n_chunk - 1:
                cmat[j, ratio * j + i] = c

    tile = pl.BlockSpec((1, t, GROUP_WIDTH), lambda bi, qi: (bi, qi, 0))
    const = lambda shape: pl.BlockSpec(shape, lambda bi, qi: tuple(0 for _ in shape))
    w4 = N_HEADS * t
    return pl.pallas_call(
        functools.partial(_nsa_kernel, seq, t),
        grid=(b, seq // t),
        in_specs=[pl.BlockSpec((1, GROUP_WIDTH, t), lambda bi, qi: (bi, 0, qi)),
                  pl.BlockSpec((1, n_chunk, cw), lambda bi, qi: (bi, 0, 0)),
                  pl.BlockSpec((1, 2, seq, HEAD_DIM), lambda bi, qi: (bi, 0, 0, 0)),
                  pl.BlockSpec((1, 2, VT_ROWS, seq), lambda bi, qi: (bi, 0, 0, 0)),
                  pl.BlockSpec((1, t, LANES), lambda bi, qi: (bi, qi, 0)),
                  tile,
                  const((1, HEAD_DIM)),
                  const((1, cw)), const((1, cw)), const((2, cw, 2 * HEAD_DIM)), const((2, cw, 2 * HEAD_DIM)),
                  const((n_chunk, HEAD_DIM // 2)), const((n_chunk, HEAD_DIM // 2)),
                  const((n_blk, n_chunk))],
        out_specs=tile,
        out_shape=jax.ShapeDtypeStruct((b, seq, GROUP_WIDTH), BF16),
        scratch_shapes=[pltpu.VMEM((2, n_chunk, HEAD_DIM), BF16),
                        pltpu.VMEM((HEAD_DIM, n_chunk), BF16),
                        pltpu.VMEM((seq // t, t // NSA_SLC_BLOCK, w4), F32),
                        pltpu.VMEM((2, w4), F32),
                        pltpu.VMEM((2, VT_ROWS, w4), F32),
                        pltpu.VMEM((2, t, w4), F32)],
        compiler_params=_cparams(("parallel", "arbitrary")),
        name="nsa_attn",
    )(qT, kcvc, k2, v2T, gl, z,
      qk_gain[1][None, :],
      pe_lo, pe_hi, w_lo, w_hi, cos_c, sin_c,
      jnp.asarray(cmat))


def _rope_cos_sin(pos, dim):
    half = dim // 2
    inv_freq = ROPE_THETA ** (-jnp.arange(half, dtype=F32) / half)
    ang = pos.astype(F32)[:, None] * inv_freq[None, :]
    return jnp.cos(ang), jnp.sin(ang)


def _layer(x, mem, layer_idx, tabs_t, norm_gain, w_in, w_out, nsa_qk_gain, nsa_cmp_pe, nsa_w_cmp,
           diff_qk_gain, diff_lambda, diff_subln_gain, mla_cq_gain, mla_ckv_gain, mla_w_uq, mla_w_ukv,
           mla_qk_gain, mem_norm_gain, mem_w_kv, mem_qk_gain):
    b, seq, d = x.shape
    u = _in_proj(x, norm_gain, w_in, nsa_qk_gain, diff_qk_gain, mem_qk_gain, mla_cq_gain, mla_ckv_gain,
                 mla_w_uq, mla_w_ukv, mla_qk_gain, tabs_t)

    y_nsa = _nsa_call(u["nsa_qT"], u["nsa_kcvc"], u["nsa_k2"], u["nsa_v2T"], u["nsa_gl"], u["nsa_z"],
                      nsa_qk_gain, nsa_cmp_pe, nsa_w_cmp)
    lambda_init = 0.8 - 0.6 * math.exp(-0.3 * layer_idx)
    y_diff = _diff_call(u["diff_qT"], u["diff_k"], u["diff_vT"], u["diff_z"], diff_lambda, diff_subln_gain,
                        lambda_init)
    y_mla, y_mem = _mla_mem_call(u["mla_qT"], u["mla_k"], u["mla_vT"], u["mla_z"],
                                 u["mem_qT"], mem, u["mem_z"], mem_norm_gain, mem_w_kv, mem_qk_gain)

    ys = [y.reshape(b * seq, GROUP_WIDTH) for y in (y_nsa, y_diff, y_mla, y_mem)]
    return _out_proj(x.reshape(b * seq, d), ys, w_out.astype(BF16)).reshape(b, seq, d)


def kernel(x, mem, norm_gain, w_in, w_out, nsa_qk_gain, nsa_cmp_pe, nsa_w_cmp, diff_qk_gain, diff_lambda,
           diff_subln_gain, mla_cq_gain, mla_ckv_gain, mla_w_uq, mla_w_ukv, mla_qk_gain, mem_norm_gain,
           mem_w_kv, mem_qk_gain):
    seq = x.shape[1]
    pos = jnp.arange(seq, dtype=jnp.int32)
    cos32, sin32 = _rope_cos_sin(pos, DIFF_D)
    cos64, sin64 = _rope_cos_sin(pos, HEAD_DIM)
    tabs_t = (cos32.T, sin32.T, cos64.T, sin64.T)
    for l in range(DEPTH):
        x = _layer(x, mem, l, tabs_t, norm_gain[l], w_in[l], w_out[l], nsa_qk_gain[l], nsa_cmp_pe[l],
                   nsa_w_cmp[l], diff_qk_gain[l], diff_lambda[l], diff_subln_gain[l], mla_cq_gain[l],
                   mla_ckv_gain[l], mla_w_uq[l], mla_w_ukv[l], mla_qk_gain[l], mem_norm_gain[l], mem_w_kv[l],
                   mem_qk_gain[l])
    return x
```

```python
import functools
import math

import numpy as np
import jax
import jax.numpy as jnp
from jax import lax
from jax.experimental import pallas as pl
from jax.experimental.pallas import tpu as pltpu

F32 = jnp.float32
BF16 = jnp.bfloat16

D_MODEL = 1024
DEPTH = 2
N_HEADS = 4
HEAD_DIM = 64
GROUP_WIDTH = N_HEADS * HEAD_DIM
ROPE_THETA = 10000.0
EPS = 1e-6
NEG_INF = -1e30
BIG = 1e30
LOG2E = 1.4426950408889634

NSA_CMP_BLOCK = 32
NSA_CMP_STRIDE = 16
NSA_SLC_BLOCK = 64
NSA_N_SELECT = 16
NSA_N_LOCAL = 2
NSA_WINDOW = 512
DIFF_D = HEAD_DIM // 2
MLA_Q_RANK = 256
MLA_KV_RANK = 128
MLA_NOPE = 64
MLA_ROPE = 32
MLA_QK = MLA_NOPE + MLA_ROPE

VMEM_LIMIT_BYTES = 48 * 1024 * 1024
LANES = 128
KEY_TILE = 256
VT_ROWS = 80


def _cparams(sem):
    return pltpu.CompilerParams(dimension_semantics=sem, vmem_limit_bytes=VMEM_LIMIT_BYTES)


def _group_ones(width, group):
    g = np.arange(width) // group
    return jnp.asarray(g[:, None] == g[None, :], dtype=BF16)


def _group_rsqrt(x, gmat, denom):
    sq = x * x
    hi = sq.astype(BF16)
    lo = (sq - hi.astype(F32)).astype(BF16)
    ss = jnp.dot(hi, gmat, preferred_element_type=F32) + jnp.dot(lo, gmat, preferred_element_type=F32)
    return lax.rsqrt(ss / denom + EPS)


def _norm_rope_t(u_t, groups, dim, gain, cos=None, sin=None):
    x = u_t.reshape(groups, dim, u_t.shape[-1])
    ms = jnp.mean(x * x, axis=1, keepdims=True)
    x = x * lax.rsqrt(ms + EPS) * gain
    if cos is not None:
        half = dim // 2
        x1, x2 = x[:, :half], x[:, half:]
        x = jnp.concatenate([x1 * cos - x2 * sin, x2 * cos + x1 * sin], axis=1)
    return x.reshape(groups * dim, u_t.shape[-1])


def _silu(z):
    z = z.astype(F32)
    return z * jax.nn.sigmoid(z)


def _ones_rows(n):
    r = lax.broadcasted_iota(jnp.int32, (VT_ROWS - HEAD_DIM, n), 0)
    return jnp.where(r == 0, 1.0, 0.0).astype(BF16)


def _split_bf16(x):
    hi = x.astype(BF16)
    return hi, (x - hi.astype(F32)).astype(BF16)


def _dot_x3(a_hi, a_lo, b_hi, b_lo):
    dot = lambda a, b: jnp.dot(a, b, preferred_element_type=F32)
    return dot(a_hi, b_hi) + dot(a_hi, b_lo) + dot(a_lo, b_hi)


def _flash_step_t(s, v_t, m_ref, acc_ref, i, cols=slice(None)):
    m_old = m_ref[i:i + 1, cols]
    m_new = jnp.maximum(m_old, jnp.max(s, axis=0, keepdims=True))
    alpha = jnp.exp2(m_old - m_new)
    p = jnp.exp2(s - m_new).astype(BF16)
    acc_ref[i, :, cols] = alpha * acc_ref[i, :, cols] + jnp.dot(v_t, p, preferred_element_type=F32)
    m_ref[i:i + 1, cols] = m_new


def _flash_reset(m_ref, acc_ref):
    m_ref[...] = jnp.full(m_ref.shape, NEG_INF, F32)
    acc_ref[...] = jnp.zeros(acc_ref.shape, F32)


def _flash_out(acc_ref, i, cols=slice(None)):
    return acc_ref[i, 0:HEAD_DIM, cols] / acc_ref[i, HEAD_DIM:HEAD_DIM + 1, cols]


def _causal_flash_t(n, k_tile, q_t, v_tile, qi, tq, tk, m_ref, acc_ref, s_ref):
    _flash_reset(m_ref, acc_ref)
    n_diag = tq // tk

    all_q = slice(0, tq)

    def scores(i, off, cols):
        return jnp.dot(k_tile(i, off), q_t(i)[:, cols], preferred_element_type=F32)

    def step(off, cols, off_next, cols_next, mask):
        for i in range(n):
            s = s_ref[i, :, cols]
            if off_next is not None:
                s_ref[i, :, cols_next] = scores(i, off_next, cols_next)
            if mask is not None:
                s = jnp.where(mask, s, NEG_INF)
            _flash_step_t(s, v_tile(i, off), m_ref, acc_ref, i, cols)

    for i in range(n):
        s_ref[i] = scores(i, 0, all_q)

    def body(j, carry):
        step(pl.multiple_of(j * tk, tk), all_q, pl.multiple_of(j * tk + tk, tk), all_q, None)
        return carry

    lax.fori_loop(0, qi * n_diag, body, 0)
    krow = lax.broadcasted_iota(jnp.int32, (tk, tq), 0)
    qcol = lax.broadcasted_iota(jnp.int32, (tk, tq), 1)
    causal = krow <= qcol
    diag_cols = [slice(d * tk, tq) for d in range(n_diag)]
    for d in range(n_diag):
        off = pl.multiple_of(qi * tq + d * tk, tk)
        last = d + 1 == n_diag
        off_next = None if last else pl.multiple_of(qi * tq + (d + 1) * tk, tk)
        step(off, diag_cols[d], off_next, None if last else diag_cols[d + 1], causal[:, 0:tq - d * tk])


_ROW_SEGS = (("nsa_kcvc", 128), ("nsa_gl", 128), ("nsa_z", 256), ("diff_z", 256), ("mla_z", 256),
             ("mem_z", 256))
_COL_SEGS = (("nsa_q", 256), ("diff_q", 256), ("mem_q", 256), ("diff_k", 256), ("nsa_k2", 128),
             ("diff_v", 256), ("nsa_v2", 128), ("mla_cq", 256), ("mla_ckv", 128), ("mla_kr", 32))


def _seg_offsets(segs):
    out, off = {}, 0
    for name, w in segs:
        out[name] = (off, w)
        off += w
    return out, off


_ROW_OFF, _ROW_W = _seg_offsets(_ROW_SEGS)
_COL_OFF, _COL_W = _seg_offsets(_COL_SEGS)


def _split_w_in(w_in):
    c = lambda a, b: w_in[:, a:b]
    z = lambda n: jnp.zeros((w_in.shape[0], n), w_in.dtype)
    w_row = jnp.concatenate([
        c(256, 384),
        c(640, 652), z(116),
        c(652, 908), c(1676, 1932), c(2348, 2604), c(2860, 3116),
    ], axis=1)
    w_col = jnp.concatenate([
        c(0, 256), c(908, 1164), c(2604, 2860),
        c(1164, 1420),
        c(384, 448), c(512, 576),
        c(1420, 1676),
        c(448, 512), c(576, 640),
        c(1932, 2188), c(2188, 2316), c(2316, 2348),
    ], axis=1)
    return w_row.astype(BF16), w_col.T.astype(BF16)


def _mla_head_norm_t(x_t, gain):
    ms = jnp.sum(x_t * x_t, axis=0, keepdims=True) / float(MLA_QK)
    return x_t * lax.rsqrt(ms + EPS) * gain


def _rope_t(x_t, cos, sin):
    half = x_t.shape[0] // 2
    x1, x2 = x_t[:half], x_t[half:]
    return jnp.concatenate([x1 * cos - x2 * sin, x2 * cos + x1 * sin], axis=0)


def _in_proj_kernel(x_ref, g_ref, wrow_ref, wcol_ref, nqg_ref, dqg_ref, dkg_ref, k2g_ref, mqg_ref,
                    cqg_ref, ckvg_ref, wuq_ref, wuk_ref, wuv_ref, lqg_ref, lkg_ref,
                    c32_ref, s32_ref, c64_ref, s64_ref,
                    kcvc_o, gl_o, nz_o, dz_o, mz_o, ez_o, dk_o, k2_o,
                    nq_o, dq_o, mq_o, dv_o, v2_o, lq_o, lk_o, lv_o, kcvc_s):
    x = x_ref[...]
    tm = x.shape[0]
    ms = jnp.mean(x * x, axis=-1, keepdims=True)
    h = x * lax.rsqrt(ms + EPS) * g_ref[...]
    hb = h.astype(BF16)
    h_t = h.T.astype(BF16)

    def row(name):
        off, w = _ROW_OFF[name]
        return jnp.dot(hb, wrow_ref[:, off:off + w], preferred_element_type=F32)

    u_t = jnp.dot(wcol_ref[...], h_t, preferred_element_type=F32)

    def col(name):
        off, w = _COL_OFF[name]
        return u_t[off:off + w]

    c32, s32, c64, s64 = c32_ref[...], s32_ref[...], c64_ref[...], s64_ref[...]
    g3 = lambda ref, groups: ref[...].reshape(groups, -1, 1)

    rows_out = {name: row(name) for name, _ in _ROW_SEGS}

    kcvc_s[...] = rows_out["nsa_kcvc"]
    n_chunk_rows = tm // NSA_CMP_STRIDE
    for tok in range(NSA_CMP_STRIDE):
        kcvc_o[0, :, tok * LANES:(tok + 1) * LANES] = kcvc_s[pl.ds(tok, n_chunk_rows, stride=NSA_CMP_STRIDE), :]
    gl_o[...] = rows_out["nsa_gl"]
    nz_o[...] = rows_out["nsa_z"].astype(BF16)
    dz_o[...] = rows_out["diff_z"].astype(BF16)
    mz_o[...] = rows_out["mla_z"].astype(BF16)
    ez_o[...] = rows_out["mem_z"].astype(BF16)

    nq_o[0] = _norm_rope_t(col("nsa_q"), N_HEADS, HEAD_DIM, g3(nqg_ref, 1), c64, s64) * (HEAD_DIM ** -0.5 * LOG2E)
    dq = _norm_rope_t(col("diff_q"), 2 * N_HEADS, DIFF_D, g3(dqg_ref, 1), c32, s32) * (DIFF_D ** -0.5 * LOG2E)
    dq_o[0] = dq.astype(BF16)
    mq = _norm_rope_t(col("mem_q"), N_HEADS, HEAD_DIM, g3(mqg_ref, 1)) * (HEAD_DIM ** -0.5 * LOG2E)
    mq_o[0] = mq.astype(BF16)
    dk = _norm_rope_t(col("diff_k"), 2 * N_HEADS, DIFF_D, g3(dkg_ref, 1), c32, s32).T
    for mp in range(2 * N_HEADS):
        dk_o[0, mp] = dk[:, mp * DIFF_D:(mp + 1) * DIFF_D].astype(BF16)
    k2 = _norm_rope_t(col("nsa_k2"), 2, HEAD_DIM, g3(k2g_ref, 2), c64, s64).T
    for br in range(2):
        k2_o[0, br] = k2[:, br * HEAD_DIM:(br + 1) * HEAD_DIM].astype(BF16)
    ones = _ones_rows(tm)
    dv = col("diff_v")
    for hd in range(N_HEADS):
        dv_o[0, hd, 0:HEAD_DIM, :] = dv[hd * HEAD_DIM:(hd + 1) * HEAD_DIM].astype(BF16)
        dv_o[0, hd, HEAD_DIM:VT_ROWS, :] = ones
    v2 = col("nsa_v2")
    for br in range(2):
        v2_o[0, br, 0:HEAD_DIM, :] = v2[br * HEAD_DIM:(br + 1) * HEAD_DIM].astype(BF16)
        v2_o[0, br, HEAD_DIM:VT_ROWS, :] = ones

    def latent(name, gain_ref):
        c = col(name)
        return (c * lax.rsqrt(jnp.mean(c * c, axis=0, keepdims=True) + EPS) * gain_ref[...]).astype(BF16)

    qa = jnp.dot(wuq_ref[...], latent("mla_cq", cqg_ref), preferred_element_type=F32)
    ckv = latent("mla_ckv", ckvg_ref)
    kn = jnp.dot(wuk_ref[...], ckv, preferred_element_type=F32)
    lv = jnp.dot(wuv_ref[...], ckv, preferred_element_type=F32)
    rope_rows = slice(MLA_NOPE, MLA_QK)
    lqg = lqg_ref[...] * (MLA_QK ** -0.5 * LOG2E)
    kr = _rope_t(col("mla_kr"), c32, s32)
    zpad = jnp.zeros((LANES - MLA_QK, tm), F32)
    for hd in range(N_HEADS):
        q_h = qa[hd * LANES:(hd + 1) * LANES]
        q_h = jnp.concatenate([q_h[:MLA_NOPE], _rope_t(q_h[rope_rows], c32, s32), q_h[MLA_QK:]], axis=0)
        lq_o[0, hd * LANES:(hd + 1) * LANES, :] = _mla_head_norm_t(q_h, lqg).astype(BF16)
        k_h = jnp.concatenate([kn[hd * LANES:hd * LANES + MLA_NOPE], kr, zpad], axis=0)
        lk_o[0, hd] = _mla_head_norm_t(k_h, lkg_ref[...]).T.astype(BF16)
        lv_o[0, hd, 0:HEAD_DIM, :] = lv[hd * HEAD_DIM:(hd + 1) * HEAD_DIM].astype(BF16)
        lv_o[0, hd, HEAD_DIM:VT_ROWS, :] = ones


def _in_proj(x, gain, w_in, nsa_qk_gain, diff_qk_gain, mem_qk_gain, mla_cq_gain, mla_ckv_gain, mla_w_uq,
             mla_w_ukv, mla_qk_gain, tabs_t, tm=512):
    b, seq, d = x.shape
    n = b * seq
    nb = seq // tm
    w_row, w_col = _split_w_in(w_in)
    c32, s32, c64, s64 = tabs_t
    npad = LANES - MLA_QK
    uq = jnp.pad(mla_w_uq.reshape(MLA_Q_RANK, N_HEADS, MLA_QK), ((0, 0), (0, 0), (0, npad)))
    w_uq_t = uq.reshape(MLA_Q_RANK, N_HEADS * LANES).T.astype(BF16)
    ukv = mla_w_ukv.reshape(MLA_KV_RANK, N_HEADS, MLA_NOPE + HEAD_DIM)
    uk = jnp.pad(ukv[:, :, :MLA_NOPE], ((0, 0), (0, 0), (0, LANES - MLA_NOPE)))
    w_uk_t = uk.reshape(MLA_KV_RANK, N_HEADS * LANES).T.astype(BF16)
    w_uv_t = ukv[:, :, MLA_NOPE:].reshape(MLA_KV_RANK, GROUP_WIDTH).T.astype(BF16)
    pad_gain = lambda g: jnp.pad(g, (0, npad))[:, None]

    rowspec = lambda w: pl.BlockSpec((tm, w), lambda i: (i, 0))
    colspec = lambda r: pl.BlockSpec((1, r, tm), lambda i: (i // nb, 0, i % nb))
    vtspec = lambda c: pl.BlockSpec((1, c, VT_ROWS, tm), lambda i: (i // nb, 0, 0, i % nb))
    const = lambda shape: pl.BlockSpec(shape, lambda i: tuple(0 for _ in shape))
    tabspec = lambda r: pl.BlockSpec((r, tm), lambda i: (0, i % nb))
    row_out = lambda w, dt: jax.ShapeDtypeStruct((n, w), dt)
    col_out = lambda r, dt: jax.ShapeDtypeStruct((b, r, seq), dt)
    vt_out = lambda c: jax.ShapeDtypeStruct((b, c, VT_ROWS, seq), BF16)
    slabspec = lambda c, w: pl.BlockSpec((1, c, tm, w), lambda i: (i // nb, 0, i % nb, 0))
    slab_out = lambda c, w: jax.ShapeDtypeStruct((b, c, seq, w), BF16)
    outs = pl.pallas_call(
        _in_proj_kernel,
        grid=(n // tm,),
        in_specs=[rowspec(d), const((1, d)), const((d, _ROW_W)), const((_COL_W, d)),
                  const((HEAD_DIM, 1)), const((DIFF_D, 1)), const((DIFF_D, 1)), const((2 * HEAD_DIM, 1)),
                  const((HEAD_DIM, 1)),
                  const((MLA_Q_RANK, 1)), const((MLA_KV_RANK, 1)),
                  const((N_HEADS * LANES, MLA_Q_RANK)), const((N_HEADS * LANES, MLA_KV_RANK)),
                  const((GROUP_WIDTH, MLA_KV_RANK)), const((LANES, 1)), const((LANES, 1)),
                  tabspec(DIFF_D // 2), tabspec(DIFF_D // 2), tabspec(HEAD_DIM // 2), tabspec(HEAD_DIM // 2)],
        out_specs=[pl.BlockSpec((1, tm // NSA_CMP_STRIDE, NSA_CMP_STRIDE * LANES), lambda i: (i // nb, i % nb, 0)),
                   rowspec(128), rowspec(256), rowspec(256), rowspec(256), rowspec(256),
                   slabspec(2 * N_HEADS, DIFF_D), slabspec(2, HEAD_DIM),
                   colspec(256), colspec(256), colspec(256), vtspec(N_HEADS), vtspec(2),
                   colspec(N_HEADS * LANES), slabspec(N_HEADS, LANES), vtspec(N_HEADS)],
        out_shape=[jax.ShapeDtypeStruct((b, seq // NSA_CMP_STRIDE, NSA_CMP_STRIDE * LANES), F32),
                   row_out(128, F32), row_out(256, BF16), row_out(256, BF16),
                   row_out(256, BF16), row_out(256, BF16), slab_out(2 * N_HEADS, DIFF_D), slab_out(2, HEAD_DIM),
                   col_out(256, F32), col_out(256, BF16), col_out(256, BF16), vt_out(N_HEADS), vt_out(2),
                   col_out(N_HEADS * LANES, BF16), slab_out(N_HEADS, LANES), vt_out(N_HEADS)],
        scratch_shapes=[pltpu.VMEM((tm, LANES), F32)],
        compiler_params=_cparams(("parallel",)),
        name="in_proj",
    )(x.reshape(n, d), gain[None, :], w_row, w_col,
      nsa_qk_gain[0][:, None], diff_qk_gain[0][:, None], diff_qk_gain[1][:, None],
      jnp.concatenate([nsa_qk_gain[2], nsa_qk_gain[3]])[:, None], mem_qk_gain[0][:, None],
      mla_cq_gain[:, None], mla_ckv_gain[:, None], w_uq_t, w_uk_t, w_uv_t,
      pad_gain(mla_qk_gain[0]), pad_gain(mla_qk_gain[1]),
      c32, s32, c64, s64)
    names = ("nsa_kcvc", "nsa_gl", "nsa_z", "diff_z", "mla_z", "mem_z", "diff_k", "nsa_k2",
             "nsa_qT", "diff_qT", "mem_qT", "diff_vT", "nsa_v2T", "mla_qT", "mla_k", "mla_vT")
    u = dict(zip(names, outs))
    for name in names[1:6]:
        u[name] = u[name].reshape(b, seq, -1)
    return u


def _out_proj_kernel(x_ref, y0_ref, y1_ref, y2_ref, y3_ref, w_ref, o_ref):
    acc = x_ref[...]
    for g, y_ref in enumerate((y0_ref, y1_ref, y2_ref, y3_ref)):
        acc = acc + jnp.dot(y_ref[...], w_ref[g * GROUP_WIDTH:(g + 1) * GROUP_WIDTH, :],
                            preferred_element_type=F32)
    o_ref[...] = acc


def _out_proj(x2, ys, w_out_b, tm=1024):
    n = x2.shape[0]
    yspec = pl.BlockSpec((tm, GROUP_WIDTH), lambda i: (i, 0))
    return pl.pallas_call(
        _out_proj_kernel,
        grid=(n // tm,),
        in_specs=[pl.BlockSpec((tm, D_MODEL), lambda i: (i, 0)), yspec, yspec, yspec, yspec,
                  pl.BlockSpec((D_MODEL, D_MODEL), lambda i: (0, 0))],
        out_specs=pl.BlockSpec((tm, D_MODEL), lambda i: (i, 0)),
        out_shape=jax.ShapeDtypeStruct((n, D_MODEL), F32),
        compiler_params=_cparams(("parallel",)),
        name="out_proj",
    )(x2, *ys, w_out_b)


def _diff_kernel(lambda_init, t, qT_ref, k_ref, vT_ref, z_ref, lam_ref, sg_ref, y_ref, m_s, acc_s, s_s):
    qi = pl.program_id(1)
    n_maps = 2 * N_HEADS
    qT = qT_ref[0]
    _causal_flash_t(
        n_maps,
        lambda i, off: k_ref[0, i, pl.ds(off, KEY_TILE), :],
        lambda i: qT[i * DIFF_D:(i + 1) * DIFF_D],
        lambda i, off: vT_ref[0, i // 2, :, pl.ds(off, KEY_TILE)],
        qi, t, KEY_TILE, m_s, acc_s, s_s)

    lam = lam_ref[...]
    lmbda = (jnp.exp(jnp.sum(lam[0:1] * lam[1:2], axis=-1, keepdims=True))
             - jnp.exp(jnp.sum(lam[2:3] * lam[3:4], axis=-1, keepdims=True)) + lambda_init)
    outs = []
    for h in range(N_HEADS):
        d = _flash_out(acc_s, 2 * h) - lmbda * _flash_out(acc_s, 2 * h + 1)
        ms = jnp.mean(d * d, axis=0, keepdims=True)
        outs.append(d * lax.rsqrt(ms + EPS) * sg_ref[...] * (1.0 - lambda_init))
    y = jnp.concatenate(outs, axis=0).T * _silu(z_ref[0])
    y_ref[0] = y.astype(BF16)


def _diff_call(qT, k, vT, z, lam, subln_gain, lambda_init, t=512):
    b, _, seq, _ = k.shape
    const = lambda shape: pl.BlockSpec(shape, lambda bi, qi: tuple(0 for _ in shape))
    tile = pl.BlockSpec((1, t, GROUP_WIDTH), lambda bi, qi: (bi, qi, 0))
    return pl.pallas_call(
        functools.partial(_diff_kernel, lambda_init, t),
        grid=(b, seq // t),
        in_specs=[pl.BlockSpec((1, GROUP_WIDTH, t), lambda bi, qi: (bi, 0, qi)),
                  pl.BlockSpec((1, 2 * N_HEADS, seq, DIFF_D), lambda bi, qi: (bi, 0, 0, 0)),
                  pl.BlockSpec((1, N_HEADS, VT_ROWS, seq), lambda bi, qi: (bi, 0, 0, 0)),
                  tile, const((4, DIFF_D)), const((HEAD_DIM, 1))],
        out_specs=tile,
        out_shape=jax.ShapeDtypeStruct((b, seq, GROUP_WIDTH), BF16),
        scratch_shapes=[pltpu.VMEM((2 * N_HEADS, t), F32),
                        pltpu.VMEM((2 * N_HEADS, VT_ROWS, t), F32),
                        pltpu.VMEM((2 * N_HEADS, KEY_TILE, t), F32)],
        compiler_params=_cparams(("parallel", "parallel")),
        name="diff_attn",
    )(qT, k, vT, z, lam, subln_gain[:, None])


def _mla_mem_kernel(t, qT_ref, k_ref, vT_ref, z_ref,
                    mqT_ref, mem_ref, mz_ref, mg_ref, wk_ref, wvT_ref, kg_ref, gm_ref,
                    y_ref, ymem_ref, m_s, acc_s, s_s, mk_s, mvT_s, mm_s, macc_s):
    qi = pl.program_id(1)
    nt = (((1,), (1,)), ((), ()))
    m_len = mem_ref.shape[1]

    @pl.when(qi == 0)
    def _prep_memory_kv():
        mem = mem_ref[0]
        ms = jnp.mean(mem * mem, axis=-1, keepdims=True)
        mb = (mem * lax.rsqrt(ms + EPS) * mg_ref[...]).astype(BF16)
        k = jnp.dot(mb, wk_ref[...], preferred_element_type=F32)
        kn = k * _group_rsqrt(k, gm_ref[...], float(HEAD_DIM)) * kg_ref[...]
        vT = lax.dot_general(wvT_ref[...], mb, nt, preferred_element_type=F32)
        for h in range(N_HEADS):
            mk_s[h] = kn[:, h * HEAD_DIM:(h + 1) * HEAD_DIM].astype(BF16)
            mvT_s[h, 0:HEAD_DIM, :] = vT[h * HEAD_DIM:(h + 1) * HEAD_DIM].astype(BF16)
            mvT_s[h, HEAD_DIM:VT_ROWS, :] = _ones_rows(m_len)

    mqT = mqT_ref[0]
    _flash_reset(mm_s, macc_s)
    ss = [jnp.dot(mk_s[h], mqT[h * HEAD_DIM:(h + 1) * HEAD_DIM], preferred_element_type=F32)
          for h in range(N_HEADS)]
    for h in range(N_HEADS):
        _flash_step_t(ss[h], mvT_s[h], mm_s, macc_s, h)

    qT = qT_ref[0]
    _causal_flash_t(
        N_HEADS,
        lambda h, off: k_ref[0, h, pl.ds(off, KEY_TILE), :],
        lambda h: qT[h * LANES:(h + 1) * LANES],
        lambda h, off: vT_ref[0, h, :, pl.ds(off, KEY_TILE)],
        qi, t, KEY_TILE, m_s, acc_s, s_s)

    y_mem = jnp.concatenate([_flash_out(macc_s, h) for h in range(N_HEADS)], axis=0).T * _silu(mz_ref[0])
    ymem_ref[0] = y_mem.astype(BF16)
    y = jnp.concatenate([_flash_out(acc_s, h) for h in range(N_HEADS)], axis=0).T * _silu(z_ref[0])
    y_ref[0] = y.astype(BF16)


def _mla_mem_call(qT, k, vT, z, mem_qT, mem, mem_z, mem_gain, w_kv, mem_qk_gain, t=512):
    b, _, seq, _ = k.shape
    m_len = mem.shape[1]
    w_k = w_kv[:, :GROUP_WIDTH].astype(BF16)
    w_vT = w_kv[:, GROUP_WIDTH:].T.astype(BF16)
    tile = pl.BlockSpec((1, t, GROUP_WIDTH), lambda bi, qi: (bi, qi, 0))
    const = lambda shape: pl.BlockSpec(shape, lambda bi, qi: tuple(0 for _ in shape))
    out = jax.ShapeDtypeStruct((b, seq, GROUP_WIDTH), BF16)
    return pl.pallas_call(
        functools.partial(_mla_mem_kernel, t),
        grid=(b, seq // t),
        in_specs=[pl.BlockSpec((1, N_HEADS * LANES, t), lambda bi, qi: (bi, 0, qi)),
                  pl.BlockSpec((1, N_HEADS, seq, LANES), lambda bi, qi: (bi, 0, 0, 0)),
                  pl.BlockSpec((1, N_HEADS, VT_ROWS, seq), lambda bi, qi: (bi, 0, 0, 0)),
                  tile,
                  pl.BlockSpec((1, GROUP_WIDTH, t), lambda bi, qi: (bi, 0, qi)),
                  pl.BlockSpec((1, m_len, D_MODEL), lambda bi, qi: (bi, 0, 0)), tile,
                  const((1, D_MODEL)), const((D_MODEL, GROUP_WIDTH)), const((GROUP_WIDTH, D_MODEL)),
                  const((1, GROUP_WIDTH)), const((GROUP_WIDTH, GROUP_WIDTH))],
        out_specs=[tile, tile],
        out_shape=[out, out],
        scratch_shapes=[pltpu.VMEM((N_HEADS, t), F32),
                        pltpu.VMEM((N_HEADS, VT_ROWS, t), F32),
                        pltpu.VMEM((N_HEADS, KEY_TILE, t), F32),
                        pltpu.VMEM((N_HEADS, m_len, HEAD_DIM), BF16),
                        pltpu.VMEM((N_HEADS, VT_ROWS, m_len), BF16),
                        pltpu.VMEM((N_HEADS, t), F32),
                        pltpu.VMEM((N_HEADS, VT_ROWS, t), F32)],
        compiler_params=_cparams(("parallel", "arbitrary")),
        name="mla_mem_attn",
    )(qT, k, vT, z, mem_qT, mem, mem_z, mem_gain[None, :], w_k, w_vT,
      jnp.tile(mem_qk_gain[1], N_HEADS)[None, :], _group_ones(GROUP_WIDTH, HEAD_DIM))


def _nsa_kernel(seq, t,
                qT_ref, kcvc_ref, k2_ref, v2T_ref, gl_ref, z_ref,
                cg_ref, pelo_ref, pehi_ref, wlo_ref, whi_ref, cosc_ref, sinc_ref,
                y_ref, kc_s, vcT_s, bias_s, m_s, acc_s, s_s, pg_s):
    qi = pl.program_id(1)
    half = HEAD_DIM // 2
    n_chunk = seq // NSA_CMP_STRIDE
    n_cmp = n_chunk - 1
    n_blk = seq // NSA_SLC_BLOCK
    blk_per_tile = t // NSA_SLC_BLOCK
    win_tiles = NSA_WINDOW // t
    w4 = N_HEADS * t
    SLC, WIN = 0, 1

    @pl.when(qi == 0)
    def _prep():
        ch = kcvc_ref[0]
        a = _dot_x3(*_split_bf16(ch + pelo_ref[...]), wlo_ref[0], wlo_ref[1])
        bm = _dot_x3(*_split_bf16(ch + pehi_ref[...]), whi_ref[0], whi_ref[1])
        cmp = a + pltpu.roll(bm, n_chunk - 1, 0)
        kc = cmp[:, :HEAD_DIM]
        ms = jnp.mean(kc * kc, axis=-1, keepdims=True)
        kc = kc * lax.rsqrt(ms + EPS) * cg_ref[...]
        x1, x2 = kc[:, :half], kc[:, half:]
        c, s = cosc_ref[...], sinc_ref[...]
        kc_hi, kc_lo = _split_bf16(jnp.concatenate([x1 * c - x2 * s, x2 * c + x1 * s], axis=-1))
        kc_s[0] = kc_hi
        kc_s[1] = kc_lo
        vcT_s[...] = cmp.T[HEAD_DIM:].astype(BF16)
        pg_s[:, n_chunk:, :] = jnp.zeros((t // LANES, pg_s.shape[1] - n_chunk, LANES), F32)

    qs = qi * t
    qT = qT_ref[0]
    q4f = jnp.concatenate([qT[h * HEAD_DIM:(h + 1) * HEAD_DIM] for h in range(N_HEADS)], axis=1)
    q4 = q4f.astype(BF16)

    def scores(br, j):
        off = pl.multiple_of(j * t, t)
        return jnp.dot(k2_ref[0, br, pl.ds(off, t), :], q4, preferred_element_type=F32)

    s_s[WIN] = scores(WIN, jnp.maximum(qi - win_tiles, 0))
    qcol = lax.broadcasted_iota(jnp.int32, (1, w4), 1) & (t - 1)
    pos_c = qs + qcol

    q_lo = (q4f - q4.astype(F32)).astype(BF16)
    heads = [slice(h * t, (h + 1) * t) for h in range(N_HEADS)]
    scs = [_dot_x3(kc_s[0], kc_s[1], q4[:, c], q_lo[:, c]) for c in heads]
    n_idx = lax.broadcasted_iota(jnp.int32, (n_chunk, 1), 0)
    cvalid = (n_idx * NSA_CMP_STRIDE + (NSA_CMP_BLOCK - 1) <= pos_c[:, :t]) & (n_idx < n_cmp)
    o_cmp, pg = [], None
    for h in range(N_HEADS):
        sc = jnp.where(cvalid, scs[h], NEG_INF)
        e = jnp.exp2(sc - jnp.max(sc, axis=0, keepdims=True))
        p = jnp.where(cvalid, e / jnp.sum(e, axis=0, keepdims=True), 0.0)
        o_cmp.append(jnp.dot(vcT_s[...], p.astype(BF16), preferred_element_type=F32))
        pg = p if pg is None else pg + p

    for c0 in range(t // LANES):
        pg_s[c0, 0:n_chunk, :] = pg[:, c0 * LANES:(c0 + 1) * LANES]
    ratio = NSA_SLC_BLOCK // NSA_CMP_STRIDE
    coef = np.convolve(np.ones(ratio), np.ones(NSA_CMP_BLOCK // NSA_CMP_STRIDE))
    p_slc = None
    for i, c in enumerate(coef):
        term = float(c) * jnp.concatenate(
            [pg_s[c0, pl.ds(i, n_blk, stride=ratio), :] for c0 in range(t // LANES)], axis=1)
        p_slc = term if p_slc is None else p_slc + term
    blk = lax.broadcasted_iota(jnp.int32, (n_blk, 1), 0)
    cur = lax.shift_right_logical(pos_c[:, :t], NSA_SLC_BLOCK.bit_length() - 1)
    forced = (blk == 0) | ((blk <= cur) & (blk > cur - NSA_N_LOCAL))
    score = jnp.where(blk > cur, NEG_INF, jnp.where(forced, BIG, p_slc))
    sub = 8
    groups = [score[g * sub:(g + 1) * sub] for g in range(n_blk // sub)]
    cnts = [jnp.zeros((sub, t), F32) for _ in groups]
    for i in range(n_blk):
        ri = score[i:i + 1, :]
        for g, sg in enumerate(groups):
            if g < i // sub:
                beats = ri > sg
            elif g > i // sub:
                beats = ri >= sg
            else:
                beats = (ri > sg) | ((ri == sg) & (blk[g * sub:(g + 1) * sub] > i))
            cnts[g] = cnts[g] + jnp.where(beats, 1.0, 0.0)
    cnt = jnp.concatenate(cnts, axis=0)
    bias = jnp.where(cnt < float(NSA_N_SELECT), 0.0, NEG_INF)
    bias = jnp.concatenate([bias] * N_HEADS, axis=1)
    for r in range(n_blk // blk_per_tile):
        bias_s[r] = bias[r * blk_per_tile:(r + 1) * blk_per_tile, :]

    _flash_reset(m_s, acc_s)
    krow = lax.broadcasted_iota(jnp.int32, (t, 1), 0)
    causal = krow <= qcol
    beyond = krow > qcol

    def slc_scores(j):
        s = scores(SLC, j).reshape(blk_per_tile, NSA_SLC_BLOCK, w4) + bias_s[j][:, None, :]
        return s.reshape(t, w4)

    def flash(br, s, j):
        off = pl.multiple_of(j * t, t)
        _flash_step_t(s, v2T_ref[0, br, :, pl.ds(off, t)], m_s, acc_s, br)

    s_s[SLC] = slc_scores(0)

    def old_body(j, carry):
        s = s_s[SLC]
        s_s[SLC] = slc_scores(j + 1)
        flash(SLC, s, j)
        return carry

    lax.fori_loop(0, jnp.maximum(qi - win_tiles, 0), old_body, 0)

    def win_step(j, win_mask):
        s_slc = s_s[SLC]
        s_s[SLC] = slc_scores(j + 1)
        flash(SLC, s_slc, j)
        s_win = s_s[WIN]
        if win_mask is not None:
            s_win = jnp.where(win_mask, s_win, NEG_INF)
        s_s[WIN] = scores(WIN, j + 1)
        flash(WIN, s_win, j)

    @pl.when(qi >= win_tiles)
    def _oldest_window_tile():
        win_step(qi - win_tiles, beyond)

    def win_body(j, carry):
        win_step(j, None)
        return carry

    lax.fori_loop(jnp.maximum(qi - win_tiles + 1, 0), qi, win_body, 0)
    flash(SLC, jnp.where(causal, s_s[SLC], NEG_INF), qi)
    flash(WIN, jnp.where(causal, s_s[WIN], NEG_INF), qi)

    g = jax.nn.sigmoid(gl_ref[0]).T
    outs = []
    for h in range(N_HEADS):
        cols = slice(h * t, (h + 1) * t)
        o_s = _flash_out(acc_s, SLC, cols)
        o_w = _flash_out(acc_s, WIN, cols)
        outs.append(g[h:h + 1, :] * o_cmp[h] + g[N_HEADS + h:N_HEADS + h + 1, :] * o_s
                    + g[2 * N_HEADS + h:2 * N_HEADS + h + 1, :] * o_w)
    y = jnp.concatenate(outs, axis=0).T * _silu(z_ref[0])
    y_ref[0] = y.astype(BF16)


def _nsa_call(qT, kcvc, k2, v2T, gl, z, qk_gain, cmp_pe, w_cmp, t=256):
    b, _, seq, _ = k2.shape
    n_chunk = seq // NSA_CMP_STRIDE
    n_blk = seq // NSA_SLC_BLOCK
    half_blk = NSA_CMP_BLOCK // 2
    cw = half_blk * 2 * HEAD_DIM
    assert t % NSA_SLC_BLOCK == 0 and NSA_WINDOW % t == 0 and t & (t - 1) == 0

    wk = w_cmp[0].reshape(NSA_CMP_BLOCK, HEAD_DIM, HEAD_DIM)
    wv = w_cmp[1].reshape(NSA_CMP_BLOCK, HEAD_DIM, HEAD_DIM)
    zero = jnp.zeros_like(wk)
    w_all = jnp.concatenate([jnp.concatenate([wk, zero], axis=2),
                             jnp.concatenate([zero, wv], axis=2)], axis=1)
    hi_lo = lambda w: jnp.stack([w.astype(BF16), (w - w.astype(BF16).astype(F32)).astype(BF16)])
    w_lo = hi_lo(w_all[:half_blk].reshape(cw, 2 * HEAD_DIM))
    w_hi = hi_lo(w_all[half_blk:].reshape(cw, 2 * HEAD_DIM))
    pe_all = jnp.concatenate([cmp_pe[0], cmp_pe[1]], axis=1)
    pe_lo = pe_all[:half_blk].reshape(1, cw)
    pe_hi = pe_all[half_blk:].reshape(1, cw)

    cmp_end = jnp.arange(n_chunk, dtype=jnp.int32) * NSA_CMP_STRIDE + (NSA_CMP_BLOCK - 1)
    cos_c, sin_c = _rope_cos_sin(cmp_end, HEAD_DIM)

    tile =pl.BlockSpec((1, t, GROUP_WIDTH), lambda bi, qi: (bi, qi, 0))
    const = lambda shape: pl.BlockSpec(shape, lambda bi, qi: tuple(0 for _ in shape))
    w4 = N_HEADS * t
    return pl.pallas_call(
        functools.partial(_nsa_kernel, seq, t),
        grid=(b, seq // t),
        in_specs=[pl.BlockSpec((1, GROUP_WIDTH, t), lambda bi, qi: (bi, 0, qi)),
                  pl.BlockSpec((1, n_chunk, cw), lambda bi, qi: (bi, 0, 0)),
                  pl.BlockSpec((1, 2, seq, HEAD_DIM), lambda bi, qi: (bi, 0, 0, 0)),
                  pl.BlockSpec((1, 2, VT_ROWS, seq), lambda bi, qi: (bi, 0, 0, 0)),
                  pl.BlockSpec((1, t, LANES), lambda bi, qi: (bi, qi, 0)),
                  tile,
                  const((1, HEAD_DIM)),
                  const((1, cw)), const((1, cw)), const((2, cw, 2 * HEAD_DIM)), const((2, cw, 2 * HEAD_DIM)),
                  const((n_chunk, HEAD_DIM // 2)), const((n_chunk, HEAD_DIM // 2))],
        out_specs=tile,
        out_shape=jax.ShapeDtypeStruct((b, seq, GROUP_WIDTH), BF16),
        scratch_shapes=[pltpu.VMEM((2, n_chunk, HEAD_DIM), BF16),
                        pltpu.VMEM((HEAD_DIM, n_chunk), BF16),
                        pltpu.VMEM((seq // t, t // NSA_SLC_BLOCK, w4), F32),
                        pltpu.VMEM((2, w4), F32),
                        pltpu.VMEM((2, VT_ROWS, w4), F32),
                        pltpu.VMEM((2, t, w4), F32),
                        pltpu.VMEM((t // LANES, n_chunk + 8, LANES), F32)],
        compiler_params=_cparams(("parallel", "arbitrary")),
        name="nsa_attn",
    )(qT, kcvc, k2, v2T, gl, z,
      qk_gain[1][None, :],
      pe_lo, pe_hi, w_lo, w_hi, cos_c, sin_c)


def _rope_cos_sin(pos, dim):
    half = dim // 2
    inv_freq = ROPE_THETA ** (-jnp.arange(half, dtype=F32) / half)
    ang = pos.astype(F32)[:, None] * inv_freq[None, :]
    return jnp.cos(ang), jnp.sin(ang)


def _layer(x, mem, layer_idx, tabs_t, norm_gain, w_in, w_out, nsa_qk_gain, nsa_cmp_pe, nsa_w_cmp,
           diff_qk_gain, diff_lambda, diff_subln_gain, mla_cq_gain, mla_ckv_gain, mla_w_uq, mla_w_ukv,
           mla_qk_gain, mem_norm_gain, mem_w_kv, mem_qk_gain):
    b, seq, d = x.shape
    u = _in_proj(x, norm_gain, w_in, nsa_qk_gain, diff_qk_gain, mem_qk_gain, mla_cq_gain, mla_ckv_gain,
                 mla_w_uq, mla_w_ukv, mla_qk_gain, tabs_t)

    y_nsa = _nsa_call(u["nsa_qT"], u["nsa_kcvc"], u["nsa_k2"], u["nsa_v2T"], u["nsa_gl"], u["nsa_z"],
                      nsa_qk_gain, nsa_cmp_pe, nsa_w_cmp)
    lambda_init = 0.8 - 0.6 * math.exp(-0.3 * layer_idx)
    y_diff = _diff_call(u["diff_qT"], u["diff_k"], u["diff_vT"], u["diff_z"], diff_lambda, diff_subln_gain,
                        lambda_init)
    y_mla, y_mem = _mla_mem_call(u["mla_qT"], u["mla_k"], u["mla_vT"], u["mla_z"],
                                 u["mem_qT"], mem, u["mem_z"], mem_norm_gain, mem_w_kv, mem_qk_gain)

    ys = [y.reshape(b * seq, GROUP_WIDTH) for y in (y_nsa, y_diff, y_mla, y_mem)]
    return _out_proj(x.reshape(b * seq, d), ys, w_out.astype(BF16)).reshape(b, seq, d)


def kernel(x, mem, norm_gain, w_in, w_out, nsa_qk_gain, nsa_cmp_pe, nsa_w_cmp, diff_qk_gain, diff_lambda,
           diff_subln_gain, mla_cq_gain, mla_ckv_gain, mla_w_uq, mla_w_ukv, mla_qk_gain, mem_norm_gain,
           mem_w_kv, mem_qk_gain):
    seq = x.shape[1]
    pos = jnp.arange(seq, dtype=jnp.int32)
    cos32, sin32 = _rope_cos_sin(pos, DIFF_D)
    cos64, sin64 = _rope_cos_sin(pos, HEAD_DIM)
    tabs_t = (cos32.T, sin32.T, cos64.T, sin64.T)
    for l in range(DEPTH):
        x = _layer(x, mem, l, tabs_t, norm_gain[l], w_in[l], w_out[l], nsa_qk_gain[l], nsa_cmp_pe[l],
                   nsa_w_cmp[l], diff_qk_gain[l], diff_lambda[l], diff_subln_gain[l], mla_cq_gain[l],
                   mla_ckv_gain[l], mla_w_uq[l], mla_w_ukv[l], mla_qk_gain[l], mem_norm_gain[l], mem_w_kv[l],
                   mem_qk_gain[l])
    return x
```

```python
import functools
import math

import numpy as np
import jax
import jax.numpy as jnp
from jax import lax
from jax.experimental import pallas as pl
from jax.experimental.pallas import tpu as pltpu

F32 = jnp.float32
BF16 = jnp.bfloat16

D_MODEL = 1024
DEPTH = 2
N_HEADS = 4
HEAD_DIM = 64
GROUP_WIDTH = N_HEADS * HEAD_DIM
ROPE_THETA = 10000.0
EPS = 1e-6
NEG_INF = -1e30
BIG = 1e30
LOG2E = 1.4426950408889634

NSA_CMP_BLOCK = 32
NSA_CMP_STRIDE = 16
NSA_SLC_BLOCK = 64
NSA_N_SELECT = 16
NSA_N_LOCAL = 2
NSA_WINDOW = 512
DIFF_D = HEAD_DIM // 2
MLA_Q_RANK = 256
MLA_KV_RANK = 128
MLA_NOPE = 64
MLA_ROPE = 32
MLA_QK = MLA_NOPE + MLA_ROPE

VMEM_LIMIT_BYTES = 48 * 1024 * 1024
LANES = 128
KEY_TILE = 256
VT_ROWS = 80


def _cparams(sem):
    return pltpu.CompilerParams(dimension_semantics=sem, vmem_limit_bytes=VMEM_LIMIT_BYTES)


def _group_ones(width, group):
    g = np.arange(width) // group
    return jnp.asarray(g[:, None] == g[None, :], dtype=BF16)


def _group_rsqrt(x, gmat, denom):
    sq = x * x
    hi = sq.astype(BF16)
    lo = (sq - hi.astype(F32)).astype(BF16)
    ss = jnp.dot(hi, gmat, preferred_element_type=F32) + jnp.dot(lo, gmat, preferred_element_type=F32)
    return lax.rsqrt(ss / denom + EPS)


def _norm_rope_t(u_t, groups, dim, gain, cos=None, sin=None):
    x = u_t.reshape(groups, dim, u_t.shape[-1])
    ms = jnp.mean(x * x, axis=1, keepdims=True)
    x = x * lax.rsqrt(ms + EPS) * gain
    if cos is not None:
        half = dim // 2
        x1, x2 = x[:, :half], x[:, half:]
        x = jnp.concatenate([x1 * cos - x2 * sin, x2 * cos + x1 * sin], axis=1)
    return x.reshape(groups * dim, u_t.shape[-1])


def _silu(z):
    z = z.astype(F32)
    return z * jax.nn.sigmoid(z)


def _ones_rows(n):
    r = lax.broadcasted_iota(jnp.int32, (VT_ROWS - HEAD_DIM, n), 0)
    return jnp.where(r == 0, 1.0, 0.0).astype(BF16)


def _split_bf16(x):
    hi = x.astype(BF16)
    return hi, (x - hi.astype(F32)).astype(BF16)


def _dot_x3(a_hi, a_lo, b_hi, b_lo):
    dot = lambda a, b: jnp.dot(a, b, preferred_element_type=F32)
    return dot(a_hi, b_hi) + dot(a_hi, b_lo) + dot(a_lo, b_hi)


def _flash_step_t(s, v_t, m_ref, acc_ref, i, cols=slice(None)):
    m_old = m_ref[i:i + 1, cols]
    m_new = jnp.maximum(m_old, jnp.max(s, axis=0, keepdims=True))
    alpha = jnp.exp2(m_old - m_new)
    p = jnp.exp2(s - m_new).astype(BF16)
    acc_ref[i, :, cols] = alpha * acc_ref[i, :, cols] + jnp.dot(v_t, p, preferred_element_type=F32)
    m_ref[i:i + 1, cols] = m_new


def _flash_reset(m_ref, acc_ref):
    m_ref[...] = jnp.full(m_ref.shape, NEG_INF, F32)
    acc_ref[...] = jnp.zeros(acc_ref.shape, F32)


def _flash_out(acc_ref, i, cols=slice(None)):
    return acc_ref[i, 0:HEAD_DIM, cols] / acc_ref[i, HEAD_DIM:HEAD_DIM + 1, cols]


def _causal_flash_t(n, k_tile, q_t, v_tile, qi, tq, tk, m_ref, acc_ref, s_ref):
    _flash_reset(m_ref, acc_ref)
    n_diag = tq // tk

    all_q = slice(0, tq)

    def scores(i, off, cols):
        return jnp.dot(k_tile(i, off), q_t(i)[:, cols], preferred_element_type=F32)

    def step(off, cols, off_next, cols_next, mask):
        for i in range(n):
            s = s_ref[i, :, cols]
            if off_next is not None:
                s_ref[i, :, cols_next] = scores(i, off_next, cols_next)
            if mask is not None:
                s = jnp.where(mask, s, NEG_INF)
            _flash_step_t(s, v_tile(i, off), m_ref, acc_ref, i, cols)

    for i in range(n):
        s_ref[i] = scores(i, 0, all_q)

    def body(j, carry):
        step(pl.multiple_of(j * tk, tk), all_q, pl.multiple_of(j * tk + tk, tk), all_q, None)
        return carry

    lax.fori_loop(0, qi * n_diag, body, 0)
    krow = lax.broadcasted_iota(jnp.int32, (tk, tq), 0)
    qcol = lax.broadcasted_iota(jnp.int32, (tk, tq), 1)
    causal = krow <= qcol
    diag_cols = [slice(d * tk, tq) for d in range(n_diag)]
    for d in range(n_diag):
        off = pl.multiple_of(qi * tq + d * tk, tk)
        last = d + 1 == n_diag
        off_next = None if last else pl.multiple_of(qi * tq + (d + 1) * tk, tk)
        step(off, diag_cols[d], off_next, None if last else diag_cols[d + 1], causal[:, 0:tq - d * tk])


_ROW_SEGS = (("nsa_kcvc", 128), ("nsa_gl", 128), ("nsa_z", 256), ("diff_z", 256), ("mla_z", 256),
             ("mem_z", 256))
_COL_SEGS = (("nsa_q", 256), ("diff_q", 256), ("mem_q", 256), ("diff_k", 256), ("nsa_k2", 128),
             ("diff_v", 256), ("nsa_v2", 128), ("mla_cq", 256), ("mla_ckv", 128), ("mla_kr", 32))


def _seg_offsets(segs):
    out, off = {}, 0
    for name, w in segs:
        out[name] = (off, w)
        off += w
    return out, off


_ROW_OFF, _ROW_W = _seg_offsets(_ROW_SEGS)
_COL_OFF, _COL_W = _seg_offsets(_COL_SEGS)


_ROW_SRC = ((256, 384),
            (640, 652), (None, 116),
            (652, 908), (1676, 1932), (2348, 2604), (2860, 3116))
_COL_SRC = ((0, 256), (908, 1164), (2604, 2860),
            (1164, 1420),
            (384, 448), (512, 576),
            (1420, 1676),
            (448, 512), (576, 640),
            (1932, 2188), (2188, 2316), (2316, 2348))


def _w_split_kernel(w_ref, wrow_o, wcol_o):
    rt = w_ref.shape[1]

    def gather(src):
        return jnp.concatenate([jnp.zeros((rt, b), F32) if a is None else w_ref[0, :, a:b] for a, b in src], axis=1)

    wrow_o[0] = gather(_ROW_SRC).astype(BF16)
    pad = (-_COL_W) % LANES
    wcol_o[0] = gather(_COL_SRC + ((None, pad),)).T[:_COL_W].astype(BF16)


def _split_w_in(w_in, rt=256):
    layers, d, d_in = w_in.shape
    return pl.pallas_call(
        _w_split_kernel,
        grid=(layers, d // rt),
        in_specs=[pl.BlockSpec((1, rt, d_in), lambda l, r: (l, r, 0))],
        out_specs=[pl.BlockSpec((1, rt, _ROW_W), lambda l, r: (l, r, 0)),
                   pl.BlockSpec((1, _COL_W, rt), lambda l, r: (l, 0, r))],
        out_shape=[jax.ShapeDtypeStruct((layers, d, _ROW_W), BF16),
                   jax.ShapeDtypeStruct((layers, _COL_W, d), BF16)],
        compiler_params=_cparams(("parallel", "parallel")),
        name="w_in_split",
    )(w_in)


def _mla_head_norm_t(x_t, gain):
    ms = jnp.sum(x_t * x_t, axis=0, keepdims=True) / float(MLA_QK)
    return x_t * lax.rsqrt(ms + EPS) * gain


def _rope_t(x_t, cos, sin):
    half = x_t.shape[0] // 2
    x1, x2 = x_t[:half], x_t[half:]
    return jnp.concatenate([x1 * cos - x2 * sin, x2 * cos + x1 * sin], axis=0)


def _in_proj_kernel(x_ref, g_ref, wrow_ref, wcol_ref, nqg_ref, dqg_ref, dkg_ref, k2g_ref, mqg_ref,
                    cqg_ref, ckvg_ref, wuq_ref, wuk_ref, wuv_ref, lqg_ref, lkg_ref,
                    c32_ref, s32_ref, c64_ref, s64_ref,
                    kcvc_o, gl_o, nz_o, dz_o, mz_o, ez_o, dk_o, k2_o,
                    nq_o, dq_o, mq_o, dv_o, v2_o, lq_o, lk_o, lv_o, kcvc_s):
    x = x_ref[...]
    tm = x.shape[0]
    ms = jnp.mean(x * x, axis=-1, keepdims=True)
    h = x * lax.rsqrt(ms + EPS) * g_ref[...]
    hb = h.astype(BF16)
    h_t = h.T.astype(BF16)

    def row(name):
        off, w = _ROW_OFF[name]
        return jnp.dot(hb, wrow_ref[:, off:off + w], preferred_element_type=F32)

    u_t = jnp.dot(wcol_ref[...], h_t, preferred_element_type=F32)

    def col(name):
        off, w = _COL_OFF[name]
        return u_t[off:off + w]

    c32, s32, c64, s64 = c32_ref[...], s32_ref[...], c64_ref[...], s64_ref[...]
    g3 = lambda ref, groups: ref[...].reshape(groups, -1, 1)

    rows_out = {name: row(name) for name, _ in _ROW_SEGS}

    kcvc_s[...] = rows_out["nsa_kcvc"]
    n_chunk_rows = tm // NSA_CMP_STRIDE
    for tok in range(NSA_CMP_STRIDE):
        kcvc_o[0, :, tok * LANES:(tok + 1) * LANES] = kcvc_s[pl.ds(tok, n_chunk_rows, stride=NSA_CMP_STRIDE), :]
    gl_o[...] = rows_out["nsa_gl"]
    nz_o[...] = rows_out["nsa_z"].astype(BF16)
    dz_o[...] = rows_out["diff_z"].astype(BF16)
    mz_o[...] = rows_out["mla_z"].astype(BF16)
    ez_o[...] = rows_out["mem_z"].astype(BF16)

    nq_o[0] = _norm_rope_t(col("nsa_q"), N_HEADS, HEAD_DIM, g3(nqg_ref, 1), c64, s64) * (HEAD_DIM ** -0.5 * LOG2E)
    dq = _norm_rope_t(col("diff_q"), 2 * N_HEADS, DIFF_D, g3(dqg_ref, 1), c32, s32) * (DIFF_D ** -0.5 * LOG2E)
    dq_o[0] = dq.astype(BF16)
    mq = _norm_rope_t(col("mem_q"), N_HEADS, HEAD_DIM, g3(mqg_ref, 1)) * (HEAD_DIM ** -0.5 * LOG2E)
    mq_o[0] = mq.astype(BF16)
    dk = _norm_rope_t(col("diff_k"), 2 * N_HEADS, DIFF_D, g3(dkg_ref, 1), c32, s32).T
    for mp in range(2 * N_HEADS):
        dk_o[0, mp] = dk[:, mp * DIFF_D:(mp + 1) * DIFF_D].astype(BF16)
    k2 = _norm_rope_t(col("nsa_k2"), 2, HEAD_DIM, g3(k2g_ref, 2), c64, s64).T
    for br in range(2):
        k2_o[0, br] = k2[:, br * HEAD_DIM:(br + 1) * HEAD_DIM].astype(BF16)
    ones = _ones_rows(tm)
    dv = col("diff_v")
    for hd in range(N_HEADS):
        dv_o[0, hd, 0:HEAD_DIM, :] = dv[hd * HEAD_DIM:(hd + 1) * HEAD_DIM].astype(BF16)
        dv_o[0, hd, HEAD_DIM:VT_ROWS, :] = ones
    v2 = col("nsa_v2")
    for br in range(2):
        v2_o[0, br, 0:HEAD_DIM, :] = v2[br * HEAD_DIM:(br + 1) * HEAD_DIM].astype(BF16)
        v2_o[0, br, HEAD_DIM:VT_ROWS, :] = ones

    def latent(name, gain_ref):
        c = col(name)
        return (c * lax.rsqrt(jnp.mean(c * c, axis=0, keepdims=True) + EPS) * gain_ref[...]).astype(BF16)

    qa = jnp.dot(wuq_ref[...], latent("mla_cq", cqg_ref), preferred_element_type=F32)
    ckv = latent("mla_ckv", ckvg_ref)
    kn = jnp.dot(wuk_ref[...], ckv, preferred_element_type=F32)
    lv = jnp.dot(wuv_ref[...], ckv, preferred_element_type=F32)
    rope_rows = slice(MLA_NOPE, MLA_QK)
    lqg = lqg_ref[...] * (MLA_QK ** -0.5 * LOG2E)
    kr = _rope_t(col("mla_kr"), c32, s32)
    zpad = jnp.zeros((LANES - MLA_QK, tm), F32)
    for hd in range(N_HEADS):
        q_h = qa[hd * LANES:(hd + 1) * LANES]
        q_h = jnp.concatenate([q_h[:MLA_NOPE], _rope_t(q_h[rope_rows], c32, s32), q_h[MLA_QK:]], axis=0)
        lq_o[0, hd * LANES:(hd + 1) * LANES, :] = _mla_head_norm_t(q_h, lqg).astype(BF16)
        k_h = jnp.concatenate([kn[hd * LANES:hd * LANES + MLA_NOPE], kr, zpad], axis=0)
        lk_o[0, hd] = _mla_head_norm_t(k_h, lkg_ref[...]).T.astype(BF16)
        lv_o[0, hd, 0:HEAD_DIM, :] = lv[hd * HEAD_DIM:(hd + 1) * HEAD_DIM].astype(BF16)
        lv_o[0, hd, HEAD_DIM:VT_ROWS, :] = ones


def _in_proj(x, gain, w_in, nsa_qk_gain, diff_qk_gain, mem_qk_gain, mla_cq_gain, mla_ckv_gain, mla_w_uq,
             mla_w_ukv, mla_qk_gain, tabs_t, tm=512):
    b, seq, d = x.shape
    n = b * seq
    nb = seq // tm
    w_row, w_col = w_in
    c32, s32, c64, s64 = tabs_t
    npad = LANES - MLA_QK
    uq = jnp.pad(mla_w_uq.reshape(MLA_Q_RANK, N_HEADS, MLA_QK), ((0, 0), (0, 0), (0, npad)))
    w_uq_t = uq.reshape(MLA_Q_RANK, N_HEADS * LANES).T.astype(BF16)
    ukv = mla_w_ukv.reshape(MLA_KV_RANK, N_HEADS, MLA_NOPE + HEAD_DIM)
    uk = jnp.pad(ukv[:, :, :MLA_NOPE], ((0, 0), (0, 0), (0, LANES - MLA_NOPE)))
    w_uk_t = uk.reshape(MLA_KV_RANK, N_HEADS * LANES).T.astype(BF16)
    w_uv_t = ukv[:, :, MLA_NOPE:].reshape(MLA_KV_RANK, GROUP_WIDTH).T.astype(BF16)
    pad_gain = lambda g: jnp.pad(g, (0, npad))[:, None]

    rowspec = lambda w: pl.BlockSpec((tm, w), lambda i: (i, 0))
    colspec = lambda r: pl.BlockSpec((1, r, tm), lambda i: (i // nb, 0, i % nb))
    vtspec = lambda c: pl.BlockSpec((1, c, VT_ROWS, tm), lambda i: (i // nb, 0, 0, i % nb))
    const = lambda shape: pl.BlockSpec(shape, lambda i: tuple(0 for _ in shape))
    tabspec = lambda r: pl.BlockSpec((r, tm), lambda i: (0, i % nb))
    row_out = lambda w, dt: jax.ShapeDtypeStruct((n, w), dt)
    col_out = lambda r, dt: jax.ShapeDtypeStruct((b, r, seq), dt)
    vt_out = lambda c: jax.ShapeDtypeStruct((b, c, VT_ROWS, seq), BF16)
    slabspec = lambda c, w: pl.BlockSpec((1, c, tm, w), lambda i: (i // nb, 0, i % nb, 0))
    slab_out = lambda c, w: jax.ShapeDtypeStruct((b, c, seq, w), BF16)
    outs = pl.pallas_call(
        _in_proj_kernel,
        grid=(n // tm,),
        in_specs=[rowspec(d), const((1, d)), const((d, _ROW_W)), const((_COL_W, d)),
                  const((HEAD_DIM, 1)), const((DIFF_D, 1)), const((DIFF_D, 1)), const((2 * HEAD_DIM, 1)),
                  const((HEAD_DIM, 1)),
                  const((MLA_Q_RANK, 1)), const((MLA_KV_RANK, 1)),
                  const((N_HEADS * LANES, MLA_Q_RANK)), const((N_HEADS * LANES, MLA_KV_RANK)),
                  const((GROUP_WIDTH, MLA_KV_RANK)), const((LANES, 1)), const((LANES, 1)),
                  tabspec(DIFF_D // 2), tabspec(DIFF_D // 2), tabspec(HEAD_DIM // 2), tabspec(HEAD_DIM // 2)],
        out_specs=[pl.BlockSpec((1, tm // NSA_CMP_STRIDE, NSA_CMP_STRIDE * LANES), lambda i: (i // nb, i % nb, 0)),
                   rowspec(128), rowspec(256), rowspec(256), rowspec(256), rowspec(256),
                   slabspec(2 * N_HEADS, DIFF_D), slabspec(2, HEAD_DIM),
                   colspec(256), colspec(256), colspec(256), vtspec(N_HEADS), vtspec(2),
                   colspec(N_HEADS * LANES), slabspec(N_HEADS, LANES), vtspec(N_HEADS)],
        out_shape=[jax.ShapeDtypeStruct((b, seq // NSA_CMP_STRIDE, NSA_CMP_STRIDE * LANES), F32),
                   row_out(128, F32), row_out(256, BF16), row_out(256, BF16),
                   row_out(256, BF16), row_out(256, BF16), slab_out(2 * N_HEADS, DIFF_D), slab_out(2, HEAD_DIM),
                   col_out(256, F32), col_out(256, BF16), col_out(256, BF16), vt_out(N_HEADS), vt_out(2),
                   col_out(N_HEADS * LANES, BF16), slab_out(N_HEADS, LANES), vt_out(N_HEADS)],
        scratch_shapes=[pltpu.VMEM((tm, LANES), F32)],
        compiler_params=_cparams(("parallel",)),
        name="in_proj",
    )(x.reshape(n, d), gain[None, :], w_row, w_col,
      nsa_qk_gain[0][:, None], diff_qk_gain[0][:, None], diff_qk_gain[1][:, None],
      jnp.concatenate([nsa_qk_gain[2], nsa_qk_gain[3]])[:, None], mem_qk_gain[0][:, None],
      mla_cq_gain[:, None], mla_ckv_gain[:, None], w_uq_t, w_uk_t, w_uv_t,
      pad_gain(mla_qk_gain[0]), pad_gain(mla_qk_gain[1]),
      c32, s32, c64, s64)
    names = ("nsa_kcvc", "nsa_gl", "nsa_z", "diff_z", "mla_z", "mem_z", "diff_k", "nsa_k2",
             "nsa_qT", "diff_qT", "mem_qT", "diff_vT", "nsa_v2T", "mla_qT", "mla_k", "mla_vT")
    u = dict(zip(names, outs))
    for name in names[1:6]:
        u[name] = u[name].reshape(b, seq, -1)
    return u


def _out_proj_kernel(x_ref, y0_ref, y1_ref, y2_ref, y3_ref, w_ref, o_ref):
    acc = x_ref[...]
    for g, y_ref in enumerate((y0_ref, y1_ref, y2_ref, y3_ref)):
        acc = acc + jnp.dot(y_ref[...], w_ref[g * GROUP_WIDTH:(g + 1) * GROUP_WIDTH, :],
                            preferred_element_type=F32)
    o_ref[...] = acc


def _out_proj(x2, ys, w_out_b, tm=1024):
    n = x2.shape[0]
    yspec = pl.BlockSpec((tm, GROUP_WIDTH), lambda i: (i, 0))
    return pl.pallas_call(
        _out_proj_kernel,
        grid=(n // tm,),
        in_specs=[pl.BlockSpec((tm, D_MODEL), lambda i: (i, 0)), yspec, yspec, yspec, yspec,
                  pl.BlockSpec((D_MODEL, D_MODEL), lambda i: (0, 0))],
        out_specs=pl.BlockSpec((tm, D_MODEL), lambda i: (i, 0)),
        out_shape=jax.ShapeDtypeStruct((n, D_MODEL), F32),
        compiler_params=_cparams(("parallel",)),
        name="out_proj",
    )(x2, *ys, w_out_b)


def _diff_kernel(lambda_init, t, qT_ref, k_ref, vT_ref, z_ref, lam_ref, sg_ref, y_ref, m_s, acc_s, s_s):
    qi = pl.program_id(1)
    n_maps = 2 * N_HEADS
    qT = qT_ref[0]
    _causal_flash_t(
        n_maps,
        lambda i, off: k_ref[0, i, pl.ds(off, KEY_TILE), :],
        lambda i: qT[i * DIFF_D:(i + 1) * DIFF_D],
        lambda i, off: vT_ref[0, i // 2, :, pl.ds(off, KEY_TILE)],
        qi, t, KEY_TILE, m_s, acc_s, s_s)

    lam = lam_ref[...]
    lmbda = (jnp.exp(jnp.sum(lam[0:1] * lam[1:2], axis=-1, keepdims=True))
             - jnp.exp(jnp.sum(lam[2:3] * lam[3:4], axis=-1, keepdims=True)) + lambda_init)
    outs = []
    for h in range(N_HEADS):
        d = _flash_out(acc_s, 2 * h) - lmbda * _flash_out(acc_s, 2 * h + 1)
        ms = jnp.mean(d * d, axis=0, keepdims=True)
        outs.append(d * lax.rsqrt(ms + EPS) * sg_ref[...] * (1.0 - lambda_init))
    y = jnp.concatenate(outs, axis=0).T * _silu(z_ref[0])
    y_ref[0] = y.astype(BF16)


def _diff_call(qT, k, vT, z, lam, subln_gain, lambda_init, t=512):
    b, _, seq, _ = k.shape
    const = lambda shape: pl.BlockSpec(shape, lambda bi, qi: tuple(0 for _ in shape))
    tile = pl.BlockSpec((1, t, GROUP_WIDTH), lambda bi, qi: (bi, qi, 0))
    return pl.pallas_call(
        functools.partial(_diff_kernel, lambda_init, t),
        grid=(b, seq // t),
        in_specs=[pl.BlockSpec((1, GROUP_WIDTH, t), lambda bi, qi: (bi, 0, qi)),
                  pl.BlockSpec((1, 2 * N_HEADS, seq, DIFF_D), lambda bi, qi: (bi, 0, 0, 0)),
                  pl.BlockSpec((1, N_HEADS, VT_ROWS, seq), lambda bi, qi: (bi, 0, 0, 0)),
                  tile, const((4, DIFF_D)), const((HEAD_DIM, 1))],
        out_specs=tile,
        out_shape=jax.ShapeDtypeStruct((b, seq, GROUP_WIDTH), BF16),
        scratch_shapes=[pltpu.VMEM((2 * N_HEADS, t), F32),
                        pltpu.VMEM((2 * N_HEADS, VT_ROWS, t), F32),
                        pltpu.VMEM((2 * N_HEADS, KEY_TILE, t), F32)],
        compiler_params=_cparams(("parallel", "parallel")),
        name="diff_attn",
    )(qT, k, vT, z, lam, subln_gain[:, None])


def _mla_mem_kernel(t, qT_ref, k_ref, vT_ref, z_ref,
                    mqT_ref, mem_ref, mz_ref, mg_ref, wk_ref, wvT_ref, kg_ref, gm_ref,
                    y_ref, ymem_ref, m_s, acc_s, s_s, mk_s, mvT_s, mm_s, macc_s):
    qi = pl.program_id(1)
    nt = (((1,), (1,)), ((), ()))
    m_len = mem_ref.shape[1]

    @pl.when(qi == 0)
    def _prep_memory_kv():
        mem = mem_ref[0]
        ms = jnp.mean(mem * mem, axis=-1, keepdims=True)
        mb = (mem * lax.rsqrt(ms + EPS) * mg_ref[...]).astype(BF16)
        k = jnp.dot(mb, wk_ref[...], preferred_element_type=F32)
        kn = k * _group_rsqrt(k, gm_ref[...], float(HEAD_DIM)) * kg_ref[...]
        vT = lax.dot_general(wvT_ref[...], mb, nt, preferred_element_type=F32)
        for h in range(N_HEADS):
            mk_s[h] = kn[:, h * HEAD_DIM:(h + 1) * HEAD_DIM].astype(BF16)
            mvT_s[h, 0:HEAD_DIM, :] = vT[h * HEAD_DIM:(h + 1) * HEAD_DIM].astype(BF16)
            mvT_s[h, HEAD_DIM:VT_ROWS, :] = _ones_rows(m_len)

    mqT = mqT_ref[0]
    _flash_reset(mm_s, macc_s)
    ss = [jnp.dot(mk_s[h], mqT[h * HEAD_DIM:(h + 1) * HEAD_DIM], preferred_element_type=F32)
          for h in range(N_HEADS)]
    for h in range(N_HEADS):
        _flash_step_t(ss[h], mvT_s[h], mm_s, macc_s, h)

    qT = qT_ref[0]
    _causal_flash_t(
        N_HEADS,
        lambda h, off: k_ref[0, h, pl.ds(off, KEY_TILE), :],
        lambda h: qT[h * LANES:(h + 1) * LANES],
        lambda h, off: vT_ref[0, h, :, pl.ds(off, KEY_TILE)],
        qi, t, KEY_TILE, m_s, acc_s, s_s)

    y_mem = jnp.concatenate([_flash_out(macc_s, h) for h in range(N_HEADS)], axis=0).T * _silu(mz_ref[0])
    ymem_ref[0] = y_mem.astype(BF16)
    y = jnp.concatenate([_flash_out(acc_s, h) for h in range(N_HEADS)], axis=0).T * _silu(z_ref[0])
    y_ref[0] = y.astype(BF16)


def _mla_mem_call(qT, k, vT, z, mem_qT, mem, mem_z, mem_gain, w_kv, mem_qk_gain, t=512):
    b, _, seq, _ = k.shape
    m_len = mem.shape[1]
    w_k = w_kv[:, :GROUP_WIDTH].astype(BF16)
    w_vT = w_kv[:, GROUP_WIDTH:].T.astype(BF16)
    tile = pl.BlockSpec((1, t, GROUP_WIDTH), lambda bi, qi: (bi, qi, 0))
    const = lambda shape: pl.BlockSpec(shape, lambda bi, qi: tuple(0 for _ in shape))
    out = jax.ShapeDtypeStruct((b, seq, GROUP_WIDTH), BF16)
    return pl.pallas_call(
        functools.partial(_mla_mem_kernel, t),
        grid=(b, seq // t),
        in_specs=[pl.BlockSpec((1, N_HEADS * LANES, t), lambda bi, qi: (bi, 0, qi)),
                  pl.BlockSpec((1, N_HEADS, seq, LANES), lambda bi, qi: (bi, 0, 0, 0)),
                  pl.BlockSpec((1, N_HEADS, VT_ROWS, seq), lambda bi, qi: (bi, 0, 0, 0)),
                  tile,
                  pl.BlockSpec((1, GROUP_WIDTH, t), lambda bi, qi: (bi, 0, qi)),
                  pl.BlockSpec((1, m_len, D_MODEL), lambda bi, qi: (bi, 0, 0)), tile,
                  const((1, D_MODEL)), const((D_MODEL, GROUP_WIDTH)), const((GROUP_WIDTH, D_MODEL)),
                  const((1, GROUP_WIDTH)), const((GROUP_WIDTH, GROUP_WIDTH))],
        out_specs=[tile, tile],
        out_shape=[out, out],
        scratch_shapes=[pltpu.VMEM((N_HEADS, t), F32),
                        pltpu.VMEM((N_HEADS, VT_ROWS, t), F32),
                        pltpu.VMEM((N_HEADS, KEY_TILE, t), F32),
                        pltpu.VMEM((N_HEADS, m_len, HEAD_DIM), BF16),
                        pltpu.VMEM((N_HEADS, VT_ROWS, m_len), BF16),
                        pltpu.VMEM((N_HEADS, t), F32),
                        pltpu.VMEM((N_HEADS, VT_ROWS, t), F32)],
        compiler_params=_cparams(("parallel", "arbitrary")),
        name="mla_mem_attn",
    )(qT, k, vT, z, mem_qT, mem, mem_z, mem_gain[None, :], w_k, w_vT,
      jnp.tile(mem_qk_gain[1], N_HEADS)[None, :], _group_ones(GROUP_WIDTH, HEAD_DIM))


def _nsa_kernel(seq, t,
                qT_ref, kcvc_ref, k2_ref, v2T_ref, gl_ref, z_ref,
                cg_ref, pelo_ref, pehi_ref, wlo_ref, whi_ref, cosc_ref, sinc_ref,
                y_ref, kc_s, vcT_s, bias_s, m_s, acc_s, s_s, pg_s):
    qi = pl.program_id(1)
    half = HEAD_DIM // 2
    n_chunk = seq // NSA_CMP_STRIDE
    n_cmp = n_chunk - 1
    n_blk = seq // NSA_SLC_BLOCK
    blk_per_tile = t // NSA_SLC_BLOCK
    win_tiles = NSA_WINDOW // t
    w4 = N_HEADS * t
    SLC, WIN = 0, 1

    @pl.when(qi == 0)
    def _prep():
        ch = kcvc_ref[0]
        a = _dot_x3(*_split_bf16(ch + pelo_ref[...]), wlo_ref[0], wlo_ref[1])
        bm = _dot_x3(*_split_bf16(ch + pehi_ref[...]), whi_ref[0], whi_ref[1])
        cmp = a + pltpu.roll(bm, n_chunk - 1, 0)
        kc = cmp[:, :HEAD_DIM]
        ms = jnp.mean(kc * kc, axis=-1, keepdims=True)
        kc = kc * lax.rsqrt(ms + EPS) * cg_ref[...]
        x1, x2 = kc[:, :half], kc[:, half:]
        c, s = cosc_ref[...], sinc_ref[...]
        kc_hi, kc_lo = _split_bf16(jnp.concatenate([x1 * c - x2 * s, x2 * c + x1 * s], axis=-1))
        kc_s[0] = kc_hi
        kc_s[1] = kc_lo
        vcT_s[...] = cmp.T[HEAD_DIM:].astype(BF16)
        pg_s[:, n_chunk:, :] = jnp.zeros((t // LANES, pg_s.shape[1] - n_chunk, LANES), F32)

    qs = qi * t
    qT = qT_ref[0]
    q4f = jnp.concatenate([qT[h * HEAD_DIM:(h + 1) * HEAD_DIM] for h in range(N_HEADS)], axis=1)
    q4 = q4f.astype(BF16)

    def scores(br, j):
        off = pl.multiple_of(j * t, t)
        return jnp.dot(k2_ref[0, br, pl.ds(off, t), :], q4, preferred_element_type=F32)

    s_s[WIN] = scores(WIN, jnp.maximum(qi - win_tiles, 0))
    qcol = lax.broadcasted_iota(jnp.int32, (1, w4), 1) & (t - 1)
    pos_c = qs + qcol

    q_lo = (q4f - q4.astype(F32)).astype(BF16)
    heads = [slice(h * t, (h + 1) * t) for h in range(N_HEADS)]
    scs = [_dot_x3(kc_s[0], kc_s[1], q4[:, c], q_lo[:, c]) for c in heads]
    n_idx = lax.broadcasted_iota(jnp.int32, (n_chunk, 1), 0)
    cvalid = (n_idx * NSA_CMP_STRIDE + (NSA_CMP_BLOCK - 1) <= pos_c[:, :t]) & (n_idx < n_cmp)
    o_cmp, pg = [], None
    for h in range(N_HEADS):
        sc = jnp.where(cvalid, scs[h], NEG_INF)
        e = jnp.exp2(sc - jnp.max(sc, axis=0, keepdims=True))
        p = jnp.where(cvalid, e / jnp.sum(e, axis=0, keepdims=True), 0.0)
        o_cmp.append(jnp.dot(vcT_s[...], p.astype(BF16), preferred_element_type=F32))
        pg = p if pg is None else pg + p

    for c0 in range(t // LANES):
        pg_s[c0, 0:n_chunk, :] = pg[:, c0 * LANES:(c0 + 1) * LANES]
    ratio = NSA_SLC_BLOCK // NSA_CMP_STRIDE
    coef = np.convolve(np.ones(ratio), np.ones(NSA_CMP_BLOCK // NSA_CMP_STRIDE))
    p_slc = None
    for i, c in enumerate(coef):
        term = float(c) * jnp.concatenate(
            [pg_s[c0, pl.ds(i, n_blk, stride=ratio), :] for c0 in range(t // LANES)], axis=1)
        p_slc = term if p_slc is None else p_slc + term
    blk = lax.broadcasted_iota(jnp.int32, (n_blk, 1), 0)
    cur = lax.shift_right_logical(pos_c[:, :t], NSA_SLC_BLOCK.bit_length() - 1)
    forced = (blk == 0) | ((blk <= cur) & (blk > cur - NSA_N_LOCAL))
    score = jnp.where(blk > cur, NEG_INF, jnp.where(forced, BIG, p_slc))
    sub = 8
    groups = [score[g * sub:(g + 1) * sub] for g in range(n_blk // sub)]
    cnts = [jnp.zeros((sub, t), F32) for _ in groups]
    for i in range(n_blk):
        ri = score[i:i + 1, :]
        for g, sg in enumerate(groups):
            if g < i // sub:
                beats = ri > sg
            elif g > i // sub:
                beats = ri >= sg
            else:
                beats = (ri > sg) | ((ri == sg) & (blk[g * sub:(g + 1) * sub] > i))
            cnts[g] = cnts[g] + jnp.where(beats, 1.0, 0.0)
    cnt = jnp.concatenate(cnts, axis=0)
    bias = jnp.where(cnt < float(NSA_N_SELECT), 0.0, NEG_INF)
    bias = jnp.concatenate([bias] * N_HEADS, axis=1)
    for r in range(n_blk // blk_per_tile):
        bias_s[r] = bias[r * blk_per_tile:(r + 1) * blk_per_tile, :]

    _flash_reset(m_s, acc_s)
    krow = lax.broadcasted_iota(jnp.int32, (t, 1), 0)
    causal = krow <= qcol
    beyond = krow > qcol

    def slc_scores(j):
        s = scores(SLC, j).reshape(blk_per_tile, NSA_SLC_BLOCK, w4) + bias_s[j][:, None, :]
        return s.reshape(t, w4)

    def flash(br, s, j):
        off = pl.multiple_of(j * t, t)
        _flash_step_t(s, v2T_ref[0, br, :, pl.ds(off, t)], m_s, acc_s, br)

    s_s[SLC] = slc_scores(0)

    def old_body(j, carry):
        s = s_s[SLC]
        s_s[SLC] = slc_scores(j + 1)
        flash(SLC, s, j)
        return carry

    lax.fori_loop(0, jnp.maximum(qi - win_tiles, 0), old_body, 0)

    def win_step(j, win_mask):
        s_slc = s_s[SLC]
        s_s[SLC] = slc_scores(j + 1)
        flash(SLC, s_slc, j)
        s_win = s_s[WIN]
        if win_mask is not None:
            s_win = jnp.where(win_mask, s_win, NEG_INF)
        s_s[WIN] = scores(WIN, j + 1)
        flash(WIN, s_win, j)

    @pl.when(qi >= win_tiles)
    def _oldest_window_tile():
        win_step(qi - win_tiles, beyond)

    def win_body(j, carry):
        win_step(j, None)
        return carry

    lax.fori_loop(jnp.maximum(qi - win_tiles + 1, 0), qi, win_body, 0)
    flash(SLC, jnp.where(causal, s_s[SLC], NEG_INF), qi)
    flash(WIN, jnp.where(causal, s_s[WIN], NEG_INF), qi)

    g = jax.nn.sigmoid(gl_ref[0]).T
    outs = []
    for h in range(N_HEADS):
        cols = slice(h * t, (h + 1) * t)
        o_s = _flash_out(acc_s, SLC, cols)
        o_w = _flash_out(acc_s, WIN, cols)
        outs.append(g[h:h + 1, :] * o_cmp[h] + g[N_HEADS + h:N_HEADS + h + 1, :] * o_s
                    + g[2 * N_HEADS + h:2 * N_HEADS + h + 1, :] * o_w)
    y = jnp.concatenate(outs, axis=0).T * _silu(z_ref[0])
    y_ref[0] = y.astype(BF16)


def _nsa_call(qT, kcvc, k2, v2T, gl, z, qk_gain, cmp_pe, w_cmp, t=256):
    b, _, seq, _ = k2.shape
    n_chunk = seq // NSA_CMP_STRIDE
    n_blk = seq // NSA_SLC_BLOCK
    half_blk = NSA_CMP_BLOCK // 2
    cw = half_blk * 2 * HEAD_DIM
    assert t % NSA_SLC_BLOCK == 0 and NSA_WINDOW % t == 0 and t & (t - 1) == 0

    wk = w_cmp[0].reshape(NSA_CMP_BLOCK, HEAD_DIM, HEAD_DIM)
    wv = w_cmp[1].reshape(NSA_CMP_BLOCK, HEAD_DIM, HEAD_DIM)
    zero = jnp.zeros_like(wk)
    w_all = jnp.concatenate([jnp.concatenate([wk, zero], axis=2),
                             jnp.concatenate([zero, wv], axis=2)], axis=1)
    hi_lo = lambda w: jnp.stack([w.astype(BF16), (w - w.astype(BF16).astype(F32)).astype(BF16)])
    w_lo = hi_lo(w_all[:half_blk].reshape(cw, 2 * HEAD_DIM))
    w_hi = hi_lo(w_all[half_blk:].reshape(cw, 2 * HEAD_DIM))
    pe_all = jnp.concatenate([cmp_pe[0], cmp_pe[1]], axis=1)
    pe_lo = pe_all[:half_blk].reshape(1, cw)
    pe_hi = pe_all[half_blk:].reshape(1, cw)

    cmp_end = jnp.arange(n_chunk, dtype=jnp.int32) * NSA_CMP_STRIDE + (NSA_CMP_BLOCK - 1)
    cos_c, sin_c = _rope_cos_sin(cmp_end, HEAD_DIM)

    tile =pl.BlockSpec((1, t, GROUP_WIDTH), lambda bi, qi: (bi, qi, 0))
    const = lambda shape: pl.BlockSpec(shape, lambda bi, qi: tuple(0 for _ in shape))
    w4 = N_HEADS * t
    return pl.pallas_call(
        functools.partial(_nsa_kernel, seq, t),
        grid=(b, seq // t),
        in_specs=[pl.BlockSpec((1, GROUP_WIDTH, t), lambda bi, qi: (bi, 0, qi)),
                  pl.BlockSpec((1, n_chunk, cw), lambda bi, qi: (bi, 0, 0)),
                  pl.BlockSpec((1, 2, seq, HEAD_DIM), lambda bi, qi: (bi, 0, 0, 0)),
                  pl.BlockSpec((1, 2, VT_ROWS, seq), lambda bi, qi: (bi, 0, 0, 0)),
                  pl.BlockSpec((1, t, LANES), lambda bi, qi: (bi, qi, 0)),
                  tile,
                  const((1, HEAD_DIM)),
                  const((1, cw)), const((1, cw)), const((2, cw, 2 * HEAD_DIM)), const((2, cw, 2 * HEAD_DIM)),
                  const((n_chunk, HEAD_DIM // 2)), const((n_chunk, HEAD_DIM // 2))],
        out_specs=tile,
        out_shape=jax.ShapeDtypeStruct((b, seq, GROUP_WIDTH), BF16),
        scratch_shapes=[pltpu.VMEM((2, n_chunk, HEAD_DIM), BF16),
                        pltpu.VMEM((HEAD_DIM, n_chunk), BF16),
                        pltpu.VMEM((seq // t, t // NSA_SLC_BLOCK, w4), F32),
                        pltpu.VMEM((2, w4), F32),
                        pltpu.VMEM((2, VT_ROWS, w4), F32),
                        pltpu.VMEM((2, t, w4), F32),
                        pltpu.VMEM((t // LANES, n_chunk + 8, LANES), F32)],
        compiler_params=_cparams(("parallel", "arbitrary")),
        name="nsa_attn",
    )(qT, kcvc, k2, v2T, gl, z,
      qk_gain[1][None, :],
      pe_lo, pe_hi, w_lo, w_hi, cos_c, sin_c)


def _rope_cos_sin(pos, dim):
    half = dim // 2
    inv_freq = ROPE_THETA ** (-jnp.arange(half, dtype=F32) / half)
    ang = pos.astype(F32)[:, None] * inv_freq[None, :]
    return jnp.cos(ang), jnp.sin(ang)


def _layer(x, mem, layer_idx, tabs_t, norm_gain, w_in, w_out, nsa_qk_gain, nsa_cmp_pe, nsa_w_cmp,
           diff_qk_gain, diff_lambda, diff_subln_gain, mla_cq_gain, mla_ckv_gain, mla_w_uq, mla_w_ukv,
           mla_qk_gain, mem_norm_gain, mem_w_kv, mem_qk_gain):
    b, seq, d = x.shape
    u = _in_proj(x, norm_gain, w_in, nsa_qk_gain, diff_qk_gain, mem_qk_gain, mla_cq_gain, mla_ckv_gain,
                 mla_w_uq, mla_w_ukv, mla_qk_gain, tabs_t)

    y_nsa = _nsa_call(u["nsa_qT"], u["nsa_kcvc"], u["nsa_k2"], u["nsa_v2T"], u["nsa_gl"], u["nsa_z"],
                      nsa_qk_gain, nsa_cmp_pe, nsa_w_cmp)
    lambda_init = 0.8 - 0.6 * math.exp(-0.3 * layer_idx)
    y_diff = _diff_call(u["diff_qT"], u["diff_k"], u["diff_vT"], u["diff_z"], diff_lambda, diff_subln_gain,
                        lambda_init)
    y_mla, y_mem = _mla_mem_call(u["mla_qT"], u["mla_k"], u["mla_vT"], u["mla_z"],
                                 u["mem_qT"], mem, u["mem_z"], mem_norm_gain, mem_w_kv, mem_qk_gain)

    ys = [y.reshape(b * seq, GROUP_WIDTH) for y in (y_nsa, y_diff, y_mla, y_mem)]
    return _out_proj(x.reshape(b * seq, d), ys, w_out.astype(BF16)).reshape(b, seq, d)


def kernel(x, mem, norm_gain, w_in, w_out, nsa_qk_gain, nsa_cmp_pe, nsa_w_cmp, diff_qk_gain, diff_lambda,
           diff_subln_gain, mla_cq_gain, mla_ckv_gain, mla_w_uq, mla_w_ukv, mla_qk_gain, mem_norm_gain,
           mem_w_kv, mem_qk_gain):
    seq = x.shape[1]
    pos = jnp.arange(seq, dtype=jnp.int32)
    cos32, sin32 = _rope_cos_sin(pos, DIFF_D)
    cos64, sin64 = _rope_cos_sin(pos, HEAD_DIM)
    tabs_t = (cos32.T, sin32.T, cos64.T, sin64.T)
    w_row, w_col = _split_w_in(w_in)
    for l in range(DEPTH):
        x = _layer(x, mem, l, tabs_t, norm_gain[l], (w_row[l], w_col[l]), w_out[l], nsa_qk_gain[l], nsa_cmp_pe[l],
                   nsa_w_cmp[l], diff_qk_gain[l], diff_lambda[l], diff_subln_gain[l], mla_cq_gain[l],
                   mla_ckv_gain[l], mla_w_uq[l], mla_w_ukv[l], mla_qk_gain[l], mem_norm_gain[l], mem_w_kv[l],
                   mem_qk_gain[l])
    return x
```

```python
import functools
import math

import numpy as np
import jax
import jax.numpy as jnp
from jax import lax
from jax.experimental import pallas as pl
from jax.experimental.pallas import tpu as pltpu

F32 = jnp.float32
BF16 = jnp.bfloat16

D_MODEL = 1024
DEPTH = 2
N_HEADS = 4
HEAD_DIM = 64
GROUP_WIDTH = N_HEADS * HEAD_DIM
ROPE_THETA = 10000.0
EPS = 1e-6
NEG_INF = -1e30
BIG = 1e30
LOG2E = 1.4426950408889634

NSA_CMP_BLOCK = 32
NSA_CMP_STRIDE = 16
NSA_SLC_BLOCK = 64
NSA_N_SELECT = 16
NSA_N_LOCAL = 2
NSA_WINDOW = 512
DIFF_D = HEAD_DIM // 2
MLA_Q_RANK = 256
MLA_KV_RANK = 128
MLA_NOPE = 64
MLA_ROPE = 32
MLA_QK = MLA_NOPE + MLA_ROPE

VMEM_LIMIT_BYTES = 48 * 1024 * 1024
IN_PROJ_VMEM_LIMIT_BYTES = 56 * 1024 * 1024
LANES = 128
KEY_TILE = 256
VT_ROWS = 80


def _cparams(sem, vmem_limit_bytes=VMEM_LIMIT_BYTES):
    return pltpu.CompilerParams(dimension_semantics=sem, vmem_limit_bytes=vmem_limit_bytes)


def _group_ones(width, group):
    g = np.arange(width) // group
    return jnp.asarray(g[:, None] == g[None, :], dtype=BF16)


def _group_rsqrt(x, gmat, denom):
    sq = x * x
    hi = sq.astype(BF16)
    lo = (sq - hi.astype(F32)).astype(BF16)
    ss = jnp.dot(hi, gmat, preferred_element_type=F32) + jnp.dot(lo, gmat, preferred_element_type=F32)
    return lax.rsqrt(ss / denom + EPS)


def _norm_rope_t(u_t, groups, dim, gain, cos=None, sin=None):
    x = u_t.reshape(groups, dim, u_t.shape[-1])
    ms = jnp.mean(x * x, axis=1, keepdims=True)
    x = x * lax.rsqrt(ms + EPS) * gain
    if cos is not None:
        half = dim // 2
        x1, x2 = x[:, :half], x[:, half:]
        x = jnp.concatenate([x1 * cos - x2 * sin, x2 * cos + x1 * sin], axis=1)
    return x.reshape(groups * dim, u_t.shape[-1])


def _silu(z):
    z = z.astype(F32)
    return z * jax.nn.sigmoid(z)


def _ones_rows(n):
    r = lax.broadcasted_iota(jnp.int32, (VT_ROWS - HEAD_DIM, n), 0)
    return jnp.where(r == 0, 1.0, 0.0).astype(BF16)


def _split_bf16(x):
    hi = x.astype(BF16)
    return hi, (x - hi.astype(F32)).astype(BF16)


def _dot_x3(a_hi, a_lo, b_hi, b_lo):
    dot = lambda a, b: jnp.dot(a, b, preferred_element_type=F32)
    return dot(a_hi, b_hi) + dot(a_hi, b_lo) + dot(a_lo, b_hi)


def _flash_step_t(s, v_t, m_ref, acc_ref, i, cols=slice(None)):
    m_old = m_ref[i:i + 1, cols]
    m_new = jnp.maximum(m_old, jnp.max(s, axis=0, keepdims=True))
    alpha = jnp.exp2(m_old - m_new)
    p = jnp.exp2(s - m_new).astype(BF16)
    acc_ref[i, :, cols] = alpha * acc_ref[i, :, cols] + jnp.dot(v_t, p, preferred_element_type=F32)
    m_ref[i:i + 1, cols] = m_new


def _flash_reset(m_ref, acc_ref):
    m_ref[...] = jnp.full(m_ref.shape, NEG_INF, F32)
    acc_ref[...] = jnp.zeros(acc_ref.shape, F32)


def _flash_out(acc_ref, i, cols=slice(None)):
    return acc_ref[i, 0:HEAD_DIM, cols] / acc_ref[i, HEAD_DIM:HEAD_DIM + 1, cols]


def _causal_flash_t(n, k_tile, q_t, v_tile, qi, tq, tk, m_ref, acc_ref, s_ref):
    _flash_reset(m_ref, acc_ref)
    n_diag = tq // tk

    all_q = slice(0, tq)

    def scores(i, off, cols):
        return jnp.dot(k_tile(i, off), q_t(i)[:, cols], preferred_element_type=F32)

    def step(off, cols, off_next, cols_next, mask):
        for i in range(n):
            s = s_ref[i, :, cols]
            if off_next is not None:
                s_ref[i, :, cols_next] = scores(i, off_next, cols_next)
            if mask is not None:
                s = jnp.where(mask, s, NEG_INF)
            _flash_step_t(s, v_tile(i, off), m_ref, acc_ref, i, cols)

    for i in range(n):
        s_ref[i] = scores(i, 0, all_q)

    def body(j, carry):
        step(pl.multiple_of(j * tk, tk), all_q, pl.multiple_of(j * tk + tk, tk), all_q, None)
        return carry

    lax.fori_loop(0, qi * n_diag, body, 0)
    krow = lax.broadcasted_iota(jnp.int32, (tk, tq), 0)
    qcol = lax.broadcasted_iota(jnp.int32, (tk, tq), 1)
    causal = krow <= qcol
    diag_cols = [slice(d * tk, tq) for d in range(n_diag)]
    for d in range(n_diag):
        off = pl.multiple_of(qi * tq + d * tk, tk)
        last = d + 1 == n_diag
        off_next = None if last else pl.multiple_of(qi * tq + (d + 1) * tk, tk)
        step(off, diag_cols[d], off_next, None if last else diag_cols[d + 1], causal[:, 0:tq - d * tk])


_ROW_SEGS = (("nsa_kcvc", 128), ("nsa_gl", 128), ("nsa_z", 256), ("diff_z", 256), ("mla_z", 256),
             ("mem_z", 256))
_COL_SEGS = (("nsa_q", 256), ("diff_q", 256), ("mem_q", 256), ("diff_k", 256), ("nsa_k2", 128),
             ("diff_v", 256), ("nsa_v2", 128), ("mla_cq", 256), ("mla_ckv", 128), ("mla_kr", 32))


def _seg_offsets(segs):
    out, off = {}, 0
    for name, w in segs:
        out[name] = (off, w)
        off += w
    return out, off


_ROW_OFF, _ROW_W = _seg_offsets(_ROW_SEGS)
_COL_OFF, _COL_W = _seg_offsets(_COL_SEGS)


_ROW_SRC = ((256, 384),
            (640, 652), (None, 116),
            (652, 908), (1676, 1932), (2348, 2604), (2860, 3116))
_COL_SRC = ((0, 256), (908, 1164), (2604, 2860),
            (1164, 1420),
            (384, 448), (512, 576),
            (1420, 1676),
            (448, 512), (576, 640),
            (1932, 2188), (2188, 2316), (2316, 2348))


def _w_split_kernel(w_ref, wrow_o, wcol_o):
    rt = w_ref.shape[1]

    def gather(src):
        return jnp.concatenate([jnp.zeros((rt, b), F32) if a is None else w_ref[0, :, a:b] for a, b in src], axis=1)

    wrow_o[0] = gather(_ROW_SRC).astype(BF16)
    pad = (-_COL_W) % LANES
    wcol_o[0] = gather(_COL_SRC + ((None, pad),)).T[:_COL_W].astype(BF16)


def _split_w_in(w_in, rt=256):
    layers, d, d_in = w_in.shape
    return pl.pallas_call(
        _w_split_kernel,
        grid=(layers, d // rt),
        in_specs=[pl.BlockSpec((1, rt, d_in), lambda l, r: (l, r, 0))],
        out_specs=[pl.BlockSpec((1, rt, _ROW_W), lambda l, r: (l, r, 0)),
                   pl.BlockSpec((1, _COL_W, rt), lambda l, r: (l, 0, r))],
        out_shape=[jax.ShapeDtypeStruct((layers, d, _ROW_W), BF16),
                   jax.ShapeDtypeStruct((layers, _COL_W, d), BF16)],
        compiler_params=_cparams(("parallel", "parallel")),
        name="w_in_split",
    )(w_in)


def _mla_head_norm_t(x_t, gain):
    ms = jnp.sum(x_t * x_t, axis=0, keepdims=True) / float(MLA_QK)
    return x_t * lax.rsqrt(ms + EPS) * gain


def _rope_t(x_t, cos, sin):
    half = x_t.shape[0] // 2
    x1, x2 = x_t[:half], x_t[half:]
    return jnp.concatenate([x1 * cos - x2 * sin, x2 * cos + x1 * sin], axis=0)


def _in_proj_kernel(x_ref, g_ref, wrow_ref, wcol_ref, nqg_ref, dqg_ref, dkg_ref, k2g_ref, mqg_ref,
                    cqg_ref, ckvg_ref, wuq_ref, wuk_ref, wuv_ref, lqg_ref, lkg_ref,
                    c32_ref, s32_ref, c64_ref, s64_ref,
                    kcvc_o, gl_o, nz_o, dz_o, mz_o, ez_o, dk_o, k2_o,
                    nq_o, dq_o, mq_o, dv_o, v2_o, lq_o, lk_o, lv_o, kcvc_s):
    x = x_ref[...]
    tm = x.shape[0]
    ms = jnp.mean(x * x, axis=-1, keepdims=True)
    h = x * lax.rsqrt(ms + EPS) * g_ref[...]
    hb = h.astype(BF16)
    h_t = h.T.astype(BF16)

    def row(name):
        off, w = _ROW_OFF[name]
        return jnp.dot(hb, wrow_ref[:, off:off + w], preferred_element_type=F32)

    u_t = jnp.dot(wcol_ref[...], h_t, preferred_element_type=F32)

    def col(name):
        off, w = _COL_OFF[name]
        return u_t[off:off + w]

    c32, s32, c64, s64 = c32_ref[...], s32_ref[...], c64_ref[...], s64_ref[...]
    g3 = lambda ref, groups: ref[...].reshape(groups, -1, 1)

    rows_out = {name: row(name) for name, _ in _ROW_SEGS}

    kcvc_s[...] = rows_out["nsa_kcvc"]
    n_chunk_rows = tm // NSA_CMP_STRIDE
    for tok in range(NSA_CMP_STRIDE):
        kcvc_o[0, :, tok * LANES:(tok + 1) * LANES] = kcvc_s[pl.ds(tok, n_chunk_rows, stride=NSA_CMP_STRIDE), :]
    gl_o[...] = rows_out["nsa_gl"]
    nz_o[...] = rows_out["nsa_z"].astype(BF16)
    dz_o[...] = rows_out["diff_z"].astype(BF16)
    mz_o[...] = rows_out["mla_z"].astype(BF16)
    ez_o[...] = rows_out["mem_z"].astype(BF16)

    nq_o[0] = _norm_rope_t(col("nsa_q"), N_HEADS, HEAD_DIM, g3(nqg_ref, 1), c64, s64) * (HEAD_DIM ** -0.5 * LOG2E)
    dq = _norm_rope_t(col("diff_q"), 2 * N_HEADS, DIFF_D, g3(dqg_ref, 1), c32, s32) * (DIFF_D ** -0.5 * LOG2E)
    dq_o[0] = dq.astype(BF16)
    mq = _norm_rope_t(col("mem_q"), N_HEADS, HEAD_DIM, g3(mqg_ref, 1)) * (HEAD_DIM ** -0.5 * LOG2E)
    mq_o[0] = mq.astype(BF16)
    dk = _norm_rope_t(col("diff_k"), 2 * N_HEADS, DIFF_D, g3(dkg_ref, 1), c32, s32).T
    for mp in range(2 * N_HEADS):
        dk_o[0, mp] = dk[:, mp * DIFF_D:(mp + 1) * DIFF_D].astype(BF16)
    k2 = _norm_rope_t(col("nsa_k2"), 2, HEAD_DIM, g3(k2g_ref, 2), c64, s64).T
    for br in range(2):
        k2_o[0, br] = k2[:, br * HEAD_DIM:(br + 1) * HEAD_DIM].astype(BF16)
    ones = _ones_rows(tm)
    dv = col("diff_v")
    for hd in range(N_HEADS):
        dv_o[0, hd, 0:HEAD_DIM, :] = dv[hd * HEAD_DIM:(hd + 1) * HEAD_DIM].astype(BF16)
        dv_o[0, hd, HEAD_DIM:VT_ROWS, :] = ones
    v2 = col("nsa_v2")
    for br in range(2):
        v2_o[0, br, 0:HEAD_DIM, :] = v2[br * HEAD_DIM:(br + 1) * HEAD_DIM].astype(BF16)
        v2_o[0, br, HEAD_DIM:VT_ROWS, :] = ones

    def latent(name, gain_ref):
        c = col(name)
        return (c * lax.rsqrt(jnp.mean(c * c, axis=0, keepdims=True) + EPS) * gain_ref[...]).astype(BF16)

    qa = jnp.dot(wuq_ref[...], latent("mla_cq", cqg_ref), preferred_element_type=F32)
    ckv = latent("mla_ckv", ckvg_ref)
    kn = jnp.dot(wuk_ref[...], ckv, preferred_element_type=F32)
    lv = jnp.dot(wuv_ref[...], ckv, preferred_element_type=F32)
    rope_rows = slice(MLA_NOPE, MLA_QK)
    lqg = lqg_ref[...] * (MLA_QK ** -0.5 * LOG2E)
    kr = _rope_t(col("mla_kr"), c32, s32)
    zpad = jnp.zeros((LANES - MLA_QK, tm), F32)
    for hd in range(N_HEADS):
        q_h = qa[hd * LANES:(hd + 1) * LANES]
        q_h = jnp.concatenate([q_h[:MLA_NOPE], _rope_t(q_h[rope_rows], c32, s32), q_h[MLA_QK:]], axis=0)
        lq_o[0, hd * LANES:(hd + 1) * LANES, :] = _mla_head_norm_t(q_h, lqg).astype(BF16)
        k_h = jnp.concatenate([kn[hd * LANES:hd * LANES + MLA_NOPE], kr, zpad], axis=0)
        lk_o[0, hd] = _mla_head_norm_t(k_h, lkg_ref[...]).T.astype(BF16)
        lv_o[0, hd, 0:HEAD_DIM, :] = lv[hd * HEAD_DIM:(hd + 1) * HEAD_DIM].astype(BF16)
        lv_o[0, hd, HEAD_DIM:VT_ROWS, :] = ones


def _in_proj(x, gain, w_in, nsa_qk_gain, diff_qk_gain, mem_qk_gain, mla_cq_gain, mla_ckv_gain, mla_w_uq,
             mla_w_ukv, mla_qk_gain, tabs_t, tm=1024):
    b, seq, d = x.shape
    n = b * seq
    nb = seq // tm
    w_row, w_col = w_in
    c32, s32, c64, s64 = tabs_t
    npad = LANES - MLA_QK
    uq = jnp.pad(mla_w_uq.reshape(MLA_Q_RANK, N_HEADS, MLA_QK), ((0, 0), (0, 0), (0, npad)))
    w_uq_t = uq.reshape(MLA_Q_RANK, N_HEADS * LANES).T.astype(BF16)
    ukv = mla_w_ukv.reshape(MLA_KV_RANK, N_HEADS, MLA_NOPE + HEAD_DIM)
    uk = jnp.pad(ukv[:, :, :MLA_NOPE], ((0, 0), (0, 0), (0, LANES - MLA_NOPE)))
    w_uk_t = uk.reshape(MLA_KV_RANK, N_HEADS * LANES).T.astype(BF16)
    w_uv_t = ukv[:, :, MLA_NOPE:].reshape(MLA_KV_RANK, GROUP_WIDTH).T.astype(BF16)
    pad_gain = lambda g: jnp.pad(g, (0, npad))[:, None]

    rowspec = lambda w: pl.BlockSpec((tm, w), lambda i: (i, 0))
    colspec = lambda r: pl.BlockSpec((1, r, tm), lambda i: (i // nb, 0, i % nb))
    vtspec = lambda c: pl.BlockSpec((1, c, VT_ROWS, tm), lambda i: (i // nb, 0, 0, i % nb))
    const = lambda shape: pl.BlockSpec(shape, lambda i: tuple(0 for _ in shape))
    tabspec = lambda r: pl.BlockSpec((r, tm), lambda i: (0, i % nb))
    row_out = lambda w, dt: jax.ShapeDtypeStruct((n, w), dt)
    col_out = lambda r, dt: jax.ShapeDtypeStruct((b, r, seq), dt)
    vt_out = lambda c: jax.ShapeDtypeStruct((b, c, VT_ROWS, seq), BF16)
    slabspec = lambda c, w: pl.BlockSpec((1, c, tm, w), lambda i: (i // nb, 0, i % nb, 0))
    slab_out = lambda c, w: jax.ShapeDtypeStruct((b, c, seq, w), BF16)
    outs = pl.pallas_call(
        _in_proj_kernel,
        grid=(n // tm,),
        in_specs=[rowspec(d), const((1, d)), const((d, _ROW_W)), const((_COL_W, d)),
                  const((HEAD_DIM, 1)), const((DIFF_D, 1)), const((DIFF_D, 1)), const((2 * HEAD_DIM, 1)),
                  const((HEAD_DIM, 1)),
                  const((MLA_Q_RANK, 1)), const((MLA_KV_RANK, 1)),
                  const((N_HEADS * LANES, MLA_Q_RANK)), const((N_HEADS * LANES, MLA_KV_RANK)),
                  const((GROUP_WIDTH, MLA_KV_RANK)), const((LANES, 1)), const((LANES, 1)),
                  tabspec(DIFF_D // 2), tabspec(DIFF_D // 2), tabspec(HEAD_DIM // 2), tabspec(HEAD_DIM // 2)],
        out_specs=[pl.BlockSpec((1, tm // NSA_CMP_STRIDE, NSA_CMP_STRIDE * LANES), lambda i: (i // nb, i % nb, 0)),
                   rowspec(128), rowspec(256), rowspec(256), rowspec(256), rowspec(256),
                   slabspec(2 * N_HEADS, DIFF_D), slabspec(2, HEAD_DIM),
                   colspec(256), colspec(256), colspec(256), vtspec(N_HEADS), vtspec(2),
                   colspec(N_HEADS * LANES), slabspec(N_HEADS, LANES), vtspec(N_HEADS)],
        out_shape=[jax.ShapeDtypeStruct((b, seq // NSA_CMP_STRIDE, NSA_CMP_STRIDE * LANES), F32),
                   row_out(128, F32), row_out(256, BF16), row_out(256, BF16),
                   row_out(256, BF16), row_out(256, BF16), slab_out(2 * N_HEADS, DIFF_D), slab_out(2, HEAD_DIM),
                   col_out(256, F32), col_out(256, BF16), col_out(256, BF16), vt_out(N_HEADS), vt_out(2),
                   col_out(N_HEADS * LANES, BF16), slab_out(N_HEADS, LANES), vt_out(N_HEADS)],
        scratch_shapes=[pltpu.VMEM((tm, LANES), F32)],
        compiler_params=_cparams(("parallel",), IN_PROJ_VMEM_LIMIT_BYTES),
        name="in_proj",
    )(x.reshape(n, d), gain[None, :], w_row, w_col,
      nsa_qk_gain[0][:, None], diff_qk_gain[0][:, None], diff_qk_gain[1][:, None],
      jnp.concatenate([nsa_qk_gain[2], nsa_qk_gain[3]])[:, None], mem_qk_gain[0][:, None],
      mla_cq_gain[:, None], mla_ckv_gain[:, None], w_uq_t, w_uk_t, w_uv_t,
      pad_gain(mla_qk_gain[0]), pad_gain(mla_qk_gain[1]),
      c32, s32, c64, s64)
    names = ("nsa_kcvc", "nsa_gl", "nsa_z", "diff_z", "mla_z", "mem_z", "diff_k", "nsa_k2",
             "nsa_qT", "diff_qT", "mem_qT", "diff_vT", "nsa_v2T", "mla_qT", "mla_k", "mla_vT")
    u = dict(zip(names, outs))
    for name in names[1:6]:
        u[name] = u[name].reshape(b, seq, -1)
    return u


def _out_proj_kernel(x_ref, y0_ref, y1_ref, y2_ref, y3_ref, w_ref, o_ref):
    acc = x_ref[...]
    for g, y_ref in enumerate((y0_ref, y1_ref, y2_ref, y3_ref)):
        acc = acc + jnp.dot(y_ref[...], w_ref[g * GROUP_WIDTH:(g + 1) * GROUP_WIDTH, :],
                            preferred_element_type=F32)
    o_ref[...] = acc


def _out_proj(x2, ys, w_out_b, tm=1024):
    n = x2.shape[0]
    yspec = pl.BlockSpec((tm, GROUP_WIDTH), lambda i: (i, 0))
    return pl.pallas_call(
        _out_proj_kernel,
        grid=(n // tm,),
        in_specs=[pl.BlockSpec((tm, D_MODEL), lambda i: (i, 0)), yspec, yspec, yspec, yspec,
                  pl.BlockSpec((D_MODEL, D_MODEL), lambda i: (0, 0))],
        out_specs=pl.BlockSpec((tm, D_MODEL), lambda i: (i, 0)),
        out_shape=jax.ShapeDtypeStruct((n, D_MODEL), F32),
        compiler_params=_cparams(("parallel",)),
        name="out_proj",
    )(x2, *ys, w_out_b)


def _diff_kernel(lambda_init, t, qT_ref, k_ref, vT_ref, z_ref, lam_ref, sg_ref, y_ref, m_s, acc_s, s_s):
    qi = pl.program_id(1)
    n_maps = 2 * N_HEADS
    qT = qT_ref[0]
    _causal_flash_t(
        n_maps,
        lambda i, off: k_ref[0, i, pl.ds(off, KEY_TILE), :],
        lambda i: qT[i * DIFF_D:(i + 1) * DIFF_D],
        lambda i, off: vT_ref[0, i // 2, :, pl.ds(off, KEY_TILE)],
        qi, t, KEY_TILE, m_s, acc_s, s_s)

    lam = lam_ref[...]
    lmbda = (jnp.exp(jnp.sum(lam[0:1] * lam[1:2], axis=-1, keepdims=True))
             - jnp.exp(jnp.sum(lam[2:3] * lam[3:4], axis=-1, keepdims=True)) + lambda_init)
    outs = []
    for h in range(N_HEADS):
        d = _flash_out(acc_s, 2 * h) - lmbda * _flash_out(acc_s, 2 * h + 1)
        ms = jnp.mean(d * d, axis=0, keepdims=True)
        outs.append(d * lax.rsqrt(ms + EPS) * sg_ref[...] * (1.0 - lambda_init))
    y = jnp.concatenate(outs, axis=0).T * _silu(z_ref[0])
    y_ref[0] = y.astype(BF16)


def _diff_call(qT, k, vT, z, lam, subln_gain, lambda_init, t=512):
    b, _, seq, _ = k.shape
    const = lambda shape: pl.BlockSpec(shape, lambda bi, qi: tuple(0 for _ in shape))
    tile = pl.BlockSpec((1, t, GROUP_WIDTH), lambda bi, qi: (bi, qi, 0))
    return pl.pallas_call(
        functools.partial(_diff_kernel, lambda_init, t),
        grid=(b, seq // t),
        in_specs=[pl.BlockSpec((1, GROUP_WIDTH, t), lambda bi, qi: (bi, 0, qi)),
                  pl.BlockSpec((1, 2 * N_HEADS, seq, DIFF_D), lambda bi, qi: (bi, 0, 0, 0)),
                  pl.BlockSpec((1, N_HEADS, VT_ROWS, seq), lambda bi, qi: (bi, 0, 0, 0)),
                  tile, const((4, DIFF_D)), const((HEAD_DIM, 1))],
        out_specs=tile,
        out_shape=jax.ShapeDtypeStruct((b, seq, GROUP_WIDTH), BF16),
        scratch_shapes=[pltpu.VMEM((2 * N_HEADS, t), F32),
                        pltpu.VMEM((2 * N_HEADS, VT_ROWS, t), F32),
                        pltpu.VMEM((2 * N_HEADS, KEY_TILE, t), F32)],
        compiler_params=_cparams(("parallel", "parallel")),
        name="diff_attn",
    )(qT, k, vT, z, lam, subln_gain[:, None])


def _mla_mem_kernel(t, qT_ref, k_ref, vT_ref, z_ref,
                    mqT_ref, mem_ref, mz_ref, mg_ref, wk_ref, wvT_ref, kg_ref, gm_ref,
                    y_ref, ymem_ref, m_s, acc_s, s_s, mk_s, mvT_s, mm_s, macc_s):
    qi = pl.program_id(1)
    nt = (((1,), (1,)), ((), ()))
    m_len = mem_ref.shape[1]

    @pl.when(qi == 0)
    def _prep_memory_kv():
        mem = mem_ref[0]
        ms = jnp.mean(mem * mem, axis=-1, keepdims=True)
        mb = (mem * lax.rsqrt(ms + EPS) * mg_ref[...]).astype(BF16)
        k = jnp.dot(mb, wk_ref[...], preferred_element_type=F32)
        kn = k * _group_rsqrt(k, gm_ref[...], float(HEAD_DIM)) * kg_ref[...]
        vT = lax.dot_general(wvT_ref[...], mb, nt, preferred_element_type=F32)
        for h in range(N_HEADS):
            mk_s[h] = kn[:, h * HEAD_DIM:(h + 1) * HEAD_DIM].astype(BF16)
            mvT_s[h, 0:HEAD_DIM, :] = vT[h * HEAD_DIM:(h + 1) * HEAD_DIM].astype(BF16)
            mvT_s[h, HEAD_DIM:VT_ROWS, :] = _ones_rows(m_len)

    mqT = mqT_ref[0]
    _flash_reset(mm_s, macc_s)
    ss = [jnp.dot(mk_s[h], mqT[h * HEAD_DIM:(h + 1) * HEAD_DIM], preferred_element_type=F32)
          for h in range(N_HEADS)]
    for h in range(N_HEADS):
        _flash_step_t(ss[h], mvT_s[h], mm_s, macc_s, h)

    qT = qT_ref[0]
    _causal_flash_t(
        N_HEADS,
        lambda h, off: k_ref[0, h, pl.ds(off, KEY_TILE), :],
        lambda h: qT[h * LANES:(h + 1) * LANES],
        lambda h, off: vT_ref[0, h, :, pl.ds(off, KEY_TILE)],
        qi, t, KEY_TILE, m_s, acc_s, s_s)

    y_mem = jnp.concatenate([_flash_out(macc_s, h) for h in range(N_HEADS)], axis=0).T * _silu(mz_ref[0])
    ymem_ref[0] = y_mem.astype(BF16)
    y = jnp.concatenate([_flash_out(acc_s, h) for h in range(N_HEADS)], axis=0).T * _silu(z_ref[0])
    y_ref[0] = y.astype(BF16)


def _mla_mem_call(qT, k, vT, z, mem_qT, mem, mem_z, mem_gain, w_kv, mem_qk_gain, t=512):
    b, _, seq, _ = k.shape
    m_len = mem.shape[1]
    w_k = w_kv[:, :GROUP_WIDTH].astype(BF16)
    w_vT = w_kv[:, GROUP_WIDTH:].T.astype(BF16)
    tile = pl.BlockSpec((1, t, GROUP_WIDTH), lambda bi, qi: (bi, qi, 0))
    const = lambda shape: pl.BlockSpec(shape, lambda bi, qi: tuple(0 for _ in shape))
    out = jax.ShapeDtypeStruct((b, seq, GROUP_WIDTH), BF16)
    return pl.pallas_call(
        functools.partial(_mla_mem_kernel, t),
        grid=(b, seq // t),
        in_specs=[pl.BlockSpec((1, N_HEADS * LANES, t), lambda bi, qi: (bi, 0, qi)),
                  pl.BlockSpec((1, N_HEADS, seq, LANES), lambda bi, qi: (bi, 0, 0, 0)),
                  pl.BlockSpec((1, N_HEADS, VT_ROWS, seq), lambda bi, qi: (bi, 0, 0, 0)),
                  tile,
                  pl.BlockSpec((1, GROUP_WIDTH, t), lambda bi, qi: (bi, 0, qi)),
                  pl.BlockSpec((1, m_len, D_MODEL), lambda bi, qi: (bi, 0, 0)), tile,
                  const((1, D_MODEL)), const((D_MODEL, GROUP_WIDTH)), const((GROUP_WIDTH, D_MODEL)),
                  const((1, GROUP_WIDTH)), const((GROUP_WIDTH, GROUP_WIDTH))],
        out_specs=[tile, tile],
        out_shape=[out, out],
        scratch_shapes=[pltpu.VMEM((N_HEADS, t), F32),
                        pltpu.VMEM((N_HEADS, VT_ROWS, t), F32),
                        pltpu.VMEM((N_HEADS, KEY_TILE, t), F32),
                        pltpu.VMEM((N_HEADS, m_len, HEAD_DIM), BF16),
                        pltpu.VMEM((N_HEADS, VT_ROWS, m_len), BF16),
                        pltpu.VMEM((N_HEADS, t), F32),
                        pltpu.VMEM((N_HEADS, VT_ROWS, t), F32)],
        compiler_params=_cparams(("parallel", "arbitrary")),
        name="mla_mem_attn",
    )(qT, k, vT, z, mem_qT, mem, mem_z, mem_gain[None, :], w_k, w_vT,
      jnp.tile(mem_qk_gain[1], N_HEADS)[None, :], _group_ones(GROUP_WIDTH, HEAD_DIM))


def _nsa_kernel(seq, t,
                qT_ref, kcvc_ref, k2_ref, v2T_ref, gl_ref, z_ref,
                cg_ref, pelo_ref, pehi_ref, wlo_ref, whi_ref, cosc_ref, sinc_ref,
                y_ref, kc_s, vcT_s, bias_s, m_s, acc_s, s_s, pg_s):
    qi = pl.program_id(1)
    half = HEAD_DIM // 2
    n_chunk = seq // NSA_CMP_STRIDE
    n_cmp = n_chunk - 1
    n_blk = seq // NSA_SLC_BLOCK
    blk_per_tile = t // NSA_SLC_BLOCK
    win_tiles = NSA_WINDOW // t
    w4 = N_HEADS * t
    SLC, WIN = 0, 1

    @pl.when(qi == 0)
    def _prep():
        ch = kcvc_ref[0]
        a = _dot_x3(*_split_bf16(ch + pelo_ref[...]), wlo_ref[0], wlo_ref[1])
        bm = _dot_x3(*_split_bf16(ch + pehi_ref[...]), whi_ref[0], whi_ref[1])
        cmp = a + pltpu.roll(bm, n_chunk - 1, 0)
        kc = cmp[:, :HEAD_DIM]
        ms = jnp.mean(kc * kc, axis=-1, keepdims=True)
        kc = kc * lax.rsqrt(ms + EPS) * cg_ref[...]
        x1, x2 = kc[:, :half], kc[:, half:]
        c, s = cosc_ref[...], sinc_ref[...]
        kc_hi, kc_lo = _split_bf16(jnp.concatenate([x1 * c - x2 * s, x2 * c + x1 * s], axis=-1))
        kc_s[0] = kc_hi
        kc_s[1] = kc_lo
        vcT_s[...] = cmp.T[HEAD_DIM:].astype(BF16)
        pg_s[:, n_chunk:, :] = jnp.zeros((t // LANES, pg_s.shape[1] - n_chunk, LANES), F32)

    qs = qi * t
    qT = qT_ref[0]
    q4f = jnp.concatenate([qT[h * HEAD_DIM:(h + 1) * HEAD_DIM] for h in range(N_HEADS)], axis=1)
    q4 = q4f.astype(BF16)

    def scores(br, j):
        off = pl.multiple_of(j * t, t)
        return jnp.dot(k2_ref[0, br, pl.ds(off, t), :], q4, preferred_element_type=F32)

    s_s[WIN] = scores(WIN, jnp.maximum(qi - win_tiles, 0))
    qcol = lax.broadcasted_iota(jnp.int32, (1, w4), 1) & (t - 1)
    pos_c = qs + qcol

    q_lo = (q4f - q4.astype(F32)).astype(BF16)
    heads = [slice(h * t, (h + 1) * t) for h in range(N_HEADS)]
    scs = [_dot_x3(kc_s[0], kc_s[1], q4[:, c], q_lo[:, c]) for c in heads]
    n_idx = lax.broadcasted_iota(jnp.int32, (n_chunk, 1), 0)
    cvalid = (n_idx * NSA_CMP_STRIDE + (NSA_CMP_BLOCK - 1) <= pos_c[:, :t]) & (n_idx < n_cmp)
    o_cmp, pg = [], None
    for h in range(N_HEADS):
        sc = jnp.where(cvalid, scs[h], NEG_INF)
        e = jnp.exp2(sc - jnp.max(sc, axis=0, keepdims=True))
        p = jnp.where(cvalid, e / jnp.sum(e, axis=0, keepdims=True), 0.0)
        o_cmp.append(jnp.dot(vcT_s[...], p.astype(BF16), preferred_element_type=F32))
        pg = p if pg is None else pg + p

    for c0 in range(t // LANES):
        pg_s[c0, 0:n_chunk, :] = pg[:, c0 * LANES:(c0 + 1) * LANES]
    ratio = NSA_SLC_BLOCK // NSA_CMP_STRIDE
    coef = np.convolve(np.ones(ratio), np.ones(NSA_CMP_BLOCK // NSA_CMP_STRIDE))
    p_slc = None
    for i, c in enumerate(coef):
        term = float(c) * jnp.concatenate(
            [pg_s[c0, pl.ds(i, n_blk, stride=ratio), :] for c0 in range(t // LANES)], axis=1)
        p_slc = term if p_slc is None else p_slc + term
    blk = lax.broadcasted_iota(jnp.int32, (n_blk, 1), 0)
    cur = lax.shift_right_logical(pos_c[:, :t], NSA_SLC_BLOCK.bit_length() - 1)
    forced = (blk == 0) | ((blk <= cur) & (blk > cur - NSA_N_LOCAL))
    score = jnp.where(blk > cur, NEG_INF, jnp.where(forced, BIG, p_slc))
    sub = 8
    groups = [score[g * sub:(g + 1) * sub] for g in range(n_blk // sub)]
    cnts = [jnp.zeros((sub, t), F32) for _ in groups]
    for i in range(n_blk):
        ri = score[i:i + 1, :]
        for g, sg in enumerate(groups):
            if g < i // sub:
                beats = ri > sg
            elif g > i // sub:
                beats = ri >= sg
            else:
                beats = (ri > sg) | ((ri == sg) & (blk[g * sub:(g + 1) * sub] > i))
            cnts[g] = cnts[g] + jnp.where(beats, 1.0, 0.0)
    cnt = jnp.concatenate(cnts, axis=0)
    bias = jnp.where(cnt < float(NSA_N_SELECT), 0.0, NEG_INF)
    bias = jnp.concatenate([bias] * N_HEADS, axis=1)
    for r in range(n_blk // blk_per_tile):
        bias_s[r] = bias[r * blk_per_tile:(r + 1) * blk_per_tile, :]

    _flash_reset(m_s, acc_s)
    krow = lax.broadcasted_iota(jnp.int32, (t, 1), 0)
    causal = krow <= qcol
    beyond = krow > qcol

    def slc_scores(j):
        s = scores(SLC, j).reshape(blk_per_tile, NSA_SLC_BLOCK, w4) + bias_s[j][:, None, :]
        return s.reshape(t, w4)

    def flash(br, s, j):
        off = pl.multiple_of(j * t, t)
        _flash_step_t(s, v2T_ref[0, br, :, pl.ds(off, t)], m_s, acc_s, br)

    s_s[SLC] = slc_scores(0)

    def old_body(j, carry):
        s = s_s[SLC]
        s_s[SLC] = slc_scores(j + 1)
        flash(SLC, s, j)
        return carry

    lax.fori_loop(0, jnp.maximum(qi - win_tiles, 0), old_body, 0)

    def win_step(j, win_mask):
        s_slc = s_s[SLC]
        s_s[SLC] = slc_scores(j + 1)
        flash(SLC, s_slc, j)
        s_win = s_s[WIN]
        if win_mask is not None:
            s_win = jnp.where(win_mask, s_win, NEG_INF)
        s_s[WIN] = scores(WIN, j + 1)
        flash(WIN, s_win, j)

    @pl.when(qi >= win_tiles)
    def _oldest_window_tile():
        win_step(qi - win_tiles, beyond)

    def win_body(j, carry):
        win_step(j, None)
        return carry

    lax.fori_loop(jnp.maximum(qi - win_tiles + 1, 0), qi, win_body, 0)
    flash(SLC, jnp.where(causal, s_s[SLC], NEG_INF), qi)
    flash(WIN, jnp.where(causal, s_s[WIN], NEG_INF), qi)

    g = jax.nn.sigmoid(gl_ref[0]).T
    outs = []
    for h in range(N_HEADS):
        cols = slice(h * t, (h + 1) * t)
        o_s = _flash_out(acc_s, SLC, cols)
        o_w = _flash_out(acc_s, WIN, cols)
        outs.append(g[h:h + 1, :] * o_cmp[h] + g[N_HEADS + h:N_HEADS + h + 1, :] * o_s
                    + g[2 * N_HEADS + h:2 * N_HEADS + h + 1, :] * o_w)
    y = jnp.concatenate(outs, axis=0).T * _silu(z_ref[0])
    y_ref[0] = y.astype(BF16)


def _nsa_call(qT, kcvc, k2, v2T, gl, z, qk_gain, cmp_pe, w_cmp, t=256):
    b, _, seq, _ = k2.shape
    n_chunk = seq // NSA_CMP_STRIDE
    n_blk = seq // NSA_SLC_BLOCK
    half_blk = NSA_CMP_BLOCK // 2
    cw = half_blk * 2 * HEAD_DIM
    assert t % NSA_SLC_BLOCK == 0 and NSA_WINDOW % t == 0 and t & (t - 1) == 0

    wk = w_cmp[0].reshape(NSA_CMP_BLOCK, HEAD_DIM, HEAD_DIM)
    wv = w_cmp[1].reshape(NSA_CMP_BLOCK, HEAD_DIM, HEAD_DIM)
    zero = jnp.zeros_like(wk)
    w_all = jnp.concatenate([jnp.concatenate([wk, zero], axis=2),
                             jnp.concatenate([zero, wv], axis=2)], axis=1)
    hi_lo = lambda w: jnp.stack([w.astype(BF16), (w - w.astype(BF16).astype(F32)).astype(BF16)])
    w_lo = hi_lo(w_all[:half_blk].reshape(cw, 2 * HEAD_DIM))
    w_hi = hi_lo(w_all[half_blk:].reshape(cw, 2 * HEAD_DIM))
    pe_all = jnp.concatenate([cmp_pe[0], cmp_pe[1]], axis=1)
    pe_lo = pe_all[:half_blk].reshape(1, cw)
    pe_hi = pe_all[half_blk:].reshape(1, cw)

    cmp_end = jnp.arange(n_chunk, dtype=jnp.int32) * NSA_CMP_STRIDE + (NSA_CMP_BLOCK - 1)
    cos_c, sin_c = _rope_cos_sin(cmp_end, HEAD_DIM)

    tile =pl.BlockSpec((1, t, GROUP_WIDTH), lambda bi, qi: (bi, qi, 0))
    const = lambda shape: pl.BlockSpec(shape, lambda bi, qi: tuple(0 for _ in shape))
    w4 = N_HEADS * t
    return pl.pallas_call(
        functools.partial(_nsa_kernel, seq, t),
        grid=(b, seq // t),
        in_specs=[pl.BlockSpec((1, GROUP_WIDTH, t), lambda bi, qi: (bi, 0, qi)),
                  pl.BlockSpec((1, n_chunk, cw), lambda bi, qi: (bi, 0, 0)),
                  pl.BlockSpec((1, 2, seq, HEAD_DIM), lambda bi, qi: (bi, 0, 0, 0)),
                  pl.BlockSpec((1, 2, VT_ROWS, seq), lambda bi, qi: (bi, 0, 0, 0)),
                  pl.BlockSpec((1, t, LANES), lambda bi, qi: (bi, qi, 0)),
                  tile,
                  const((1, HEAD_DIM)),
                  const((1, cw)), const((1, cw)), const((2, cw, 2 * HEAD_DIM)), const((2, cw, 2 * HEAD_DIM)),
                  const((n_chunk, HEAD_DIM // 2)), const((n_chunk, HEAD_DIM // 2))],
        out_specs=tile,
        out_shape=jax.ShapeDtypeStruct((b, seq, GROUP_WIDTH), BF16),
        scratch_shapes=[pltpu.VMEM((2, n_chunk, HEAD_DIM), BF16),
                        pltpu.VMEM((HEAD_DIM, n_chunk), BF16),
                        pltpu.VMEM((seq // t, t // NSA_SLC_BLOCK, w4), F32),
                        pltpu.VMEM((2, w4), F32),
                        pltpu.VMEM((2, VT_ROWS, w4), F32),
                        pltpu.VMEM((2, t, w4), F32),
                        pltpu.VMEM((t // LANES, n_chunk + 8, LANES), F32)],
        compiler_params=_cparams(("parallel", "arbitrary")),
        name="nsa_attn",
    )(qT, kcvc, k2, v2T, gl, z,
      qk_gain[1][None, :],
      pe_lo, pe_hi, w_lo, w_hi, cos_c, sin_c)


def _rope_cos_sin(pos, dim):
    half = dim // 2
    inv_freq = ROPE_THETA ** (-jnp.arange(half, dtype=F32) / half)
    ang = pos.astype(F32)[:, None] * inv_freq[None, :]
    return jnp.cos(ang), jnp.sin(ang)


def _layer(x, mem, layer_idx, tabs_t, norm_gain, w_in, w_out, nsa_qk_gain, nsa_cmp_pe, nsa_w_cmp,
           diff_qk_gain, diff_lambda, diff_subln_gain, mla_cq_gain, mla_ckv_gain, mla_w_uq, mla_w_ukv,
           mla_qk_gain, mem_norm_gain, mem_w_kv, mem_qk_gain):
    b, seq, d = x.shape
    u = _in_proj(x, norm_gain, w_in, nsa_qk_gain, diff_qk_gain, mem_qk_gain, mla_cq_gain, mla_ckv_gain,
                 mla_w_uq, mla_w_ukv, mla_qk_gain, tabs_t)

    y_nsa = _nsa_call(u["nsa_qT"], u["nsa_kcvc"], u["nsa_k2"], u["nsa_v2T"], u["nsa_gl"], u["nsa_z"],
                      nsa_qk_gain, nsa_cmp_pe, nsa_w_cmp)
    lambda_init = 0.8 - 0.6 * math.exp(-0.3 * layer_idx)
    y_diff = _diff_call(u["diff_qT"], u["diff_k"], u["diff_vT"], u["diff_z"], diff_lambda, diff_subln_gain,
                        lambda_init)
    y_mla, y_mem = _mla_mem_call(u["mla_qT"], u["mla_k"], u["mla_vT"], u["mla_z"],
                                 u["mem_qT"], mem, u["mem_z"], mem_norm_gain, mem_w_kv, mem_qk_gain)

    ys = [y.reshape(b * seq, GROUP_WIDTH) for y in (y_nsa, y_diff, y_mla, y_mem)]
    return _out_proj(x.reshape(b * seq, d), ys, w_out.astype(BF16)).reshape(b, seq, d)


def kernel(x, mem, norm_gain, w_in, w_out, nsa_qk_gain, nsa_cmp_pe, nsa_w_cmp, diff_qk_gain, diff_lambda,
           diff_subln_gain, mla_cq_gain, mla_ckv_gain, mla_w_uq, mla_w_ukv, mla_qk_gain, mem_norm_gain,
           mem_w_kv, mem_qk_gain):
    seq = x.shape[1]
    pos = jnp.arange(seq, dtype=jnp.int32)
    cos32, sin32 = _rope_cos_sin(pos, DIFF_D)
    cos64, sin64 = _rope_cos_sin(pos, HEAD_DIM)
    tabs_t = (cos32.T, sin32.T, cos64.T, sin64.T)
    w_row, w_col = _split_w_in(w_in)
    for l in range(DEPTH):
        x = _layer(x, mem, l, tabs_t, norm_gain[l], (w_row[l], w_col[l]), w_out[l], nsa_qk_gain[l], nsa_cmp_pe[l],
                   nsa_w_cmp[l], diff_qk_gain[l], diff_lambda[l], diff_subln_gain[l], mla_cq_gain[l],
                   mla_ckv_gain[l], mla_w_uq[l], mla_w_ukv[l], mla_qk_gain[l], mem_norm_gain[l], mem_w_kv[l],
                   mem_qk_gain[l])
    return x
```

```python
import functools
import math

import numpy as np
import jax
import jax.numpy as jnp
from jax import lax
from jax.experimental import pallas as pl
from jax.experimental.pallas import tpu as pltpu

F32 = jnp.float32
BF16 = jnp.bfloat16

D_MODEL = 1024
DEPTH = 2
N_HEADS = 4
HEAD_DIM = 64
GROUP_WIDTH = N_HEADS * HEAD_DIM
ROPE_THETA = 10000.0
EPS = 1e-6
NEG_INF = -1e30
BIG = 1e30
LOG2E = 1.4426950408889634

NSA_CMP_BLOCK = 32
NSA_CMP_STRIDE = 16
NSA_SLC_BLOCK = 64
NSA_N_SELECT = 16
NSA_N_LOCAL = 2
NSA_WINDOW = 512
DIFF_D = HEAD_DIM // 2
MLA_Q_RANK = 256
MLA_KV_RANK = 128
MLA_NOPE = 64
MLA_ROPE = 32
MLA_QK = MLA_NOPE + MLA_ROPE

VMEM_LIMIT_BYTES = 48 * 1024 * 1024
LANES = 128
KEY_TILE = 256
VT_ROWS = 80


def _cparams(sem):
    return pltpu.CompilerParams(dimension_semantics=sem, vmem_limit_bytes=VMEM_LIMIT_BYTES)


def _group_ones(width, group):
    g = np.arange(width) // group
    return jnp.asarray(g[:, None] == g[None, :], dtype=BF16)


def _group_rsqrt(x, gmat, denom):
    sq = x * x
    hi = sq.astype(BF16)
    lo = (sq - hi.astype(F32)).astype(BF16)
    ss = jnp.dot(hi, gmat, preferred_element_type=F32) + jnp.dot(lo, gmat, preferred_element_type=F32)
    return lax.rsqrt(ss / denom + EPS)


def _norm_rope_t(u_t, groups, dim, gain, cos=None, sin=None):
    x = u_t.reshape(groups, dim, u_t.shape[-1])
    ms = jnp.mean(x * x, axis=1, keepdims=True)
    x = x * lax.rsqrt(ms + EPS) * gain
    if cos is not None:
        half = dim // 2
        x1, x2 = x[:, :half], x[:, half:]
        x = jnp.concatenate([x1 * cos - x2 * sin, x2 * cos + x1 * sin], axis=1)
    return x.reshape(groups * dim, u_t.shape[-1])


def _silu(z):
    z = z.astype(F32)
    return z * jax.nn.sigmoid(z)


def _ones_rows(n):
    r = lax.broadcasted_iota(jnp.int32, (VT_ROWS - HEAD_DIM, n), 0)
    return jnp.where(r == 0, 1.0, 0.0).astype(BF16)


def _split_bf16(x):
    hi = x.astype(BF16)
    return hi, (x - hi.astype(F32)).astype(BF16)


def _dot_x3(a_hi, a_lo, b_hi, b_lo):
    dot = lambda a, b: jnp.dot(a, b, preferred_element_type=F32)
    return dot(a_hi, b_hi) + dot(a_hi, b_lo) + dot(a_lo, b_hi)


def _flash_step_t(s, v_t, m_ref, acc_ref, i, cols=slice(None)):
    m_old = m_ref[i:i + 1, cols]
    m_new = jnp.maximum(m_old, jnp.max(s, axis=0, keepdims=True))
    alpha = jnp.exp2(m_old - m_new)
    p = jnp.exp2(s - m_new).astype(BF16)
    acc_ref[i, :, cols] = alpha * acc_ref[i, :, cols] + jnp.dot(v_t, p, preferred_element_type=F32)
    m_ref[i:i + 1, cols] = m_new


def _flash_reset(m_ref, acc_ref):
    m_ref[...] = jnp.full(m_ref.shape, NEG_INF, F32)
    acc_ref[...] = jnp.zeros(acc_ref.shape, F32)


def _flash_out(acc_ref, i, cols=slice(None)):
    return acc_ref[i, 0:HEAD_DIM, cols] / acc_ref[i, HEAD_DIM:HEAD_DIM + 1, cols]


def _causal_flash_t(n, k_tile, q_t, v_tile, qi, tq, tk, m_ref, acc_ref, s_ref):
    _flash_reset(m_ref, acc_ref)
    n_diag = tq // tk

    all_q = slice(0, tq)

    def scores(i, off, cols):
        return jnp.dot(k_tile(i, off), q_t(i)[:, cols], preferred_element_type=F32)

    def step(off, cols, off_next, cols_next, mask):
        for i in range(n):
            s = s_ref[i, :, cols]
            if off_next is not None:
                s_ref[i, :, cols_next] = scores(i, off_next, cols_next)
            if mask is not None:
                s = jnp.where(mask, s, NEG_INF)
            _flash_step_t(s, v_tile(i, off), m_ref, acc_ref, i, cols)

    for i in range(n):
        s_ref[i] = scores(i, 0, all_q)

    def body(j, carry):
        step(pl.multiple_of(j * tk, tk), all_q, pl.multiple_of(j * tk + tk, tk), all_q, None)
        return carry

    lax.fori_loop(0, qi * n_diag, body, 0)
    krow = lax.broadcasted_iota(jnp.int32, (tk, tq), 0)
    qcol = lax.broadcasted_iota(jnp.int32, (tk, tq), 1)
    causal = krow <= qcol
    diag_cols = [slice(d * tk, tq) for d in range(n_diag)]
    for d in range(n_diag):
        off = pl.multiple_of(qi * tq + d * tk, tk)
        last = d + 1 == n_diag
        off_next = None if last else pl.multiple_of(qi * tq + (d + 1) * tk, tk)
        step(off, diag_cols[d], off_next, None if last else diag_cols[d + 1], causal[:, 0:tq - d * tk])


_ROW_SEGS = (("nsa_kcvc", 128), ("nsa_gl", 128), ("nsa_z", 256), ("diff_z", 256), ("mla_z", 256),
             ("mem_z", 256))
_COL_SEGS = (("nsa_q", 256), ("diff_q", 256), ("mem_q", 256), ("diff_k", 256), ("nsa_k2", 128),
             ("diff_v", 256), ("nsa_v2", 128), ("mla_cq", 256), ("mla_ckv", 128), ("mla_kr", 32))


def _seg_offsets(segs):
    out, off = {}, 0
    for name, w in segs:
        out[name] = (off, w)
        off += w
    return out, off


_ROW_OFF, _ROW_W = _seg_offsets(_ROW_SEGS)
_COL_OFF, _COL_W = _seg_offsets(_COL_SEGS)


_ROW_SRC = ((256, 384),
            (640, 652), (None, 116),
            (652, 908), (1676, 1932), (2348, 2604), (2860, 3116))
_COL_SRC = ((0, 256), (908, 1164), (2604, 2860),
            (1164, 1420),
            (384, 448), (512, 576),
            (1420, 1676),
            (448, 512), (576, 640),
            (1932, 2188), (2188, 2316), (2316, 2348))


def _w_split_kernel(wt_ref, wrow_o, wcol_o):
    _, layers, ct = wt_ref.shape

    for l in range(layers):
        def gather(src):
            return jnp.concatenate(
                [jnp.zeros((b, ct), F32) if a is None else wt_ref[a:b, l, :] for a, b in src], axis=0)

        wcol_o[l] = gather(_COL_SRC).astype(BF16)
        wrow_o[l] = gather(_ROW_SRC).T.astype(BF16)


def _split_w_in(w_in, ct=128):
    layers, d, d_in = w_in.shape
    w_t = jnp.transpose(w_in, (2, 0, 1))
    return pl.pallas_call(
        _w_split_kernel,
        grid=(d // ct,),
        in_specs=[pl.BlockSpec((d_in, layers, ct), lambda c: (0, 0, c))],
        out_specs=[pl.BlockSpec((layers, ct, _ROW_W), lambda c: (0, c, 0)),
                   pl.BlockSpec((layers, _COL_W, ct), lambda c: (0, 0, c))],
        out_shape=[jax.ShapeDtypeStruct((layers, d, _ROW_W), BF16),
                   jax.ShapeDtypeStruct((layers, _COL_W, d), BF16)],
        compiler_params=_cparams(("parallel",)),
        name="w_in_split",
    )(w_t)


def _mla_head_norm_t(x_t, gain):
    ms = jnp.sum(x_t * x_t, axis=0, keepdims=True) / float(MLA_QK)
    return x_t * lax.rsqrt(ms + EPS) * gain


def _rope_t(x_t, cos, sin):
    half = x_t.shape[0] // 2
    x1, x2 = x_t[:half], x_t[half:]
    return jnp.concatenate([x1 * cos - x2 * sin, x2 * cos + x1 * sin], axis=0)


def _in_proj_kernel(x_ref, g_ref, wrow_ref, wcol_ref, nqg_ref, dqg_ref, dkg_ref, k2g_ref, mqg_ref,
                    cqg_ref, ckvg_ref, wuq_ref, wuk_ref, wuv_ref, lqg_ref, lkg_ref,
                    c32_ref, s32_ref, c64_ref, s64_ref,
                    kcvc_o, gl_o, nz_o, dz_o, mz_o, ez_o, dk_o, k2_o,
                    nq_o, dq_o, mq_o, dv_o, v2_o, lq_o, lk_o, lv_o, kcvc_s):
    x = x_ref[...]
    tm = x.shape[0]
    ms = jnp.mean(x * x, axis=-1, keepdims=True)
    h = x * lax.rsqrt(ms + EPS) * g_ref[...]
    hb = h.astype(BF16)
    h_t = h.T.astype(BF16)

    def row(name):
        off, w = _ROW_OFF[name]
        return jnp.dot(hb, wrow_ref[:, off:off + w], preferred_element_type=F32)

    u_t = jnp.dot(wcol_ref[...], h_t, preferred_element_type=F32)

    def col(name):
        off, w = _COL_OFF[name]
        return u_t[off:off + w]

    c32, s32, c64, s64 = c32_ref[...], s32_ref[...], c64_ref[...], s64_ref[...]
    g3 = lambda ref, groups: ref[...].reshape(groups, -1, 1)

    rows_out = {name: row(name) for name, _ in _ROW_SEGS}

    kcvc_s[...] = rows_out["nsa_kcvc"]
    n_chunk_rows = tm // NSA_CMP_STRIDE
    for tok in range(NSA_CMP_STRIDE):
        kcvc_o[0, :, tok * LANES:(tok + 1) * LANES] = kcvc_s[pl.ds(tok, n_chunk_rows, stride=NSA_CMP_STRIDE), :]
    gl_o[...] = rows_out["nsa_gl"]
    nz_o[...] = rows_out["nsa_z"].astype(BF16)
    dz_o[...] = rows_out["diff_z"].astype(BF16)
    mz_o[...] = rows_out["mla_z"].astype(BF16)
    ez_o[...] = rows_out["mem_z"].astype(BF16)

    nq_o[0] = _norm_rope_t(col("nsa_q"), N_HEADS, HEAD_DIM, g3(nqg_ref, 1), c64, s64) * (HEAD_DIM ** -0.5 * LOG2E)
    dq = _norm_rope_t(col("diff_q"), 2 * N_HEADS, DIFF_D, g3(dqg_ref, 1), c32, s32) * (DIFF_D ** -0.5 * LOG2E)
    dq_o[0] = dq.astype(BF16)
    mq = _norm_rope_t(col("mem_q"), N_HEADS, HEAD_DIM, g3(mqg_ref, 1)) * (HEAD_DIM ** -0.5 * LOG2E)
    mq_o[0] = mq.astype(BF16)
    dk = _norm_rope_t(col("diff_k"), 2 * N_HEADS, DIFF_D, g3(dkg_ref, 1), c32, s32).T
    for mp in range(2 * N_HEADS):
        dk_o[0, mp] = dk[:, mp * DIFF_D:(mp + 1) * DIFF_D].astype(BF16)
    k2 = _norm_rope_t(col("nsa_k2"), 2, HEAD_DIM, g3(k2g_ref, 2), c64, s64).T
    for br in range(2):
        k2_o[0, br] = k2[:, br * HEAD_DIM:(br + 1) * HEAD_DIM].astype(BF16)
    ones = _ones_rows(tm)
    dv = col("diff_v")
    for hd in range(N_HEADS):
        dv_o[0, hd, 0:HEAD_DIM, :] = dv[hd * HEAD_DIM:(hd + 1) * HEAD_DIM].astype(BF16)
        dv_o[0, hd, HEAD_DIM:VT_ROWS, :] = ones
    v2 = col("nsa_v2")
    for br in range(2):
        v2_o[0, br, 0:HEAD_DIM, :] = v2[br * HEAD_DIM:(br + 1) * HEAD_DIM].astype(BF16)
        v2_o[0, br, HEAD_DIM:VT_ROWS, :] = ones

    def latent(name, gain_ref):
        c = col(name)
        return (c * lax.rsqrt(jnp.mean(c * c, axis=0, keepdims=True) + EPS) * gain_ref[...]).astype(BF16)

    qa = jnp.dot(wuq_ref[...], latent("mla_cq", cqg_ref), preferred_element_type=F32)
    ckv = latent("mla_ckv", ckvg_ref)
    kn = jnp.dot(wuk_ref[...], ckv, preferred_element_type=F32)
    lv = jnp.dot(wuv_ref[...], ckv, preferred_element_type=F32)
    rope_rows = slice(MLA_NOPE, MLA_QK)
    lqg = lqg_ref[...] * (MLA_QK ** -0.5 * LOG2E)
    kr = _rope_t(col("mla_kr"), c32, s32)
    zpad = jnp.zeros((LANES - MLA_QK, tm), F32)
    for hd in range(N_HEADS):
        q_h = qa[hd * LANES:(hd + 1) * LANES]
        q_h = jnp.concatenate([q_h[:MLA_NOPE], _rope_t(q_h[rope_rows], c32, s32), q_h[MLA_QK:]], axis=0)
        lq_o[0, hd * LANES:(hd + 1) * LANES, :] = _mla_head_norm_t(q_h, lqg).astype(BF16)
        k_h = jnp.concatenate([kn[hd * LANES:hd * LANES + MLA_NOPE], kr, zpad], axis=0)
        lk_o[0, hd] = _mla_head_norm_t(k_h, lkg_ref[...]).T.astype(BF16)
        lv_o[0, hd, 0:HEAD_DIM, :] = lv[hd * HEAD_DIM:(hd + 1) * HEAD_DIM].astype(BF16)
        lv_o[0, hd, HEAD_DIM:VT_ROWS, :] = ones


def _in_proj(x, gain, w_in, nsa_qk_gain, diff_qk_gain, mem_qk_gain, mla_cq_gain, mla_ckv_gain, mla_w_uq,
             mla_w_ukv, mla_qk_gain, tabs_t, tm=512):
    b, seq, d = x.shape
    n = b * seq
    nb = seq // tm
    w_row, w_col = w_in
    c32, s32, c64, s64 = tabs_t
    npad = LANES - MLA_QK
    uq = jnp.pad(mla_w_uq.reshape(MLA_Q_RANK, N_HEADS, MLA_QK), ((0, 0), (0, 0), (0, npad)))
    w_uq_t = uq.reshape(MLA_Q_RANK, N_HEADS * LANES).T.astype(BF16)
    ukv = mla_w_ukv.reshape(MLA_KV_RANK, N_HEADS, MLA_NOPE + HEAD_DIM)
    uk = jnp.pad(ukv[:, :, :MLA_NOPE], ((0, 0), (0, 0), (0, LANES - MLA_NOPE)))
    w_uk_t = uk.reshape(MLA_KV_RANK, N_HEADS * LANES).T.astype(BF16)
    w_uv_t = ukv[:, :, MLA_NOPE:].reshape(MLA_KV_RANK, GROUP_WIDTH).T.astype(BF16)
    pad_gain = lambda g: jnp.pad(g, (0, npad))[:, None]

    rowspec = lambda w: pl.BlockSpec((tm, w), lambda i: (i, 0))
    colspec = lambda r: pl.BlockSpec((1, r, tm), lambda i: (i // nb, 0, i % nb))
    vtspec = lambda c: pl.BlockSpec((1, c, VT_ROWS, tm), lambda i: (i // nb, 0, 0, i % nb))
    const = lambda shape: pl.BlockSpec(shape, lambda i: tuple(0 for _ in shape))
    tabspec = lambda r: pl.BlockSpec((r, tm), lambda i: (0, i % nb))
    row_out = lambda w, dt: jax.ShapeDtypeStruct((n, w), dt)
    col_out = lambda r, dt: jax.ShapeDtypeStruct((b, r, seq), dt)
    vt_out = lambda c: jax.ShapeDtypeStruct((b, c, VT_ROWS, seq), BF16)
    slabspec = lambda c, w: pl.BlockSpec((1, c, tm, w), lambda i: (i // nb, 0, i % nb, 0))
    slab_out = lambda c, w: jax.ShapeDtypeStruct((b, c, seq, w), BF16)
    outs = pl.pallas_call(
        _in_proj_kernel,
        grid=(n // tm,),
        in_specs=[rowspec(d), const((1, d)), const((d, _ROW_W)), const((_COL_W, d)),
                  const((HEAD_DIM, 1)), const((DIFF_D, 1)), const((DIFF_D, 1)), const((2 * HEAD_DIM, 1)),
                  const((HEAD_DIM, 1)),
                  const((MLA_Q_RANK, 1)), const((MLA_KV_RANK, 1)),
                  const((N_HEADS * LANES, MLA_Q_RANK)), const((N_HEADS * LANES, MLA_KV_RANK)),
                  const((GROUP_WIDTH, MLA_KV_RANK)), const((LANES, 1)), const((LANES, 1)),
                  tabspec(DIFF_D // 2), tabspec(DIFF_D // 2), tabspec(HEAD_DIM // 2), tabspec(HEAD_DIM // 2)],
        out_specs=[pl.BlockSpec((1, tm // NSA_CMP_STRIDE, NSA_CMP_STRIDE * LANES), lambda i: (i // nb, i % nb, 0)),
                   rowspec(128), rowspec(256), rowspec(256), rowspec(256), rowspec(256),
                   slabspec(2 * N_HEADS, DIFF_D), slabspec(2, HEAD_DIM),
                   colspec(256), colspec(256), colspec(256), vtspec(N_HEADS), vtspec(2),
                   colspec(N_HEADS * LANES), slabspec(N_HEADS, LANES), vtspec(N_HEADS)],
        out_shape=[jax.ShapeDtypeStruct((b, seq // NSA_CMP_STRIDE, NSA_CMP_STRIDE * LANES), F32),
                   row_out(128, F32), row_out(256, BF16), row_out(256, BF16),
                   row_out(256, BF16), row_out(256, BF16), slab_out(2 * N_HEADS, DIFF_D), slab_out(2, HEAD_DIM),
                   col_out(256, F32), col_out(256, BF16), col_out(256, BF16), vt_out(N_HEADS), vt_out(2),
                   col_out(N_HEADS * LANES, BF16), slab_out(N_HEADS, LANES), vt_out(N_HEADS)],
        scratch_shapes=[pltpu.VMEM((tm, LANES), F32)],
        compiler_params=_cparams(("parallel",)),
        name="in_proj",
    )(x.reshape(n, d), gain[None, :], w_row, w_col,
      nsa_qk_gain[0][:, None], diff_qk_gain[0][:, None], diff_qk_gain[1][:, None],
      jnp.concatenate([nsa_qk_gain[2], nsa_qk_gain[3]])[:, None], mem_qk_gain[0][:, None],
      mla_cq_gain[:, None], mla_ckv_gain[:, None], w_uq_t, w_uk_t, w_uv_t,
      pad_gain(mla_qk_gain[0]), pad_gain(mla_qk_gain[1]),
      c32, s32, c64, s64)
    names = ("nsa_kcvc", "nsa_gl", "nsa_z", "diff_z", "mla_z", "mem_z", "diff_k", "nsa_k2",
             "nsa_qT", "diff_qT", "mem_qT", "diff_vT", "nsa_v2T", "mla_qT", "mla_k", "mla_vT")
    u = dict(zip(names, outs))
    for name in names[1:6]:
        u[name] = u[name].reshape(b, seq, -1)
    return u


def _out_proj_kernel(x_ref, y0_ref, y1_ref, y2_ref, y3_ref, w_ref, o_ref):
    acc = x_ref[...]
    for g, y_ref in enumerate((y0_ref, y1_ref, y2_ref, y3_ref)):
        acc = acc + jnp.dot(y_ref[...], w_ref[g * GROUP_WIDTH:(g + 1) * GROUP_WIDTH, :],
                            preferred_element_type=F32)
    o_ref[...] = acc


def _out_proj(x2, ys, w_out_b, tm=1024):
    n = x2.shape[0]
    yspec = pl.BlockSpec((tm, GROUP_WIDTH), lambda i: (i, 0))
    return pl.pallas_call(
        _out_proj_kernel,
        grid=(n // tm,),
        in_specs=[pl.BlockSpec((tm, D_MODEL), lambda i: (i, 0)), yspec, yspec, yspec, yspec,
                  pl.BlockSpec((D_MODEL, D_MODEL), lambda i: (0, 0))],
        out_specs=pl.BlockSpec((tm, D_MODEL), lambda i: (i, 0)),
        out_shape=jax.ShapeDtypeStruct((n, D_MODEL), F32),
        compiler_params=_cparams(("parallel",)),
        name="out_proj",
    )(x2, *ys, w_out_b)


def _diff_kernel(lambda_init, t, qT_ref, k_ref, vT_ref, z_ref, lam_ref, sg_ref, y_ref, m_s, acc_s, s_s):
    qi = pl.program_id(1)
    n_maps = 2 * N_HEADS
    qT = qT_ref[0]
    _causal_flash_t(
        n_maps,
        lambda i, off: k_ref[0, i, pl.ds(off, KEY_TILE), :],
        lambda i: qT[i * DIFF_D:(i + 1) * DIFF_D],
        lambda i, off: vT_ref[0, i // 2, :, pl.ds(off, KEY_TILE)],
        qi, t, KEY_TILE, m_s, acc_s, s_s)

    lam = lam_ref[...]
    lmbda = (jnp.exp(jnp.sum(lam[0:1] * lam[1:2], axis=-1, keepdims=True))
             - jnp.exp(jnp.sum(lam[2:3] * lam[3:4], axis=-1, keepdims=True)) + lambda_init)
    outs = []
    for h in range(N_HEADS):
        d = _flash_out(acc_s, 2 * h) - lmbda * _flash_out(acc_s, 2 * h + 1)
        ms = jnp.mean(d * d, axis=0, keepdims=True)
        outs.append(d * lax.rsqrt(ms + EPS) * sg_ref[...] * (1.0 - lambda_init))
    y = jnp.concatenate(outs, axis=0).T * _silu(z_ref[0])
    y_ref[0] = y.astype(BF16)


def _diff_call(qT, k, vT, z, lam, subln_gain, lambda_init, t=512):
    b, _, seq, _ = k.shape
    const = lambda shape: pl.BlockSpec(shape, lambda bi, qi: tuple(0 for _ in shape))
    tile = pl.BlockSpec((1, t, GROUP_WIDTH), lambda bi, qi: (bi, qi, 0))
    return pl.pallas_call(
        functools.partial(_diff_kernel, lambda_init, t),
        grid=(b, seq // t),
        in_specs=[pl.BlockSpec((1, GROUP_WIDTH, t), lambda bi, qi: (bi, 0, qi)),
                  pl.BlockSpec((1, 2 * N_HEADS, seq, DIFF_D), lambda bi, qi: (bi, 0, 0, 0)),
                  pl.BlockSpec((1, N_HEADS, VT_ROWS, seq), lambda bi, qi: (bi, 0, 0, 0)),
                  tile, const((4, DIFF_D)), const((HEAD_DIM, 1))],
        out_specs=tile,
        out_shape=jax.ShapeDtypeStruct((b, seq, GROUP_WIDTH), BF16),
        scratch_shapes=[pltpu.VMEM((2 * N_HEADS, t), F32),
                        pltpu.VMEM((2 * N_HEADS, VT_ROWS, t), F32),
                        pltpu.VMEM((2 * N_HEADS, KEY_TILE, t), F32)],
        compiler_params=_cparams(("parallel", "parallel")),
        name="diff_attn",
    )(qT, k, vT, z, lam, subln_gain[:, None])


def _mla_mem_kernel(t, qT_ref, k_ref, vT_ref, z_ref,
                    mqT_ref, mem_ref, mz_ref, mg_ref, wk_ref, wvT_ref, kg_ref, gm_ref,
                    y_ref, ymem_ref, m_s, acc_s, s_s, mk_s, mvT_s, mm_s, macc_s):
    qi = pl.program_id(1)
    nt = (((1,), (1,)), ((), ()))
    m_len = mem_ref.shape[1]

    @pl.when(qi == 0)
    def _prep_memory_kv():
        mem = mem_ref[0]
        ms = jnp.mean(mem * mem, axis=-1, keepdims=True)
        mb = (mem * lax.rsqrt(ms + EPS) * mg_ref[...]).astype(BF16)
        k = jnp.dot(mb, wk_ref[...], preferred_element_type=F32)
        kn = k * _group_rsqrt(k, gm_ref[...], float(HEAD_DIM)) * kg_ref[...]
        vT = lax.dot_general(wvT_ref[...], mb, nt, preferred_element_type=F32)
        for h in range(N_HEADS):
            mk_s[h] = kn[:, h * HEAD_DIM:(h + 1) * HEAD_DIM].astype(BF16)
            mvT_s[h, 0:HEAD_DIM, :] = vT[h * HEAD_DIM:(h + 1) * HEAD_DIM].astype(BF16)
            mvT_s[h, HEAD_DIM:VT_ROWS, :] = _ones_rows(m_len)

    mqT = mqT_ref[0]
    _flash_reset(mm_s, macc_s)
    ss = [jnp.dot(mk_s[h], mqT[h * HEAD_DIM:(h + 1) * HEAD_DIM], preferred_element_type=F32)
          for h in range(N_HEADS)]
    for h in range(N_HEADS):
        _flash_step_t(ss[h], mvT_s[h], mm_s, macc_s, h)

    qT = qT_ref[0]
    _causal_flash_t(
        N_HEADS,
        lambda h, off: k_ref[0, h, pl.ds(off, KEY_TILE), :],
        lambda h: qT[h * LANES:(h + 1) * LANES],
        lambda h, off: vT_ref[0, h, :, pl.ds(off, KEY_TILE)],
        qi, t, KEY_TILE, m_s, acc_s, s_s)

    y_mem = jnp.concatenate([_flash_out(macc_s, h) for h in range(N_HEADS)], axis=0).T * _silu(mz_ref[0])
    ymem_ref[0] = y_mem.astype(BF16)
    y = jnp.concatenate([_flash_out(acc_s, h) for h in range(N_HEADS)], axis=0).T * _silu(z_ref[0])
    y_ref[0] = y.astype(BF16)


def _mla_mem_call(qT, k, vT, z, mem_qT, mem, mem_z, mem_gain, w_kv, mem_qk_gain, t=512):
    b, _, seq, _ = k.shape
    m_len = mem.shape[1]
    w_k = w_kv[:, :GROUP_WIDTH].astype(BF16)
    w_vT = w_kv[:, GROUP_WIDTH:].T.astype(BF16)
    tile = pl.BlockSpec((1, t, GROUP_WIDTH), lambda bi, qi: (bi, qi, 0))
    const = lambda shape: pl.BlockSpec(shape, lambda bi, qi: tuple(0 for _ in shape))
    out = jax.ShapeDtypeStruct((b, seq, GROUP_WIDTH), BF16)
    return pl.pallas_call(
        functools.partial(_mla_mem_kernel, t),
        grid=(b, seq // t),
        in_specs=[pl.BlockSpec((1, N_HEADS * LANES, t), lambda bi, qi: (bi, 0, qi)),
                  pl.BlockSpec((1, N_HEADS, seq, LANES), lambda bi, qi: (bi, 0, 0, 0)),
                  pl.BlockSpec((1, N_HEADS, VT_ROWS, seq), lambda bi, qi: (bi, 0, 0, 0)),
                  tile,
                  pl.BlockSpec((1, GROUP_WIDTH, t), lambda bi, qi: (bi, 0, qi)),
                  pl.BlockSpec((1, m_len, D_MODEL), lambda bi, qi: (bi, 0, 0)), tile,
                  const((1, D_MODEL)), const((D_MODEL, GROUP_WIDTH)), const((GROUP_WIDTH, D_MODEL)),
                  const((1, GROUP_WIDTH)), const((GROUP_WIDTH, GROUP_WIDTH))],
        out_specs=[tile, tile],
        out_shape=[out, out],
        scratch_shapes=[pltpu.VMEM((N_HEADS, t), F32),
                        pltpu.VMEM((N_HEADS, VT_ROWS, t), F32),
                        pltpu.VMEM((N_HEADS, KEY_TILE, t), F32),
                        pltpu.VMEM((N_HEADS, m_len, HEAD_DIM), BF16),
                        pltpu.VMEM((N_HEADS, VT_ROWS, m_len), BF16),
                        pltpu.VMEM((N_HEADS, t), F32),
                        pltpu.VMEM((N_HEADS, VT_ROWS, t), F32)],
        compiler_params=_cparams(("parallel", "arbitrary")),
        name="mla_mem_attn",
    )(qT, k, vT, z, mem_qT, mem, mem_z, mem_gain[None, :], w_k, w_vT,
      jnp.tile(mem_qk_gain[1], N_HEADS)[None, :], _group_ones(GROUP_WIDTH, HEAD_DIM))


def _nsa_kernel(seq, t,
                qT_ref, kcvc_ref, k2_ref, v2T_ref, gl_ref, z_ref,
                cg_ref, pelo_ref, pehi_ref, wlo_ref, whi_ref, cosc_ref, sinc_ref,
                y_ref, kc_s, vcT_s, bias_s, m_s, acc_s, s_s, pg_s):
    qi = pl.program_id(1)
    half = HEAD_DIM // 2
    n_chunk = seq // NSA_CMP_STRIDE
    n_cmp = n_chunk - 1
    n_blk = seq // NSA_SLC_BLOCK
    blk_per_tile = t // NSA_SLC_BLOCK
    win_tiles = NSA_WINDOW // t
    w4 = N_HEADS * t
    SLC, WIN = 0, 1

    @pl.when(qi == 0)
    def _prep():
        ch = kcvc_ref[0]
        a = _dot_x3(*_split_bf16(ch + pelo_ref[...]), wlo_ref[0], wlo_ref[1])
        bm = _dot_x3(*_split_bf16(ch + pehi_ref[...]), whi_ref[0], whi_ref[1])
        cmp = a + pltpu.roll(bm, n_chunk - 1, 0)
        kc = cmp[:, :HEAD_DIM]
        ms = jnp.mean(kc * kc, axis=-1, keepdims=True)
        kc = kc * lax.rsqrt(ms + EPS) * cg_ref[...]
        x1, x2 = kc[:, :half], kc[:, half:]
        c, s = cosc_ref[...], sinc_ref[...]
        kc_hi, kc_lo = _split_bf16(jnp.concatenate([x1 * c - x2 * s, x2 * c + x1 * s], axis=-1))
        kc_s[0] = kc_hi
        kc_s[1] = kc_lo
        vcT_s[...] = cmp.T[HEAD_DIM:].astype(BF16)
        pg_s[:, n_chunk:, :] = jnp.zeros((t // LANES, pg_s.shape[1] - n_chunk, LANES), F32)

    qs = qi * t
    qT = qT_ref[0]
    q4f = jnp.concatenate([qT[h * HEAD_DIM:(h + 1) * HEAD_DIM] for h in range(N_HEADS)], axis=1)
    q4 = q4f.astype(BF16)

    def scores(br, j):
        off = pl.multiple_of(j * t, t)
        return jnp.dot(k2_ref[0, br, pl.ds(off, t), :], q4, preferred_element_type=F32)

    s_s[WIN] = scores(WIN, jnp.maximum(qi - win_tiles, 0))
    qcol = lax.broadcasted_iota(jnp.int32, (1, w4), 1) & (t - 1)
    pos_c = qs + qcol

    q_lo = (q4f - q4.astype(F32)).astype(BF16)
    heads = [slice(h * t, (h + 1) * t) for h in range(N_HEADS)]
    scs = [_dot_x3(kc_s[0], kc_s[1], q4[:, c], q_lo[:, c]) for c in heads]
    n_idx = lax.broadcasted_iota(jnp.int32, (n_chunk, 1), 0)
    cvalid = (n_idx * NSA_CMP_STRIDE + (NSA_CMP_BLOCK - 1) <= pos_c[:, :t]) & (n_idx < n_cmp)
    o_cmp, pg = [], None
    for h in range(N_HEADS):
        sc = jnp.where(cvalid, scs[h], NEG_INF)
        e = jnp.exp2(sc - jnp.max(sc, axis=0, keepdims=True))
        p = jnp.where(cvalid, e / jnp.sum(e, axis=0, keepdims=True), 0.0)
        o_cmp.append(jnp.dot(vcT_s[...], p.astype(BF16), preferred_element_type=F32))
        pg = p if pg is None else pg + p

    for c0 in range(t // LANES):
        pg_s[c0, 0:n_chunk, :] = pg[:, c0 * LANES:(c0 + 1) * LANES]
    ratio = NSA_SLC_BLOCK // NSA_CMP_STRIDE
    coef = np.convolve(np.ones(ratio), np.ones(NSA_CMP_BLOCK // NSA_CMP_STRIDE))
    p_slc = None
    for i, c in enumerate(coef):
        term = float(c) * jnp.concatenate(
            [pg_s[c0, pl.ds(i, n_blk, stride=ratio), :] for c0 in range(t // LANES)], axis=1)
        p_slc = term if p_slc is None else p_slc + term
    blk = lax.broadcasted_iota(jnp.int32, (n_blk, 1), 0)
    cur = lax.shift_right_logical(pos_c[:, :t], NSA_SLC_BLOCK.bit_length() - 1)
    forced = (blk == 0) | ((blk <= cur) & (blk > cur - NSA_N_LOCAL))
    score = jnp.where(blk > cur, NEG_INF, jnp.where(forced, BIG, p_slc))
    sub = 8
    groups = [score[g * sub:(g + 1) * sub] for g in range(n_blk // sub)]
    cnts = [jnp.zeros((sub, t), F32) for _ in groups]
    for i in range(n_blk):
        ri = score[i:i + 1, :]
        for g, sg in enumerate(groups):
            if g < i // sub:
                beats = ri > sg
            elif g > i // sub:
                beats = ri >= sg
            else:
                beats = (ri > sg) | ((ri == sg) & (blk[g * sub:(g + 1) * sub] > i))
            cnts[g] = cnts[g] + jnp.where(beats, 1.0, 0.0)
    cnt = jnp.concatenate(cnts, axis=0)
    bias = jnp.where(cnt < float(NSA_N_SELECT), 0.0, NEG_INF)
    bias = jnp.concatenate([bias] * N_HEADS, axis=1)
    for r in range(n_blk // blk_per_tile):
        bias_s[r] = bias[r * blk_per_tile:(r + 1) * blk_per_tile, :]

    _flash_reset(m_s, acc_s)
    krow = lax.broadcasted_iota(jnp.int32, (t, 1), 0)
    causal = krow <= qcol
    beyond = krow > qcol

    def slc_scores(j):
        s = scores(SLC, j).reshape(blk_per_tile, NSA_SLC_BLOCK, w4) + bias_s[j][:, None, :]
        return s.reshape(t, w4)

    def flash(br, s, j):
        off = pl.multiple_of(j * t, t)
        _flash_step_t(s, v2T_ref[0, br, :, pl.ds(off, t)], m_s, acc_s, br)

    s_s[SLC] = slc_scores(0)

    def old_body(j, carry):
        s = s_s[SLC]
        s_s[SLC] = slc_scores(j + 1)
        flash(SLC, s, j)
        return carry

    lax.fori_loop(0, jnp.maximum(qi - win_tiles, 0), old_body, 0)

    def win_step(j, win_mask):
        s_slc = s_s[SLC]
        s_s[SLC] = slc_scores(j + 1)
        flash(SLC, s_slc, j)
        s_win = s_s[WIN]
        if win_mask is not None:
            s_win = jnp.where(win_mask, s_win, NEG_INF)
        s_s[WIN] = scores(WIN, j + 1)
        flash(WIN, s_win, j)

    @pl.when(qi >= win_tiles)
    def _oldest_window_tile():
        win_step(qi - win_tiles, beyond)

    def win_body(j, carry):
        win_step(j, None)
        return carry

    lax.fori_loop(jnp.maximum(qi - win_tiles + 1, 0), qi, win_body, 0)
    flash(SLC, jnp.where(causal, s_s[SLC], NEG_INF), qi)
    flash(WIN, jnp.where(causal, s_s[WIN], NEG_INF), qi)

    g = jax.nn.sigmoid(gl_ref[0]).T
    outs = []
    for h in range(N_HEADS):
        cols = slice(h * t, (h + 1) * t)
        o_s = _flash_out(acc_s, SLC, cols)
        o_w = _flash_out(acc_s, WIN, cols)
        outs.append(g[h:h + 1, :] * o_cmp[h] + g[N_HEADS + h:N_HEADS + h + 1, :] * o_s
                    + g[2 * N_HEADS + h:2 * N_HEADS + h + 1, :] * o_w)
    y = jnp.concatenate(outs, axis=0).T * _silu(z_ref[0])
    y_ref[0] = y.astype(BF16)


def _nsa_call(qT, kcvc, k2, v2T, gl, z, qk_gain, cmp_pe, w_cmp, t=256):
    b, _, seq, _ = k2.shape
    n_chunk = seq // NSA_CMP_STRIDE
    n_blk = seq // NSA_SLC_BLOCK
    half_blk = NSA_CMP_BLOCK // 2
    cw = half_blk * 2 * HEAD_DIM
    assert t % NSA_SLC_BLOCK == 0 and NSA_WINDOW % t == 0 and t & (t - 1) == 0

    wk = w_cmp[0].reshape(NSA_CMP_BLOCK, HEAD_DIM, HEAD_DIM)
    wv = w_cmp[1].reshape(NSA_CMP_BLOCK, HEAD_DIM, HEAD_DIM)
    zero = jnp.zeros_like(wk)
    w_all = jnp.concatenate([jnp.concatenate([wk, zero], axis=2),
                             jnp.concatenate([zero, wv], axis=2)], axis=1)
    hi_lo = lambda w: jnp.stack([w.astype(BF16), (w - w.astype(BF16).astype(F32)).astype(BF16)])
    w_lo = hi_lo(w_all[:half_blk].reshape(cw, 2 * HEAD_DIM))
    w_hi = hi_lo(w_all[half_blk:].reshape(cw, 2 * HEAD_DIM))
    pe_all = jnp.concatenate([cmp_pe[0], cmp_pe[1]], axis=1)
    pe_lo = pe_all[:half_blk].reshape(1, cw)
    pe_hi = pe_all[half_blk:].reshape(1, cw)

    cmp_end = jnp.arange(n_chunk, dtype=jnp.int32) * NSA_CMP_STRIDE + (NSA_CMP_BLOCK - 1)
    cos_c, sin_c = _rope_cos_sin(cmp_end, HEAD_DIM)

    tile =pl.BlockSpec((1, t, GROUP_WIDTH), lambda bi, qi: (bi, qi, 0))
    const = lambda shape: pl.BlockSpec(shape, lambda bi, qi: tuple(0 for _ in shape))
    w4 = N_HEADS * t
    return pl.pallas_call(
        functools.partial(_nsa_kernel, seq, t),
        grid=(b, seq // t),
        in_specs=[pl.BlockSpec((1, GROUP_WIDTH, t), lambda bi, qi: (bi, 0, qi)),
                  pl.BlockSpec((1, n_chunk, cw), lambda bi, qi: (bi, 0, 0)),
                  pl.BlockSpec((1, 2, seq, HEAD_DIM), lambda bi, qi: (bi, 0, 0, 0)),
                  pl.BlockSpec((1, 2, VT_ROWS, seq), lambda bi, qi: (bi, 0, 0, 0)),
                  pl.BlockSpec((1, t, LANES), lambda bi, qi: (bi, qi, 0)),
                  tile,
                  const((1, HEAD_DIM)),
                  const((1, cw)), const((1, cw)), const((2, cw, 2 * HEAD_DIM)), const((2, cw, 2 * HEAD_DIM)),
                  const((n_chunk, HEAD_DIM // 2)), const((n_chunk, HEAD_DIM // 2))],
        out_specs=tile,
        out_shape=jax.ShapeDtypeStruct((b, seq, GROUP_WIDTH), BF16),
        scratch_shapes=[pltpu.VMEM((2, n_chunk, HEAD_DIM), BF16),
                        pltpu.VMEM((HEAD_DIM, n_chunk), BF16),
                        pltpu.VMEM((seq // t, t // NSA_SLC_BLOCK, w4), F32),
                        pltpu.VMEM((2, w4), F32),
                        pltpu.VMEM((2, VT_ROWS, w4), F32),
                        pltpu.VMEM((2, t, w4), F32),
                        pltpu.VMEM((t // LANES, n_chunk + 8, LANES), F32)],
        compiler_params=_cparams(("parallel", "arbitrary")),
        name="nsa_attn",
    )(qT, kcvc, k2, v2T, gl, z,
      qk_gain[1][None, :],
      pe_lo, pe_hi, w_lo, w_hi, cos_c, sin_c)


def _rope_cos_sin(pos, dim):
    half = dim // 2
    inv_freq = ROPE_THETA ** (-jnp.arange(half, dtype=F32) / half)
    ang = pos.astype(F32)[:, None] * inv_freq[None, :]
    return jnp.cos(ang), jnp.sin(ang)


def _layer(x, mem, layer_idx, tabs_t, norm_gain, w_in, w_out, nsa_qk_gain, nsa_cmp_pe, nsa_w_cmp,
           diff_qk_gain, diff_lambda, diff_subln_gain, mla_cq_gain, mla_ckv_gain, mla_w_uq, mla_w_ukv,
           mla_qk_gain, mem_norm_gain, mem_w_kv, mem_qk_gain):
    b, seq, d = x.shape
    u = _in_proj(x, norm_gain, w_in, nsa_qk_gain, diff_qk_gain, mem_qk_gain, mla_cq_gain, mla_ckv_gain,
                 mla_w_uq, mla_w_ukv, mla_qk_gain, tabs_t)

    y_nsa = _nsa_call(u["nsa_qT"], u["nsa_kcvc"], u["nsa_k2"], u["nsa_v2T"], u["nsa_gl"], u["nsa_z"],
                      nsa_qk_gain, nsa_cmp_pe, nsa_w_cmp)
    lambda_init = 0.8 - 0.6 * math.exp(-0.3 * layer_idx)
    y_diff = _diff_call(u["diff_qT"], u["diff_k"], u["diff_vT"], u["diff_z"], diff_lambda, diff_subln_gain,
                        lambda_init)
    y_mla, y_mem = _mla_mem_call(u["mla_qT"], u["mla_k"], u["mla_vT"], u["mla_z"],
                                 u["mem_qT"], mem, u["mem_z"], mem_norm_gain, mem_w_kv, mem_qk_gain)

    ys = [y.reshape(b * seq, GROUP_WIDTH) for y in (y_nsa, y_diff, y_mla, y_mem)]
    return _out_proj(x.reshape(b * seq, d), ys, w_out.astype(BF16)).reshape(b, seq, d)


def kernel(x, mem, norm_gain, w_in, w_out, nsa_qk_gain, nsa_cmp_pe, nsa_w_cmp, diff_qk_gain, diff_lambda,
           diff_subln_gain, mla_cq_gain, mla_ckv_gain, mla_w_uq, mla_w_ukv, mla_qk_gain, mem_norm_gain,
           mem_w_kv, mem_qk_gain):
    seq = x.shape[1]
    pos = jnp.arange(seq, dtype=jnp.int32)
    cos32, sin32 = _rope_cos_sin(pos, DIFF_D)
    cos64, sin64 = _rope_cos_sin(pos, HEAD_DIM)
    tabs_t = (cos32.T, sin32.T, cos64.T, sin64.T)
    w_row, w_col = _split_w_in(w_in)
    for l in range(DEPTH):
        x = _layer(x, mem, l, tabs_t, norm_gain[l], (w_row[l], w_col[l]), w_out[l], nsa_qk_gain[l], nsa_cmp_pe[l],
                   nsa_w_cmp[l], diff_qk_gain[l], diff_lambda[l], diff_subln_gain[l], mla_cq_gain[l],
                   mla_ckv_gain[l], mla_w_uq[l], mla_w_ukv[l], mla_qk_gain[l], mem_norm_gain[l], mem_w_kv[l],
                   mem_qk_gain[l])
    return x
```

```python
import functools
import math

import numpy as np
import jax
import jax.numpy as jnp
from jax import lax
from jax.experimental import pallas as pl
from jax.experimental.pallas import tpu as pltpu

F32 = jnp.float32
BF16 = jnp.bfloat16

D_MODEL = 1024
DEPTH = 2
N_HEADS = 4
HEAD_DIM = 64
GROUP_WIDTH = N_HEADS * HEAD_DIM
ROPE_THETA = 10000.0
EPS = 1e-6
NEG_INF = -1e30
BIG = 1e30
LOG2E = 1.4426950408889634

NSA_CMP_BLOCK = 32
NSA_CMP_STRIDE = 16
NSA_SLC_BLOCK = 64
NSA_N_SELECT = 16
NSA_N_LOCAL = 2
NSA_WINDOW = 512
DIFF_D = HEAD_DIM // 2
MLA_Q_RANK = 256
MLA_KV_RANK = 128
MLA_NOPE = 64
MLA_ROPE = 32
MLA_QK = MLA_NOPE + MLA_ROPE

VMEM_LIMIT_BYTES = 48 * 1024 * 1024
LANES = 128
KEY_TILE = 256
VT_ROWS = 80


def _cparams(sem):
    return pltpu.CompilerParams(dimension_semantics=sem, vmem_limit_bytes=VMEM_LIMIT_BYTES)


def _layer_spec(layer, *shape):
    return pl.BlockSpec((None,) + shape, lambda *_: (layer,) + (0,) * len(shape))


def _group_ones(width, group):
    g = np.arange(width) // group
    return jnp.asarray(g[:, None] == g[None, :], dtype=BF16)


def _group_rsqrt(x, gmat, denom):
    sq = x * x
    hi = sq.astype(BF16)
    lo = (sq - hi.astype(F32)).astype(BF16)
    ss = jnp.dot(hi, gmat, preferred_element_type=F32) + jnp.dot(lo, gmat, preferred_element_type=F32)
    return lax.rsqrt(ss / denom + EPS)


def _norm_rope_t(u_t, groups, dim, gain, cos=None, sin=None):
    x = u_t.reshape(groups, dim, u_t.shape[-1])
    ms = jnp.mean(x * x, axis=1, keepdims=True)
    x = x * lax.rsqrt(ms + EPS) * gain
    if cos is not None:
        half = dim // 2
        x1, x2 = x[:, :half], x[:, half:]
        x = jnp.concatenate([x1 * cos - x2 * sin, x2 * cos + x1 * sin], axis=1)
    return x.reshape(groups * dim, u_t.shape[-1])


def _silu(z):
    z = z.astype(F32)
    return z * jax.nn.sigmoid(z)


def _ones_rows(n):
    r = lax.broadcasted_iota(jnp.int32, (VT_ROWS - HEAD_DIM, n), 0)
    return jnp.where(r == 0, 1.0, 0.0).astype(BF16)


def _split_bf16(x):
    hi = x.astype(BF16)
    return hi, (x - hi.astype(F32)).astype(BF16)


def _dot_x3(a_hi, a_lo, b_hi, b_lo):
    dot = lambda a, b: jnp.dot(a, b, preferred_element_type=F32)
    return dot(a_hi, b_hi) + dot(a_hi, b_lo) + dot(a_lo, b_hi)


def _flash_step_t(s, v_t, m_ref, acc_ref, i, cols=slice(None)):
    m_old = m_ref[i:i + 1, cols]
    m_new = jnp.maximum(m_old, jnp.max(s, axis=0, keepdims=True))
    alpha = jnp.exp2(m_old - m_new)
    p = jnp.exp2(s - m_new).astype(BF16)
    acc_ref[i, :, cols] = alpha * acc_ref[i, :, cols] + jnp.dot(v_t, p, preferred_element_type=F32)
    m_ref[i:i + 1, cols] = m_new


def _flash_reset(m_ref, acc_ref):
    m_ref[...] = jnp.full(m_ref.shape, NEG_INF, F32)
    acc_ref[...] = jnp.zeros(acc_ref.shape, F32)


def _flash_out(acc_ref, i, cols=slice(None)):
    return acc_ref[i, 0:HEAD_DIM, cols] / acc_ref[i, HEAD_DIM:HEAD_DIM + 1, cols]


def _causal_flash_t(n, k_tile, q_t, v_tile, qi, tq, tk, m_ref, acc_ref, s_ref):
    _flash_reset(m_ref, acc_ref)
    n_diag = tq // tk

    all_q = slice(0, tq)

    def scores(i, off, cols):
        return jnp.dot(k_tile(i, off), q_t(i)[:, cols], preferred_element_type=F32)

    def step(off, cols, off_next, cols_next, mask):
        for i in range(n):
            s = s_ref[i, :, cols]
            if off_next is not None:
                s_ref[i, :, cols_next] = scores(i, off_next, cols_next)
            if mask is not None:
                s = jnp.where(mask, s, NEG_INF)
            _flash_step_t(s, v_tile(i, off), m_ref, acc_ref, i, cols)

    for i in range(n):
        s_ref[i] = scores(i, 0, all_q)

    def body(j, carry):
        step(pl.multiple_of(j * tk, tk), all_q, pl.multiple_of(j * tk + tk, tk), all_q, None)
        return carry

    lax.fori_loop(0, qi * n_diag, body, 0)
    krow = lax.broadcasted_iota(jnp.int32, (tk, tq), 0)
    qcol = lax.broadcasted_iota(jnp.int32, (tk, tq), 1)
    causal = krow <= qcol
    diag_cols = [slice(d * tk, tq) for d in range(n_diag)]
    for d in range(n_diag):
        off = pl.multiple_of(qi * tq + d * tk, tk)
        last = d + 1 == n_diag
        off_next = None if last else pl.multiple_of(qi * tq + (d + 1) * tk, tk)
        step(off, diag_cols[d], off_next, None if last else diag_cols[d + 1], causal[:, 0:tq - d * tk])


_ROW_SEGS = (("nsa_kcvc", 128), ("nsa_gl", 128), ("nsa_z", 256), ("diff_z", 256), ("mla_z", 256),
             ("mem_z", 256))
_COL_SEGS = (("nsa_q", 256), ("diff_q", 256), ("mem_q", 256), ("diff_k", 256), ("nsa_k2", 128),
             ("diff_v", 256), ("nsa_v2", 128), ("mla_cq", 256), ("mla_ckv", 128), ("mla_kr", 32))


def _seg_offsets(segs):
    out, off = {}, 0
    for name, w in segs:
        out[name] = (off, w)
        off += w
    return out, off


_ROW_OFF, _ROW_W = _seg_offsets(_ROW_SEGS)
_COL_OFF, _COL_W = _seg_offsets(_COL_SEGS)


_ROW_SRC = ((256, 384),
            (640, 652), (None, 116),
            (652, 908), (1676, 1932), (2348, 2604), (2860, 3116))
_COL_SRC = ((0, 256), (908, 1164), (2604, 2860),
            (1164, 1420),
            (384, 448), (512, 576),
            (1420, 1676),
            (448, 512), (576, 640),
            (1932, 2188), (2188, 2316), (2316, 2348))


def _w_split_kernel(wt_ref, wrow_o, wcol_o):
    _, layers, ct = wt_ref.shape

    for l in range(layers):
        def gather(src):
            return jnp.concatenate(
                [jnp.zeros((b, ct), F32) if a is None else wt_ref[a:b, l, :] for a, b in src], axis=0)

        wcol_o[l] = gather(_COL_SRC).astype(BF16)
        wrow_o[l] = gather(_ROW_SRC).T.astype(BF16)


def _split_w_in(w_in, ct=128):
    layers, d, d_in = w_in.shape
    w_t = jnp.transpose(w_in, (2, 0, 1))
    return pl.pallas_call(
        _w_split_kernel,
        grid=(d // ct,),
        in_specs=[pl.BlockSpec((d_in, layers, ct), lambda c: (0, 0, c))],
        out_specs=[pl.BlockSpec((layers, ct, _ROW_W), lambda c: (0, c, 0)),
                   pl.BlockSpec((layers, _COL_W, ct), lambda c: (0, 0, c))],
        out_shape=[jax.ShapeDtypeStruct((layers, d, _ROW_W), BF16),
                   jax.ShapeDtypeStruct((layers, _COL_W, d), BF16)],
        compiler_params=_cparams(("parallel",)),
        name="w_in_split",
    )(w_t)


def _mla_head_norm_t(x_t, gain):
    ms = jnp.sum(x_t * x_t, axis=0, keepdims=True) / float(MLA_QK)
    return x_t * lax.rsqrt(ms + EPS) * gain


def _rope_t(x_t, cos, sin):
    half = x_t.shape[0] // 2
    x1, x2 = x_t[:half], x_t[half:]
    return jnp.concatenate([x1 * cos - x2 * sin, x2 * cos + x1 * sin], axis=0)


def _in_proj_kernel(x_ref, g_ref, wrow_ref, wcol_ref, nqg_ref, dqg_ref, dkg_ref, k2g_ref, mqg_ref,
                    cqg_ref, ckvg_ref, wuq_ref, wuk_ref, wuv_ref, lqg_ref, lkg_ref,
                    c32_ref, s32_ref, c64_ref, s64_ref,
                    kcvc_o, gl_o, nz_o, dz_o, mz_o, ez_o, dk_o, k2_o,
                    nq_o, dq_o, mq_o, dv_o, v2_o, lq_o, lk_o, lv_o, kcvc_s):
    x = x_ref[...]
    tm = x.shape[0]
    ms = jnp.mean(x * x, axis=-1, keepdims=True)
    h = x * lax.rsqrt(ms + EPS) * g_ref[...]
    hb = h.astype(BF16)
    h_t = h.T.astype(BF16)

    def row(name):
        off, w = _ROW_OFF[name]
        return jnp.dot(hb, wrow_ref[:, off:off + w], preferred_element_type=F32)

    u_t = jnp.dot(wcol_ref[...], h_t, preferred_element_type=F32)

    def col(name):
        off, w = _COL_OFF[name]
        return u_t[off:off + w]

    c32, s32, c64, s64 = c32_ref[...], s32_ref[...], c64_ref[...], s64_ref[...]
    g3 = lambda ref, groups: ref[...].reshape(groups, -1, 1)

    rows_out = {name: row(name) for name, _ in _ROW_SEGS}

    kcvc_s[...] = rows_out["nsa_kcvc"]
    n_chunk_rows = tm // NSA_CMP_STRIDE
    for tok in range(NSA_CMP_STRIDE):
        kcvc_o[0, :, tok * LANES:(tok + 1) * LANES] = kcvc_s[pl.ds(tok, n_chunk_rows, stride=NSA_CMP_STRIDE), :]
    gl_o[...] = rows_out["nsa_gl"]
    nz_o[...] = rows_out["nsa_z"].astype(BF16)
    dz_o[...] = rows_out["diff_z"].astype(BF16)
    mz_o[...] = rows_out["mla_z"].astype(BF16)
    ez_o[...] = rows_out["mem_z"].astype(BF16)

    nq_o[0] = _norm_rope_t(col("nsa_q"), N_HEADS, HEAD_DIM, g3(nqg_ref, 1), c64, s64) * (HEAD_DIM ** -0.5 * LOG2E)
    dq = _norm_rope_t(col("diff_q"), 2 * N_HEADS, DIFF_D, g3(dqg_ref, 1), c32, s32) * (DIFF_D ** -0.5 * LOG2E)
    dq_o[0] = dq.astype(BF16)
    mq = _norm_rope_t(col("mem_q"), N_HEADS, HEAD_DIM, g3(mqg_ref, 1)) * (HEAD_DIM ** -0.5 * LOG2E)
    mq_o[0] = mq.astype(BF16)
    dk = _norm_rope_t(col("diff_k"), 2 * N_HEADS, DIFF_D, g3(dkg_ref, 1), c32, s32).T
    for mp in range(2 * N_HEADS):
        dk_o[0, mp] = dk[:, mp * DIFF_D:(mp + 1) * DIFF_D].astype(BF16)
    k2 = _norm_rope_t(col("nsa_k2"), 2, HEAD_DIM, g3(k2g_ref, 2), c64, s64).T
    for br in range(2):
        k2_o[0, br] = k2[:, br * HEAD_DIM:(br + 1) * HEAD_DIM].astype(BF16)
    ones = _ones_rows(tm)
    dv = col("diff_v")
    for hd in range(N_HEADS):
        dv_o[0, hd, 0:HEAD_DIM, :] = dv[hd * HEAD_DIM:(hd + 1) * HEAD_DIM].astype(BF16)
        dv_o[0, hd, HEAD_DIM:VT_ROWS, :] = ones
    v2 = col("nsa_v2")
    for br in range(2):
        v2_o[0, br, 0:HEAD_DIM, :] = v2[br * HEAD_DIM:(br + 1) * HEAD_DIM].astype(BF16)
        v2_o[0, br, HEAD_DIM:VT_ROWS, :] = ones

    def latent(name, gain_ref):
        c = col(name)
        return (c * lax.rsqrt(jnp.mean(c * c, axis=0, keepdims=True) + EPS) * gain_ref[...]).astype(BF16)

    qa = jnp.dot(wuq_ref[...], latent("mla_cq", cqg_ref), preferred_element_type=F32)
    ckv = latent("mla_ckv", ckvg_ref)
    kn = jnp.dot(wuk_ref[...], ckv, preferred_element_type=F32)
    lv = jnp.dot(wuv_ref[...], ckv, preferred_element_type=F32)
    rope_rows = slice(MLA_NOPE, MLA_QK)
    lqg = lqg_ref[...] * (MLA_QK ** -0.5 * LOG2E)
    kr = _rope_t(col("mla_kr"), c32, s32)
    zpad = jnp.zeros((LANES - MLA_QK, tm), F32)
    for hd in range(N_HEADS):
        q_h = qa[hd * LANES:(hd + 1) * LANES]
        q_h = jnp.concatenate([q_h[:MLA_NOPE], _rope_t(q_h[rope_rows], c32, s32), q_h[MLA_QK:]], axis=0)
        lq_o[0, hd * LANES:(hd + 1) * LANES, :] = _mla_head_norm_t(q_h, lqg).astype(BF16)
        k_h = jnp.concatenate([kn[hd * LANES:hd * LANES + MLA_NOPE], kr, zpad], axis=0)
        lk_o[0, hd] = _mla_head_norm_t(k_h, lkg_ref[...]).T.astype(BF16)
        lv_o[0, hd, 0:HEAD_DIM, :] = lv[hd * HEAD_DIM:(hd + 1) * HEAD_DIM].astype(BF16)
        lv_o[0, hd, HEAD_DIM:VT_ROWS, :] = ones


def _in_proj(x, layer, prm, tabs_t, tm=512):
    b, seq, d = x.shape
    n = b * seq
    nb = seq // tm
    c32, s32, c64, s64 = tabs_t
    lay = functools.partial(_layer_spec, layer)
    gain_col = lambda j, rows: pl.BlockSpec((None, None, rows, 1), lambda i: (layer, j, 0, 0))

    rowspec = lambda w: pl.BlockSpec((tm, w), lambda i: (i, 0))
    colspec = lambda r: pl.BlockSpec((1, r, tm), lambda i: (i // nb, 0, i % nb))
    vtspec = lambda c: pl.BlockSpec((1, c, VT_ROWS, tm), lambda i: (i // nb, 0, 0, i % nb))
    const = lambda shape: pl.BlockSpec(shape, lambda i: tuple(0 for _ in shape))
    tabspec = lambda r: pl.BlockSpec((r, tm), lambda i: (0, i % nb))
    row_out = lambda w, dt: jax.ShapeDtypeStruct((n, w), dt)
    col_out = lambda r, dt: jax.ShapeDtypeStruct((b, r, seq), dt)
    vt_out = lambda c: jax.ShapeDtypeStruct((b, c, VT_ROWS, seq), BF16)
    slabspec = lambda c, w: pl.BlockSpec((1, c, tm, w), lambda i: (i // nb, 0, i % nb, 0))
    slab_out = lambda c, w: jax.ShapeDtypeStruct((b, c, seq, w), BF16)
    outs = pl.pallas_call(
        _in_proj_kernel,
        grid=(n // tm,),
        in_specs=[rowspec(d), lay(1, d), lay(d, _ROW_W), lay(_COL_W, d),
                  gain_col(0, HEAD_DIM), gain_col(1, DIFF_D), gain_col(2, DIFF_D), gain_col(3, 2 * HEAD_DIM),
                  gain_col(4, HEAD_DIM),
                  gain_col(5, MLA_Q_RANK), gain_col(6, MLA_KV_RANK),
                  lay(N_HEADS * LANES, MLA_Q_RANK), lay(N_HEADS * LANES, MLA_KV_RANK),
                  lay(GROUP_WIDTH, MLA_KV_RANK), gain_col(7, LANES), gain_col(8, LANES),
                  tabspec(DIFF_D // 2), tabspec(DIFF_D // 2), tabspec(HEAD_DIM // 2), tabspec(HEAD_DIM // 2)],
        out_specs=[pl.BlockSpec((1, tm // NSA_CMP_STRIDE, NSA_CMP_STRIDE * LANES), lambda i: (i // nb, i % nb, 0)),
                   rowspec(128), rowspec(256), rowspec(256), rowspec(256), rowspec(256),
                   slabspec(2 * N_HEADS, DIFF_D), slabspec(2, HEAD_DIM),
                   colspec(256), colspec(256), colspec(256), vtspec(N_HEADS), vtspec(2),
                   colspec(N_HEADS * LANES), slabspec(N_HEADS, LANES), vtspec(N_HEADS)],
        out_shape=[jax.ShapeDtypeStruct((b, seq // NSA_CMP_STRIDE, NSA_CMP_STRIDE * LANES), F32),
                   row_out(128, F32), row_out(256, BF16), row_out(256, BF16),
                   row_out(256, BF16), row_out(256, BF16), slab_out(2 * N_HEADS, DIFF_D), slab_out(2, HEAD_DIM),
                   col_out(256, F32), col_out(256, BF16), col_out(256, BF16), vt_out(N_HEADS), vt_out(2),
                   col_out(N_HEADS * LANES, BF16), slab_out(N_HEADS, LANES), vt_out(N_HEADS)],
        scratch_shapes=[pltpu.VMEM((tm, LANES), F32)],
        compiler_params=_cparams(("parallel",)),
        name="in_proj",
    )(x.reshape(n, d), prm["norm_gain"], prm["w_row"], prm["w_col"],
      *([prm["gain_cols"]] * 7), prm["w_uq_t"], prm["w_uk_t"], prm["w_uv_t"], prm["gain_cols"], prm["gain_cols"],
      c32, s32, c64, s64)
    names = ("nsa_kcvc", "nsa_gl", "nsa_z", "diff_z", "mla_z", "mem_z", "diff_k", "nsa_k2",
             "nsa_qT", "diff_qT", "mem_qT", "diff_vT", "nsa_v2T", "mla_qT", "mla_k", "mla_vT")
    u = dict(zip(names, outs))
    for name in names[1:6]:
        u[name] = u[name].reshape(b, seq, -1)
    return u


def _out_proj_kernel(x_ref, y0_ref, y1_ref, y2_ref, y3_ref, w_ref, o_ref):
    acc = x_ref[...]
    for g, y_ref in enumerate((y0_ref, y1_ref, y2_ref, y3_ref)):
        acc = acc + jnp.dot(y_ref[...], w_ref[g * GROUP_WIDTH:(g + 1) * GROUP_WIDTH, :],
                            preferred_element_type=F32)
    o_ref[...] = acc


def _out_proj(x2, ys, layer, w_out_b, tm=1024):
    n = x2.shape[0]
    yspec = pl.BlockSpec((tm, GROUP_WIDTH), lambda i: (i, 0))
    return pl.pallas_call(
        _out_proj_kernel,
        grid=(n // tm,),
        in_specs=[pl.BlockSpec((tm, D_MODEL), lambda i: (i, 0)), yspec, yspec, yspec, yspec,
                  _layer_spec(layer, D_MODEL, D_MODEL)],
        out_specs=pl.BlockSpec((tm, D_MODEL), lambda i: (i, 0)),
        out_shape=jax.ShapeDtypeStruct((n, D_MODEL), F32),
        compiler_params=_cparams(("parallel",)),
        name="out_proj",
    )(x2, *ys, w_out_b)


def _diff_kernel(lambda_init, t, qT_ref, k_ref, vT_ref, z_ref, lam_ref, sg_ref, y_ref, m_s, acc_s, s_s):
    qi = pl.program_id(1)
    n_maps = 2 * N_HEADS
    qT = qT_ref[0]
    _causal_flash_t(
        n_maps,
        lambda i, off: k_ref[0, i, pl.ds(off, KEY_TILE), :],
        lambda i: qT[i * DIFF_D:(i + 1) * DIFF_D],
        lambda i, off: vT_ref[0, i // 2, :, pl.ds(off, KEY_TILE)],
        qi, t, KEY_TILE, m_s, acc_s, s_s)

    lam = lam_ref[...]
    lmbda = (jnp.exp(jnp.sum(lam[0:1] * lam[1:2], axis=-1, keepdims=True))
             - jnp.exp(jnp.sum(lam[2:3] * lam[3:4], axis=-1, keepdims=True)) + lambda_init)
    outs = []
    for h in range(N_HEADS):
        d = _flash_out(acc_s, 2 * h) - lmbda * _flash_out(acc_s, 2 * h + 1)
        ms = jnp.mean(d * d, axis=0, keepdims=True)
        outs.append(d * lax.rsqrt(ms + EPS) * sg_ref[...] * (1.0 - lambda_init))
    y = jnp.concatenate(outs, axis=0).T * _silu(z_ref[0])
    y_ref[0] = y.astype(BF16)


def _diff_call(qT, k, vT, z, layer, lam, subln_col, lambda_init, t=512):
    b, _, seq, _ = k.shape
    const = lambda shape: pl.BlockSpec(shape, lambda bi, qi: tuple(0 for _ in shape))
    tile = pl.BlockSpec((1, t, GROUP_WIDTH), lambda bi, qi: (bi, qi, 0))
    return pl.pallas_call(
        functools.partial(_diff_kernel, lambda_init, t),
        grid=(b, seq // t),
        in_specs=[pl.BlockSpec((1, GROUP_WIDTH, t), lambda bi, qi: (bi, 0, qi)),
                  pl.BlockSpec((1, 2 * N_HEADS, seq, DIFF_D), lambda bi, qi: (bi, 0, 0, 0)),
                  pl.BlockSpec((1, N_HEADS, VT_ROWS, seq), lambda bi, qi: (bi, 0, 0, 0)),
                  tile, _layer_spec(layer, 4, DIFF_D), _layer_spec(layer, HEAD_DIM, 1)],
        out_specs=tile,
        out_shape=jax.ShapeDtypeStruct((b, seq, GROUP_WIDTH), BF16),
        scratch_shapes=[pltpu.VMEM((2 * N_HEADS, t), F32),
                        pltpu.VMEM((2 * N_HEADS, VT_ROWS, t), F32),
                        pltpu.VMEM((2 * N_HEADS, KEY_TILE, t), F32)],
        compiler_params=_cparams(("parallel", "parallel")),
        name="diff_attn",
    )(qT, k, vT, z, lam, subln_col)


def _mla_mem_kernel(t, qT_ref, k_ref, vT_ref, z_ref,
                    mqT_ref, mem_ref, mz_ref, mg_ref, wk_ref, wvT_ref, kg_ref, gm_ref,
                    y_ref, ymem_ref, m_s, acc_s, s_s, mk_s, mvT_s, mm_s, macc_s):
    qi = pl.program_id(1)
    nt = (((1,), (1,)), ((), ()))
    m_len = mem_ref.shape[1]

    @pl.when(qi == 0)
    def _prep_memory_kv():
        mem = mem_ref[0]
        ms = jnp.mean(mem * mem, axis=-1, keepdims=True)
        mb = (mem * lax.rsqrt(ms + EPS) * mg_ref[...]).astype(BF16)
        k = jnp.dot(mb, wk_ref[...], preferred_element_type=F32)
        kn = k * _group_rsqrt(k, gm_ref[...], float(HEAD_DIM)) * kg_ref[...]
        vT = lax.dot_general(wvT_ref[...], mb, nt, preferred_element_type=F32)
        for h in range(N_HEADS):
            mk_s[h] = kn[:, h * HEAD_DIM:(h + 1) * HEAD_DIM].astype(BF16)
            mvT_s[h, 0:HEAD_DIM, :] = vT[h * HEAD_DIM:(h + 1) * HEAD_DIM].astype(BF16)
            mvT_s[h, HEAD_DIM:VT_ROWS, :] = _ones_rows(m_len)

    mqT = mqT_ref[0]
    _flash_reset(mm_s, macc_s)
    ss = [jnp.dot(mk_s[h], mqT[h * HEAD_DIM:(h + 1) * HEAD_DIM], preferred_element_type=F32)
          for h in range(N_HEADS)]
    for h in range(N_HEADS):
        _flash_step_t(ss[h], mvT_s[h], mm_s, macc_s, h)

    qT = qT_ref[0]
    _causal_flash_t(
        N_HEADS,
        lambda h, off: k_ref[0, h, pl.ds(off, KEY_TILE), :],
        lambda h: qT[h * LANES:(h + 1) * LANES],
        lambda h, off: vT_ref[0, h, :, pl.ds(off, KEY_TILE)],
        qi, t, KEY_TILE, m_s, acc_s, s_s)

    y_mem = jnp.concatenate([_flash_out(macc_s, h) for h in range(N_HEADS)], axis=0).T * _silu(mz_ref[0])
    ymem_ref[0] = y_mem.astype(BF16)
    y = jnp.concatenate([_flash_out(acc_s, h) for h in range(N_HEADS)], axis=0).T * _silu(z_ref[0])
    y_ref[0] = y.astype(BF16)


def _mla_mem_call(qT, k, vT, z, mem_qT, mem, mem_z, layer, prm, t=512):
    b, _, seq, _ = k.shape
    m_len = mem.shape[1]
    lay = functools.partial(_layer_spec, layer)
    tile = pl.BlockSpec((1, t, GROUP_WIDTH), lambda bi, qi: (bi, qi, 0))
    const = lambda shape: pl.BlockSpec(shape, lambda bi, qi: tuple(0 for _ in shape))
    out = jax.ShapeDtypeStruct((b, seq, GROUP_WIDTH), BF16)
    return pl.pallas_call(
        functools.partial(_mla_mem_kernel, t),
        grid=(b, seq // t),
        in_specs=[pl.BlockSpec((1, N_HEADS * LANES, t), lambda bi, qi: (bi, 0, qi)),
                  pl.BlockSpec((1, N_HEADS, seq, LANES), lambda bi, qi: (bi, 0, 0, 0)),
                  pl.BlockSpec((1, N_HEADS, VT_ROWS, seq), lambda bi, qi: (bi, 0, 0, 0)),
                  tile,
                  pl.BlockSpec((1, GROUP_WIDTH, t), lambda bi, qi: (bi, 0, qi)),
                  pl.BlockSpec((1, m_len, D_MODEL), lambda bi, qi: (bi, 0, 0)), tile,
                  lay(1, D_MODEL), lay(D_MODEL, GROUP_WIDTH), lay(GROUP_WIDTH, D_MODEL),
                  lay(1, GROUP_WIDTH), const((GROUP_WIDTH, GROUP_WIDTH))],
        out_specs=[tile, tile],
        out_shape=[out, out],
        scratch_shapes=[pltpu.VMEM((N_HEADS, t), F32),
                        pltpu.VMEM((N_HEADS, VT_ROWS, t), F32),
                        pltpu.VMEM((N_HEADS, KEY_TILE, t), F32),
                        pltpu.VMEM((N_HEADS, m_len, HEAD_DIM), BF16),
                        pltpu.VMEM((N_HEADS, VT_ROWS, m_len), BF16),
                        pltpu.VMEM((N_HEADS, t), F32),
                        pltpu.VMEM((N_HEADS, VT_ROWS, t), F32)],
        compiler_params=_cparams(("parallel", "arbitrary")),
        name="mla_mem_attn",
    )(qT, k, vT, z, mem_qT, mem, mem_z, prm["mem_gain"], prm["mem_w_k"], prm["mem_w_vt"],
      prm["mem_k_gain"], _group_ones(GROUP_WIDTH, HEAD_DIM))


def _nsa_kernel(seq, t,
                qT_ref, kcvc_ref, k2_ref, v2T_ref, gl_ref, z_ref,
                cg_ref, pelo_ref, pehi_ref, wlo_ref, whi_ref, cosc_ref, sinc_ref,
                y_ref, kc_s, vcT_s, bias_s, m_s, acc_s, s_s, pg_s):
    qi = pl.program_id(1)
    half = HEAD_DIM // 2
    n_chunk = seq // NSA_CMP_STRIDE
    n_cmp = n_chunk - 1
    n_blk = seq // NSA_SLC_BLOCK
    blk_per_tile = t // NSA_SLC_BLOCK
    win_tiles = NSA_WINDOW // t
    w4 = N_HEADS * t
    SLC, WIN = 0, 1

    @pl.when(qi == 0)
    def _prep():
        ch = kcvc_ref[0]
        a = _dot_x3(*_split_bf16(ch + pelo_ref[...]), wlo_ref[0], wlo_ref[1])
        bm = _dot_x3(*_split_bf16(ch + pehi_ref[...]), whi_ref[0], whi_ref[1])
        cmp = a + pltpu.roll(bm, n_chunk - 1, 0)
        kc = cmp[:, :HEAD_DIM]
        ms = jnp.mean(kc * kc, axis=-1, keepdims=True)
        kc = kc * lax.rsqrt(ms + EPS) * cg_ref[...]
        x1, x2 = kc[:, :half], kc[:, half:]
        c, s = cosc_ref[...], sinc_ref[...]
        kc_hi, kc_lo = _split_bf16(jnp.concatenate([x1 * c - x2 * s, x2 * c + x1 * s], axis=-1))
        kc_s[0] = kc_hi
        kc_s[1] = kc_lo
        vcT_s[...] = cmp.T[HEAD_DIM:].astype(BF16)
        pg_s[:, n_chunk:, :] = jnp.zeros((t // LANES, pg_s.shape[1] - n_chunk, LANES), F32)

    qs = qi * t
    qT = qT_ref[0]
    q4f = jnp.concatenate([qT[h * HEAD_DIM:(h + 1) * HEAD_DIM] for h in range(N_HEADS)], axis=1)
    q4 = q4f.astype(BF16)

    def scores(br, j):
        off = pl.multiple_of(j * t, t)
        return jnp.dot(k2_ref[0, br, pl.ds(off, t), :], q4, preferred_element_type=F32)

    s_s[WIN] = scores(WIN, jnp.maximum(qi - win_tiles, 0))
    qcol = lax.broadcasted_iota(jnp.int32, (1, w4), 1) & (t - 1)
    pos_c = qs + qcol

    q_lo = (q4f - q4.astype(F32)).astype(BF16)
    heads = [slice(h * t, (h + 1) * t) for h in range(N_HEADS)]
    scs = [_dot_x3(kc_s[0], kc_s[1], q4[:, c], q_lo[:, c]) for c in heads]
    n_idx = lax.broadcasted_iota(jnp.int32, (n_chunk, 1), 0)
    cvalid = (n_idx * NSA_CMP_STRIDE + (NSA_CMP_BLOCK - 1) <= pos_c[:, :t]) & (n_idx < n_cmp)
    o_cmp, pg = [], None
    for h in range(N_HEADS):
        sc = jnp.where(cvalid, scs[h], NEG_INF)
        e = jnp.exp2(sc - jnp.max(sc, axis=0, keepdims=True))
        p = jnp.where(cvalid, e / jnp.sum(e, axis=0, keepdims=True), 0.0)
        o_cmp.append(jnp.dot(vcT_s[...], p.astype(BF16), preferred_element_type=F32))
        pg = p if pg is None else pg + p

    for c0 in range(t // LANES):
        pg_s[c0, 0:n_chunk, :] = pg[:, c0 * LANES:(c0 + 1) * LANES]
    ratio = NSA_SLC_BLOCK // NSA_CMP_STRIDE
    coef = np.convolve(np.ones(ratio), np.ones(NSA_CMP_BLOCK // NSA_CMP_STRIDE))
    p_slc = None
    for i, c in enumerate(coef):
        term = float(c) * jnp.concatenate(
            [pg_s[c0, pl.ds(i, n_blk, stride=ratio), :] for c0 in range(t // LANES)], axis=1)
        p_slc = term if p_slc is None else p_slc + term
    blk = lax.broadcasted_iota(jnp.int32, (n_blk, 1), 0)
    cur = lax.shift_right_logical(pos_c[:, :t], NSA_SLC_BLOCK.bit_length() - 1)
    forced = (blk == 0) | ((blk <= cur) & (blk > cur - NSA_N_LOCAL))
    score = jnp.where(blk > cur, NEG_INF, jnp.where(forced, BIG, p_slc))
    sub = 8
    groups = [score[g * sub:(g + 1) * sub] for g in range(n_blk // sub)]
    cnts = [jnp.zeros((sub, t), F32) for _ in groups]
    for i in range(n_blk):
        ri = score[i:i + 1, :]
        for g, sg in enumerate(groups):
            if g < i // sub:
                beats = ri > sg
            elif g > i // sub:
                beats = ri >= sg
            else:
                beats = (ri > sg) | ((ri == sg) & (blk[g * sub:(g + 1) * sub] > i))
            cnts[g] = cnts[g] + jnp.where(beats, 1.0, 0.0)
    cnt = jnp.concatenate(cnts, axis=0)
    bias = jnp.where(cnt < float(NSA_N_SELECT), 0.0, NEG_INF)
    bias = jnp.concatenate([bias] * N_HEADS, axis=1)
    for r in range(n_blk // blk_per_tile):
        bias_s[r] = bias[r * blk_per_tile:(r + 1) * blk_per_tile, :]

    _flash_reset(m_s, acc_s)
    krow = lax.broadcasted_iota(jnp.int32, (t, 1), 0)
    causal = krow <= qcol
    beyond = krow > qcol

    def slc_scores(j):
        s = scores(SLC, j).reshape(blk_per_tile, NSA_SLC_BLOCK, w4) + bias_s[j][:, None, :]
        return s.reshape(t, w4)

    def flash(br, s, j):
        off = pl.multiple_of(j * t, t)
        _flash_step_t(s, v2T_ref[0, br, :, pl.ds(off, t)], m_s, acc_s, br)

    s_s[SLC] = slc_scores(0)

    def old_body(j, carry):
        s = s_s[SLC]
        s_s[SLC] = slc_scores(j + 1)
        flash(SLC, s, j)
        return carry

    lax.fori_loop(0, jnp.maximum(qi - win_tiles, 0), old_body, 0)

    def win_step(j, win_mask):
        s_slc = s_s[SLC]
        s_s[SLC] = slc_scores(j + 1)
        flash(SLC, s_slc, j)
        s_win = s_s[WIN]
        if win_mask is not None:
            s_win = jnp.where(win_mask, s_win, NEG_INF)
        s_s[WIN] = scores(WIN, j + 1)
        flash(WIN, s_win, j)

    @pl.when(qi >= win_tiles)
    def _oldest_window_tile():
        win_step(qi - win_tiles, beyond)

    def win_body(j, carry):
        win_step(j, None)
        return carry

    lax.fori_loop(jnp.maximum(qi - win_tiles + 1, 0), qi, win_body, 0)
    flash(SLC, jnp.where(causal, s_s[SLC], NEG_INF), qi)
    flash(WIN, jnp.where(causal, s_s[WIN], NEG_INF), qi)

    g = jax.nn.sigmoid(gl_ref[0]).T
    outs = []
    for h in range(N_HEADS):
        cols = slice(h * t, (h + 1) * t)
        o_s = _flash_out(acc_s, SLC, cols)
        o_w = _flash_out(acc_s, WIN, cols)
        outs.append(g[h:h + 1, :] * o_cmp[h] + g[N_HEADS + h:N_HEADS + h + 1, :] * o_s
                    + g[2 * N_HEADS + h:2 * N_HEADS + h + 1, :] * o_w)
    y = jnp.concatenate(outs, axis=0).T * _silu(z_ref[0])
    y_ref[0] = y.astype(BF16)


def _nsa_call(qT, kcvc, k2, v2T, gl, z, layer, prm, t=256):
    b, _, seq, _ = k2.shape
    n_chunk = seq // NSA_CMP_STRIDE
    cw = NSA_CMP_STRIDE * 2 * HEAD_DIM
    assert t % NSA_SLC_BLOCK == 0 and NSA_WINDOW % t == 0 and t & (t - 1) == 0
    lay = functools.partial(_layer_spec, layer)

    tile = pl.BlockSpec((1, t, GROUP_WIDTH), lambda bi, qi: (bi, qi, 0))
    const = lambda shape: pl.BlockSpec(shape, lambda bi, qi: tuple(0 for _ in shape))
    w4 = N_HEADS * t
    return pl.pallas_call(
        functools.partial(_nsa_kernel, seq, t),
        grid=(b, seq // t),
        in_specs=[pl.BlockSpec((1, GROUP_WIDTH, t), lambda bi, qi: (bi, 0, qi)),
                  pl.BlockSpec((1, n_chunk, cw), lambda bi, qi: (bi, 0, 0)),
                  pl.BlockSpec((1, 2, seq, HEAD_DIM), lambda bi, qi: (bi, 0, 0, 0)),
                  pl.BlockSpec((1, 2, VT_ROWS, seq), lambda bi, qi: (bi, 0, 0, 0)),
                  pl.BlockSpec((1, t, LANES), lambda bi, qi: (bi, qi, 0)),
                  tile,
                  lay(1, HEAD_DIM),
                  lay(1, cw), lay(1, cw), lay(2, cw, 2 * HEAD_DIM), lay(2, cw, 2 * HEAD_DIM),
                  const((n_chunk, HEAD_DIM // 2)), const((n_chunk, HEAD_DIM // 2))],
        out_specs=tile,
        out_shape=jax.ShapeDtypeStruct((b, seq, GROUP_WIDTH), BF16),
        scratch_shapes=[pltpu.VMEM((2, n_chunk, HEAD_DIM), BF16),
                        pltpu.VMEM((HEAD_DIM, n_chunk), BF16),
                        pltpu.VMEM((seq // t, t // NSA_SLC_BLOCK, w4), F32),
                        pltpu.VMEM((2, w4), F32),
                        pltpu.VMEM((2, VT_ROWS, w4), F32),
                        pltpu.VMEM((2, t, w4), F32),
                        pltpu.VMEM((t // LANES, n_chunk + 8, LANES), F32)],
        compiler_params=_cparams(("parallel", "arbitrary")),
        name="nsa_attn",
    )(qT, kcvc, k2, v2T, gl, z,
      prm["nsa_cmp_gain"], prm["nsa_pe_lo"], prm["nsa_pe_hi"], prm["nsa_w_lo"], prm["nsa_w_hi"],
      prm["cos_c"], prm["sin_c"])


def _rope_cos_sin(pos, dim):
    half = dim // 2
    inv_freq = ROPE_THETA ** (-jnp.arange(half, dtype=F32) / half)
    ang = pos.astype(F32)[:, None] * inv_freq[None, :]
    return jnp.cos(ang), jnp.sin(ang)


def _prepare_params(seq, norm_gain, w_in, w_out, nsa_qk_gain, nsa_cmp_pe, nsa_w_cmp, diff_qk_gain, diff_subln_gain,
                    mla_cq_gain, mla_ckv_gain, mla_w_uq, mla_w_ukv, mla_qk_gain, mem_norm_gain, mem_w_kv, mem_qk_gain):
    layers = w_in.shape[0]
    prm = {}
    prm["w_row"], prm["w_col"] = _split_w_in(w_in)
    prm["w_out"] = w_out.astype(BF16)
    prm["norm_gain"] = norm_gain[:, None, :]

    col = lambda a: jnp.pad(a, ((0, 0), (0, GROUP_WIDTH - a.shape[1])))
    prm["gain_cols"] = jnp.stack([
        col(nsa_qk_gain[:, 0]), col(diff_qk_gain[:, 0]), col(diff_qk_gain[:, 1]),
        col(jnp.concatenate([nsa_qk_gain[:, 2], nsa_qk_gain[:, 3]], axis=1)), col(mem_qk_gain[:, 0]),
        col(mla_cq_gain), col(mla_ckv_gain), col(mla_qk_gain[:, 0]), col(mla_qk_gain[:, 1])], axis=1)[..., None]

    npad = LANES - MLA_QK
    uq = jnp.pad(mla_w_uq.reshape(layers, MLA_Q_RANK, N_HEADS, MLA_QK), ((0, 0), (0, 0), (0, 0), (0, npad)))
    prm["w_uq_t"] = jnp.swapaxes(uq.reshape(layers, MLA_Q_RANK, N_HEADS * LANES), 1, 2).astype(BF16)
    ukv = mla_w_ukv.reshape(layers, MLA_KV_RANK, N_HEADS, MLA_NOPE + HEAD_DIM)
    uk = jnp.pad(ukv[..., :MLA_NOPE], ((0, 0), (0, 0), (0, 0), (0, LANES - MLA_NOPE)))
    prm["w_uk_t"] = jnp.swapaxes(uk.reshape(layers, MLA_KV_RANK, N_HEADS * LANES), 1, 2).astype(BF16)
    uv = ukv[..., MLA_NOPE:].reshape(layers, MLA_KV_RANK, GROUP_WIDTH)
    prm["w_uv_t"] = jnp.swapaxes(uv, 1, 2).astype(BF16)

    half_blk = NSA_CMP_BLOCK // 2
    cw = half_blk * 2 * HEAD_DIM
    wk = nsa_w_cmp[:, 0].reshape(layers, NSA_CMP_BLOCK, HEAD_DIM, HEAD_DIM)
    wv = nsa_w_cmp[:, 1].reshape(layers, NSA_CMP_BLOCK, HEAD_DIM, HEAD_DIM)
    zero = jnp.zeros_like(wk)
    w_all = jnp.concatenate([jnp.concatenate([wk, zero], axis=3),
                             jnp.concatenate([zero, wv], axis=3)], axis=2)
    hi_lo = lambda w: jnp.stack([w.astype(BF16), (w - w.astype(BF16).astype(F32)).astype(BF16)], axis=1)
    prm["nsa_w_lo"] = hi_lo(w_all[:, :half_blk].reshape(layers, cw, 2 * HEAD_DIM))
    prm["nsa_w_hi"] = hi_lo(w_all[:, half_blk:].reshape(layers, cw, 2 * HEAD_DIM))
    pe_all = jnp.concatenate([nsa_cmp_pe[:, 0], nsa_cmp_pe[:, 1]], axis=2)
    prm["nsa_pe_lo"] = pe_all[:, :half_blk].reshape(layers, 1, cw)
    prm["nsa_pe_hi"] = pe_all[:, half_blk:].reshape(layers, 1, cw)
    prm["nsa_cmp_gain"] = nsa_qk_gain[:, 1][:, None, :]
    cmp_end = jnp.arange(seq // NSA_CMP_STRIDE, dtype=jnp.int32) * NSA_CMP_STRIDE + (NSA_CMP_BLOCK - 1)
    prm["cos_c"], prm["sin_c"] = _rope_cos_sin(cmp_end, HEAD_DIM)

    prm["diff_subln_col"] = diff_subln_gain[:, :, None]
    prm["mem_gain"] = mem_norm_gain[:, None, :]
    prm["mem_w_k"] = mem_w_kv[:, :, :GROUP_WIDTH].astype(BF16)
    prm["mem_w_vt"] = jnp.swapaxes(mem_w_kv[:, :, GROUP_WIDTH:], 1, 2).astype(BF16)
    prm["mem_k_gain"] = jnp.tile(mem_qk_gain[:, 1], (1, N_HEADS))[:, None, :]
    return prm


def _layer(x, mem, layer, prm, tabs_t, diff_lambda):
    b, seq, d = x.shape
    u = _in_proj(x, layer, prm, tabs_t)
    y_nsa = _nsa_call(u["nsa_qT"], u["nsa_kcvc"], u["nsa_k2"], u["nsa_v2T"], u["nsa_gl"], u["nsa_z"], layer, prm)
    lambda_init = 0.8 - 0.6 * math.exp(-0.3 * layer)
    y_diff = _diff_call(u["diff_qT"], u["diff_k"], u["diff_vT"], u["diff_z"], layer, diff_lambda,
                        prm["diff_subln_col"], lambda_init)
    y_mla, y_mem = _mla_mem_call(u["mla_qT"], u["mla_k"], u["mla_vT"], u["mla_z"],
                                 u["mem_qT"], mem, u["mem_z"], layer, prm)
    ys = [y.reshape(b * seq, GROUP_WIDTH) for y in (y_nsa, y_diff, y_mla, y_mem)]
    return _out_proj(x.reshape(b * seq, d), ys, layer, prm["w_out"]).reshape(b, seq, d)


def kernel(x, mem, norm_gain, w_in, w_out, nsa_qk_gain, nsa_cmp_pe, nsa_w_cmp, diff_qk_gain, diff_lambda,
           diff_subln_gain, mla_cq_gain, mla_ckv_gain, mla_w_uq, mla_w_ukv, mla_qk_gain, mem_norm_gain,
           mem_w_kv, mem_qk_gain):
    seq = x.shape[1]
    pos = jnp.arange(seq, dtype=jnp.int32)
    cos32, sin32 = _rope_cos_sin(pos, DIFF_D)
    cos64, sin64 = _rope_cos_sin(pos, HEAD_DIM)
    tabs_t = (cos32.T, sin32.T, cos64.T, sin64.T)
    prm = _prepare_params(seq, norm_gain, w_in, w_out, nsa_qk_gain, nsa_cmp_pe, nsa_w_cmp, diff_qk_gain,
                          diff_subln_gain, mla_cq_gain, mla_ckv_gain, mla_w_uq, mla_w_ukv, mla_qk_gain,
                          mem_norm_gain, mem_w_kv, mem_qk_gain)
    for l in range(DEPTH):
        x = _layer(x, mem, l, prm, tabs_t, diff_lambda)
    return x
```

```python
import functools
import math

import numpy as np
import jax
import jax.numpy as jnp
from jax import lax
from jax.experimental import pallas as pl
from jax.experimental.pallas import tpu as pltpu

F32 = jnp.float32
BF16 = jnp.bfloat16

D_MODEL = 1024
DEPTH = 2
N_HEADS = 4
HEAD_DIM = 64
GROUP_WIDTH = N_HEADS * HEAD_DIM
ROPE_THETA = 10000.0
EPS = 1e-6
NEG_INF = -1e30
BIG = 1e30
LOG2E = 1.4426950408889634

NSA_CMP_BLOCK = 32
NSA_CMP_STRIDE = 16
NSA_SLC_BLOCK = 64
NSA_N_SELECT = 16
NSA_N_LOCAL = 2
NSA_WINDOW = 512
DIFF_D = HEAD_DIM // 2
MLA_Q_RANK = 256
MLA_KV_RANK = 128
MLA_NOPE = 64
MLA_ROPE = 32
MLA_QK = MLA_NOPE + MLA_ROPE

VMEM_LIMIT_BYTES = 48 * 1024 * 1024
LANES = 128
KEY_TILE = 256
VT_ROWS = 80


def _cparams(sem):
    return pltpu.CompilerParams(dimension_semantics=sem, vmem_limit_bytes=VMEM_LIMIT_BYTES)


def _layer_spec(layer, *shape):
    return pl.BlockSpec((None,) + shape, lambda *_: (layer,) + (0,) * len(shape))


def _group_ones(width, group):
    g = np.arange(width) // group
    return jnp.asarray(g[:, None] == g[None, :], dtype=BF16)


def _group_rsqrt(x, gmat, denom):
    sq = x * x
    hi = sq.astype(BF16)
    lo = (sq - hi.astype(F32)).astype(BF16)
    ss = jnp.dot(hi, gmat, preferred_element_type=F32) + jnp.dot(lo, gmat, preferred_element_type=F32)
    return lax.rsqrt(ss / denom + EPS)


def _norm_rope_t(u_t, groups, dim, gain, cos=None, sin=None):
    x = u_t.reshape(groups, dim, u_t.shape[-1])
    ms = jnp.mean(x * x, axis=1, keepdims=True)
    x = x * lax.rsqrt(ms + EPS) * gain
    if cos is not None:
        half = dim // 2
        x1, x2 = x[:, :half], x[:, half:]
        x = jnp.concatenate([x1 * cos - x2 * sin, x2 * cos + x1 * sin], axis=1)
    return x.reshape(groups * dim, u_t.shape[-1])


def _silu(z):
    z = z.astype(F32)
    return z * jax.nn.sigmoid(z)


def _ones_rows(n):
    r = lax.broadcasted_iota(jnp.int32, (VT_ROWS - HEAD_DIM, n), 0)
    return jnp.where(r == 0, 1.0, 0.0).astype(BF16)


def _split_bf16(x):
    hi = x.astype(BF16)
    return hi, (x - hi.astype(F32)).astype(BF16)


def _dot_x3(a_hi, a_lo, b_hi, b_lo):
    dot = lambda a, b: jnp.dot(a, b, preferred_element_type=F32)
    return dot(a_hi, b_hi) + dot(a_hi, b_lo) + dot(a_lo, b_hi)


def _flash_step_t(s, v_t, m_ref, acc_ref, i, cols=slice(None)):
    m_old = m_ref[i:i + 1, cols]
    m_new = jnp.maximum(m_old, jnp.max(s, axis=0, keepdims=True))
    alpha = jnp.exp2(m_old - m_new)
    p = jnp.exp2(s - m_new).astype(BF16)
    acc_ref[i, :, cols] = alpha * acc_ref[i, :, cols] + jnp.dot(v_t, p, preferred_element_type=F32)
    m_ref[i:i + 1, cols] = m_new


def _flash_reset(m_ref, acc_ref):
    m_ref[...] = jnp.full(m_ref.shape, NEG_INF, F32)
    acc_ref[...] = jnp.zeros(acc_ref.shape, F32)


def _flash_out(acc_ref, i, cols=slice(None)):
    return acc_ref[i, 0:HEAD_DIM, cols] / acc_ref[i, HEAD_DIM:HEAD_DIM + 1, cols]


def _causal_flash_t(n, k_tile, q_t, v_tile, qi, tq, tk, m_ref, acc_ref, s_ref):
    _flash_reset(m_ref, acc_ref)
    n_diag = tq // tk

    all_q = slice(0, tq)

    def scores(i, off, cols):
        return jnp.dot(k_tile(i, off), q_t(i)[:, cols], preferred_element_type=F32)

    def step(off, cols, off_next, cols_next, mask):
        for i in range(n):
            s = s_ref[i, :, cols]
            if off_next is not None:
                s_ref[i, :, cols_next] = scores(i, off_next, cols_next)
            if mask is not None:
                s = jnp.where(mask, s, NEG_INF)
            _flash_step_t(s, v_tile(i, off), m_ref, acc_ref, i, cols)

    for i in range(n):
        s_ref[i] = scores(i, 0, all_q)

    def body(j, carry):
        step(pl.multiple_of(j * tk, tk), all_q, pl.multiple_of(j * tk + tk, tk), all_q, None)
        return carry

    lax.fori_loop(0, qi * n_diag, body, 0)
    krow = lax.broadcasted_iota(jnp.int32, (tk, tq), 0)
    qcol = lax.broadcasted_iota(jnp.int32, (tk, tq), 1)
    causal = krow <= qcol
    diag_cols = [slice(d * tk, tq) for d in range(n_diag)]
    for d in range(n_diag):
        off = pl.multiple_of(qi * tq + d * tk, tk)
        last = d + 1 == n_diag
        off_next = None if last else pl.multiple_of(qi * tq + (d + 1) * tk, tk)
        step(off, diag_cols[d], off_next, None if last else diag_cols[d + 1], causal[:, 0:tq - d * tk])


_ROW_SEGS = (("nsa_kcvc", 128), ("nsa_gl", 128), ("nsa_z", 256), ("diff_z", 256), ("mla_z", 256),
             ("mem_z", 256))
_COL_SEGS = (("nsa_q", 256), ("diff_q", 256), ("mem_q", 256), ("diff_k", 256), ("nsa_k2", 128),
             ("diff_v", 256), ("nsa_v2", 128), ("mla_cq", 256), ("mla_ckv", 128), ("mla_kr", 32))


def _seg_offsets(segs):
    out, off = {}, 0
    for name, w in segs:
        out[name] = (off, w)
        off += w
    return out, off


_ROW_OFF, _ROW_W = _seg_offsets(_ROW_SEGS)
_COL_OFF, _COL_W = _seg_offsets(_COL_SEGS)


_ROW_SRC = ((256, 384),
            (640, 652), (None, 116),
            (652, 908), (1676, 1932), (2348, 2604), (2860, 3116))
_COL_SRC = ((0, 256), (908, 1164), (2604, 2860),
            (1164, 1420),
            (384, 448), (512, 576),
            (1420, 1676),
            (448, 512), (576, 640),
            (1932, 2188), (2188, 2316), (2316, 2348))


def _w_split_kernel(wt_ref, wrow_o, wcol_o):
    _, layers, ct = wt_ref.shape

    for l in range(layers):
        def gather(src):
            return jnp.concatenate(
                [jnp.zeros((b, ct), F32) if a is None else wt_ref[a:b, l, :] for a, b in src], axis=0)

        wcol_o[l] = gather(_COL_SRC).astype(BF16)
        wrow_o[l] = gather(_ROW_SRC).T.astype(BF16)


def _split_w_in(w_in, ct=128):
    layers, d, d_in = w_in.shape
    w_t = jnp.transpose(w_in, (2, 0, 1))
    return pl.pallas_call(
        _w_split_kernel,
        grid=(d // ct,),
        in_specs=[pl.BlockSpec((d_in, layers, ct), lambda c: (0, 0, c))],
        out_specs=[pl.BlockSpec((layers, ct, _ROW_W), lambda c: (0, c, 0)),
                   pl.BlockSpec((layers, _COL_W, ct), lambda c: (0, 0, c))],
        out_shape=[jax.ShapeDtypeStruct((layers, d, _ROW_W), BF16),
                   jax.ShapeDtypeStruct((layers, _COL_W, d), BF16)],
        compiler_params=_cparams(("parallel",)),
        name="w_in_split",
    )(w_t)


def _mla_head_norm_t(x_t, gain):
    ms = jnp.sum(x_t * x_t, axis=0, keepdims=True) / float(MLA_QK)
    return x_t * lax.rsqrt(ms + EPS) * gain


def _rope_t(x_t, cos, sin):
    half = x_t.shape[0] // 2
    x1, x2 = x_t[:half], x_t[half:]
    return jnp.concatenate([x1 * cos - x2 * sin, x2 * cos + x1 * sin], axis=0)


def _in_proj_kernel(x_ref, g_ref, wrow_ref, wcol_ref, nqg_ref, dqg_ref, dkg_ref, k2g_ref, mqg_ref,
                    cqg_ref, ckvg_ref, wuq_ref, wuk_ref, wuv_ref, lqg_ref, lkg_ref,
                    c32_ref, s32_ref, c64_ref, s64_ref,
                    kcvc_o, gl_o, nz_o, dz_o, mz_o, ez_o, dk_o, k2_o,
                    nq_o, dq_o, mq_o, dv_o, v2_o, lq_o, lk_o, lv_o, kcvc_s):
    x = x_ref[...]
    tm = x.shape[0]
    ms = jnp.mean(x * x, axis=-1, keepdims=True)
    h = x * lax.rsqrt(ms + EPS) * g_ref[...]
    hb = h.astype(BF16)
    h_t = h.T.astype(BF16)

    def row(name):
        off, w = _ROW_OFF[name]
        return jnp.dot(hb, wrow_ref[:, off:off + w], preferred_element_type=F32)

    u_t = jnp.dot(wcol_ref[...], h_t, preferred_element_type=F32)

    def col(name):
        off, w = _COL_OFF[name]
        return u_t[off:off + w]

    c32, s32, c64, s64 = c32_ref[...], s32_ref[...], c64_ref[...], s64_ref[...]
    g3 = lambda ref, groups: ref[...].reshape(groups, -1, 1)

    rows_out = {name: row(name) for name, _ in _ROW_SEGS}

    kcvc_s[...] = rows_out["nsa_kcvc"]
    n_chunk_rows = tm // NSA_CMP_STRIDE
    for tok in range(NSA_CMP_STRIDE):
        kcvc_o[0, :, tok * LANES:(tok + 1) * LANES] = kcvc_s[pl.ds(tok, n_chunk_rows, stride=NSA_CMP_STRIDE), :]
    gl_o[...] = rows_out["nsa_gl"]
    nz_o[...] = rows_out["nsa_z"].astype(BF16)
    dz_o[...] = rows_out["diff_z"].astype(BF16)
    mz_o[...] = rows_out["mla_z"].astype(BF16)
    ez_o[...] = rows_out["mem_z"].astype(BF16)

    nq_o[0] = _norm_rope_t(col("nsa_q"), N_HEADS, HEAD_DIM, g3(nqg_ref, 1), c64, s64) * (HEAD_DIM ** -0.5 * LOG2E)
    dq = _norm_rope_t(col("diff_q"), 2 * N_HEADS, DIFF_D, g3(dqg_ref, 1), c32, s32) * (DIFF_D ** -0.5 * LOG2E)
    dq_o[0] = dq.astype(BF16)
    mq = _norm_rope_t(col("mem_q"), N_HEADS, HEAD_DIM, g3(mqg_ref, 1)) * (HEAD_DIM ** -0.5 * LOG2E)
    mq_o[0] = mq.astype(BF16)
    dk = _norm_rope_t(col("diff_k"), 2 * N_HEADS, DIFF_D, g3(dkg_ref, 1), c32, s32).T
    for mp in range(2 * N_HEADS):
        dk_o[0, mp] = dk[:, mp * DIFF_D:(mp + 1) * DIFF_D].astype(BF16)
    k2 = _norm_rope_t(col("nsa_k2"), 2, HEAD_DIM, g3(k2g_ref, 2), c64, s64).T
    for br in range(2):
        k2_o[0, br] = k2[:, br * HEAD_DIM:(br + 1) * HEAD_DIM].astype(BF16)
    ones = _ones_rows(tm)
    dv = col("diff_v")
    for hd in range(N_HEADS):
        dv_o[0, hd, 0:HEAD_DIM, :] = dv[hd * HEAD_DIM:(hd + 1) * HEAD_DIM].astype(BF16)
        dv_o[0, hd, HEAD_DIM:VT_ROWS, :] = ones
    v2 = col("nsa_v2")
    for br in range(2):
        v2_o[0, br, 0:HEAD_DIM, :] = v2[br * HEAD_DIM:(br + 1) * HEAD_DIM].astype(BF16)
        v2_o[0, br, HEAD_DIM:VT_ROWS, :] = ones

    def latent(name, gain_ref):
        c = col(name)
        return (c * lax.rsqrt(jnp.mean(c * c, axis=0, keepdims=True) + EPS) * gain_ref[...]).astype(BF16)

    qa = jnp.dot(wuq_ref[...], latent("mla_cq", cqg_ref), preferred_element_type=F32)
    ckv = latent("mla_ckv", ckvg_ref)
    kn = jnp.dot(wuk_ref[...], ckv, preferred_element_type=F32)
    lv = jnp.dot(wuv_ref[...], ckv, preferred_element_type=F32)
    rope_rows = slice(MLA_NOPE, MLA_QK)
    lqg = lqg_ref[...] * (MLA_QK ** -0.5 * LOG2E)
    kr = _rope_t(col("mla_kr"), c32, s32)
    zpad = jnp.zeros((LANES - MLA_QK, tm), F32)
    for hd in range(N_HEADS):
        q_h = qa[hd * LANES:(hd + 1) * LANES]
        q_h = jnp.concatenate([q_h[:MLA_NOPE], _rope_t(q_h[rope_rows], c32, s32), q_h[MLA_QK:]], axis=0)
        lq_o[0, hd * LANES:(hd + 1) * LANES, :] = _mla_head_norm_t(q_h, lqg).astype(BF16)
        k_h = jnp.concatenate([kn[hd * LANES:hd * LANES + MLA_NOPE], kr, zpad], axis=0)
        lk_o[0, hd] = _mla_head_norm_t(k_h, lkg_ref[...]).T.astype(BF16)
        lv_o[0, hd, 0:HEAD_DIM, :] = lv[hd * HEAD_DIM:(hd + 1) * HEAD_DIM].astype(BF16)
        lv_o[0, hd, HEAD_DIM:VT_ROWS, :] = ones


def _in_proj(x, layer, prm, tabs_t, tm=512):
    b, seq, d = x.shape
    n = b * seq
    nb = seq // tm
    c32, s32, c64, s64 = tabs_t
    lay = functools.partial(_layer_spec, layer)
    gain_col = lambda j, rows: pl.BlockSpec((None, None, rows, 1), lambda i: (layer, j, 0, 0))

    rowspec = lambda w: pl.BlockSpec((tm, w), lambda i: (i, 0))
    colspec = lambda r: pl.BlockSpec((1, r, tm), lambda i: (i // nb, 0, i % nb))
    vtspec = lambda c: pl.BlockSpec((1, c, VT_ROWS, tm), lambda i: (i // nb, 0, 0, i % nb))
    const = lambda shape: pl.BlockSpec(shape, lambda i: tuple(0 for _ in shape))
    tabspec = lambda r: pl.BlockSpec((r, tm), lambda i: (0, i % nb))
    row_out = lambda w, dt: jax.ShapeDtypeStruct((n, w), dt)
    col_out = lambda r, dt: jax.ShapeDtypeStruct((b, r, seq), dt)
    vt_out = lambda c: jax.ShapeDtypeStruct((b, c, VT_ROWS, seq), BF16)
    slabspec = lambda c, w: pl.BlockSpec((1, c, tm, w), lambda i: (i // nb, 0, i % nb, 0))
    slab_out = lambda c, w: jax.ShapeDtypeStruct((b, c, seq, w), BF16)
    outs = pl.pallas_call(
        _in_proj_kernel,
        grid=(n // tm,),
        in_specs=[rowspec(d), lay(1, d), lay(d, _ROW_W), lay(_COL_W, d),
                  gain_col(0, HEAD_DIM), gain_col(1, DIFF_D), gain_col(2, DIFF_D), gain_col(3, 2 * HEAD_DIM),
                  gain_col(4, HEAD_DIM),
                  gain_col(5, MLA_Q_RANK), gain_col(6, MLA_KV_RANK),
                  lay(N_HEADS * LANES, MLA_Q_RANK), lay(N_HEADS * LANES, MLA_KV_RANK),
                  lay(GROUP_WIDTH, MLA_KV_RANK), gain_col(7, LANES), gain_col(8, LANES),
                  tabspec(DIFF_D // 2), tabspec(DIFF_D // 2), tabspec(HEAD_DIM // 2), tabspec(HEAD_DIM // 2)],
        out_specs=[pl.BlockSpec((1, tm // NSA_CMP_STRIDE, NSA_CMP_STRIDE * LANES), lambda i: (i // nb, i % nb, 0)),
                   rowspec(128), rowspec(256), rowspec(256), rowspec(256), rowspec(256),
                   slabspec(2 * N_HEADS, DIFF_D), slabspec(2, HEAD_DIM),
                   colspec(256), colspec(256), colspec(256), vtspec(N_HEADS), vtspec(2),
                   colspec(N_HEADS * LANES), slabspec(N_HEADS, LANES), vtspec(N_HEADS)],
        out_shape=[jax.ShapeDtypeStruct((b, seq // NSA_CMP_STRIDE, NSA_CMP_STRIDE * LANES), F32),
                   row_out(128, F32), row_out(256, BF16), row_out(256, BF16),
                   row_out(256, BF16), row_out(256, BF16), slab_out(2 * N_HEADS, DIFF_D), slab_out(2, HEAD_DIM),
                   col_out(256, F32), col_out(256, BF16), col_out(256, BF16), vt_out(N_HEADS), vt_out(2),
                   col_out(N_HEADS * LANES, BF16), slab_out(N_HEADS, LANES), vt_out(N_HEADS)],
        scratch_shapes=[pltpu.VMEM((tm, LANES), F32)],
        compiler_params=_cparams(("parallel",)),
        name="in_proj",
    )(x.reshape(n, d), prm["norm_gain"], prm["w_row"], prm["w_col"],
      *([prm["gain_cols"]] * 7), prm["w_uq_t"], prm["w_uk_t"], prm["w_uv_t"], prm["gain_cols"], prm["gain_cols"],
      c32, s32, c64, s64)
    names = ("nsa_kcvc", "nsa_gl", "nsa_z", "diff_z", "mla_z", "mem_z", "diff_k", "nsa_k2",
             "nsa_qT", "diff_qT", "mem_qT", "diff_vT", "nsa_v2T", "mla_qT", "mla_k", "mla_vT")
    u = dict(zip(names, outs))
    for name in names[1:6]:
        u[name] = u[name].reshape(b, seq, -1)
    return u


def _out_proj_kernel(x_ref, y0_ref, y1_ref, y2_ref, y3_ref, w_ref, o_ref):
    acc = x_ref[...]
    for g, y_ref in enumerate((y0_ref, y1_ref, y2_ref, y3_ref)):
        acc = acc + jnp.dot(y_ref[...], w_ref[g * GROUP_WIDTH:(g + 1) * GROUP_WIDTH, :],
                            preferred_element_type=F32)
    o_ref[...] = acc


def _out_proj(x2, ys, layer, w_out_b, tm=1024):
    n = x2.shape[0]
    yspec = pl.BlockSpec((tm, GROUP_WIDTH), lambda i: (i, 0))
    return pl.pallas_call(
        _out_proj_kernel,
        grid=(n // tm,),
        in_specs=[pl.BlockSpec((tm, D_MODEL), lambda i: (i, 0)), yspec, yspec, yspec, yspec,
                  _layer_spec(layer, D_MODEL, D_MODEL)],
        out_specs=pl.BlockSpec((tm, D_MODEL), lambda i: (i, 0)),
        out_shape=jax.ShapeDtypeStruct((n, D_MODEL), F32),
        compiler_params=_cparams(("parallel",)),
        name="out_proj",
    )(x2, *ys, w_out_b)


def _diff_kernel(lambda_init, t, qT_ref, k_ref, vT_ref, z_ref, lam_ref, sg_ref, y_ref, m_s, acc_s, s_s):
    qi = pl.program_id(1)
    n_maps = 2 * N_HEADS
    qT = qT_ref[0]
    _causal_flash_t(
        n_maps,
        lambda i, off: k_ref[0, i, pl.ds(off, KEY_TILE), :],
        lambda i: qT[i * DIFF_D:(i + 1) * DIFF_D],
        lambda i, off: vT_ref[0, i // 2, :, pl.ds(off, KEY_TILE)],
        qi, t, KEY_TILE, m_s, acc_s, s_s)

    lam = lam_ref[...]
    lmbda = (jnp.exp(jnp.sum(lam[0:1] * lam[1:2], axis=-1, keepdims=True))
             - jnp.exp(jnp.sum(lam[2:3] * lam[3:4], axis=-1, keepdims=True)) + lambda_init)
    outs = []
    for h in range(N_HEADS):
        d = _flash_out(acc_s, 2 * h) - lmbda * _flash_out(acc_s, 2 * h + 1)
        ms = jnp.mean(d * d, axis=0, keepdims=True)
        outs.append(d * lax.rsqrt(ms + EPS) * sg_ref[...] * (1.0 - lambda_init))
    y = jnp.concatenate(outs, axis=0).T * _silu(z_ref[0])
    y_ref[0] = y.astype(BF16)


def _diff_call(qT, k, vT, z, layer, lam, subln_col, lambda_init, t=1024):
    b, _, seq, _ = k.shape
    const = lambda shape: pl.BlockSpec(shape, lambda bi, qi: tuple(0 for _ in shape))
    tile = pl.BlockSpec((1, t, GROUP_WIDTH), lambda bi, qi: (bi, qi, 0))
    return pl.pallas_call(
        functools.partial(_diff_kernel, lambda_init, t),
        grid=(b, seq // t),
        in_specs=[pl.BlockSpec((1, GROUP_WIDTH, t), lambda bi, qi: (bi, 0, qi)),
                  pl.BlockSpec((1, 2 * N_HEADS, seq, DIFF_D), lambda bi, qi: (bi, 0, 0, 0)),
                  pl.BlockSpec((1, N_HEADS, VT_ROWS, seq), lambda bi, qi: (bi, 0, 0, 0)),
                  tile, _layer_spec(layer, 4, DIFF_D), _layer_spec(layer, HEAD_DIM, 1)],
        out_specs=tile,
        out_shape=jax.ShapeDtypeStruct((b, seq, GROUP_WIDTH), BF16),
        scratch_shapes=[pltpu.VMEM((2 * N_HEADS, t), F32),
                        pltpu.VMEM((2 * N_HEADS, VT_ROWS, t), F32),
                        pltpu.VMEM((2 * N_HEADS, KEY_TILE, t), F32)],
        compiler_params=_cparams(("parallel", "parallel")),
        name="diff_attn",
    )(qT, k, vT, z, lam, subln_col)


def _mla_mem_kernel(t, qT_ref, k_ref, vT_ref, z_ref,
                    mqT_ref, mem_ref, mz_ref, mg_ref, wk_ref, wvT_ref, kg_ref, gm_ref,
                    y_ref, ymem_ref, m_s, acc_s, s_s, mk_s, mvT_s, mm_s, macc_s):
    qi = pl.program_id(1)
    nt = (((1,), (1,)), ((), ()))
    m_len = mem_ref.shape[1]

    @pl.when(qi == 0)
    def _prep_memory_kv():
        mem = mem_ref[0]
        ms = jnp.mean(mem * mem, axis=-1, keepdims=True)
        mb = (mem * lax.rsqrt(ms + EPS) * mg_ref[...]).astype(BF16)
        k = jnp.dot(mb, wk_ref[...], preferred_element_type=F32)
        kn = k * _group_rsqrt(k, gm_ref[...], float(HEAD_DIM)) * kg_ref[...]
        vT = lax.dot_general(wvT_ref[...], mb, nt, preferred_element_type=F32)
        for h in range(N_HEADS):
            mk_s[h] = kn[:, h * HEAD_DIM:(h + 1) * HEAD_DIM].astype(BF16)
            mvT_s[h, 0:HEAD_DIM, :] = vT[h * HEAD_DIM:(h + 1) * HEAD_DIM].astype(BF16)
            mvT_s[h, HEAD_DIM:VT_ROWS, :] = _ones_rows(m_len)

    mqT = mqT_ref[0]
    _flash_reset(mm_s, macc_s)
    ss = [jnp.dot(mk_s[h], mqT[h * HEAD_DIM:(h + 1) * HEAD_DIM], preferred_element_type=F32)
          for h in range(N_HEADS)]
    for h in range(N_HEADS):
        _flash_step_t(ss[h], mvT_s[h], mm_s, macc_s, h)

    qT = qT_ref[0]
    _causal_flash_t(
        N_HEADS,
        lambda h, off: k_ref[0, h, pl.ds(off, KEY_TILE), :],
        lambda h: qT[h * LANES:(h + 1) * LANES],
        lambda h, off: vT_ref[0, h, :, pl.ds(off, KEY_TILE)],
        qi, t, KEY_TILE, m_s, acc_s, s_s)

    y_mem = jnp.concatenate([_flash_out(macc_s, h) for h in range(N_HEADS)], axis=0).T * _silu(mz_ref[0])
    ymem_ref[0] = y_mem.astype(BF16)
    y = jnp.concatenate([_flash_out(acc_s, h) for h in range(N_HEADS)], axis=0).T * _silu(z_ref[0])
    y_ref[0] = y.astype(BF16)


def _mla_mem_call(qT, k, vT, z, mem_qT, mem, mem_z, layer, prm, t=1024):
    b, _, seq, _ = k.shape
    m_len = mem.shape[1]
    lay = functools.partial(_layer_spec, layer)
    tile = pl.BlockSpec((1, t, GROUP_WIDTH), lambda bi, qi: (bi, qi, 0))
    const = lambda shape: pl.BlockSpec(shape, lambda bi, qi: tuple(0 for _ in shape))
    out = jax.ShapeDtypeStruct((b, seq, GROUP_WIDTH), BF16)
    return pl.pallas_call(
        functools.partial(_mla_mem_kernel, t),
        grid=(b, seq // t),
        in_specs=[pl.BlockSpec((1, N_HEADS * LANES, t), lambda bi, qi: (bi, 0, qi)),
                  pl.BlockSpec((1, N_HEADS, seq, LANES), lambda bi, qi: (bi, 0, 0, 0)),
                  pl.BlockSpec((1, N_HEADS, VT_ROWS, seq), lambda bi, qi: (bi, 0, 0, 0)),
                  tile,
                  pl.BlockSpec((1, GROUP_WIDTH, t), lambda bi, qi: (bi, 0, qi)),
                  pl.BlockSpec((1, m_len, D_MODEL), lambda bi, qi: (bi, 0, 0)), tile,
                  lay(1, D_MODEL), lay(D_MODEL, GROUP_WIDTH), lay(GROUP_WIDTH, D_MODEL),
                  lay(1, GROUP_WIDTH), const((GROUP_WIDTH, GROUP_WIDTH))],
        out_specs=[tile, tile],
        out_shape=[out, out],
        scratch_shapes=[pltpu.VMEM((N_HEADS, t), F32),
                        pltpu.VMEM((N_HEADS, VT_ROWS, t), F32),
                        pltpu.VMEM((N_HEADS, KEY_TILE, t), F32),
                        pltpu.VMEM((N_HEADS, m_len, HEAD_DIM), BF16),
                        pltpu.VMEM((N_HEADS, VT_ROWS, m_len), BF16),
                        pltpu.VMEM((N_HEADS, t), F32),
                        pltpu.VMEM((N_HEADS, VT_ROWS, t), F32)],
        compiler_params=_cparams(("parallel", "arbitrary")),
        name="mla_mem_attn",
    )(qT, k, vT, z, mem_qT, mem, mem_z, prm["mem_gain"], prm["mem_w_k"], prm["mem_w_vt"],
      prm["mem_k_gain"], _group_ones(GROUP_WIDTH, HEAD_DIM))


def _nsa_kernel(seq, t,
                qT_ref, kcvc_ref, k2_ref, v2T_ref, gl_ref, z_ref,
                cg_ref, pelo_ref, pehi_ref, wlo_ref, whi_ref, cosc_ref, sinc_ref,
                y_ref, kc_s, vcT_s, bias_s, m_s, acc_s, s_s, pg_s):
    qi = pl.program_id(1)
    half = HEAD_DIM // 2
    n_chunk = seq // NSA_CMP_STRIDE
    n_cmp = n_chunk - 1
    n_blk = seq // NSA_SLC_BLOCK
    blk_per_tile = t // NSA_SLC_BLOCK
    win_tiles = NSA_WINDOW // t
    w4 = N_HEADS * t
    SLC, WIN = 0, 1

    @pl.when(qi == 0)
    def _prep():
        ch = kcvc_ref[0]
        a = _dot_x3(*_split_bf16(ch + pelo_ref[...]), wlo_ref[0], wlo_ref[1])
        bm = _dot_x3(*_split_bf16(ch + pehi_ref[...]), whi_ref[0], whi_ref[1])
        cmp = a + pltpu.roll(bm, n_chunk - 1, 0)
        kc = cmp[:, :HEAD_DIM]
        ms = jnp.mean(kc * kc, axis=-1, keepdims=True)
        kc = kc * lax.rsqrt(ms + EPS) * cg_ref[...]
        x1, x2 = kc[:, :half], kc[:, half:]
        c, s = cosc_ref[...], sinc_ref[...]
        kc_hi, kc_lo = _split_bf16(jnp.concatenate([x1 * c - x2 * s, x2 * c + x1 * s], axis=-1))
        kc_s[0] = kc_hi
        kc_s[1] = kc_lo
        vcT_s[...] = cmp.T[HEAD_DIM:].astype(BF16)
        pg_s[:, n_chunk:, :] = jnp.zeros((t // LANES, pg_s.shape[1] - n_chunk, LANES), F32)

    qs = qi * t
    qT = qT_ref[0]
    q4f = jnp.concatenate([qT[h * HEAD_DIM:(h + 1) * HEAD_DIM] for h in range(N_HEADS)], axis=1)
    q4 = q4f.astype(BF16)

    def scores(br, j):
        off = pl.multiple_of(j * t, t)
        return jnp.dot(k2_ref[0, br, pl.ds(off, t), :], q4, preferred_element_type=F32)

    s_s[WIN] = scores(WIN, jnp.maximum(qi - win_tiles, 0))
    qcol = lax.broadcasted_iota(jnp.int32, (1, w4), 1) & (t - 1)
    pos_c = qs + qcol

    q_lo = (q4f - q4.astype(F32)).astype(BF16)
    heads = [slice(h * t, (h + 1) * t) for h in range(N_HEADS)]
    scs = [_dot_x3(kc_s[0], kc_s[1], q4[:, c], q_lo[:, c]) for c in heads]
    n_idx = lax.broadcasted_iota(jnp.int32, (n_chunk, 1), 0)
    cvalid = (n_idx * NSA_CMP_STRIDE + (NSA_CMP_BLOCK - 1) <= pos_c[:, :t]) & (n_idx < n_cmp)
    o_cmp, pg = [], None
    for h in range(N_HEADS):
        sc = jnp.where(cvalid, scs[h], NEG_INF)
        e = jnp.exp2(sc - jnp.max(sc, axis=0, keepdims=True))
        p = jnp.where(cvalid, e / jnp.sum(e, axis=0, keepdims=True), 0.0)
        o_cmp.append(jnp.dot(vcT_s[...], p.astype(BF16), preferred_element_type=F32))
        pg = p if pg is None else pg + p

    for c0 in range(t // LANES):
        pg_s[c0, 0:n_chunk, :] = pg[:, c0 * LANES:(c0 + 1) * LANES]
    ratio = NSA_SLC_BLOCK // NSA_CMP_STRIDE
    coef = np.convolve(np.ones(ratio), np.ones(NSA_CMP_BLOCK // NSA_CMP_STRIDE))
    p_slc = None
    for i, c in enumerate(coef):
        term = float(c) * jnp.concatenate(
            [pg_s[c0, pl.ds(i, n_blk, stride=ratio), :] for c0 in range(t // LANES)], axis=1)
        p_slc = term if p_slc is None else p_slc + term
    blk = lax.broadcasted_iota(jnp.int32, (n_blk, 1), 0)
    cur = lax.shift_right_logical(pos_c[:, :t], NSA_SLC_BLOCK.bit_length() - 1)
    forced = (blk == 0) | ((blk <= cur) & (blk > cur - NSA_N_LOCAL))
    score = jnp.where(blk > cur, NEG_INF, jnp.where(forced, BIG, p_slc))
    sub = 8
    groups = [score[g * sub:(g + 1) * sub] for g in range(n_blk // sub)]
    cnts = [jnp.zeros((sub, t), F32) for _ in groups]
    for i in range(n_blk):
        ri = score[i:i + 1, :]
        for g, sg in enumerate(groups):
            if g < i // sub:
                beats = ri > sg
            elif g > i // sub:
                beats = ri >= sg
            else:
                beats = (ri > sg) | ((ri == sg) & (blk[g * sub:(g + 1) * sub] > i))
            cnts[g] = cnts[g] + jnp.where(beats, 1.0, 0.0)
    cnt = jnp.concatenate(cnts, axis=0)
    bias = jnp.where(cnt < float(NSA_N_SELECT), 0.0, NEG_INF)
    bias = jnp.concatenate([bias] * N_HEADS, axis=1)
    for r in range(n_blk // blk_per_tile):
        bias_s[r] = bias[r * blk_per_tile:(r + 1) * blk_per_tile, :]

    _flash_reset(m_s, acc_s)
    krow = lax.broadcasted_iota(jnp.int32, (t, 1), 0)
    causal = krow <= qcol
    beyond = krow > qcol

    def slc_scores(j):
        s = scores(SLC, j).reshape(blk_per_tile, NSA_SLC_BLOCK, w4) + bias_s[j][:, None, :]
        return s.reshape(t, w4)

    def flash(br, s, j):
        off = pl.multiple_of(j * t, t)
        _flash_step_t(s, v2T_ref[0, br, :, pl.ds(off, t)], m_s, acc_s, br)

    s_s[SLC] = slc_scores(0)

    def old_body(j, carry):
        s = s_s[SLC]
        s_s[SLC] = slc_scores(j + 1)
        flash(SLC, s, j)
        return carry

    lax.fori_loop(0, jnp.maximum(qi - win_tiles, 0), old_body, 0)

    def win_step(j, win_mask):
        s_slc = s_s[SLC]
        s_s[SLC] = slc_scores(j + 1)
        flash(SLC, s_slc, j)
        s_win = s_s[WIN]
        if win_mask is not None:
            s_win = jnp.where(win_mask, s_win, NEG_INF)
        s_s[WIN] = scores(WIN, j + 1)
        flash(WIN, s_win, j)

    @pl.when(qi >= win_tiles)
    def _oldest_window_tile():
        win_step(qi - win_tiles, beyond)

    def win_body(j, carry):
        win_step(j, None)
        return carry

    lax.fori_loop(jnp.maximum(qi - win_tiles + 1, 0), qi, win_body, 0)
    flash(SLC, jnp.where(causal, s_s[SLC], NEG_INF), qi)
    flash(WIN, jnp.where(causal, s_s[WIN], NEG_INF), qi)

    g = jax.nn.sigmoid(gl_ref[0]).T
    outs = []
    for h in range(N_HEADS):
        cols = slice(h * t, (h + 1) * t)
        o_s = _flash_out(acc_s, SLC, cols)
        o_w = _flash_out(acc_s, WIN, cols)
        outs.append(g[h:h + 1, :] * o_cmp[h] + g[N_HEADS + h:N_HEADS + h + 1, :] * o_s
                    + g[2 * N_HEADS + h:2 * N_HEADS + h + 1, :] * o_w)
    y = jnp.concatenate(outs, axis=0).T * _silu(z_ref[0])
    y_ref[0] = y.astype(BF16)


def _nsa_call(qT, kcvc, k2, v2T, gl, z, layer, prm, t=256):
    b, _, seq, _ = k2.shape
    n_chunk = seq // NSA_CMP_STRIDE
    cw = NSA_CMP_STRIDE * 2 * HEAD_DIM
    assert t % NSA_SLC_BLOCK == 0 and NSA_WINDOW % t == 0 and t & (t - 1) == 0
    lay = functools.partial(_layer_spec, layer)

    tile = pl.BlockSpec((1, t, GROUP_WIDTH), lambda bi, qi: (bi, qi, 0))
    const = lambda shape: pl.BlockSpec(shape, lambda bi, qi: tuple(0 for _ in shape))
    w4 = N_HEADS * t
    return pl.pallas_call(
        functools.partial(_nsa_kernel, seq, t),
        grid=(b, seq // t),
        in_specs=[pl.BlockSpec((1, GROUP_WIDTH, t), lambda bi, qi: (bi, 0, qi)),
                  pl.BlockSpec((1, n_chunk, cw), lambda bi, qi: (bi, 0, 0)),
                  pl.BlockSpec((1, 2, seq, HEAD_DIM), lambda bi, qi: (bi, 0, 0, 0)),
                  pl.BlockSpec((1, 2, VT_ROWS, seq), lambda bi, qi: (bi, 0, 0, 0)),
                  pl.BlockSpec((1, t, LANES), lambda bi, qi: (bi, qi, 0)),
                  tile,
                  lay(1, HEAD_DIM),
                  lay(1, cw), lay(1, cw), lay(2, cw, 2 * HEAD_DIM), lay(2, cw, 2 * HEAD_DIM),
                  const((n_chunk, HEAD_DIM // 2)), const((n_chunk, HEAD_DIM // 2))],
        out_specs=tile,
        out_shape=jax.ShapeDtypeStruct((b, seq, GROUP_WIDTH), BF16),
        scratch_shapes=[pltpu.VMEM((2, n_chunk, HEAD_DIM), BF16),
                        pltpu.VMEM((HEAD_DIM, n_chunk), BF16),
                        pltpu.VMEM((seq // t, t // NSA_SLC_BLOCK, w4), F32),
                        pltpu.VMEM((2, w4), F32),
                        pltpu.VMEM((2, VT_ROWS, w4), F32),
                        pltpu.VMEM((2, t, w4), F32),
                        pltpu.VMEM((t // LANES, n_chunk + 8, LANES), F32)],
        compiler_params=_cparams(("parallel", "arbitrary")),
        name="nsa_attn",
    )(qT, kcvc, k2, v2T, gl, z,
      prm["nsa_cmp_gain"], prm["nsa_pe_lo"], prm["nsa_pe_hi"], prm["nsa_w_lo"], prm["nsa_w_hi"],
      prm["cos_c"], prm["sin_c"])


def _rope_cos_sin(pos, dim):
    half = dim // 2
    inv_freq = ROPE_THETA ** (-jnp.arange(half, dtype=F32) / half)
    ang = pos.astype(F32)[:, None] * inv_freq[None, :]
    return jnp.cos(ang), jnp.sin(ang)


def _prepare_params(seq, norm_gain, w_in, w_out, nsa_qk_gain, nsa_cmp_pe, nsa_w_cmp, diff_qk_gain, diff_subln_gain,
                    mla_cq_gain, mla_ckv_gain, mla_w_uq, mla_w_ukv, mla_qk_gain, mem_norm_gain, mem_w_kv, mem_qk_gain):
    layers = w_in.shape[0]
    prm = {}
    prm["w_row"], prm["w_col"] = _split_w_in(w_in)
    prm["w_out"] = w_out.astype(BF16)
    prm["norm_gain"] = norm_gain[:, None, :]

    col = lambda a: jnp.pad(a, ((0, 0), (0, GROUP_WIDTH - a.shape[1])))
    prm["gain_cols"] = jnp.stack([
        col(nsa_qk_gain[:, 0]), col(diff_qk_gain[:, 0]), col(diff_qk_gain[:, 1]),
        col(jnp.concatenate([nsa_qk_gain[:, 2], nsa_qk_gain[:, 3]], axis=1)), col(mem_qk_gain[:, 0]),
        col(mla_cq_gain), col(mla_ckv_gain), col(mla_qk_gain[:, 0]), col(mla_qk_gain[:, 1])], axis=1)[..., None]

    npad = LANES - MLA_QK
    uq = jnp.pad(mla_w_uq.reshape(layers, MLA_Q_RANK, N_HEADS, MLA_QK), ((0, 0), (0, 0), (0, 0), (0, npad)))
    prm["w_uq_t"] = jnp.swapaxes(uq.reshape(layers, MLA_Q_RANK, N_HEADS * LANES), 1, 2).astype(BF16)
    ukv = mla_w_ukv.reshape(layers, MLA_KV_RANK, N_HEADS, MLA_NOPE + HEAD_DIM)
    uk = jnp.pad(ukv[..., :MLA_NOPE], ((0, 0), (0, 0), (0, 0), (0, LANES - MLA_NOPE)))
    prm["w_uk_t"] = jnp.swapaxes(uk.reshape(layers, MLA_KV_RANK, N_HEADS * LANES), 1, 2).astype(BF16)
    uv = ukv[..., MLA_NOPE:].reshape(layers, MLA_KV_RANK, GROUP_WIDTH)
    prm["w_uv_t"] = jnp.swapaxes(uv, 1, 2).astype(BF16)

    half_blk = NSA_CMP_BLOCK // 2
    cw = half_blk * 2 * HEAD_DIM
    wk = nsa_w_cmp[:, 0].reshape(layers, NSA_CMP_BLOCK, HEAD_DIM, HEAD_DIM)
    wv = nsa_w_cmp[:, 1].reshape(layers, NSA_CMP_BLOCK, HEAD_DIM, HEAD_DIM)
    zero = jnp.zeros_like(wk)
    w_all = jnp.concatenate([jnp.concatenate([wk, zero], axis=3),
                             jnp.concatenate([zero, wv], axis=3)], axis=2)
    hi_lo = lambda w: jnp.stack([w.astype(BF16), (w - w.astype(BF16).astype(F32)).astype(BF16)], axis=1)
    prm["nsa_w_lo"] = hi_lo(w_all[:, :half_blk].reshape(layers, cw, 2 * HEAD_DIM))
    prm["nsa_w_hi"] = hi_lo(w_all[:, half_blk:].reshape(layers, cw, 2 * HEAD_DIM))
    pe_all = jnp.concatenate([nsa_cmp_pe[:, 0], nsa_cmp_pe[:, 1]], axis=2)
    prm["nsa_pe_lo"] = pe_all[:, :half_blk].reshape(layers, 1, cw)
    prm["nsa_pe_hi"] = pe_all[:, half_blk:].reshape(layers, 1, cw)
    prm["nsa_cmp_gain"] = nsa_qk_gain[:, 1][:, None, :]
    cmp_end = jnp.arange(seq // NSA_CMP_STRIDE, dtype=jnp.int32) * NSA_CMP_STRIDE + (NSA_CMP_BLOCK - 1)
    prm["cos_c"], prm["sin_c"] = _rope_cos_sin(cmp_end, HEAD_DIM)

    prm["diff_subln_col"] = diff_subln_gain[:, :, None]
    prm["mem_gain"] = mem_norm_gain[:, None, :]
    prm["mem_w_k"] = mem_w_kv[:, :, :GROUP_WIDTH].astype(BF16)
    prm["mem_w_vt"] = jnp.swapaxes(mem_w_kv[:, :, GROUP_WIDTH:], 1, 2).astype(BF16)
    prm["mem_k_gain"] = jnp.tile(mem_qk_gain[:, 1], (1, N_HEADS))[:, None, :]
    return prm


def _layer(x, mem, layer, prm, tabs_t, diff_lambda):
    b, seq, d = x.shape
    u = _in_proj(x, layer, prm, tabs_t)
    y_nsa = _nsa_call(u["nsa_qT"], u["nsa_kcvc"], u["nsa_k2"], u["nsa_v2T"], u["nsa_gl"], u["nsa_z"], layer, prm)
    lambda_init = 0.8 - 0.6 * math.exp(-0.3 * layer)
    y_diff = _diff_call(u["diff_qT"], u["diff_k"], u["diff_vT"], u["diff_z"], layer, diff_lambda,
                        prm["diff_subln_col"], lambda_init)
    y_mla, y_mem = _mla_mem_call(u["mla_qT"], u["mla_k"], u["mla_vT"], u["mla_z"],
                                 u["mem_qT"], mem, u["mem_z"], layer, prm)
    ys = [y.reshape(b * seq, GROUP_WIDTH) for y in (y_nsa, y_diff, y_mla, y_mem)]
    return _out_proj(x.reshape(b * seq, d), ys, layer, prm["w_out"]).reshape(b, seq, d)


def kernel(x, mem, norm_gain, w_in, w_out, nsa_qk_gain, nsa_cmp_pe, nsa_w_cmp, diff_qk_gain, diff_lambda,
           diff_subln_gain, mla_cq_gain, mla_ckv_gain, mla_w_uq, mla_w_ukv, mla_qk_gain, mem_norm_gain,
           mem_w_kv, mem_qk_gain):
    seq = x.shape[1]
    pos = jnp.arange(seq, dtype=jnp.int32)
    cos32, sin32 = _rope_cos_sin(pos, DIFF_D)
    cos64, sin64 = _rope_cos_sin(pos, HEAD_DIM)
    tabs_t = (cos32.T, sin32.T, cos64.T, sin64.T)
    prm = _prepare_params(seq, norm_gain, w_in, w_out, nsa_qk_gain, nsa_cmp_pe, nsa_w_cmp, diff_qk_gain,
                          diff_subln_gain, mla_cq_gain, mla_ckv_gain, mla_w_uq, mla_w_ukv, mla_qk_gain,
                          mem_norm_gain, mem_w_kv, mem_qk_gain)
    for l in range(DEPTH):
        x = _layer(x, mem, l, prm, tabs_t, diff_lambda)
    return x
```

```python
import functools
import math

import numpy as np
import jax
import jax.numpy as jnp
from jax import lax
from jax.experimental import pallas as pl
from jax.experimental.pallas import tpu as pltpu

F32 = jnp.float32
BF16 = jnp.bfloat16

D_MODEL = 1024
DEPTH = 2
N_HEADS = 4
HEAD_DIM = 64
GROUP_WIDTH = N_HEADS * HEAD_DIM
ROPE_THETA = 10000.0
EPS = 1e-6
NEG_INF = -1e30
BIG = 1e30
LOG2E = 1.4426950408889634

NSA_CMP_BLOCK = 32
NSA_CMP_STRIDE = 16
NSA_SLC_BLOCK = 64
NSA_N_SELECT = 16
NSA_N_LOCAL = 2
NSA_WINDOW = 512
DIFF_D = HEAD_DIM // 2
MLA_Q_RANK = 256
MLA_KV_RANK = 128
MLA_NOPE = 64
MLA_ROPE = 32
MLA_QK = MLA_NOPE + MLA_ROPE

VMEM_LIMIT_BYTES = 48 * 1024 * 1024
LANES = 128
KEY_TILE = 256
VT_ROWS = 80


def _cparams(sem):
    return pltpu.CompilerParams(dimension_semantics=sem, vmem_limit_bytes=VMEM_LIMIT_BYTES)


def _layer_spec(layer, *shape):
    return pl.BlockSpec((None,) + shape, lambda *_: (layer,) + (0,) * len(shape))


def _group_ones(width, group):
    g = np.arange(width) // group
    return jnp.asarray(g[:, None] == g[None, :], dtype=BF16)


def _group_rsqrt(x, gmat, denom):
    sq = x * x
    hi = sq.astype(BF16)
    lo = (sq - hi.astype(F32)).astype(BF16)
    ss = jnp.dot(hi, gmat, preferred_element_type=F32) + jnp.dot(lo, gmat, preferred_element_type=F32)
    return lax.rsqrt(ss / denom + EPS)


def _norm_rope_t(u_t, groups, dim, gain, cos=None, sin=None):
    x = u_t.reshape(groups, dim, u_t.shape[-1])
    ms = jnp.mean(x * x, axis=1, keepdims=True)
    x = x * lax.rsqrt(ms + EPS) * gain
    if cos is not None:
        half = dim // 2
        x1, x2 = x[:, :half], x[:, half:]
        x = jnp.concatenate([x1 * cos - x2 * sin, x2 * cos + x1 * sin], axis=1)
    return x.reshape(groups * dim, u_t.shape[-1])


def _silu(z):
    z = z.astype(F32)
    return z * jax.nn.sigmoid(z)


def _ones_rows(n):
    r = lax.broadcasted_iota(jnp.int32, (VT_ROWS - HEAD_DIM, n), 0)
    return jnp.where(r == 0, 1.0, 0.0).astype(BF16)


def _split_bf16(x):
    hi = x.astype(BF16)
    return hi, (x - hi.astype(F32)).astype(BF16)


def _dot_x3(a_hi, a_lo, b_hi, b_lo):
    dot = lambda a, b: jnp.dot(a, b, preferred_element_type=F32)
    return dot(a_hi, b_hi) + dot(a_hi, b_lo) + dot(a_lo, b_hi)


def _flash_step_t(s, v_t, m_ref, acc_ref, i, cols=slice(None)):
    m_old = m_ref[i:i + 1, cols]
    m_new = jnp.maximum(m_old, jnp.max(s, axis=0, keepdims=True))
    alpha = jnp.exp2(m_old - m_new)
    p = jnp.exp2(s - m_new).astype(BF16)
    acc_ref[i, :, cols] = alpha * acc_ref[i, :, cols] + jnp.dot(v_t, p, preferred_element_type=F32)
    m_ref[i:i + 1, cols] = m_new


def _flash_reset(m_ref, acc_ref):
    m_ref[...] = jnp.full(m_ref.shape, NEG_INF, F32)
    acc_ref[...] = jnp.zeros(acc_ref.shape, F32)


def _flash_out(acc_ref, i, cols=slice(None)):
    return acc_ref[i, 0:HEAD_DIM, cols] / acc_ref[i, HEAD_DIM:HEAD_DIM + 1, cols]


def _causal_flash_t(n, k_tile, q_t, v_tile, qi, tq, tk, m_ref, acc_ref, s_ref):
    _flash_reset(m_ref, acc_ref)
    n_diag = tq // tk

    all_q = slice(0, tq)

    def scores(i, off, cols):
        return jnp.dot(k_tile(i, off), q_t(i)[:, cols], preferred_element_type=F32)

    def step(off, cols, off_next, cols_next, mask):
        for i in range(n):
            s = s_ref[i, :, cols]
            if off_next is not None:
                s_ref[i, :, cols_next] = scores(i, off_next, cols_next)
            if mask is not None:
                s = jnp.where(mask, s, NEG_INF)
            _flash_step_t(s, v_tile(i, off), m_ref, acc_ref, i, cols)

    for i in range(n):
        s_ref[i] = scores(i, 0, all_q)

    def body(j, carry):
        step(pl.multiple_of(j * tk, tk), all_q, pl.multiple_of(j * tk + tk, tk), all_q, None)
        return carry

    lax.fori_loop(0, qi * n_diag, body, 0)
    krow = lax.broadcasted_iota(jnp.int32, (tk, tq), 0)
    qcol = lax.broadcasted_iota(jnp.int32, (tk, tq), 1)
    causal = krow <= qcol
    diag_cols = [slice(d * tk, tq) for d in range(n_diag)]
    for d in range(n_diag):
        off = pl.multiple_of(qi * tq + d * tk, tk)
        last = d + 1 == n_diag
        off_next = None if last else pl.multiple_of(qi * tq + (d + 1) * tk, tk)
        step(off, diag_cols[d], off_next, None if last else diag_cols[d + 1], causal[:, 0:tq - d * tk])


_ROW_SEGS = (("nsa_kcvc", 128), ("nsa_gl", 128), ("nsa_z", 256), ("diff_z", 256), ("mla_z", 256),
             ("mem_z", 256))
_COL_SEGS = (("nsa_q", 256), ("diff_q", 256), ("mem_q", 256), ("diff_k", 256), ("nsa_k2", 128),
             ("diff_v", 256), ("nsa_v2", 128), ("mla_cq", 256), ("mla_ckv", 128), ("mla_kr", 32))


def _seg_offsets(segs):
    out, off = {}, 0
    for name, w in segs:
        out[name] = (off, w)
        off += w
    return out, off


_ROW_OFF, _ROW_W = _seg_offsets(_ROW_SEGS)
_COL_OFF, _COL_W = _seg_offsets(_COL_SEGS)


_ROW_SRC = ((256, 384),
            (640, 652), (None, 116),
            (652, 908), (1676, 1932), (2348, 2604), (2860, 3116))
_COL_SRC = ((0, 256), (908, 1164), (2604, 2860),
            (1164, 1420),
            (384, 448), (512, 576),
            (1420, 1676),
            (448, 512), (576, 640),
            (1932, 2188), (2188, 2316), (2316, 2348))


def _w_split_kernel(wt_ref, wrow_o, wcol_o):
    _, layers, ct = wt_ref.shape

    for l in range(layers):
        def gather(src):
            return jnp.concatenate(
                [jnp.zeros((b, ct), F32) if a is None else wt_ref[a:b, l, :] for a, b in src], axis=0)

        wcol_o[l] = gather(_COL_SRC).astype(BF16)
        wrow_o[l] = gather(_ROW_SRC).T.astype(BF16)


def _split_w_in(w_in, ct=128):
    layers, d, d_in = w_in.shape
    w_t = jnp.transpose(w_in, (2, 0, 1))
    return pl.pallas_call(
        _w_split_kernel,
        grid=(d // ct,),
        in_specs=[pl.BlockSpec((d_in, layers, ct), lambda c: (0, 0, c))],
        out_specs=[pl.BlockSpec((layers, ct, _ROW_W), lambda c: (0, c, 0)),
                   pl.BlockSpec((layers, _COL_W, ct), lambda c: (0, 0, c))],
        out_shape=[jax.ShapeDtypeStruct((layers, d, _ROW_W), BF16),
                   jax.ShapeDtypeStruct((layers, _COL_W, d), BF16)],
        compiler_params=_cparams(("parallel",)),
        name="w_in_split",
    )(w_t)


def _mla_head_norm_t(x_t, gain):
    ms = jnp.sum(x_t * x_t, axis=0, keepdims=True) / float(MLA_QK)
    return x_t * lax.rsqrt(ms + EPS) * gain


def _rope_t(x_t, cos, sin):
    half = x_t.shape[0] // 2
    x1, x2 = x_t[:half], x_t[half:]
    return jnp.concatenate([x1 * cos - x2 * sin, x2 * cos + x1 * sin], axis=0)


def _in_proj_kernel(x_ref, g_ref, wrow_ref, wcol_ref, nqg_ref, dqg_ref, dkg_ref, k2g_ref, mqg_ref,
                    cqg_ref, ckvg_ref, wuq_ref, wuk_ref, wuv_ref, lqg_ref, lkg_ref,
                    c32_ref, s32_ref, c64_ref, s64_ref,
                    kcvc_o, gl_o, nz_o, dz_o, mz_o, ez_o, dk_o, k2_o,
                    nq_o, dq_o, mq_o, dv_o, v2_o, lq_o, lk_o, lv_o, kcvc_s):
    x = x_ref[...]
    tm = x.shape[0]
    ms = jnp.mean(x * x, axis=-1, keepdims=True)
    h = x * lax.rsqrt(ms + EPS) * g_ref[...]
    hb = h.astype(BF16)
    h_t = h.T.astype(BF16)

    def row(name):
        off, w = _ROW_OFF[name]
        return jnp.dot(hb, wrow_ref[:, off:off + w], preferred_element_type=F32)

    u_t = jnp.dot(wcol_ref[...], h_t, preferred_element_type=F32)

    def col(name):
        off, w = _COL_OFF[name]
        return u_t[off:off + w]

    c32, s32, c64, s64 = c32_ref[...], s32_ref[...], c64_ref[...], s64_ref[...]
    g3 = lambda ref, groups: ref[...].reshape(groups, -1, 1)

    rows_out = {name: row(name) for name, _ in _ROW_SEGS}

    kcvc_s[...] = rows_out["nsa_kcvc"]
    n_chunk_rows = tm // NSA_CMP_STRIDE
    for tok in range(NSA_CMP_STRIDE):
        kcvc_o[0, :, tok * LANES:(tok + 1) * LANES] = kcvc_s[pl.ds(tok, n_chunk_rows, stride=NSA_CMP_STRIDE), :]
    gl_o[...] = rows_out["nsa_gl"]
    nz_o[...] = rows_out["nsa_z"].astype(BF16)
    dz_o[...] = rows_out["diff_z"].astype(BF16)
    mz_o[...] = rows_out["mla_z"].astype(BF16)
    ez_o[...] = rows_out["mem_z"].astype(BF16)

    nq_o[0] = _norm_rope_t(col("nsa_q"), N_HEADS, HEAD_DIM, g3(nqg_ref, 1), c64, s64) * (HEAD_DIM ** -0.5 * LOG2E)
    dq = _norm_rope_t(col("diff_q"), 2 * N_HEADS, DIFF_D, g3(dqg_ref, 1), c32, s32) * (DIFF_D ** -0.5 * LOG2E)
    dq_o[0] = dq.astype(BF16)
    mq = _norm_rope_t(col("mem_q"), N_HEADS, HEAD_DIM, g3(mqg_ref, 1)) * (HEAD_DIM ** -0.5 * LOG2E)
    mq_o[0] = mq.astype(BF16)
    dk = _norm_rope_t(col("diff_k"), 2 * N_HEADS, DIFF_D, g3(dkg_ref, 1), c32, s32).T
    for mp in range(2 * N_HEADS):
        dk_o[0, mp] = dk[:, mp * DIFF_D:(mp + 1) * DIFF_D].astype(BF16)
    k2 = _norm_rope_t(col("nsa_k2"), 2, HEAD_DIM, g3(k2g_ref, 2), c64, s64).T
    for br in range(2):
        k2_o[0, br] = k2[:, br * HEAD_DIM:(br + 1) * HEAD_DIM].astype(BF16)
    ones = _ones_rows(tm)
    dv = col("diff_v")
    for hd in range(N_HEADS):
        dv_o[0, hd, 0:HEAD_DIM, :] = dv[hd * HEAD_DIM:(hd + 1) * HEAD_DIM].astype(BF16)
        dv_o[0, hd, HEAD_DIM:VT_ROWS, :] = ones
    v2 = col("nsa_v2")
    for br in range(2):
        v2_o[0, br, 0:HEAD_DIM, :] = v2[br * HEAD_DIM:(br + 1) * HEAD_DIM].astype(BF16)
        v2_o[0, br, HEAD_DIM:VT_ROWS, :] = ones

    def latent(name, gain_ref):
        c = col(name)
        return (c * lax.rsqrt(jnp.mean(c * c, axis=0, keepdims=True) + EPS) * gain_ref[...]).astype(BF16)

    qa = jnp.dot(wuq_ref[...], latent("mla_cq", cqg_ref), preferred_element_type=F32)
    ckv = latent("mla_ckv", ckvg_ref)
    kn = jnp.dot(wuk_ref[...], ckv, preferred_element_type=F32)
    lv = jnp.dot(wuv_ref[...], ckv, preferred_element_type=F32)
    rope_rows = slice(MLA_NOPE, MLA_QK)
    lqg = lqg_ref[...] * (MLA_QK ** -0.5 * LOG2E)
    kr = _rope_t(col("mla_kr"), c32, s32)
    zpad = jnp.zeros((LANES - MLA_QK, tm), F32)
    for hd in range(N_HEADS):
        q_h = qa[hd * LANES:(hd + 1) * LANES]
        q_h = jnp.concatenate([q_h[:MLA_NOPE], _rope_t(q_h[rope_rows], c32, s32), q_h[MLA_QK:]], axis=0)
        lq_o[0, hd * LANES:(hd + 1) * LANES, :] = _mla_head_norm_t(q_h, lqg).astype(BF16)
        k_h = jnp.concatenate([kn[hd * LANES:hd * LANES + MLA_NOPE], kr, zpad], axis=0)
        lk_o[0, hd] = _mla_head_norm_t(k_h, lkg_ref[...]).T.astype(BF16)
        lv_o[0, hd, 0:HEAD_DIM, :] = lv[hd * HEAD_DIM:(hd + 1) * HEAD_DIM].astype(BF16)
        lv_o[0, hd, HEAD_DIM:VT_ROWS, :] = ones


def _in_proj(x, layer, prm, tabs_t, tm=512):
    b, seq, d = x.shape
    n = b * seq
    nb = seq // tm
    c32, s32, c64, s64 = tabs_t
    lay = functools.partial(_layer_spec, layer)
    gain_col = lambda j, rows: pl.BlockSpec((None, None, rows, 1), lambda i: (layer, j, 0, 0))

    rowspec = lambda w: pl.BlockSpec((tm, w), lambda i: (i, 0))
    colspec = lambda r: pl.BlockSpec((1, r, tm), lambda i: (i // nb, 0, i % nb))
    vtspec = lambda c: pl.BlockSpec((1, c, VT_ROWS, tm), lambda i: (i // nb, 0, 0, i % nb))
    const = lambda shape: pl.BlockSpec(shape, lambda i: tuple(0 for _ in shape))
    tabspec = lambda r: pl.BlockSpec((r, tm), lambda i: (0, i % nb))
    row_out = lambda w, dt: jax.ShapeDtypeStruct((n, w), dt)
    col_out = lambda r, dt: jax.ShapeDtypeStruct((b, r, seq), dt)
    vt_out = lambda c: jax.ShapeDtypeStruct((b, c, VT_ROWS, seq), BF16)
    slabspec = lambda c, w: pl.BlockSpec((1, c, tm, w), lambda i: (i // nb, 0, i % nb, 0))
    slab_out = lambda c, w: jax.ShapeDtypeStruct((b, c, seq, w), BF16)
    outs = pl.pallas_call(
        _in_proj_kernel,
        grid=(n // tm,),
        in_specs=[rowspec(d), lay(1, d), lay(d, _ROW_W), lay(_COL_W, d),
                  gain_col(0, HEAD_DIM), gain_col(1, DIFF_D), gain_col(2, DIFF_D), gain_col(3, 2 * HEAD_DIM),
                  gain_col(4, HEAD_DIM),
                  gain_col(5, MLA_Q_RANK), gain_col(6, MLA_KV_RANK),
                  lay(N_HEADS * LANES, MLA_Q_RANK), lay(N_HEADS * LANES, MLA_KV_RANK),
                  lay(GROUP_WIDTH, MLA_KV_RANK), gain_col(7, LANES), gain_col(8, LANES),
                  tabspec(DIFF_D // 2), tabspec(DIFF_D // 2), tabspec(HEAD_DIM // 2), tabspec(HEAD_DIM // 2)],
        out_specs=[pl.BlockSpec((1, tm // NSA_CMP_STRIDE, NSA_CMP_STRIDE * LANES), lambda i: (i // nb, i % nb, 0)),
                   rowspec(128), rowspec(256), rowspec(256), rowspec(256), rowspec(256),
                   slabspec(2 * N_HEADS, DIFF_D), slabspec(2, HEAD_DIM),
                   colspec(256), colspec(256), colspec(256), vtspec(N_HEADS), vtspec(2),
                   colspec(N_HEADS * LANES), slabspec(N_HEADS, LANES), vtspec(N_HEADS)],
        out_shape=[jax.ShapeDtypeStruct((b, seq // NSA_CMP_STRIDE, NSA_CMP_STRIDE * LANES), F32),
                   row_out(128, F32), row_out(256, BF16), row_out(256, BF16),
                   row_out(256, BF16), row_out(256, BF16), slab_out(2 * N_HEADS, DIFF_D), slab_out(2, HEAD_DIM),
                   col_out(256, F32), col_out(256, BF16), col_out(256, BF16), vt_out(N_HEADS), vt_out(2),
                   col_out(N_HEADS * LANES, BF16), slab_out(N_HEADS, LANES), vt_out(N_HEADS)],
        scratch_shapes=[pltpu.VMEM((tm, LANES), F32)],
        compiler_params=_cparams(("parallel",)),
        name="in_proj",
    )(x.reshape(n, d), prm["norm_gain"], prm["w_row"], prm["w_col"],
      *([prm["gain_cols"]] * 7), prm["w_uq_t"], prm["w_uk_t"], prm["w_uv_t"], prm["gain_cols"], prm["gain_cols"],
      c32, s32, c64, s64)
    names = ("nsa_kcvc", "nsa_gl", "nsa_z", "diff_z", "mla_z", "mem_z", "diff_k", "nsa_k2",
             "nsa_qT", "diff_qT", "mem_qT", "diff_vT", "nsa_v2T", "mla_qT", "mla_k", "mla_vT")
    u = dict(zip(names, outs))
    for name in names[1:6]:
        u[name] = u[name].reshape(b, seq, -1)
    return u


X_RING_SLOTS = 3


def _out_proj_kernel(tm, x_hbm, y0_ref, y1_ref, y2_ref, y3_ref, w_ref, o_ref, x_ring, sem):
    i = pl.program_id(0)
    ahead = X_RING_SLOTS - 1

    def x_copy(step):
        slot = step % X_RING_SLOTS
        return pltpu.make_async_copy(x_hbm.at[pl.ds(step * tm, tm)], x_ring.at[slot], sem.at[slot])

    @pl.when(i == 0)
    def _():
        for step in range(ahead):
            x_copy(step).start()

    @pl.when(i + ahead < pl.num_programs(0))
    def _():
        x_copy(i + ahead).start()

    x_copy(i).wait()
    acc = x_ring[i % X_RING_SLOTS]
    for g, y_ref in enumerate((y0_ref, y1_ref, y2_ref, y3_ref)):
        acc = acc + jnp.dot(y_ref[...], w_ref[g * GROUP_WIDTH:(g + 1) * GROUP_WIDTH, :],
                            preferred_element_type=F32)
    o_ref[...] = acc


def _out_proj(x2, ys, layer, w_out_b, tm=1024):
    n = x2.shape[0]
    yspec = pl.BlockSpec((tm, GROUP_WIDTH), lambda i: (i, 0))
    assert n // tm >= X_RING_SLOTS
    return pl.pallas_call(
        functools.partial(_out_proj_kernel, tm),
        grid=(n // tm,),
        in_specs=[pl.BlockSpec(memory_space=pl.ANY), yspec, yspec, yspec, yspec,
                  _layer_spec(layer, D_MODEL, D_MODEL)],
        out_specs=pl.BlockSpec((tm, D_MODEL), lambda i: (i, 0)),
        out_shape=jax.ShapeDtypeStruct((n, D_MODEL), F32),
        scratch_shapes=[pltpu.VMEM((X_RING_SLOTS, tm, D_MODEL), F32),
                        pltpu.SemaphoreType.DMA((X_RING_SLOTS,))],
        compiler_params=_cparams(("arbitrary",)),
        name="out_proj",
    )(x2, *ys, w_out_b)


def _diff_kernel(lambda_init, t, qT_ref, k_ref, vT_ref, z_ref, lam_ref, sg_ref, y_ref, m_s, acc_s, s_s):
    qi = pl.program_id(1)
    n_maps = 2 * N_HEADS
    qT = qT_ref[0]
    _causal_flash_t(
        n_maps,
        lambda i, off: k_ref[0, i, pl.ds(off, KEY_TILE), :],
        lambda i: qT[i * DIFF_D:(i + 1) * DIFF_D],
        lambda i, off: vT_ref[0, i // 2, :, pl.ds(off, KEY_TILE)],
        qi, t, KEY_TILE, m_s, acc_s, s_s)

    lam = lam_ref[...]
    lmbda = (jnp.exp(jnp.sum(lam[0:1] * lam[1:2], axis=-1, keepdims=True))
             - jnp.exp(jnp.sum(lam[2:3] * lam[3:4], axis=-1, keepdims=True)) + lambda_init)
    outs = []
    for h in range(N_HEADS):
        d = _flash_out(acc_s, 2 * h) - lmbda * _flash_out(acc_s, 2 * h + 1)
        ms = jnp.mean(d * d, axis=0, keepdims=True)
        outs.append(d * lax.rsqrt(ms + EPS) * sg_ref[...] * (1.0 - lambda_init))
    y = jnp.concatenate(outs, axis=0).T * _silu(z_ref[0])
    y_ref[0] = y.astype(BF16)


def _diff_call(qT, k, vT, z, layer, lam, subln_col, lambda_init, t=1024):
    b, _, seq, _ = k.shape
    const = lambda shape: pl.BlockSpec(shape, lambda bi, qi: tuple(0 for _ in shape))
    tile = pl.BlockSpec((1, t, GROUP_WIDTH), lambda bi, qi: (bi, qi, 0))
    return pl.pallas_call(
        functools.partial(_diff_kernel, lambda_init, t),
        grid=(b, seq // t),
        in_specs=[pl.BlockSpec((1, GROUP_WIDTH, t), lambda bi, qi: (bi, 0, qi)),
                  pl.BlockSpec((1, 2 * N_HEADS, seq, DIFF_D), lambda bi, qi: (bi, 0, 0, 0)),
                  pl.BlockSpec((1, N_HEADS, VT_ROWS, seq), lambda bi, qi: (bi, 0, 0, 0)),
                  tile, _layer_spec(layer, 4, DIFF_D), _layer_spec(layer, HEAD_DIM, 1)],
        out_specs=tile,
        out_shape=jax.ShapeDtypeStruct((b, seq, GROUP_WIDTH), BF16),
        scratch_shapes=[pltpu.VMEM((2 * N_HEADS, t), F32),
                        pltpu.VMEM((2 * N_HEADS, VT_ROWS, t), F32),
                        pltpu.VMEM((2 * N_HEADS, KEY_TILE, t), F32)],
        compiler_params=_cparams(("parallel", "parallel")),
        name="diff_attn",
    )(qT, k, vT, z, lam, subln_col)


def _mla_mem_kernel(t, qT_ref, k_ref, vT_ref, z_ref,
                    mqT_ref, mem_ref, mz_ref, mg_ref, wk_ref, wvT_ref, kg_ref, gm_ref,
                    y_ref, ymem_ref, m_s, acc_s, s_s, mk_s, mvT_s, mm_s, macc_s):
    qi = pl.program_id(1)
    nt = (((1,), (1,)), ((), ()))
    m_len = mem_ref.shape[1]

    @pl.when(qi == 0)
    def _prep_memory_kv():
        mem = mem_ref[0]
        ms = jnp.mean(mem * mem, axis=-1, keepdims=True)
        mb = (mem * lax.rsqrt(ms + EPS) * mg_ref[...]).astype(BF16)
        k = jnp.dot(mb, wk_ref[...], preferred_element_type=F32)
        kn = k * _group_rsqrt(k, gm_ref[...], float(HEAD_DIM)) * kg_ref[...]
        vT = lax.dot_general(wvT_ref[...], mb, nt, preferred_element_type=F32)
        for h in range(N_HEADS):
            mk_s[h] = kn[:, h * HEAD_DIM:(h + 1) * HEAD_DIM].astype(BF16)
            mvT_s[h, 0:HEAD_DIM, :] = vT[h * HEAD_DIM:(h + 1) * HEAD_DIM].astype(BF16)
            mvT_s[h, HEAD_DIM:VT_ROWS, :] = _ones_rows(m_len)

    mqT = mqT_ref[0]
    _flash_reset(mm_s, macc_s)
    ss = [jnp.dot(mk_s[h], mqT[h * HEAD_DIM:(h + 1) * HEAD_DIM], preferred_element_type=F32)
          for h in range(N_HEADS)]
    for h in range(N_HEADS):
        _flash_step_t(ss[h], mvT_s[h], mm_s, macc_s, h)

    qT = qT_ref[0]
    _causal_flash_t(
        N_HEADS,
        lambda h, off: k_ref[0, h, pl.ds(off, KEY_TILE), :],
        lambda h: qT[h * LANES:(h + 1) * LANES],
        lambda h, off: vT_ref[0, h, :, pl.ds(off, KEY_TILE)],
        qi, t, KEY_TILE, m_s, acc_s, s_s)

    y_mem = jnp.concatenate([_flash_out(macc_s, h) for h in range(N_HEADS)], axis=0).T * _silu(mz_ref[0])
    ymem_ref[0] = y_mem.astype(BF16)
    y = jnp.concatenate([_flash_out(acc_s, h) for h in range(N_HEADS)], axis=0).T * _silu(z_ref[0])
    y_ref[0] = y.astype(BF16)


def _mla_mem_call(qT, k, vT, z, mem_qT, mem, mem_z, layer, prm, t=1024):
    b, _, seq, _ = k.shape
    m_len = mem.shape[1]
    lay = functools.partial(_layer_spec, layer)
    tile = pl.BlockSpec((1, t, GROUP_WIDTH), lambda bi, qi: (bi, qi, 0))
    const = lambda shape: pl.BlockSpec(shape, lambda bi, qi: tuple(0 for _ in shape))
    out = jax.ShapeDtypeStruct((b, seq, GROUP_WIDTH), BF16)
    return pl.pallas_call(
        functools.partial(_mla_mem_kernel, t),
        grid=(b, seq // t),
        in_specs=[pl.BlockSpec((1, N_HEADS * LANES, t), lambda bi, qi: (bi, 0, qi)),
                  pl.BlockSpec((1, N_HEADS, seq, LANES), lambda bi, qi: (bi, 0, 0, 0)),
                  pl.BlockSpec((1, N_HEADS, VT_ROWS, seq), lambda bi, qi: (bi, 0, 0, 0)),
                  tile,
                  pl.BlockSpec((1, GROUP_WIDTH, t), lambda bi, qi: (bi, 0, qi)),
                  pl.BlockSpec((1, m_len, D_MODEL), lambda bi, qi: (bi, 0, 0)), tile,
                  lay(1, D_MODEL), lay(D_MODEL, GROUP_WIDTH), lay(GROUP_WIDTH, D_MODEL),
                  lay(1, GROUP_WIDTH), const((GROUP_WIDTH, GROUP_WIDTH))],
        out_specs=[tile, tile],
        out_shape=[out, out],
        scratch_shapes=[pltpu.VMEM((N_HEADS, t), F32),
                        pltpu.VMEM((N_HEADS, VT_ROWS, t), F32),
                        pltpu.VMEM((N_HEADS, KEY_TILE, t), F32),
                        pltpu.VMEM((N_HEADS, m_len, HEAD_DIM), BF16),
                        pltpu.VMEM((N_HEADS, VT_ROWS, m_len), BF16),
                        pltpu.VMEM((N_HEADS, t), F32),
                        pltpu.VMEM((N_HEADS, VT_ROWS, t), F32)],
        compiler_params=_cparams(("parallel", "arbitrary")),
        name="mla_mem_attn",
    )(qT, k, vT, z, mem_qT, mem, mem_z, prm["mem_gain"], prm["mem_w_k"], prm["mem_w_vt"],
      prm["mem_k_gain"], _group_ones(GROUP_WIDTH, HEAD_DIM))


def _nsa_kernel(seq, t,
                qT_ref, kcvc_ref, k2_ref, v2T_ref, gl_ref, z_ref,
                cg_ref, pelo_ref, pehi_ref, wlo_ref, whi_ref, cosc_ref, sinc_ref,
                y_ref, kc_s, vcT_s, bias_s, m_s, acc_s, s_s, pg_s):
    qi = pl.program_id(1)
    half = HEAD_DIM // 2
    n_chunk = seq // NSA_CMP_STRIDE
    n_cmp = n_chunk - 1
    n_blk = seq // NSA_SLC_BLOCK
    blk_per_tile = t // NSA_SLC_BLOCK
    win_tiles = NSA_WINDOW // t
    w4 = N_HEADS * t
    SLC, WIN = 0, 1

    @pl.when(qi == 0)
    def _prep():
        ch = kcvc_ref[0]
        a = _dot_x3(*_split_bf16(ch + pelo_ref[...]), wlo_ref[0], wlo_ref[1])
        bm = _dot_x3(*_split_bf16(ch + pehi_ref[...]), whi_ref[0], whi_ref[1])
        cmp = a + pltpu.roll(bm, n_chunk - 1, 0)
        kc = cmp[:, :HEAD_DIM]
        ms = jnp.mean(kc * kc, axis=-1, keepdims=True)
        kc = kc * lax.rsqrt(ms + EPS) * cg_ref[...]
        x1, x2 = kc[:, :half], kc[:, half:]
        c, s = cosc_ref[...], sinc_ref[...]
        kc_hi, kc_lo = _split_bf16(jnp.concatenate([x1 * c - x2 * s, x2 * c + x1 * s], axis=-1))
        kc_s[0] = kc_hi
        kc_s[1] = kc_lo
        vcT_s[...] = cmp.T[HEAD_DIM:].astype(BF16)
        pg_s[:, n_chunk:, :] = jnp.zeros((t // LANES, pg_s.shape[1] - n_chunk, LANES), F32)

    qs = qi * t
    qT = qT_ref[0]
    q4f = jnp.concatenate([qT[h * HEAD_DIM:(h + 1) * HEAD_DIM] for h in range(N_HEADS)], axis=1)
    q4 = q4f.astype(BF16)

    def scores(br, j):
        off = pl.multiple_of(j * t, t)
        return jnp.dot(k2_ref[0, br, pl.ds(off, t), :], q4, preferred_element_type=F32)

    s_s[WIN] = scores(WIN, jnp.maximum(qi - win_tiles, 0))
    qcol = lax.broadcasted_iota(jnp.int32, (1, w4), 1) & (t - 1)
    pos_c = qs + qcol

    q_lo = (q4f - q4.astype(F32)).astype(BF16)
    heads = [slice(h * t, (h + 1) * t) for h in range(N_HEADS)]
    scs = [_dot_x3(kc_s[0], kc_s[1], q4[:, c], q_lo[:, c]) for c in heads]
    n_idx = lax.broadcasted_iota(jnp.int32, (n_chunk, 1), 0)
    cvalid = (n_idx * NSA_CMP_STRIDE + (NSA_CMP_BLOCK - 1) <= pos_c[:, :t]) & (n_idx < n_cmp)
    o_cmp, pg = [], None
    for h in range(N_HEADS):
        sc = jnp.where(cvalid, scs[h], NEG_INF)
        e = jnp.exp2(sc - jnp.max(sc, axis=0, keepdims=True))
        p = jnp.where(cvalid, e / jnp.sum(e, axis=0, keepdims=True), 0.0)
        o_cmp.append(jnp.dot(vcT_s[...], p.astype(BF16), preferred_element_type=F32))
        pg = p if pg is None else pg + p

    for c0 in range(t // LANES):
        pg_s[c0, 0:n_chunk, :] = pg[:, c0 * LANES:(c0 + 1) * LANES]
    ratio = NSA_SLC_BLOCK // NSA_CMP_STRIDE
    coef = np.convolve(np.ones(ratio), np.ones(NSA_CMP_BLOCK // NSA_CMP_STRIDE))
    p_slc = None
    for i, c in enumerate(coef):
        term = float(c) * jnp.concatenate(
            [pg_s[c0, pl.ds(i, n_blk, stride=ratio), :] for c0 in range(t // LANES)], axis=1)
        p_slc = term if p_slc is None else p_slc + term
    blk = lax.broadcasted_iota(jnp.int32, (n_blk, 1), 0)
    cur = lax.shift_right_logical(pos_c[:, :t], NSA_SLC_BLOCK.bit_length() - 1)
    forced = (blk == 0) | ((blk <= cur) & (blk > cur - NSA_N_LOCAL))
    score = jnp.where(blk > cur, NEG_INF, jnp.where(forced, BIG, p_slc))
    sub = 8
    groups = [score[g * sub:(g + 1) * sub] for g in range(n_blk // sub)]
    cnts = [jnp.zeros((sub, t), F32) for _ in groups]
    for i in range(n_blk):
        ri = score[i:i + 1, :]
        for g, sg in enumerate(groups):
            if g < i // sub:
                beats = ri > sg
            elif g > i // sub:
                beats = ri >= sg
            else:
                beats = (ri > sg) | ((ri == sg) & (blk[g * sub:(g + 1) * sub] > i))
            cnts[g] = cnts[g] + jnp.where(beats, 1.0, 0.0)
    cnt = jnp.concatenate(cnts, axis=0)
    bias = jnp.where(cnt < float(NSA_N_SELECT), 0.0, NEG_INF)
    bias = jnp.concatenate([bias] * N_HEADS, axis=1)
    for r in range(n_blk // blk_per_tile):
        bias_s[r] = bias[r * blk_per_tile:(r + 1) * blk_per_tile, :]

    _flash_reset(m_s, acc_s)
    krow = lax.broadcasted_iota(jnp.int32, (t, 1), 0)
    causal = krow <= qcol
    beyond = krow > qcol

    def slc_scores(j):
        s = scores(SLC, j).reshape(blk_per_tile, NSA_SLC_BLOCK, w4) + bias_s[j][:, None, :]
        return s.reshape(t, w4)

    def flash(br, s, j):
        off = pl.multiple_of(j * t, t)
        _flash_step_t(s, v2T_ref[0, br, :, pl.ds(off, t)], m_s, acc_s, br)

    s_s[SLC] = slc_scores(0)

    def old_body(j, carry):
        s = s_s[SLC]
        s_s[SLC] = slc_scores(j + 1)
        flash(SLC, s, j)
        return carry

    lax.fori_loop(0, jnp.maximum(qi - win_tiles, 0), old_body, 0)

    def win_step(j, win_mask):
        s_slc = s_s[SLC]
        s_s[SLC] = slc_scores(j + 1)
        flash(SLC, s_slc, j)
        s_win = s_s[WIN]
        if win_mask is not None:
            s_win = jnp.where(win_mask, s_win, NEG_INF)
        s_s[WIN] = scores(WIN, j + 1)
        flash(WIN, s_win, j)

    @pl.when(qi >= win_tiles)
    def _oldest_window_tile():
        win_step(qi - win_tiles, beyond)

    def win_body(j, carry):
        win_step(j, None)
        return carry

    lax.fori_loop(jnp.maximum(qi - win_tiles + 1, 0), qi, win_body, 0)
    flash(SLC, jnp.where(causal, s_s[SLC], NEG_INF), qi)
    flash(WIN, jnp.where(causal, s_s[WIN], NEG_INF), qi)

    g = jax.nn.sigmoid(gl_ref[0]).T
    outs = []
    for h in range(N_HEADS):
        cols = slice(h * t, (h + 1) * t)
        o_s = _flash_out(acc_s, SLC, cols)
        o_w = _flash_out(acc_s, WIN, cols)
        outs.append(g[h:h + 1, :] * o_cmp[h] + g[N_HEADS + h:N_HEADS + h + 1, :] * o_s
                    + g[2 * N_HEADS + h:2 * N_HEADS + h + 1, :] * o_w)
    y = jnp.concatenate(outs, axis=0).T * _silu(z_ref[0])
    y_ref[0] = y.astype(BF16)


def _nsa_call(qT, kcvc, k2, v2T, gl, z, layer, prm, t=256):
    b, _, seq, _ = k2.shape
    n_chunk = seq // NSA_CMP_STRIDE
    cw = NSA_CMP_STRIDE * 2 * HEAD_DIM
    assert t % NSA_SLC_BLOCK == 0 and NSA_WINDOW % t == 0 and t & (t - 1) == 0
    lay = functools.partial(_layer_spec, layer)

    tile = pl.BlockSpec((1, t, GROUP_WIDTH), lambda bi, qi: (bi, qi, 0))
    const = lambda shape: pl.BlockSpec(shape, lambda bi, qi: tuple(0 for _ in shape))
    w4 = N_HEADS * t
    return pl.pallas_call(
        functools.partial(_nsa_kernel, seq, t),
        grid=(b, seq // t),
        in_specs=[pl.BlockSpec((1, GROUP_WIDTH, t), lambda bi, qi: (bi, 0, qi)),
                  pl.BlockSpec((1, n_chunk, cw), lambda bi, qi: (bi, 0, 0)),
                  pl.BlockSpec((1, 2, seq, HEAD_DIM), lambda bi, qi: (bi, 0, 0, 0)),
                  pl.BlockSpec((1, 2, VT_ROWS, seq), lambda bi, qi: (bi, 0, 0, 0)),
                  pl.BlockSpec((1, t, LANES), lambda bi, qi: (bi, qi, 0)),
                  tile,
                  lay(1, HEAD_DIM),
                  lay(1, cw), lay(1, cw), lay(2, cw, 2 * HEAD_DIM), lay(2, cw, 2 * HEAD_DIM),
                  const((n_chunk, HEAD_DIM // 2)), const((n_chunk, HEAD_DIM // 2))],
        out_specs=tile,
        out_shape=jax.ShapeDtypeStruct((b, seq, GROUP_WIDTH), BF16),
        scratch_shapes=[pltpu.VMEM((2, n_chunk, HEAD_DIM), BF16),
                        pltpu.VMEM((HEAD_DIM, n_chunk), BF16),
                        pltpu.VMEM((seq // t, t // NSA_SLC_BLOCK, w4), F32),
                        pltpu.VMEM((2, w4), F32),
                        pltpu.VMEM((2, VT_ROWS, w4), F32),
                        pltpu.VMEM((2, t, w4), F32),
                        pltpu.VMEM((t // LANES, n_chunk + 8, LANES), F32)],
        compiler_params=_cparams(("parallel", "arbitrary")),
        name="nsa_attn",
    )(qT, kcvc, k2, v2T, gl, z,
      prm["nsa_cmp_gain"], prm["nsa_pe_lo"], prm["nsa_pe_hi"], prm["nsa_w_lo"], prm["nsa_w_hi"],
      prm["cos_c"], prm["sin_c"])


def _rope_cos_sin(pos, dim):
    half = dim // 2
    inv_freq = ROPE_THETA ** (-jnp.arange(half, dtype=F32) / half)
    ang = pos.astype(F32)[:, None] * inv_freq[None, :]
    return jnp.cos(ang), jnp.sin(ang)


def _prepare_params(seq, norm_gain, w_in, w_out, nsa_qk_gain, nsa_cmp_pe, nsa_w_cmp, diff_qk_gain, diff_subln_gain,
                    mla_cq_gain, mla_ckv_gain, mla_w_uq, mla_w_ukv, mla_qk_gain, mem_norm_gain, mem_w_kv, mem_qk_gain):
    layers = w_in.shape[0]
    prm = {}
    prm["w_row"], prm["w_col"] = _split_w_in(w_in)
    prm["w_out"] = w_out.astype(BF16)
    prm["norm_gain"] = norm_gain[:, None, :]

    col = lambda a: jnp.pad(a, ((0, 0), (0, GROUP_WIDTH - a.shape[1])))
    prm["gain_cols"] = jnp.stack([
        col(nsa_qk_gain[:, 0]), col(diff_qk_gain[:, 0]), col(diff_qk_gain[:, 1]),
        col(jnp.concatenate([nsa_qk_gain[:, 2], nsa_qk_gain[:, 3]], axis=1)), col(mem_qk_gain[:, 0]),
        col(mla_cq_gain), col(mla_ckv_gain), col(mla_qk_gain[:, 0]), col(mla_qk_gain[:, 1])], axis=1)[..., None]

    npad = LANES - MLA_QK
    uq = jnp.pad(mla_w_uq.reshape(layers, MLA_Q_RANK, N_HEADS, MLA_QK), ((0, 0), (0, 0), (0, 0), (0, npad)))
    prm["w_uq_t"] = jnp.swapaxes(uq.reshape(layers, MLA_Q_RANK, N_HEADS * LANES), 1, 2).astype(BF16)
    ukv = mla_w_ukv.reshape(layers, MLA_KV_RANK, N_HEADS, MLA_NOPE + HEAD_DIM)
    uk = jnp.pad(ukv[..., :MLA_NOPE], ((0, 0), (0, 0), (0, 0), (0, LANES - MLA_NOPE)))
    prm["w_uk_t"] = jnp.swapaxes(uk.reshape(layers, MLA_KV_RANK, N_HEADS * LANES), 1, 2).astype(BF16)
    uv = ukv[..., MLA_NOPE:].reshape(layers, MLA_KV_RANK, GROUP_WIDTH)
    prm["w_uv_t"] = jnp.swapaxes(uv, 1, 2).astype(BF16)

    half_blk = NSA_CMP_BLOCK // 2
    cw = half_blk * 2 * HEAD_DIM
    wk = nsa_w_cmp[:, 0].reshape(layers, NSA_CMP_BLOCK, HEAD_DIM, HEAD_DIM)
    wv = nsa_w_cmp[:, 1].reshape(layers, NSA_CMP_BLOCK, HEAD_DIM, HEAD_DIM)
    zero = jnp.zeros_like(wk)
    w_all = jnp.concatenate([jnp.concatenate([wk, zero], axis=3),
                             jnp.concatenate([zero, wv], axis=3)], axis=2)
    hi_lo = lambda w: jnp.stack([w.astype(BF16), (w - w.astype(BF16).astype(F32)).astype(BF16)], axis=1)
    prm["nsa_w_lo"] = hi_lo(w_all[:, :half_blk].reshape(layers, cw, 2 * HEAD_DIM))
    prm["nsa_w_hi"] = hi_lo(w_all[:, half_blk:].reshape(layers, cw, 2 * HEAD_DIM))
    pe_all = jnp.concatenate([nsa_cmp_pe[:, 0], nsa_cmp_pe[:, 1]], axis=2)
    prm["nsa_pe_lo"] = pe_all[:, :half_blk].reshape(layers, 1, cw)
    prm["nsa_pe_hi"] = pe_all[:, half_blk:].reshape(layers, 1, cw)
    prm["nsa_cmp_gain"] = nsa_qk_gain[:, 1][:, None, :]
    cmp_end = jnp.arange(seq // NSA_CMP_STRIDE, dtype=jnp.int32) * NSA_CMP_STRIDE + (NSA_CMP_BLOCK - 1)
    prm["cos_c"], prm["sin_c"] = _rope_cos_sin(cmp_end, HEAD_DIM)

    prm["diff_subln_col"] = diff_subln_gain[:, :, None]
    prm["mem_gain"] = mem_norm_gain[:, None, :]
    prm["mem_w_k"] = mem_w_kv[:, :, :GROUP_WIDTH].astype(BF16)
    prm["mem_w_vt"] = jnp.swapaxes(mem_w_kv[:, :, GROUP_WIDTH:], 1, 2).astype(BF16)
    prm["mem_k_gain"] = jnp.tile(mem_qk_gain[:, 1], (1, N_HEADS))[:, None, :]
    return prm


def _layer(x, mem, layer, prm, tabs_t, diff_lambda):
    b, seq, d = x.shape
    u = _in_proj(x, layer, prm, tabs_t)
    y_nsa = _nsa_call(u["nsa_qT"], u["nsa_kcvc"], u["nsa_k2"], u["nsa_v2T"], u["nsa_gl"], u["nsa_z"], layer, prm)
    lambda_init = 0.8 - 0.6 * math.exp(-0.3 * layer)
    y_diff = _diff_call(u["diff_qT"], u["diff_k"], u["diff_vT"], u["diff_z"], layer, diff_lambda,
                        prm["diff_subln_col"], lambda_init)
    y_mla, y_mem = _mla_mem_call(u["mla_qT"], u["mla_k"], u["mla_vT"], u["mla_z"],
                                 u["mem_qT"], mem, u["mem_z"], layer, prm)
    ys = [y.reshape(b * seq, GROUP_WIDTH) for y in (y_nsa, y_diff, y_mla, y_mem)]
    return _out_proj(x.reshape(b * seq, d), ys, layer, prm["w_out"]).reshape(b, seq, d)


def kernel(x, mem, norm_gain, w_in, w_out, nsa_qk_gain, nsa_cmp_pe, nsa_w_cmp, diff_qk_gain, diff_lambda,
           diff_subln_gain, mla_cq_gain, mla_ckv_gain, mla_w_uq, mla_w_ukv, mla_qk_gain, mem_norm_gain,
           mem_w_kv, mem_qk_gain):
    seq = x.shape[1]
    pos = jnp.arange(seq, dtype=jnp.int32)
    cos32, sin32 = _rope_cos_sin(pos, DIFF_D)
    cos64, sin64 = _rope_cos_sin(pos, HEAD_DIM)
    tabs_t = (cos32.T, sin32.T, cos64.T, sin64.T)
    prm = _prepare_params(seq, norm_gain, w_in, w_out, nsa_qk_gain, nsa_cmp_pe, nsa_w_cmp, diff_qk_gain,
                          diff_subln_gain, mla_cq_gain, mla_ckv_gain, mla_w_uq, mla_w_ukv, mla_qk_gain,
                          mem_norm_gain, mem_w_kv, mem_qk_gain)
    for l in range(DEPTH):
        x = _layer(x, mem, l, prm, tabs_t, diff_lambda)
    return x
```

```python
import functools
import math

import numpy as np
import jax
import jax.numpy as jnp
from jax import lax
from jax.experimental import pallas as pl
from jax.experimental.pallas import tpu as pltpu

F32 = jnp.float32
BF16 = jnp.bfloat16

D_MODEL = 1024
DEPTH = 2
N_HEADS = 4
HEAD_DIM = 64
GROUP_WIDTH = N_HEADS * HEAD_DIM
ROPE_THETA = 10000.0
EPS = 1e-6
NEG_INF = -1e30
BIG = 1e30
LOG2E = 1.4426950408889634

NSA_CMP_BLOCK = 32
NSA_CMP_STRIDE = 16
NSA_SLC_BLOCK = 64
NSA_N_SELECT = 16
NSA_N_LOCAL = 2
NSA_WINDOW = 512
DIFF_D = HEAD_DIM // 2
MLA_Q_RANK = 256
MLA_KV_RANK = 128
MLA_NOPE = 64
MLA_ROPE = 32
MLA_QK = MLA_NOPE + MLA_ROPE

VMEM_LIMIT_BYTES = 48 * 1024 * 1024
LANES = 128
KEY_TILE = 256
VT_ROWS = 80


def _cparams(sem):
    return pltpu.CompilerParams(dimension_semantics=sem, vmem_limit_bytes=VMEM_LIMIT_BYTES)


def _layer_spec(layer, *shape):
    return pl.BlockSpec((None,) + shape, lambda *_: (layer,) + (0,) * len(shape))


def _group_ones(width, group):
    g = np.arange(width) // group
    return jnp.asarray(g[:, None] == g[None, :], dtype=BF16)


def _group_rsqrt(x, gmat, denom):
    sq = x * x
    hi = sq.astype(BF16)
    lo = (sq - hi.astype(F32)).astype(BF16)
    ss = jnp.dot(hi, gmat, preferred_element_type=F32) + jnp.dot(lo, gmat, preferred_element_type=F32)
    return lax.rsqrt(ss / denom + EPS)


def _norm_rope_t(u_t, groups, dim, gain, cos=None, sin=None):
    x = u_t.reshape(groups, dim, u_t.shape[-1])
    ms = jnp.mean(x * x, axis=1, keepdims=True)
    x = x * lax.rsqrt(ms + EPS) * gain
    if cos is not None:
        half = dim // 2
        x1, x2 = x[:, :half], x[:, half:]
        x = jnp.concatenate([x1 * cos - x2 * sin, x2 * cos + x1 * sin], axis=1)
    return x.reshape(groups * dim, u_t.shape[-1])


def _silu(z):
    z = z.astype(F32)
    return z * jax.nn.sigmoid(z)


def _ones_rows(n):
    r = lax.broadcasted_iota(jnp.int32, (VT_ROWS - HEAD_DIM, n), 0)
    return jnp.where(r == 0, 1.0, 0.0).astype(BF16)


def _split_bf16(x):
    hi = x.astype(BF16)
    return hi, (x - hi.astype(F32)).astype(BF16)


def _dot_x3(a_hi, a_lo, b_hi, b_lo):
    dot = lambda a, b: jnp.dot(a, b, preferred_element_type=F32)
    return dot(a_hi, b_hi) + dot(a_hi, b_lo) + dot(a_lo, b_hi)


def _flash_step_t(s, v_t, m_ref, acc_ref, i, cols=slice(None)):
    m_old = m_ref[i:i + 1, cols]
    m_new = jnp.maximum(m_old, jnp.max(s, axis=0, keepdims=True))
    alpha = jnp.exp2(m_old - m_new)
    p = jnp.exp2(s - m_new).astype(BF16)
    acc_ref[i, :, cols] = alpha * acc_ref[i, :, cols] + jnp.dot(v_t, p, preferred_element_type=F32)
    m_ref[i:i + 1, cols] = m_new


def _flash_reset(m_ref, acc_ref):
    m_ref[...] = jnp.full(m_ref.shape, NEG_INF, F32)
    acc_ref[...] = jnp.zeros(acc_ref.shape, F32)


def _flash_out(acc_ref, i, cols=slice(None)):
    return acc_ref[i, 0:HEAD_DIM, cols] / acc_ref[i, HEAD_DIM:HEAD_DIM + 1, cols]


def _causal_flash_t(n, k_tile, q_t, v_tile, qi, tq, tk, m_ref, acc_ref, s_ref):
    _flash_reset(m_ref, acc_ref)
    n_diag = tq // tk

    all_q = slice(0, tq)

    def scores(i, off, cols):
        return jnp.dot(k_tile(i, off), q_t(i)[:, cols], preferred_element_type=F32)

    def step(off, cols, off_next, cols_next, mask):
        for i in range(n):
            s = s_ref[i, :, cols]
            if off_next is not None:
                s_ref[i, :, cols_next] = scores(i, off_next, cols_next)
            if mask is not None:
                s = jnp.where(mask, s, NEG_INF)
            _flash_step_t(s, v_tile(i, off), m_ref, acc_ref, i, cols)

    for i in range(n):
        s_ref[i] = scores(i, 0, all_q)

    def body(j, carry):
        step(pl.multiple_of(j * tk, tk), all_q, pl.multiple_of(j * tk + tk, tk), all_q, None)
        return carry

    lax.fori_loop(0, qi * n_diag, body, 0)
    krow = lax.broadcasted_iota(jnp.int32, (tk, tq), 0)
    qcol = lax.broadcasted_iota(jnp.int32, (tk, tq), 1)
    causal = krow <= qcol
    diag_cols = [slice(d * tk, tq) for d in range(n_diag)]
    for d in range(n_diag):
        off = pl.multiple_of(qi * tq + d * tk, tk)
        last = d + 1 == n_diag
        off_next = None if last else pl.multiple_of(qi * tq + (d + 1) * tk, tk)
        step(off, diag_cols[d], off_next, None if last else diag_cols[d + 1], causal[:, 0:tq - d * tk])


_ROW_SEGS = (("nsa_kcvc", 128), ("nsa_gl", 128), ("nsa_z", 256), ("diff_z", 256), ("mla_z", 256),
             ("mem_z", 256))
_COL_SEGS = (("nsa_q", 256), ("diff_q", 256), ("mem_q", 256), ("diff_k", 256), ("nsa_k2", 128),
             ("diff_v", 256), ("nsa_v2", 128), ("mla_cq", 256), ("mla_ckv", 128), ("mla_kr", 32))


def _seg_offsets(segs):
    out, off = {}, 0
    for name, w in segs:
        out[name] = (off, w)
        off += w
    return out, off


_ROW_OFF, _ROW_W = _seg_offsets(_ROW_SEGS)
_COL_OFF, _COL_W = _seg_offsets(_COL_SEGS)


_ROW_SRC = ((256, 384),
            (640, 652), (None, 116),
            (652, 908), (1676, 1932), (2348, 2604), (2860, 3116))
_COL_SRC = ((0, 256), (908, 1164), (2604, 2860),
            (1164, 1420),
            (384, 448), (512, 576),
            (1420, 1676),
            (448, 512), (576, 640),
            (1932, 2188), (2188, 2316), (2316, 2348))


def _w_split_kernel(wt_ref, wrow_o, wcol_o):
    _, layers, ct = wt_ref.shape

    for l in range(layers):
        def gather(src):
            return jnp.concatenate(
                [jnp.zeros((b, ct), F32) if a is None else wt_ref[a:b, l, :] for a, b in src], axis=0)

        wcol_o[l] = gather(_COL_SRC).astype(BF16)
        wrow_o[l] = gather(_ROW_SRC).T.astype(BF16)


def _split_w_in(w_in, ct=128):
    layers, d, d_in = w_in.shape
    w_t = jnp.transpose(w_in, (2, 0, 1))
    return pl.pallas_call(
        _w_split_kernel,
        grid=(d // ct,),
        in_specs=[pl.BlockSpec((d_in, layers, ct), lambda c: (0, 0, c))],
        out_specs=[pl.BlockSpec((layers, ct, _ROW_W), lambda c: (0, c, 0)),
                   pl.BlockSpec((layers, _COL_W, ct), lambda c: (0, 0, c))],
        out_shape=[jax.ShapeDtypeStruct((layers, d, _ROW_W), BF16),
                   jax.ShapeDtypeStruct((layers, _COL_W, d), BF16)],
        compiler_params=_cparams(("parallel",)),
        name="w_in_split",
    )(w_t)


def _mla_head_norm_t(x_t, gain):
    ms = jnp.sum(x_t * x_t, axis=0, keepdims=True) / float(MLA_QK)
    return x_t * lax.rsqrt(ms + EPS) * gain


def _rope_t(x_t, cos, sin):
    half = x_t.shape[0] // 2
    x1, x2 = x_t[:half], x_t[half:]
    return jnp.concatenate([x1 * cos - x2 * sin, x2 * cos + x1 * sin], axis=0)


def _in_proj_kernel(x_ref, g_ref, wrow_ref, wcol_ref, nqg_ref, dqg_ref, dkg_ref, k2g_ref, mqg_ref,
                    cqg_ref, ckvg_ref, wuq_ref, wuk_ref, wuv_ref, lqg_ref, lkg_ref,
                    c32_ref, s32_ref, c64_ref, s64_ref,
                    kcvc_o, gl_o, nz_o, dz_o, mz_o, ez_o, dk_o, k2_o,
                    nq_o, dq_o, mq_o, dv_o, v2_o, lq_o, lk_o, lv_o, kcvc_s):
    x = x_ref[...]
    tm = x.shape[0]
    ms = jnp.mean(x * x, axis=-1, keepdims=True)
    h = x * lax.rsqrt(ms + EPS) * g_ref[...]
    hb = h.astype(BF16)
    h_t = h.T.astype(BF16)

    def row(name):
        off, w = _ROW_OFF[name]
        return jnp.dot(hb, wrow_ref[:, off:off + w], preferred_element_type=F32)

    u_t = jnp.dot(wcol_ref[...], h_t, preferred_element_type=F32)

    def col(name):
        off, w = _COL_OFF[name]
        return u_t[off:off + w]

    c32, s32, c64, s64 = c32_ref[...], s32_ref[...], c64_ref[...], s64_ref[...]
    g3 = lambda ref, groups: ref[...].reshape(groups, -1, 1)

    rows_out = {name: row(name) for name, _ in _ROW_SEGS}

    kcvc_s[...] = rows_out["nsa_kcvc"]
    n_chunk_rows = tm // NSA_CMP_STRIDE
    for tok in range(NSA_CMP_STRIDE):
        kcvc_o[0, :, tok * LANES:(tok + 1) * LANES] = kcvc_s[pl.ds(tok, n_chunk_rows, stride=NSA_CMP_STRIDE), :]
    gl_o[...] = rows_out["nsa_gl"]
    nz_o[...] = rows_out["nsa_z"].astype(BF16)
    dz_o[...] = rows_out["diff_z"].astype(BF16)
    mz_o[...] = rows_out["mla_z"].astype(BF16)
    ez_o[...] = rows_out["mem_z"].astype(BF16)

    nq_o[0] = _norm_rope_t(col("nsa_q"), N_HEADS, HEAD_DIM, g3(nqg_ref, 1), c64, s64) * (HEAD_DIM ** -0.5 * LOG2E)
    dq = _norm_rope_t(col("diff_q"), 2 * N_HEADS, DIFF_D, g3(dqg_ref, 1), c32, s32) * (DIFF_D ** -0.5 * LOG2E)
    dq_o[0] = dq.astype(BF16)
    mq = _norm_rope_t(col("mem_q"), N_HEADS, HEAD_DIM, g3(mqg_ref, 1)) * (HEAD_DIM ** -0.5 * LOG2E)
    mq_o[0] = mq.astype(BF16)
    dk = _norm_rope_t(col("diff_k"), 2 * N_HEADS, DIFF_D, g3(dkg_ref, 1), c32, s32).T
    for mp in range(2 * N_HEADS):
        dk_o[0, mp] = dk[:, mp * DIFF_D:(mp + 1) * DIFF_D].astype(BF16)
    k2 = _norm_rope_t(col("nsa_k2"), 2, HEAD_DIM, g3(k2g_ref, 2), c64, s64).T
    for br in range(2):
        k2_o[0, br] = k2[:, br * HEAD_DIM:(br + 1) * HEAD_DIM].astype(BF16)
    ones = _ones_rows(tm)
    dv = col("diff_v")
    for hd in range(N_HEADS):
        dv_o[0, hd, 0:HEAD_DIM, :] = dv[hd * HEAD_DIM:(hd + 1) * HEAD_DIM].astype(BF16)
        dv_o[0, hd, HEAD_DIM:VT_ROWS, :] = ones
    v2 = col("nsa_v2")
    for br in range(2):
        v2_o[0, br, 0:HEAD_DIM, :] = v2[br * HEAD_DIM:(br + 1) * HEAD_DIM].astype(BF16)
        v2_o[0, br, HEAD_DIM:VT_ROWS, :] = ones

    def latent(name, gain_ref):
        c = col(name)
        return (c * lax.rsqrt(jnp.mean(c * c, axis=0, keepdims=True) + EPS) * gain_ref[...]).astype(BF16)

    qa = jnp.dot(wuq_ref[...], latent("mla_cq", cqg_ref), preferred_element_type=F32)
    ckv = latent("mla_ckv", ckvg_ref)
    kn = jnp.dot(wuk_ref[...], ckv, preferred_element_type=F32)
    lv = jnp.dot(wuv_ref[...], ckv, preferred_element_type=F32)
    rope_rows = slice(MLA_NOPE, MLA_QK)
    lqg = lqg_ref[...] * (MLA_QK ** -0.5 * LOG2E)
    kr = _rope_t(col("mla_kr"), c32, s32)
    zpad = jnp.zeros((LANES - MLA_QK, tm), F32)
    for hd in range(N_HEADS):
        q_h = qa[hd * LANES:(hd + 1) * LANES]
        q_h = jnp.concatenate([q_h[:MLA_NOPE], _rope_t(q_h[rope_rows], c32, s32), q_h[MLA_QK:]], axis=0)
        lq_o[0, hd * LANES:(hd + 1) * LANES, :] = _mla_head_norm_t(q_h, lqg).astype(BF16)
        k_h = jnp.concatenate([kn[hd * LANES:hd * LANES + MLA_NOPE], kr, zpad], axis=0)
        lk_o[0, hd] = _mla_head_norm_t(k_h, lkg_ref[...]).T.astype(BF16)
        lv_o[0, hd, 0:HEAD_DIM, :] = lv[hd * HEAD_DIM:(hd + 1) * HEAD_DIM].astype(BF16)
        lv_o[0, hd, HEAD_DIM:VT_ROWS, :] = ones


def _in_proj(x, layer, prm, tabs_t, tm=512):
    b, seq, d = x.shape
    n = b * seq
    nb = seq // tm
    c32, s32, c64, s64 = tabs_t
    lay = functools.partial(_layer_spec, layer)
    gain_col = lambda j, rows: pl.BlockSpec((None, None, rows, 1), lambda i: (layer, j, 0, 0))

    rowspec = lambda w: pl.BlockSpec((tm, w), lambda i: (i, 0))
    colspec = lambda r: pl.BlockSpec((1, r, tm), lambda i: (i // nb, 0, i % nb))
    vtspec = lambda c: pl.BlockSpec((1, c, VT_ROWS, tm), lambda i: (i // nb, 0, 0, i % nb))
    const = lambda shape: pl.BlockSpec(shape, lambda i: tuple(0 for _ in shape))
    tabspec = lambda r: pl.BlockSpec((r, tm), lambda i: (0, i % nb))
    row_out = lambda w, dt: jax.ShapeDtypeStruct((n, w), dt)
    col_out = lambda r, dt: jax.ShapeDtypeStruct((b, r, seq), dt)
    vt_out = lambda c: jax.ShapeDtypeStruct((b, c, VT_ROWS, seq), BF16)
    slabspec = lambda c, w: pl.BlockSpec((1, c, tm, w), lambda i: (i // nb, 0, i % nb, 0))
    slab_out = lambda c, w: jax.ShapeDtypeStruct((b, c, seq, w), BF16)
    outs = pl.pallas_call(
        _in_proj_kernel,
        grid=(n // tm,),
        in_specs=[rowspec(d), lay(1, d), lay(d, _ROW_W), lay(_COL_W, d),
                  gain_col(0, HEAD_DIM), gain_col(1, DIFF_D), gain_col(2, DIFF_D), gain_col(3, 2 * HEAD_DIM),
                  gain_col(4, HEAD_DIM),
                  gain_col(5, MLA_Q_RANK), gain_col(6, MLA_KV_RANK),
                  lay(N_HEADS * LANES, MLA_Q_RANK), lay(N_HEADS * LANES, MLA_KV_RANK),
                  lay(GROUP_WIDTH, MLA_KV_RANK), gain_col(7, LANES), gain_col(8, LANES),
                  tabspec(DIFF_D // 2), tabspec(DIFF_D // 2), tabspec(HEAD_DIM // 2), tabspec(HEAD_DIM // 2)],
        out_specs=[pl.BlockSpec((1, tm // NSA_CMP_STRIDE, NSA_CMP_STRIDE * LANES), lambda i: (i // nb, i % nb, 0)),
                   rowspec(128), rowspec(256), rowspec(256), rowspec(256), rowspec(256),
                   slabspec(2 * N_HEADS, DIFF_D), slabspec(2, HEAD_DIM),
                   colspec(256), colspec(256), colspec(256), vtspec(N_HEADS), vtspec(2),
                   colspec(N_HEADS * LANES), slabspec(N_HEADS, LANES), vtspec(N_HEADS)],
        out_shape=[jax.ShapeDtypeStruct((b, seq // NSA_CMP_STRIDE, NSA_CMP_STRIDE * LANES), F32),
                   row_out(128, F32), row_out(256, BF16), row_out(256, BF16),
                   row_out(256, BF16), row_out(256, BF16), slab_out(2 * N_HEADS, DIFF_D), slab_out(2, HEAD_DIM),
                   col_out(256, F32), col_out(256, BF16), col_out(256, BF16), vt_out(N_HEADS), vt_out(2),
                   col_out(N_HEADS * LANES, BF16), slab_out(N_HEADS, LANES), vt_out(N_HEADS)],
        scratch_shapes=[pltpu.VMEM((tm, LANES), F32)],
        compiler_params=_cparams(("parallel",)),
        name="in_proj",
    )(x.reshape(n, d), prm["norm_gain"], prm["w_row"], prm["w_col"],
      *([prm["gain_cols"]] * 7), prm["w_uq_t"], prm["w_uk_t"], prm["w_uv_t"], prm["gain_cols"], prm["gain_cols"],
      c32, s32, c64, s64)
    names = ("nsa_kcvc", "nsa_gl", "nsa_z", "diff_z", "mla_z", "mem_z", "diff_k", "nsa_k2",
             "nsa_qT", "diff_qT", "mem_qT", "diff_vT", "nsa_v2T", "mla_qT", "mla_k", "mla_vT")
    u = dict(zip(names, outs))
    for name in names[1:6]:
        u[name] = u[name].reshape(b, seq, -1)
    return u


X_RING_SLOTS = 3


def _out_proj_kernel(tm, x_hbm, y0_hbm, y1_hbm, y2_hbm, y3_hbm, w_ref, o_ref, x_ring, y_ring, sem):
    i = pl.program_id(0)
    ahead = X_RING_SLOTS - 1
    ys_hbm = (y0_hbm, y1_hbm, y2_hbm, y3_hbm)

    def tile_copies(step):
        slot = step % X_RING_SLOTS
        rows = pl.ds(step * tm, tm)
        return [pltpu.make_async_copy(x_hbm.at[rows], x_ring.at[slot], sem.at[slot, 0])] + [
            pltpu.make_async_copy(y_hbm.at[rows], y_ring.at[slot, g], sem.at[slot, g + 1])
            for g, y_hbm in enumerate(ys_hbm)]

    @pl.when(i == 0)
    def _():
        for step in range(ahead):
            for c in tile_copies(step):
                c.start()

    @pl.when(i + ahead < pl.num_programs(0))
    def _():
        for c in tile_copies(i + ahead):
            c.start()

    for c in tile_copies(i):
        c.wait()
    slot = i % X_RING_SLOTS
    acc = x_ring[slot]
    for g in range(len(ys_hbm)):
        acc = acc + jnp.dot(y_ring[slot, g], w_ref[g * GROUP_WIDTH:(g + 1) * GROUP_WIDTH, :],
                            preferred_element_type=F32)
    o_ref[...] = acc


def _out_proj(x2, ys, layer, w_out_b, tm=1024):
    n = x2.shape[0]
    hbm = pl.BlockSpec(memory_space=pl.ANY)
    assert n // tm >= X_RING_SLOTS
    return pl.pallas_call(
        functools.partial(_out_proj_kernel, tm),
        grid=(n // tm,),
        in_specs=[hbm] * (1 + len(ys)) + [_layer_spec(layer, D_MODEL, D_MODEL)],
        out_specs=pl.BlockSpec((tm, D_MODEL), lambda i: (i, 0)),
        out_shape=jax.ShapeDtypeStruct((n, D_MODEL), F32),
        scratch_shapes=[pltpu.VMEM((X_RING_SLOTS, tm, D_MODEL), F32),
                        pltpu.VMEM((X_RING_SLOTS, len(ys), tm, GROUP_WIDTH), BF16),
                        pltpu.SemaphoreType.DMA((X_RING_SLOTS, 1 + len(ys)))],
        compiler_params=_cparams(("arbitrary",)),
        name="out_proj",
    )(x2, *ys, w_out_b)


def _diff_kernel(lambda_init, t, qT_ref, k_ref, vT_ref, z_ref, lam_ref, sg_ref, y_ref, m_s, acc_s, s_s):
    qi = pl.program_id(1)
    n_maps = 2 * N_HEADS
    qT = qT_ref[0]
    _causal_flash_t(
        n_maps,
        lambda i, off: k_ref[0, i, pl.ds(off, KEY_TILE), :],
        lambda i: qT[i * DIFF_D:(i + 1) * DIFF_D],
        lambda i, off: vT_ref[0, i // 2, :, pl.ds(off, KEY_TILE)],
        qi, t, KEY_TILE, m_s, acc_s, s_s)

    lam = lam_ref[...]
    lmbda = (jnp.exp(jnp.sum(lam[0:1] * lam[1:2], axis=-1, keepdims=True))
             - jnp.exp(jnp.sum(lam[2:3] * lam[3:4], axis=-1, keepdims=True)) + lambda_init)
    outs = []
    for h in range(N_HEADS):
        d = _flash_out(acc_s, 2 * h) - lmbda * _flash_out(acc_s, 2 * h + 1)
        ms = jnp.mean(d * d, axis=0, keepdims=True)
        outs.append(d * lax.rsqrt(ms + EPS) * sg_ref[...] * (1.0 - lambda_init))
    y = jnp.concatenate(outs, axis=0).T * _silu(z_ref[0])
    y_ref[0] = y.astype(BF16)


def _diff_call(qT, k, vT, z, layer, lam, subln_col, lambda_init, t=1024):
    b, _, seq, _ = k.shape
    const = lambda shape: pl.BlockSpec(shape, lambda bi, qi: tuple(0 for _ in shape))
    tile = pl.BlockSpec((1, t, GROUP_WIDTH), lambda bi, qi: (bi, qi, 0))
    return pl.pallas_call(
        functools.partial(_diff_kernel, lambda_init, t),
        grid=(b, seq // t),
        in_specs=[pl.BlockSpec((1, GROUP_WIDTH, t), lambda bi, qi: (bi, 0, qi)),
                  pl.BlockSpec((1, 2 * N_HEADS, seq, DIFF_D), lambda bi, qi: (bi, 0, 0, 0)),
                  pl.BlockSpec((1, N_HEADS, VT_ROWS, seq), lambda bi, qi: (bi, 0, 0, 0)),
                  tile, _layer_spec(layer, 4, DIFF_D), _layer_spec(layer, HEAD_DIM, 1)],
        out_specs=tile,
        out_shape=jax.ShapeDtypeStruct((b, seq, GROUP_WIDTH), BF16),
        scratch_shapes=[pltpu.VMEM((2 * N_HEADS, t), F32),
                        pltpu.VMEM((2 * N_HEADS, VT_ROWS, t), F32),
                        pltpu.VMEM((2 * N_HEADS, KEY_TILE, t), F32)],
        compiler_params=_cparams(("parallel", "parallel")),
        name="diff_attn",
    )(qT, k, vT, z, lam, subln_col)


def _mla_mem_kernel(t, qT_ref, k_ref, vT_ref, z_ref,
                    mqT_ref, mem_ref, mz_ref, mg_ref, wk_ref, wvT_ref, kg_ref, gm_ref,
                    y_ref, ymem_ref, m_s, acc_s, s_s, mk_s, mvT_s, mm_s, macc_s):
    qi = pl.program_id(1)
    nt = (((1,), (1,)), ((), ()))
    m_len = mem_ref.shape[1]

    @pl.when(qi == 0)
    def _prep_memory_kv():
        mem = mem_ref[0]
        ms = jnp.mean(mem * mem, axis=-1, keepdims=True)
        mb = (mem * lax.rsqrt(ms + EPS) * mg_ref[...]).astype(BF16)
        k = jnp.dot(mb, wk_ref[...], preferred_element_type=F32)
        kn = k * _group_rsqrt(k, gm_ref[...], float(HEAD_DIM)) * kg_ref[...]
        vT = lax.dot_general(wvT_ref[...], mb, nt, preferred_element_type=F32)
        for h in range(N_HEADS):
            mk_s[h] = kn[:, h * HEAD_DIM:(h + 1) * HEAD_DIM].astype(BF16)
            mvT_s[h, 0:HEAD_DIM, :] = vT[h * HEAD_DIM:(h + 1) * HEAD_DIM].astype(BF16)
            mvT_s[h, HEAD_DIM:VT_ROWS, :] = _ones_rows(m_len)

    mqT = mqT_ref[0]
    _flash_reset(mm_s, macc_s)
    ss = [jnp.dot(mk_s[h], mqT[h * HEAD_DIM:(h + 1) * HEAD_DIM], preferred_element_type=F32)
          for h in range(N_HEADS)]
    for h in range(N_HEADS):
        _flash_step_t(ss[h], mvT_s[h], mm_s, macc_s, h)

    qT = qT_ref[0]
    _causal_flash_t(
        N_HEADS,
        lambda h, off: k_ref[0, h, pl.ds(off, KEY_TILE), :],
        lambda h: qT[h * LANES:(h + 1) * LANES],
        lambda h, off: vT_ref[0, h, :, pl.ds(off, KEY_TILE)],
        qi, t, KEY_TILE, m_s, acc_s, s_s)

    y_mem = jnp.concatenate([_flash_out(macc_s, h) for h in range(N_HEADS)], axis=0).T * _silu(mz_ref[0])
    ymem_ref[0] = y_mem.astype(BF16)
    y = jnp.concatenate([_flash_out(acc_s, h) for h in range(N_HEADS)], axis=0).T * _silu(z_ref[0])
    y_ref[0] = y.astype(BF16)


def _mla_mem_call(qT, k, vT, z, mem_qT, mem, mem_z, layer, prm, t=1024):
    b, _, seq, _ = k.shape
    m_len = mem.shape[1]
    lay = functools.partial(_layer_spec, layer)
    tile = pl.BlockSpec((1, t, GROUP_WIDTH), lambda bi, qi: (bi, qi, 0))
    const = lambda shape: pl.BlockSpec(shape, lambda bi, qi: tuple(0 for _ in shape))
    out = jax.ShapeDtypeStruct((b, seq, GROUP_WIDTH), BF16)
    return pl.pallas_call(
        functools.partial(_mla_mem_kernel, t),
        grid=(b, seq // t),
        in_specs=[pl.BlockSpec((1, N_HEADS * LANES, t), lambda bi, qi: (bi, 0, qi)),
                  pl.BlockSpec((1, N_HEADS, seq, LANES), lambda bi, qi: (bi, 0, 0, 0)),
                  pl.BlockSpec((1, N_HEADS, VT_ROWS, seq), lambda bi, qi: (bi, 0, 0, 0)),
                  tile,
                  pl.BlockSpec((1, GROUP_WIDTH, t), lambda bi, qi: (bi, 0, qi)),
                  pl.BlockSpec((1, m_len, D_MODEL), lambda bi, qi: (bi, 0, 0)), tile,
                  lay(1, D_MODEL), lay(D_MODEL, GROUP_WIDTH), lay(GROUP_WIDTH, D_MODEL),
                  lay(1, GROUP_WIDTH), const((GROUP_WIDTH, GROUP_WIDTH))],
        out_specs=[tile, tile],
        out_shape=[out, out],
        scratch_shapes=[pltpu.VMEM((N_HEADS, t), F32),
                        pltpu.VMEM((N_HEADS, VT_ROWS, t), F32),
                        pltpu.VMEM((N_HEADS, KEY_TILE, t), F32),
                        pltpu.VMEM((N_HEADS, m_len, HEAD_DIM), BF16),
                        pltpu.VMEM((N_HEADS, VT_ROWS, m_len), BF16),
                        pltpu.VMEM((N_HEADS, t), F32),
                        pltpu.VMEM((N_HEADS, VT_ROWS, t), F32)],
        compiler_params=_cparams(("parallel", "arbitrary")),
        name="mla_mem_attn",
    )(qT, k, vT, z, mem_qT, mem, mem_z, prm["mem_gain"], prm["mem_w_k"], prm["mem_w_vt"],
      prm["mem_k_gain"], _group_ones(GROUP_WIDTH, HEAD_DIM))


def _nsa_kernel(seq, t,
                qT_ref, kcvc_ref, k2_ref, v2T_ref, gl_ref, z_ref,
                cg_ref, pelo_ref, pehi_ref, wlo_ref, whi_ref, cosc_ref, sinc_ref,
                y_ref, kc_s, vcT_s, bias_s, m_s, acc_s, s_s, pg_s):
    qi = pl.program_id(1)
    half = HEAD_DIM // 2
    n_chunk = seq // NSA_CMP_STRIDE
    n_cmp = n_chunk - 1
    n_blk = seq // NSA_SLC_BLOCK
    blk_per_tile = t // NSA_SLC_BLOCK
    win_tiles = NSA_WINDOW // t
    w4 = N_HEADS * t
    SLC, WIN = 0, 1

    @pl.when(qi == 0)
    def _prep():
        ch = kcvc_ref[0]
        a = _dot_x3(*_split_bf16(ch + pelo_ref[...]), wlo_ref[0], wlo_ref[1])
        bm = _dot_x3(*_split_bf16(ch + pehi_ref[...]), whi_ref[0], whi_ref[1])
        cmp = a + pltpu.roll(bm, n_chunk - 1, 0)
        kc = cmp[:, :HEAD_DIM]
        ms = jnp.mean(kc * kc, axis=-1, keepdims=True)
        kc = kc * lax.rsqrt(ms + EPS) * cg_ref[...]
        x1, x2 = kc[:, :half], kc[:, half:]
        c, s = cosc_ref[...], sinc_ref[...]
        kc_hi, kc_lo = _split_bf16(jnp.concatenate([x1 * c - x2 * s, x2 * c + x1 * s], axis=-1))
        kc_s[0] = kc_hi
        kc_s[1] = kc_lo
        vcT_s[...] = cmp.T[HEAD_DIM:].astype(BF16)
        pg_s[:, n_chunk:, :] = jnp.zeros((t // LANES, pg_s.shape[1] - n_chunk, LANES), F32)

    qs = qi * t
    qT = qT_ref[0]
    q4f = jnp.concatenate([qT[h * HEAD_DIM:(h + 1) * HEAD_DIM] for h in range(N_HEADS)], axis=1)
    q4 = q4f.astype(BF16)

    def scores(br, j):
        off = pl.multiple_of(j * t, t)
        return jnp.dot(k2_ref[0, br, pl.ds(off, t), :], q4, preferred_element_type=F32)

    s_s[WIN] = scores(WIN, jnp.maximum(qi - win_tiles, 0))
    qcol = lax.broadcasted_iota(jnp.int32, (1, w4), 1) & (t - 1)
    pos_c = qs + qcol

    q_lo = (q4f - q4.astype(F32)).astype(BF16)
    heads = [slice(h * t, (h + 1) * t) for h in range(N_HEADS)]
    scs = [_dot_x3(kc_s[0], kc_s[1], q4[:, c], q_lo[:, c]) for c in heads]
    n_idx = lax.broadcasted_iota(jnp.int32, (n_chunk, 1), 0)
    cvalid = (n_idx * NSA_CMP_STRIDE + (NSA_CMP_BLOCK - 1) <= pos_c[:, :t]) & (n_idx < n_cmp)
    o_cmp, pg = [], None
    for h in range(N_HEADS):
        sc = jnp.where(cvalid, scs[h], NEG_INF)
        e = jnp.exp2(sc - jnp.max(sc, axis=0, keepdims=True))
        p = jnp.where(cvalid, e / jnp.sum(e, axis=0, keepdims=True), 0.0)
        o_cmp.append(jnp.dot(vcT_s[...], p.astype(BF16), preferred_element_type=F32))
        pg = p if pg is None else pg + p

    for c0 in range(t // LANES):
        pg_s[c0, 0:n_chunk, :] = pg[:, c0 * LANES:(c0 + 1) * LANES]
    ratio = NSA_SLC_BLOCK // NSA_CMP_STRIDE
    coef = np.convolve(np.ones(ratio), np.ones(NSA_CMP_BLOCK // NSA_CMP_STRIDE))
    p_slc = None
    for i, c in enumerate(coef):
        term = float(c) * jnp.concatenate(
            [pg_s[c0, pl.ds(i, n_blk, stride=ratio), :] for c0 in range(t // LANES)], axis=1)
        p_slc = term if p_slc is None else p_slc + term
    blk = lax.broadcasted_iota(jnp.int32, (n_blk, 1), 0)
    cur = lax.shift_right_logical(pos_c[:, :t], NSA_SLC_BLOCK.bit_length() - 1)
    forced = (blk == 0) | ((blk <= cur) & (blk > cur - NSA_N_LOCAL))
    score = jnp.where(blk > cur, NEG_INF, jnp.where(forced, BIG, p_slc))
    sub = 8
    groups = [score[g * sub:(g + 1) * sub] for g in range(n_blk // sub)]
    cnts = [jnp.zeros((sub, t), F32) for _ in groups]
    for i in range(n_blk):
        ri = score[i:i + 1, :]
        for g, sg in enumerate(groups):
            if g < i // sub:
                beats = ri > sg
            elif g > i // sub:
                beats = ri >= sg
            else:
                beats = (ri > sg) | ((ri == sg) & (blk[g * sub:(g + 1) * sub] > i))
            cnts[g] = cnts[g] + jnp.where(beats, 1.0, 0.0)
    cnt = jnp.concatenate(cnts, axis=0)
    bias = jnp.where(cnt < float(NSA_N_SELECT), 0.0, NEG_INF)
    bias = jnp.concatenate([bias] * N_HEADS, axis=1)
    for r in range(n_blk // blk_per_tile):
        bias_s[r] = bias[r * blk_per_tile:(r + 1) * blk_per_tile, :]

    _flash_reset(m_s, acc_s)
    krow = lax.broadcasted_iota(jnp.int32, (t, 1), 0)
    causal = krow <= qcol
    beyond = krow > qcol

    def slc_scores(j):
        s = scores(SLC, j).reshape(blk_per_tile, NSA_SLC_BLOCK, w4) + bias_s[j][:, None, :]
        return s.reshape(t, w4)

    def flash(br, s, j):
        off = pl.multiple_of(j * t, t)
        _flash_step_t(s, v2T_ref[0, br, :, pl.ds(off, t)], m_s, acc_s, br)

    s_s[SLC] = slc_scores(0)

    def old_body(j, carry):
        s = s_s[SLC]
        s_s[SLC] = slc_scores(j + 1)
        flash(SLC, s, j)
        return carry

    lax.fori_loop(0, jnp.maximum(qi - win_tiles, 0), old_body, 0)

    def win_step(j, win_mask):
        s_slc = s_s[SLC]
        s_s[SLC] = slc_scores(j + 1)
        flash(SLC, s_slc, j)
        s_win = s_s[WIN]
        if win_mask is not None:
            s_win = jnp.where(win_mask, s_win, NEG_INF)
        s_s[WIN] = scores(WIN, j + 1)
        flash(WIN, s_win, j)

    @pl.when(qi >= win_tiles)
    def _oldest_window_tile():
        win_step(qi - win_tiles, beyond)

    def win_body(j, carry):
        win_step(j, None)
        return carry

    lax.fori_loop(jnp.maximum(qi - win_tiles + 1, 0), qi, win_body, 0)
    flash(SLC, jnp.where(causal, s_s[SLC], NEG_INF), qi)
    flash(WIN, jnp.where(causal, s_s[WIN], NEG_INF), qi)

    g = jax.nn.sigmoid(gl_ref[0]).T
    outs = []
    for h in range(N_HEADS):
        cols = slice(h * t, (h + 1) * t)
        o_s = _flash_out(acc_s, SLC, cols)
        o_w = _flash_out(acc_s, WIN, cols)
        outs.append(g[h:h + 1, :] * o_cmp[h] + g[N_HEADS + h:N_HEADS + h + 1, :] * o_s
                    + g[2 * N_HEADS + h:2 * N_HEADS + h + 1, :] * o_w)
    y = jnp.concatenate(outs, axis=0).T * _silu(z_ref[0])
    y_ref[0] = y.astype(BF16)


def _nsa_call(qT, kcvc, k2, v2T, gl, z, layer, prm, t=256):
    b, _, seq, _ = k2.shape
    n_chunk = seq // NSA_CMP_STRIDE
    cw = NSA_CMP_STRIDE * 2 * HEAD_DIM
    assert t % NSA_SLC_BLOCK == 0 and NSA_WINDOW % t == 0 and t & (t - 1) == 0
    lay = functools.partial(_layer_spec, layer)

    tile = pl.BlockSpec((1, t, GROUP_WIDTH), lambda bi, qi: (bi, qi, 0))
    const = lambda shape: pl.BlockSpec(shape, lambda bi, qi: tuple(0 for _ in shape))
    w4 = N_HEADS * t
    return pl.pallas_call(
        functools.partial(_nsa_kernel, seq, t),
        grid=(b, seq // t),
        in_specs=[pl.BlockSpec((1, GROUP_WIDTH, t), lambda bi, qi: (bi, 0, qi)),
                  pl.BlockSpec((1, n_chunk, cw), lambda bi, qi: (bi, 0, 0)),
                  pl.BlockSpec((1, 2, seq, HEAD_DIM), lambda bi, qi: (bi, 0, 0, 0)),
                  pl.BlockSpec((1, 2, VT_ROWS, seq), lambda bi, qi: (bi, 0, 0, 0)),
                  pl.BlockSpec((1, t, LANES), lambda bi, qi: (bi, qi, 0)),
                  tile,
                  lay(1, HEAD_DIM),
                  lay(1, cw), lay(1, cw), lay(2, cw, 2 * HEAD_DIM), lay(2, cw, 2 * HEAD_DIM),
                  const((n_chunk, HEAD_DIM // 2)), const((n_chunk, HEAD_DIM // 2))],
        out_specs=tile,
        out_shape=jax.ShapeDtypeStruct((b, seq, GROUP_WIDTH), BF16),
        scratch_shapes=[pltpu.VMEM((2, n_chunk, HEAD_DIM), BF16),
                        pltpu.VMEM((HEAD_DIM, n_chunk), BF16),
                        pltpu.VMEM((seq // t, t // NSA_SLC_BLOCK, w4), F32),
                        pltpu.VMEM((2, w4), F32),
                        pltpu.VMEM((2, VT_ROWS, w4), F32),
                        pltpu.VMEM((2, t, w4), F32),
                        pltpu.VMEM((t // LANES, n_chunk + 8, LANES), F32)],
        compiler_params=_cparams(("parallel", "arbitrary")),
        name="nsa_attn",
    )(qT, kcvc, k2, v2T, gl, z,
      prm["nsa_cmp_gain"], prm["nsa_pe_lo"], prm["nsa_pe_hi"], prm["nsa_w_lo"], prm["nsa_w_hi"],
      prm["cos_c"], prm["sin_c"])


def _rope_cos_sin(pos, dim):
    half = dim // 2
    inv_freq = ROPE_THETA ** (-jnp.arange(half, dtype=F32) / half)
    ang = pos.astype(F32)[:, None] * inv_freq[None, :]
    return jnp.cos(ang), jnp.sin(ang)


def _prepare_params(seq, norm_gain, w_in, w_out, nsa_qk_gain, nsa_cmp_pe, nsa_w_cmp, diff_qk_gain, diff_subln_gain,
                    mla_cq_gain, mla_ckv_gain, mla_w_uq, mla_w_ukv, mla_qk_gain, mem_norm_gain, mem_w_kv, mem_qk_gain):
    layers = w_in.shape[0]
    prm = {}
    prm["w_row"], prm["w_col"] = _split_w_in(w_in)
    prm["w_out"] = w_out.astype(BF16)
    prm["norm_gain"] = norm_gain[:, None, :]

    col = lambda a: jnp.pad(a, ((0, 0), (0, GROUP_WIDTH - a.shape[1])))
    prm["gain_cols"] = jnp.stack([
        col(nsa_qk_gain[:, 0]), col(diff_qk_gain[:, 0]), col(diff_qk_gain[:, 1]),
        col(jnp.concatenate([nsa_qk_gain[:, 2], nsa_qk_gain[:, 3]], axis=1)), col(mem_qk_gain[:, 0]),
        col(mla_cq_gain), col(mla_ckv_gain), col(mla_qk_gain[:, 0]), col(mla_qk_gain[:, 1])], axis=1)[..., None]

    npad = LANES - MLA_QK
    uq = jnp.pad(mla_w_uq.reshape(layers, MLA_Q_RANK, N_HEADS, MLA_QK), ((0, 0), (0, 0), (0, 0), (0, npad)))
    prm["w_uq_t"] = jnp.swapaxes(uq.reshape(layers, MLA_Q_RANK, N_HEADS * LANES), 1, 2).astype(BF16)
    ukv = mla_w_ukv.reshape(layers, MLA_KV_RANK, N_HEADS, MLA_NOPE + HEAD_DIM)
    uk = jnp.pad(ukv[..., :MLA_NOPE], ((0, 0), (0, 0), (0, 0), (0, LANES - MLA_NOPE)))
    prm["w_uk_t"] = jnp.swapaxes(uk.reshape(layers, MLA_KV_RANK, N_HEADS * LANES), 1, 2).astype(BF16)
    uv = ukv[..., MLA_NOPE:].reshape(layers, MLA_KV_RANK, GROUP_WIDTH)
    prm["w_uv_t"] = jnp.swapaxes(uv, 1, 2).astype(BF16)

    half_blk = NSA_CMP_BLOCK // 2
    cw = half_blk * 2 * HEAD_DIM
    wk = nsa_w_cmp[:, 0].reshape(layers, NSA_CMP_BLOCK, HEAD_DIM, HEAD_DIM)
    wv = nsa_w_cmp[:, 1].reshape(layers, NSA_CMP_BLOCK, HEAD_DIM, HEAD_DIM)
    zero = jnp.zeros_like(wk)
    w_all = jnp.concatenate([jnp.concatenate([wk, zero], axis=3),
                             jnp.concatenate([zero, wv], axis=3)], axis=2)
    hi_lo = lambda w: jnp.stack([w.astype(BF16), (w - w.astype(BF16).astype(F32)).astype(BF16)], axis=1)
    prm["nsa_w_lo"] = hi_lo(w_all[:, :half_blk].reshape(layers, cw, 2 * HEAD_DIM))
    prm["nsa_w_hi"] = hi_lo(w_all[:, half_blk:].reshape(layers, cw, 2 * HEAD_DIM))
    pe_all = jnp.concatenate([nsa_cmp_pe[:, 0], nsa_cmp_pe[:, 1]], axis=2)
    prm["nsa_pe_lo"] = pe_all[:, :half_blk].reshape(layers, 1, cw)
    prm["nsa_pe_hi"] = pe_all[:, half_blk:].reshape(layers, 1, cw)
    prm["nsa_cmp_gain"] = nsa_qk_gain[:, 1][:, None, :]
    cmp_end = jnp.arange(seq // NSA_CMP_STRIDE, dtype=jnp.int32) * NSA_CMP_STRIDE + (NSA_CMP_BLOCK - 1)
    prm["cos_c"], prm["sin_c"] = _rope_cos_sin(cmp_end, HEAD_DIM)

    prm["diff_subln_col"] = diff_subln_gain[:, :, None]
    prm["mem_gain"] = mem_norm_gain[:, None, :]
    prm["mem_w_k"] = mem_w_kv[:, :, :GROUP_WIDTH].astype(BF16)
    prm["mem_w_vt"] = jnp.swapaxes(mem_w_kv[:, :, GROUP_WIDTH:], 1, 2).astype(BF16)
    prm["mem_k_gain"] = jnp.tile(mem_qk_gain[:, 1], (1, N_HEADS))[:, None, :]
    return prm


def _layer(x, mem, layer, prm, tabs_t, diff_lambda):
    b, seq, d = x.shape
    u = _in_proj(x, layer, prm, tabs_t)
    y_nsa = _nsa_call(u["nsa_qT"], u["nsa_kcvc"], u["nsa_k2"], u["nsa_v2T"], u["nsa_gl"], u["nsa_z"], layer, prm)
    lambda_init = 0.8 - 0.6 * math.exp(-0.3 * layer)
    y_diff = _diff_call(u["diff_qT"], u["diff_k"], u["diff_vT"], u["diff_z"], layer, diff_lambda,
                        prm["diff_subln_col"], lambda_init)
    y_mla, y_mem = _mla_mem_call(u["mla_qT"], u["mla_k"], u["mla_vT"], u["mla_z"],
                                 u["mem_qT"], mem, u["mem_z"], layer, prm)
    ys = [y.reshape(b * seq, GROUP_WIDTH) for y in (y_nsa, y_diff, y_mla, y_mem)]
    return _out_proj(x.reshape(b * seq, d), ys, layer, prm["w_out"]).reshape(b, seq, d)


def kernel(x, mem, norm_gain, w_in, w_out, nsa_qk_gain, nsa_cmp_pe, nsa_w_cmp, diff_qk_gain, diff_lambda,
           diff_subln_gain, mla_cq_gain, mla_ckv_gain, mla_w_uq, mla_w_ukv, mla_qk_gain, mem_norm_gain,
           mem_w_kv, mem_qk_gain):
    seq = x.shape[1]
    pos = jnp.arange(seq, dtype=jnp.int32)
    cos32, sin32 = _rope_cos_sin(pos, DIFF_D)
    cos64, sin64 = _rope_cos_sin(pos, HEAD_DIM)
    tabs_t = (cos32.T, sin32.T, cos64.T, sin64.T)
    prm = _prepare_params(seq, norm_gain, w_in, w_out, nsa_qk_gain, nsa_cmp_pe, nsa_w_cmp, diff_qk_gain,
                          diff_subln_gain, mla_cq_gain, mla_ckv_gain, mla_w_uq, mla_w_ukv, mla_qk_gain,
                          mem_norm_gain, mem_w_kv, mem_qk_gain)
    for l in range(DEPTH):
        x = _layer(x, mem, l, prm, tabs_t, diff_lambda)
    return x
```
